```python
import jax, jax.numpy as jnp
from jax import lax
import numpy as np

D_MODEL = 1024
BATCH = 8
SEQ = 8192
DEPTH = 1

HEAD_DIM = 64
N_HEADS_DIL = 8
N_HEADS_SB = 8
D_DIL = N_HEADS_DIL * HEAD_DIM
D_SB = N_HEADS_SB * HEAD_DIM
D_MIX = D_DIL + D_SB
DILATION_PATTERNS = ((128, 1), (512, 4), (2048, 16))
SB_BLOCK = 128
N_GROUPS = 4
EXPERTS_PER_GROUP = 8
N_EXPERTS = N_GROUPS * EXPERTS_PER_GROUP
TOP_K = 2
D_FF_EXPERT = D_MODEL // 2
MOE_BLOCK = 128
NORM_EPS = 1e-6

kernel_name = "hymba_dilated_stickbreak_hmoe_block"


def _rmsnorm(x, g):
    xf = x.astype(jnp.float32)
    y = xf * lax.rsqrt(jnp.mean(xf * xf, axis=-1, keepdims=True) + NORM_EPS)
    return (y * g.astype(jnp.float32)).astype(x.dtype)


def _modulate(h, shift, scale):
    return h * (1 + scale[:, None, :]) + shift[:, None, :]


def _alibi_slopes(n_heads):
    return np.array([2.0 ** (-8.0 * (i + 1) / n_heads) for i in range(n_heads)], dtype=np.float32)


def _dilated_window_attention(q, k, v, window, dilation, slopes):
    b, s, h, e = q.shape
    n = window // dilation
    unit = n * dilation
    s_pad = -(-s // unit) * unit
    nb = s_pad // unit
    pad = ((0, 0), (0, s_pad - s), (0, 0), (0, 0))

    def split(t):
        return jnp.pad(t, pad).reshape(b, nb, n, dilation, h, e)

    def with_prev(t):
        prev = jnp.pad(t[:, :-1], ((0, 0), (1, 0), (0, 0), (0, 0), (0, 0), (0, 0)))
        return jnp.concatenate([prev, t], axis=2)

    qb = split(q)
    kc = with_prev(split(k))
    vc = with_prev(split(v)).astype(jnp.float32)
    scores = jnp.einsum('bgidhe,bgjdhe->bgdhij', qb, kc,
                        preferred_element_type=jnp.float32) * (e ** -0.5)
    i_idx = np.arange(n)[:, None]
    j_idx = np.arange(2 * n)[None, :]
    steps = i_idx + n - j_idx
    blk = np.arange(nb)[:, None, None]
    valid = (steps >= 0) & (steps <= n) & (blk * n + j_idx - n >= 0)
    bias = -slopes[:, None, None] * (steps * dilation).astype(np.float32)[None]
    logits = jnp.where(valid[None, :, None, None], scores + jnp.asarray(bias)[None, None, None], -jnp.inf)
    m = jnp.max(logits, axis=-1)
    p = jnp.exp(logits - m[..., None])
    den = jnp.sum(p, axis=-1)
    den_t = den.transpose(0, 1, 4, 2, 3)
    o = jnp.einsum('bgdhij,bgjdhe->bgidhe', p, vc) / den_t[..., None]
    o = o.reshape(b, s_pad, h, e)[:, :s]
    m = m.transpose(0, 1, 4, 2, 3).reshape(b, s_pad, h)[:, :s]
    den = den_t.reshape(b, s_pad, h)[:, :s]
    return o, m, den


def _dilated_mixture(q, k, v):
    slopes = _alibi_slopes(q.shape[2])
    outs, maxes, dens = [], [], []
    for window, dilation in DILATION_PATTERNS:
        o, m, den = _dilated_window_attention(q, k, v, window, dilation, slopes)
        outs.append(o)
        maxes.append(m)
        dens.append(den)
    outs = jnp.stack(outs)
    maxes = jnp.stack(maxes)
    dens = jnp.stack(dens)
    w = dens * jnp.exp(maxes - jnp.max(maxes, axis=0, keepdims=True))
    w = w / jnp.sum(w, axis=0, keepdims=True)
    return jnp.sum(w[..., None] * outs, axis=0).astype(q.dtype)


def _stick_breaking_attention(q, k, v):
    b, s, h, e = q.shape
    nb = s // SB_BLOCK
    qblocks = q.reshape(b, nb, SB_BLOCK, h, e).transpose(1, 0, 2, 3, 4)
    key_pos = jnp.arange(s)
    vf = v.astype(jnp.float32)

    def block(args):
        qb, idx = args
        z = jnp.einsum('bqhe,bshe->bhqs', qb, k, preferred_element_type=jnp.float32) * (e ** -0.5)
        q_pos = idx * SB_BLOCK + jnp.arange(SB_BLOCK)
        causal = key_pos[None, :] < q_pos[:, None]
        log_beta = jax.nn.log_sigmoid(z)
        log_keep = jnp.where(causal, jax.nn.log_sigmoid(-z), 0.0)
        between = lax.cumsum(log_keep, axis=3, reverse=True) - log_keep
        weights = jnp.where(causal, jnp.exp(log_beta + between), 0.0)
        return jnp.einsum('bhqs,bshe->bqhe', weights, vf)

    out = lax.map(block, (qblocks, jnp.arange(nb)))
    return out.transpose(1, 0, 2, 3, 4).reshape(b, s, h, e).astype(q.dtype)


def _hierarchical_moe(h, w_group, w_expert, w_gate, w_up, w_down):
    n, d = h.shape
    group_prob = jax.nn.softmax((h @ w_group).astype(jnp.float32), axis=-1)
    group = jnp.argmax(group_prob, axis=-1)
    group_gate = jnp.take_along_axis(group_prob, group[:, None], axis=1)
    exp_logits = (h @ w_expert).astype(jnp.float32).reshape(n, N_GROUPS, EXPERTS_PER_GROUP)
    exp_logits = jnp.take_along_axis(exp_logits, group[:, None, None], axis=1)[:, 0]
    top_p, top_local = lax.top_k(jax.nn.softmax(exp_logits, axis=-1), TOP_K)
    top_p = top_p / jnp.sum(top_p, axis=-1, keepdims=True)
    weight = (group_gate * top_p).reshape(-1)
    expert = (group[:, None] * EXPERTS_PER_GROUP + top_local).reshape(-1)
    token = jnp.repeat(jnp.arange(n), TOP_K)

    order = jnp.argsort(expert)
    e_sorted = expert[order]
    tok_sorted = token[order]
    w_sorted = weight[order]
    counts = jnp.zeros((N_EXPERTS,), jnp.int32).at[expert].add(1)
    padded = (counts + MOE_BLOCK - 1) // MOE_BLOCK * MOE_BLOCK
    start = jnp.cumsum(counts) - counts
    pad_end = jnp.cumsum(padded)
    pad_start = pad_end - padded
    nk = n * TOP_K
    dest = pad_start[e_sorted] + jnp.arange(nk) - start[e_sorted]
    cap = -(-nk // MOE_BLOCK) * MOE_BLOCK + N_EXPERTS * MOE_BLOCK
    n_blocks = cap // MOE_BLOCK
    buf = jnp.zeros((cap, d), h.dtype).at[dest].set(h[tok_sorted])
    block_expert = jnp.minimum(
        jnp.searchsorted(pad_end, jnp.arange(n_blocks) * MOE_BLOCK, side='right'), N_EXPERTS - 1)

    def expert_block(args):
        xb, e = args
        return (jax.nn.silu(xb @ w_gate[e]) * (xb @ w_up[e])) @ w_down[e]

    y = lax.map(expert_block, (buf.reshape(n_blocks, MOE_BLOCK, d), block_expert)).reshape(cap, d)
    contrib = y[dest] * w_sorted[:, None].astype(y.dtype)
    return jnp.zeros((n, d), y.dtype).at[tok_sorted].add(contrib)


def setup_inputs(seed: int = 0) -> dict:
    key = jax.random.key(seed)
    ks = jax.random.split(key, 16)
    d = D_MODEL

    def nrm(k, shape, std):
        return jax.random.normal(k, shape, jnp.float32) * std

    return {
        "x": nrm(ks[0], (BATCH, SEQ, d), 1.0),
        "c": nrm(ks[1], (BATCH, d), 1.0),
        "w_ada": nrm(ks[2], (DEPTH, d, 6 * d), 0.5 * d ** -0.5),
        "b_ada": nrm(ks[3], (DEPTH, 6 * d), 0.02),
        "g_mix": 1.0 + nrm(ks[4], (DEPTH, d), 0.02),
        "w_in": nrm(ks[5], (DEPTH, d, 3 * D_MIX), d ** -0.5),
        "g_dil_out": 1.0 + nrm(ks[6], (DEPTH, D_DIL), 0.02),
        "g_sb_out": 1.0 + nrm(ks[7], (DEPTH, D_SB), 0.02),
        "w_out": nrm(ks[8], (DEPTH, D_MIX, d), D_MIX ** -0.5),
        "g_ffn": 1.0 + nrm(ks[9], (DEPTH, d), 0.02),
        "w_group": nrm(ks[10], (DEPTH, d, N_GROUPS), d ** -0.5),
        "w_expert": nrm(ks[11], (DEPTH, d, N_EXPERTS), d ** -0.5),
        "w_gate": nrm(ks[12], (DEPTH, N_EXPERTS, d, D_FF_EXPERT), d ** -0.5),
        "w_up": nrm(ks[13], (DEPTH, N_EXPERTS, d, D_FF_EXPERT), d ** -0.5),
        "w_down": nrm(ks[14], (DEPTH, N_EXPERTS, D_FF_EXPERT, d), D_FF_EXPERT ** -0.5),
        "g_final": 1.0 + nrm(ks[15], (d,), 0.02),
    }


def reference(x, c, w_ada, b_ada, g_mix, w_in, g_dil_out, g_sb_out, w_out, g_ffn,
              w_group, w_expert, w_gate, w_up, w_down, g_final):
    b, s, d = x.shape
    cond = jax.nn.silu(c)
    for layer in range(DEPTH):
        mod = cond @ w_ada[layer] + b_ada[layer]
        shift_mix, scale_mix, gate_mix, shift_ffn, scale_ffn, gate_ffn = jnp.split(mod, 6, axis=-1)

        h = _modulate(_rmsnorm(x, g_mix[layer]), shift_mix, scale_mix)
        qkv = h @ w_in[layer]
        q_d, k_d, v_d, q_s, k_s, v_s = jnp.split(
            qkv, [D_DIL, 2 * D_DIL, 3 * D_DIL, 3 * D_DIL + D_SB, 3 * D_DIL + 2 * D_SB], axis=-1)
        heads_d = lambda t: t.reshape(b, s, N_HEADS_DIL, HEAD_DIM)
        heads_s = lambda t: t.reshape(b, s, N_HEADS_SB, HEAD_DIM)
        o_dil = _dilated_mixture(heads_d(q_d), heads_d(k_d), heads_d(v_d)).reshape(b, s, D_DIL)
        o_sb = _stick_breaking_attention(heads_s(q_s), heads_s(k_s), heads_s(v_s)).reshape(b, s, D_SB)
        mixed = jnp.concatenate([_rmsnorm(o_dil, g_dil_out[layer]),
                                 _rmsnorm(o_sb, g_sb_out[layer])], axis=-1)
        x = x + gate_mix[:, None, :] * (mixed @ w_out[layer])

        h2 = _modulate(_rmsnorm(x, g_ffn[layer]), shift_ffn, scale_ffn).reshape(b * s, d)
        y = _hierarchical_moe(h2, w_group[layer], w_expert[layer],
                              w_gate[layer], w_up[layer], w_down[layer]).reshape(b, s, d)
        x = x + gate_ffn[:, None, :] * y
    return _rmsnorm(x, g_final)
```

```python
import functools

import numpy as np
import jax
import jax.numpy as jnp
from jax import lax
from jax.experimental import pallas as pl
from jax.experimental.pallas import tpu as pltpu

HEAD_DIM = 64
N_HEADS_DIL = 8
N_HEADS_SB = 8
D_DIL = N_HEADS_DIL * HEAD_DIM
D_SB = N_HEADS_SB * HEAD_DIM
DILATION_PATTERNS = ((128, 1), (512, 4), (2048, 16))
N_GROUPS = 4
EXPERTS_PER_GROUP = 8
N_EXPERTS = N_GROUPS * EXPERTS_PER_GROUP
NORM_EPS = 1e-6

LANES = 128
DIL_STEPS = 128
DIL_UNIT = 2048
SB_BLOCK = 256
PRE_ROWS = 512
POST_ROWS = 256
EXPERT_ROWS = 512
DMA_ROWS = 2048
ROUTE_LANE0 = N_GROUPS
VMEM_LIMIT = 48 * 1024 * 1024

F32 = jnp.float32
BF16 = jnp.bfloat16
NEG_INF = float("-inf")


def _cparams(sem):
    return pltpu.CompilerParams(dimension_semantics=sem, vmem_limit_bytes=VMEM_LIMIT)


def _rms(v, g):
    return v * lax.rsqrt(jnp.mean(v * v, axis=-1, keepdims=True) + NORM_EPS) * g


def _ada_kernel(c_ref, w_ref, b_ref, o_ref):
    c = c_ref[...]
    cond = c / (1.0 + jnp.exp(-c))
    o_ref[...] = jnp.dot(cond, w_ref[...], precision=lax.Precision.HIGHEST,
                         preferred_element_type=F32) + b_ref[...]


def _ada(c, w_ada, b_ada):
    b, d = c.shape
    n = w_ada.shape[1]
    return pl.pallas_call(
        _ada_kernel,
        grid=(n // d,),
        in_specs=[pl.BlockSpec((b, d), lambda j: (0, 0)),
                  pl.BlockSpec((d, d), lambda j: (0, j)),
                  pl.BlockSpec((1, d), lambda j: (0, j))],
        out_specs=pl.BlockSpec((b, d), lambda j: (0, j)),
        out_shape=jax.ShapeDtypeStruct((b, n), F32),
        compiler_params=_cparams(("arbitrary",)),
        name="ada",
    )(c, w_ada, b_ada.reshape(1, n))


def _premix_kernel(x_ref, shift_ref, scale_ref, g_ref, w_ref, qkvd_ref, qs_ref, ks_ref, vs_ref):
    h = _rms(x_ref[0], g_ref[...]) * (1.0 + scale_ref[0]) + shift_ref[0]
    hb = h.astype(BF16)
    scale = HEAD_DIM ** -0.5
    for j in range(6):
        r = jnp.dot(hb, w_ref[:, j * 512:(j + 1) * 512], preferred_element_type=F32)
        if j == 0:
            qkvd_ref[0, :, 0:512] = r * scale
        elif j < 3:
            qkvd_ref[0, :, j * 512:(j + 1) * 512] = r
        elif j == 3:
            qs_ref[0] = (r * scale).astype(BF16)
        elif j == 4:
            ks_ref[0] = r.astype(BF16)
        else:
            vs_ref[0] = r.astype(BF16)


def _premix(x, shift, scale, g_mix, w_in_bf16):
    b, s, d = x.shape
    tm = PRE_ROWS
    mod_spec = pl.BlockSpec((1, 1, d), lambda bi, i: (bi, 0, 0))
    sb_spec = pl.BlockSpec((1, tm, D_SB), lambda bi, i: (bi, i, 0))
    return pl.pallas_call(
        _premix_kernel,
        grid=(b, s // tm),
        in_specs=[pl.BlockSpec((1, tm, d), lambda bi, i: (bi, i, 0)),
                  mod_spec, mod_spec,
                  pl.BlockSpec((1, d), lambda bi, i: (0, 0)),
                  pl.BlockSpec((d, 3 * (D_DIL + D_SB)), lambda bi, i: (0, 0))],
        out_specs=[pl.BlockSpec((1, tm, 3 * D_DIL), lambda bi, i: (bi, i, 0)),
                   sb_spec, sb_spec, sb_spec],
        out_shape=[jax.ShapeDtypeStruct((b, s, 3 * D_DIL), F32),
                   jax.ShapeDtypeStruct((b, s, D_SB), BF16),
                   jax.ShapeDtypeStruct((b, s, D_SB), BF16),
                   jax.ShapeDtypeStruct((b, s, D_SB), BF16)],
        compiler_params=_cparams(("arbitrary", "arbitrary")),
        name="premix",
    )(x, shift, scale, g_mix.reshape(1, d), w_in_bf16)


def _dilated_bias():
    n = DIL_STEPS
    slopes = np.array([2.0 ** (-8.0 * (i + 1) / N_HEADS_DIL) for i in range(N_HEADS_DIL)], dtype=np.float32)
    steps = np.arange(n)[:, None] + n - np.arange(2 * n)[None, :]
    valid = (steps >= 0) & (steps <= n)
    out = []
    for _, dilation in DILATION_PATTERNS:
        bias = -slopes[:, None, None] * (steps * dilation).astype(np.float32)[None]
        out.append(np.where(valid[None], bias, -np.inf).astype(np.float32))
    return np.stack(out)


def _dil_kernel(q_ref, kc_ref, kp_ref, vc_ref, vp_ref, bias_ref, o_ref,
                kext, vext, u_scr, m_scr, l_scr):
    n = DIL_STEPS
    g = pl.program_id(1)
    kext[0:DIL_UNIT, :] = kp_ref[0]
    kext[DIL_UNIT:2 * DIL_UNIT, :] = kc_ref[0]
    vext[0:DIL_UNIT, :] = vp_ref[0]
    vext[DIL_UNIT:2 * DIL_UNIT, :] = vc_ref[0]
    lane = lax.broadcasted_iota(jnp.int32, (n, LANES), 1)
    head0 = lane < HEAD_DIM
    col = lax.broadcasted_iota(jnp.int32, (n, 2 * n), 1)

    for p, (_, dil) in enumerate(DILATION_PATTERNS):
        unit = n * dil

        def tile(ti, carry, p=p, dil=dil, unit=unit):
            j = ti // dil
            r = ti % dil
            qstart = j * unit + r
            kstart = DIL_UNIT + qstart - unit
            if dil == 1:
                q = q_ref[0, pl.ds(qstart, n), :]
                kk = kext[pl.ds(kstart, 2 * n), :]
                vv = vext[pl.ds(kstart, 2 * n), :]
            else:
                q = q_ref[0, pl.ds(qstart, n, stride=dil), :]
                kk = kext[pl.ds(kstart, 2 * n, stride=dil), :]
                vv = vext[pl.ds(kstart, 2 * n, stride=dil), :]
            kk = kk.astype(BF16)
            vv = vv.astype(BF16)
            dead_cols = jnp.where(jnp.logical_and(g == 0, j == 0), n, 0)
            us, ms, ls = [], [], []
            for h in range(2):
                qh = jnp.where(head0 if h == 0 else jnp.logical_not(head0), q, 0.0).astype(BF16)
                s = lax.dot_general(qh, kk, (((1,), (1,)), ((), ())), preferred_element_type=F32)
                logits = jnp.where(col < dead_cols, NEG_INF, s + bias_ref[p, h])
                m = jnp.max(logits, axis=-1, keepdims=True)
                pe = jnp.exp(logits - m)
                ls.append(jnp.sum(pe, axis=-1, keepdims=True))
                ms.append(m)
                us.append(jnp.dot(pe.astype(BF16), vv, preferred_element_type=F32))
            u = jnp.where(head0, us[0], us[1])
            m = jnp.where(head0, ms[0], ms[1])
            l = jnp.where(head0, ls[0], ls[1])
            if dil == 1:
                rows = pl.ds(qstart, n)
            else:
                rows = pl.ds(qstart, n, stride=dil)
            u_scr[p, rows, :] = u
            m_scr[p, rows, :] = m
            l_scr[p, rows, :] = l
            return carry

        lax.fori_loop(0, DIL_UNIT // n, tile, 0)

    def merge(i, carry):
        rows = pl.ds(pl.multiple_of(i * n, n), n)
        m0, m1, m2 = m_scr[0, rows, :], m_scr[1, rows, :], m_scr[2, rows, :]
        mx = jnp.maximum(jnp.maximum(m0, m1), m2)
        w0, w1, w2 = jnp.exp(m0 - mx), jnp.exp(m1 - mx), jnp.exp(m2 - mx)
        num = w0 * u_scr[0, rows, :] + w1 * u_scr[1, rows, :] + w2 * u_scr[2, rows, :]
        den = w0 * l_scr[0, rows, :] + w1 * l_scr[1, rows, :] + w2 * l_scr[2, rows, :]
        o_ref[0, rows, :] = num / den
        return carry

    lax.fori_loop(0, DIL_UNIT // n, merge, 0)


def _dilated(qkv_d, bias):
    b, s, _ = qkv_d.shape
    u = DIL_UNIT
    npair = D_DIL // LANES
    cur = lambda off: pl.BlockSpec((1, u, LANES), lambda bi, g, p: (bi, g, off + p))
    prev = lambda off: pl.BlockSpec((1, u, LANES), lambda bi, g, p: (bi, jnp.maximum(g - 1, 0), off + p))
    return pl.pallas_call(
        _dil_kernel,
        grid=(b, s // u, npair),
        in_specs=[cur(0), cur(npair), prev(npair), cur(2 * npair), prev(2 * npair),
                  pl.BlockSpec((3, 2, DIL_STEPS, 2 * DIL_STEPS), lambda bi, g, p: (0, p, 0, 0))],
        out_specs=pl.BlockSpec((1, u, LANES), lambda bi, g, p: (bi, g, p)),
        out_shape=jax.ShapeDtypeStruct((b, s, D_DIL), F32),
        scratch_shapes=[pltpu.VMEM((2 * u, LANES), F32), pltpu.VMEM((2 * u, LANES), F32),
                        pltpu.VMEM((3, u, LANES), F32), pltpu.VMEM((3, u, LANES), F32),
                        pltpu.VMEM((3, u, LANES), F32)],
        compiler_params=_cparams(("arbitrary", "arbitrary", "arbitrary")),
        name="dilated",
    )(qkv_d, qkv_d, qkv_d, qkv_d, qkv_d, bias)


def _stick_kernel(q_ref, k_ref, v_ref, tri_ref, o_ref):
    blk = SB_BLOCK
    qi = pl.program_id(2)
    lane = lax.broadcasted_iota(jnp.int32, (blk, LANES), 1)
    head0 = lane < HEAD_DIM
    q = q_ref[0]
    zero = jnp.zeros_like(q)
    qh = (jnp.where(head0, q, zero), jnp.where(head0, zero, q))
    tri = tri_ref[...]
    causal = (lax.broadcasted_iota(jnp.int32, (blk, blk), 1)
              < lax.broadcasted_iota(jnp.int32, (blk, blk), 0))

    def sweep(kb, state, diag):
        start = pl.multiple_of(kb * blk, blk)
        kblk = k_ref[0, pl.ds(start, blk), :]
        vblk = v_ref[0, pl.ds(start, blk), :]
        new = []
        for h in range(2):
            carry, acc = state[2 * h], state[2 * h + 1]
            z = lax.dot_general(qh[h], kblk, (((1,), (1,)), ((), ())), preferred_element_type=F32)
            sp = jnp.log(1.0 + jnp.exp(-jnp.abs(z)))
            log_beta = jnp.minimum(z, 0.0) - sp
            log_keep = jnp.minimum(-z, 0.0) - sp
            if diag:
                log_keep = jnp.where(causal, log_keep, 0.0)
            hi = log_keep.astype(BF16)
            lo = (log_keep - hi.astype(F32)).astype(BF16)
            between = (jnp.dot(hi, tri, preferred_element_type=F32)
                       + jnp.dot(lo, tri, preferred_element_type=F32))
            w = jnp.exp(log_beta + (between + carry))
            if diag:
                w = jnp.where(causal, w, 0.0)
            acc = acc + jnp.dot(w.astype(BF16), vblk, preferred_element_type=F32)
            carry = carry + jnp.sum(log_keep, axis=-1, keepdims=True)
            new += [carry, acc]
        return tuple(new)

    init = (jnp.zeros((blk, 1), F32), jnp.zeros((blk, LANES), F32)) * 2
    state = sweep(qi, init, True)
    state = lax.fori_loop(0, qi, lambda i, st: sweep(qi - 1 - i, st, False), state)
    o_ref[0] = jnp.where(head0, state[1], state[3])


def _stick(q_s, k_s, v_s):
    b, s, _ = q_s.shape
    blk = SB_BLOCK
    tri = jnp.asarray(np.tril(np.ones((blk, blk), np.float32), -1), BF16)
    full = pl.BlockSpec((1, s, LANES), lambda bi, p, i: (bi, 0, p))
    return pl.pallas_call(
        _stick_kernel,
        grid=(b, D_SB // LANES, s // blk),
        in_specs=[pl.BlockSpec((1, blk, LANES), lambda bi, p, i: (bi, i, p)), full, full,
                  pl.BlockSpec((blk, blk), lambda bi, p, i: (0, 0))],
        out_specs=pl.BlockSpec((1, blk, LANES), lambda bi, p, i: (bi, i, p)),
        out_shape=jax.ShapeDtypeStruct((b, s, D_SB), F32),
        compiler_params=_cparams(("arbitrary", "arbitrary", "arbitrary")),
        name="stick",
    )(q_s, k_s, v_s, tri)


def _postmix_kernel(x_ref, od_ref, os_ref, gd_ref, gs_ref, wout_ref, gate_ref, shift_ref, scale_ref,
                    gffn_ref, wr_ref, tri_ref, x1_ref, h2_ref, route_ref, cnt_ref, carry_scr):
    tm = POST_ROWS

    @pl.when(jnp.logical_and(pl.program_id(0) == 0, pl.program_id(1) == 0))
    def _():
        carry_scr[...] = jnp.zeros_like(carry_scr)

    mixed = jnp.concatenate([_rms(od_ref[0], gd_ref[...]), _rms(os_ref[0], gs_ref[...])], axis=-1)
    proj = jnp.dot(mixed.astype(BF16), wout_ref[...], preferred_element_type=F32)
    x1 = x_ref[0] + gate_ref[0] * proj
    x1_ref[0] = x1
    h2 = _rms(x1, gffn_ref[...]) * (1.0 + scale_ref[0]) + shift_ref[0]
    h2_ref[0] = h2
    logits = jnp.dot(h2, wr_ref[...], precision=lax.Precision.HIGHEST, preferred_element_type=F32)

    lane = lax.broadcasted_iota(jnp.int32, (tm, LANES), 1)
    big = jnp.int32(LANES)
    lmax = lambda v: jnp.max(v, axis=-1, keepdims=True)
    lmin = lambda v: jnp.min(v, axis=-1, keepdims=True)
    lsum = lambda v: jnp.sum(v, axis=-1, keepdims=True)

    gmask = lane < N_GROUPS
    gl = jnp.where(gmask, logits, NEG_INF)
    gmx = lmax(gl)
    group = lmin(jnp.where(jnp.logical_and(gmask, gl == gmx), lane, big))
    group_gate = 1.0 / lsum(jnp.exp(gl - gmx))
    lo = ROUTE_LANE0 + group * EXPERTS_PER_GROUP
    emask = jnp.logical_and(lane >= lo, lane < lo + EXPERTS_PER_GROUP)
    el = jnp.where(emask, logits, NEG_INF)
    l1 = lmax(el)
    i1 = lmin(jnp.where(el == l1, lane, big))
    el2 = jnp.where(lane == i1, NEG_INF, el)
    l2 = lmax(el2)
    i2 = lmin(jnp.where(el2 == l2, lane, big))
    r = jnp.exp(l2 - l1)
    w1 = group_gate / (1.0 + r)
    w2 = group_gate * r / (1.0 + r)

    onehot = jnp.logical_or(lane == i1, lane == i2)
    oh = jnp.where(onehot, 1.0, 0.0)
    before = jnp.dot(tri_ref[...], oh.astype(BF16), preferred_element_type=F32) + carry_scr[...]
    rank1 = lsum(jnp.where(lane == i1, before, 0.0))
    rank2 = lsum(jnp.where(lane == i2, before, 0.0))
    carry_scr[...] = carry_scr[...] + jnp.sum(oh, axis=0, keepdims=True)
    cnt_ref[...] = carry_scr[...]

    e1 = (i1 - ROUTE_LANE0).astype(F32)
    e2 = (i2 - ROUTE_LANE0).astype(F32)
    route = jnp.zeros((tm, LANES), F32)
    for k, val in enumerate((e1, e2, rank1, rank2, w1, w2)):
        route = jnp.where(lane == k, val, route)
    route_ref[0] = route


def _postmix(x, o_dil, o_sb, g_dil, g_sb, w_out_bf16, gate, shift, scale, g_ffn, w_router):
    b, s, d = x.shape
    tm = POST_ROWS
    tri = jnp.asarray(np.tril(np.ones((tm, tm), np.float32), -1), BF16)
    row = lambda w: pl.BlockSpec((1, tm, w), lambda bi, i: (bi, i, 0))
    vec = lambda w: pl.BlockSpec((1, w), lambda bi, i: (0, 0))
    mod_spec = pl.BlockSpec((1, 1, d), lambda bi, i: (bi, 0, 0))
    return pl.pallas_call(
        _postmix_kernel,
        grid=(b, s // tm),
        in_specs=[row(d), row(D_DIL), row(D_SB), vec(D_DIL), vec(D_SB),
                  pl.BlockSpec((d, d), lambda bi, i: (0, 0)),
                  mod_spec, mod_spec, mod_spec, vec(d),
                  pl.BlockSpec((d, LANES), lambda bi, i: (0, 0)),
                  pl.BlockSpec((tm, tm), lambda bi, i: (0, 0))],
        out_specs=[row(d), row(d), row(LANES), pl.BlockSpec((1, LANES), lambda bi, i: (0, 0))],
        out_shape=[jax.ShapeDtypeStruct((b, s, d), F32), jax.ShapeDtypeStruct((b, s, d), F32),
                   jax.ShapeDtypeStruct((b, s, LANES), F32), jax.ShapeDtypeStruct((1, LANES), F32)],
        scratch_shapes=[pltpu.VMEM((1, LANES), F32)],
        compiler_params=_cparams(("arbitrary", "arbitrary")),
        name="postmix",
    )(x, o_dil, o_sb, g_dil.reshape(1, -1), g_sb.reshape(1, -1), w_out_bf16, gate, shift, scale,
      g_ffn.reshape(1, d), w_router, tri)


def _row_copy(src, dst, sem, src_row, dst_row):
    return pltpu.make_async_copy(src.at[pl.ds(src_row, 1)], dst.at[pl.ds(dst_row, 1)], sem)


def _dispatch_kernel(start_ref, e_ref, r_ref, h2_ref, buf_in_ref, buf_ref, sem):
    del buf_in_ref
    base = pl.program_id(0) * DMA_ROWS

    def issue(a, carry):
        dst = start_ref[e_ref[0, 0, a]] + r_ref[0, 0, a]
        _row_copy(h2_ref, buf_ref, sem, (base + a) // 2, dst).start()
        return carry

    lax.fori_loop(0, DMA_ROWS, issue, 0)

    def drain(a, carry):
        _row_copy(h2_ref, buf_ref, sem, 0, 0).wait()
        return carry

    lax.fori_loop(0, DMA_ROWS, drain, 0)


def _dispatch(pad_start, e_flat, r_flat, h2, buf):
    nsteps = e_flat.shape[0]
    smem = pl.BlockSpec((1, 1, DMA_ROWS), lambda i, st: (i, 0, 0), memory_space=pltpu.SMEM)
    any_spec = pl.BlockSpec(memory_space=pl.ANY)
    return pl.pallas_call(
        _dispatch_kernel,
        grid_spec=pltpu.PrefetchScalarGridSpec(
            num_scalar_prefetch=1, grid=(nsteps,),
            in_specs=[smem, smem, any_spec, any_spec],
            out_specs=any_spec,
            scratch_shapes=[pltpu.SemaphoreType.DMA(())]),
        out_shape=jax.ShapeDtypeStruct(buf.shape, buf.dtype),
        input_output_aliases={4: 0},
        compiler_params=_cparams(("arbitrary",)),
        name="dispatch",
    )(pad_start, e_flat, r_flat, h2, buf)


def _collect_kernel(start_ref, e_ref, r_ref, y_ref, out_ref, sem, *, n_tokens):
    base = pl.program_id(0) * DMA_ROWS

    def issue(a, carry):
        src = start_ref[e_ref[0, 0, a]] + r_ref[0, 0, a]
        idx = base + a
        _row_copy(y_ref, out_ref, sem, src, (idx % 2) * n_tokens + idx // 2).start()
        return carry

    lax.fori_loop(0, DMA_ROWS, issue, 0)

    def drain(a, carry):
        _row_copy(y_ref, out_ref, sem, 0, 0).wait()
        return carry

    lax.fori_loop(0, DMA_ROWS, drain, 0)


def _collect(pad_start, e_flat, r_flat, y_sorted, n_tokens):
    nsteps = e_flat.shape[0]
    d = y_sorted.shape[1]
    smem = pl.BlockSpec((1, 1, DMA_ROWS), lambda i, st: (i, 0, 0), memory_space=pltpu.SMEM)
    any_spec = pl.BlockSpec(memory_space=pl.ANY)
    return pl.pallas_call(
        functools.partial(_collect_kernel, n_tokens=n_tokens),
        grid_spec=pltpu.PrefetchScalarGridSpec(
            num_scalar_prefetch=1, grid=(nsteps,),
            in_specs=[smem, smem, any_spec],
            out_specs=any_spec,
            scratch_shapes=[pltpu.SemaphoreType.DMA(())]),
        out_shape=jax.ShapeDtypeStruct((2 * n_tokens, d), F32),
        compiler_params=_cparams(("arbitrary",)),
        name="collect",
    )(pad_start, e_flat, r_flat, y_sorted)


def _expert_kernel(be_ref, nused_ref, x_ref, wg_ref, wu_ref, wd_ref, y_ref):
    del be_ref

    @pl.when(pl.program_id(0) < nused_ref[0])
    def _():
        xb = x_ref[...].astype(BF16)
        gate = jnp.dot(xb, wg_ref[0], preferred_element_type=F32)
        up = jnp.dot(xb, wu_ref[0], preferred_element_type=F32)
        act = gate / (1.0 + jnp.exp(-gate)) * up
        y_ref[...] = jnp.dot(act.astype(BF16), wd_ref[0], preferred_element_type=F32)

    @pl.when(pl.program_id(0) >= nused_ref[0])
    def _():
        y_ref[...] = jnp.zeros_like(y_ref)


def _experts(block_expert, n_used, buf, wg, wu, wd):
    cap, d = buf.shape
    f = wg.shape[2]
    bm = EXPERT_ROWS
    return pl.pallas_call(
        _expert_kernel,
        grid_spec=pltpu.PrefetchScalarGridSpec(
            num_scalar_prefetch=2, grid=(cap // bm,),
            in_specs=[pl.BlockSpec((bm, d), lambda i, be, nu: (i, 0)),
                      pl.BlockSpec((1, d, f), lambda i, be, nu: (be[i], 0, 0)),
                      pl.BlockSpec((1, d, f), lambda i, be, nu: (be[i], 0, 0)),
                      pl.BlockSpec((1, f, d), lambda i, be, nu: (be[i], 0, 0))],
            out_specs=pl.BlockSpec((bm, d), lambda i, be, nu: (i, 0))),
        out_shape=jax.ShapeDtypeStruct((cap, d), F32),
        compiler_params=_cparams(("arbitrary",)),
        name="experts",
    )(block_expert, n_used, buf, wg, wu, wd)


def _combine_kernel(x1_ref, y1_ref, y2_ref, route_ref, gate_ref, g_ref, o_ref):
    tm = POST_ROWS
    lane = lax.broadcasted_iota(jnp.int32, (tm, LANES), 1)
    route = route_ref[0]
    w1 = jnp.sum(jnp.where(lane == 4, route, 0.0), axis=-1, keepdims=True)
    w2 = jnp.sum(jnp.where(lane == 5, route, 0.0), axis=-1, keepdims=True)
    y = y1_ref[0] * w1 + y2_ref[0] * w2
    o_ref[0] = _rms(x1_ref[0] + gate_ref[0] * y, g_ref[...])


def _combine(x1, y_tok, route, gate, g_final):
    b, s, d = x1.shape
    tm = POST_ROWS
    nt = s // tm
    y3 = y_tok.reshape(2, b * s, d)
    return pl.pallas_call(
        _combine_kernel,
        grid=(b, nt),
        in_specs=[pl.BlockSpec((1, tm, d), lambda bi, i: (bi, i, 0)),
                  pl.BlockSpec((1, tm, d), lambda bi, i: (0, bi * nt + i, 0)),
                  pl.BlockSpec((1, tm, d), lambda bi, i: (1, bi * nt + i, 0)),
                  pl.BlockSpec((1, tm, LANES), lambda bi, i: (bi, i, 0)),
                  pl.BlockSpec((1, 1, d), lambda bi, i: (bi, 0, 0)),
                  pl.BlockSpec((1, d), lambda bi, i: (0, 0))],
        out_specs=pl.BlockSpec((1, tm, d), lambda bi, i: (bi, i, 0)),
        out_shape=jax.ShapeDtypeStruct((b, s, d), F32),
        compiler_params=_cparams(("arbitrary", "arbitrary")),
        name="combine",
    )(x1, y3, y3, route, gate, g_final.reshape(1, d))


def kernel(x, c, w_ada, b_ada, g_mix, w_in, g_dil_out, g_sb_out, w_out, g_ffn,
           w_group, w_expert, w_gate, w_up, w_down, g_final):
    b, s, d = x.shape
    depth = w_ada.shape[0]
    assert s % DIL_UNIT == 0 and d == D_DIL + D_SB and (2 * b * s) % DMA_ROWS == 0
    assert depth == 1, "the final rmsnorm is fused into the last layer's combine step"
    n = b * s
    bias = jnp.asarray(_dilated_bias())
    for layer in range(depth):
        mod = _ada(c, w_ada[layer], b_ada[layer])
        shift_mix, scale_mix, gate_mix, shift_ffn, scale_ffn, gate_ffn = (
            m.reshape(b, 1, d) for m in jnp.split(mod, 6, axis=-1))

        qkv_d, q_s, k_s, v_s = _premix(x, shift_mix, scale_mix, g_mix[layer], w_in[layer].astype(BF16))
        o_dil = _dilated(qkv_d, bias)
        o_sb = _stick(q_s, k_s, v_s)

        w_router = jnp.concatenate(
            [w_group[layer], w_expert[layer],
             jnp.zeros((d, LANES - N_GROUPS - N_EXPERTS), F32)], axis=1)
        x1, h2, route, counts = _postmix(
            x, o_dil, o_sb, g_dil_out[layer], g_sb_out[layer], w_out[layer].astype(BF16),
            gate_mix, shift_ffn, scale_ffn, g_ffn[layer], w_router)

        bm = EXPERT_ROWS
        cnt = counts[0, ROUTE_LANE0:ROUTE_LANE0 + N_EXPERTS].astype(jnp.int32)
        padded = (cnt + bm - 1) // bm * bm
        pad_end = jnp.cumsum(padded)
        pad_start = (pad_end - padded).astype(jnp.int32)
        cap = 2 * n + N_EXPERTS * bm
        n_blocks = cap // bm
        block_expert = jnp.minimum(
            jnp.searchsorted(pad_end, jnp.arange(n_blocks) * bm, side="right"), N_EXPERTS - 1).astype(jnp.int32)
        n_used = (pad_end[-1:] // bm).astype(jnp.int32)

        route2 = route.reshape(n, LANES)
        e_flat = route2[:, 0:2].astype(jnp.int32).reshape(-1, 1, DMA_ROWS)
        r_flat = route2[:, 2:4].astype(jnp.int32).reshape(-1, 1, DMA_ROWS)

        buf = _dispatch(pad_start, e_flat, r_flat, h2.reshape(n, d), jnp.zeros((cap, d), F32))
        y_sorted = _experts(block_expert, n_used, buf, w_gate[layer].astype(BF16),
                            w_up[layer].astype(BF16), w_down[layer].astype(BF16))
        y_tok = _collect(pad_start, e_flat, r_flat, y_sorted, n)
        x = _combine(x1, y_tok, route, gate_ffn, g_final)
    return x
```

```python
import functools

import numpy as np
import jax
import jax.numpy as jnp
from jax import lax
from jax.experimental import pallas as pl
from jax.experimental.pallas import tpu as pltpu

HEAD_DIM = 64
N_HEADS_DIL = 8
N_HEADS_SB = 8
D_DIL = N_HEADS_DIL * HEAD_DIM
D_SB = N_HEADS_SB * HEAD_DIM
DILATION_PATTERNS = ((128, 1), (512, 4), (2048, 16))
N_GROUPS = 4
EXPERTS_PER_GROUP = 8
N_EXPERTS = N_GROUPS * EXPERTS_PER_GROUP
NORM_EPS = 1e-6

LANES = 128
SUBLANES = 8
DIL_STEPS = 128
DIL_UNIT = 2048
SB_BLOCK = 256
PRE_ROWS = 512
POST_ROWS = 512
LOCAL_ROWS = 2 * POST_ROWS + 256
EXPERT_ROWS = 512
ROUTE_LANE0 = N_GROUPS
VMEM_LIMIT = 56 * 1024 * 1024

F32 = jnp.float32
BF16 = jnp.bfloat16
NEG_INF = float("-inf")


def _cparams(sem):
    return pltpu.CompilerParams(dimension_semantics=sem, vmem_limit_bytes=VMEM_LIMIT)


def _rms(v, g):
    return v * lax.rsqrt(jnp.mean(v * v, axis=-1, keepdims=True) + NORM_EPS) * g


def _split3(v):
    hi = v.astype(BF16)
    r = v - hi.astype(F32)
    mid = r.astype(BF16)
    lo = (r - mid.astype(F32)).astype(BF16)
    return hi, mid, lo


def _ada_kernel(c_ref, w_ref, b_ref, o_ref):
    c = c_ref[...]
    cond = c / (1.0 + jnp.exp(-c))
    o_ref[...] = jnp.dot(cond, w_ref[...], precision=lax.Precision.HIGHEST,
                         preferred_element_type=F32) + b_ref[...]


def _ada(c, w_ada, b_ada):
    b, d = c.shape
    n = w_ada.shape[1]
    return pl.pallas_call(
        _ada_kernel,
        grid=(n // d,),
        in_specs=[pl.BlockSpec((b, d), lambda j: (0, 0)),
                  pl.BlockSpec((d, d), lambda j: (0, j)),
                  pl.BlockSpec((1, d), lambda j: (0, j))],
        out_specs=pl.BlockSpec((b, d), lambda j: (0, j)),
        out_shape=jax.ShapeDtypeStruct((b, n), F32),
        compiler_params=_cparams(("arbitrary",)),
        name="ada",
    )(c, w_ada, b_ada.reshape(1, n))


def _premix_kernel(x_ref, shift_ref, scale_ref, g_ref, w_ref, qkvd_ref, qs_ref, ks_ref, vs_ref):
    h = _rms(x_ref[0], g_ref[...]) * (1.0 + scale_ref[0]) + shift_ref[0]
    hb = h.astype(BF16)
    scale = HEAD_DIM ** -0.5
    for j in range(6):
        r = jnp.dot(hb, w_ref[:, j * 512:(j + 1) * 512], preferred_element_type=F32)
        if j == 0:
            qkvd_ref[0, :, 0:512] = r * scale
        elif j < 3:
            qkvd_ref[0, :, j * 512:(j + 1) * 512] = r
        elif j == 3:
            qs_ref[0] = (r * scale).astype(BF16)
        elif j == 4:
            ks_ref[0] = r.astype(BF16)
        else:
            vs_ref[0] = r.astype(BF16)


def _premix(x, shift, scale, g_mix, w_in_bf16):
    b, s, d = x.shape
    tm = PRE_ROWS
    mod_spec = pl.BlockSpec((1, 1, d), lambda bi, i: (bi, 0, 0))
    sb_spec = pl.BlockSpec((1, tm, D_SB), lambda bi, i: (bi, i, 0))
    return pl.pallas_call(
        _premix_kernel,
        grid=(b, s // tm),
        in_specs=[pl.BlockSpec((1, tm, d), lambda bi, i: (bi, i, 0)),
                  mod_spec, mod_spec,
                  pl.BlockSpec((1, d), lambda bi, i: (0, 0)),
                  pl.BlockSpec((d, 3 * (D_DIL + D_SB)), lambda bi, i: (0, 0))],
        out_specs=[pl.BlockSpec((1, tm, 3 * D_DIL), lambda bi, i: (bi, i, 0)),
                   sb_spec, sb_spec, sb_spec],
        out_shape=[jax.ShapeDtypeStruct((b, s, 3 * D_DIL), F32),
                   jax.ShapeDtypeStruct((b, s, D_SB), BF16),
                   jax.ShapeDtypeStruct((b, s, D_SB), BF16),
                   jax.ShapeDtypeStruct((b, s, D_SB), BF16)],
        compiler_params=_cparams(("arbitrary", "arbitrary")),
        name="premix",
    )(x, shift, scale, g_mix.reshape(1, d), w_in_bf16)


def _dilated_bias():
    n = DIL_STEPS
    slopes = np.array([2.0 ** (-8.0 * (i + 1) / N_HEADS_DIL) for i in range(N_HEADS_DIL)], dtype=np.float32)
    steps = np.arange(n)[:, None] + n - np.arange(2 * n)[None, :]
    valid = (steps >= 0) & (steps <= n)
    out = []
    for _, dilation in DILATION_PATTERNS:
        bias = -slopes[:, None, None] * (steps * dilation).astype(np.float32)[None]
        out.append(np.where(valid[None], bias, -np.inf).astype(np.float32))
    return np.stack(out)


def _dil_kernel(q_ref, kc_ref, kp_ref, vc_ref, vp_ref, bias_ref, o_ref,
                kext, vext, u_scr, m_scr, l_scr):
    n = DIL_STEPS
    g = pl.program_id(1)
    kext[0:DIL_UNIT, :] = kp_ref[0]
    kext[DIL_UNIT:2 * DIL_UNIT, :] = kc_ref[0]
    vext[0:DIL_UNIT, :] = vp_ref[0]
    vext[DIL_UNIT:2 * DIL_UNIT, :] = vc_ref[0]
    lane = lax.broadcasted_iota(jnp.int32, (n, LANES), 1)
    head0 = lane < HEAD_DIM
    col = lax.broadcasted_iota(jnp.int32, (n, 2 * n), 1)

    for p, (_, dil) in enumerate(DILATION_PATTERNS):
        unit = n * dil

        def tile(ti, carry, p=p, dil=dil, unit=unit):
            j = ti // dil
            r = ti % dil
            qstart = j * unit + r
            kstart = DIL_UNIT + qstart - unit
            if dil == 1:
                q = q_ref[0, pl.ds(qstart, n), :]
                kk = kext[pl.ds(kstart, 2 * n), :]
                vv = vext[pl.ds(kstart, 2 * n), :]
            else:
                q = q_ref[0, pl.ds(qstart, n, stride=dil), :]
                kk = kext[pl.ds(kstart, 2 * n, stride=dil), :]
                vv = vext[pl.ds(kstart, 2 * n, stride=dil), :]
            kk = kk.astype(BF16)
            vv = vv.astype(BF16)
            dead_cols = jnp.where(jnp.logical_and(g == 0, j == 0), n, 0)
            qh = [jnp.where(head0 if h == 0 else jnp.logical_not(head0), q, 0.0).astype(BF16)
                  for h in range(2)]
            ss = [lax.dot_general(qh[h], kk, (((1,), (1,)), ((), ())), preferred_element_type=F32)
                  for h in range(2)]
            ms, ls, pes = [], [], []
            for h in range(2):
                logits = jnp.where(col < dead_cols, NEG_INF, ss[h] + bias_ref[p, h])
                m = jnp.max(logits, axis=-1, keepdims=True)
                pe = jnp.exp(logits - m)
                ls.append(jnp.sum(pe, axis=-1, keepdims=True))
                ms.append(m)
                pes.append(pe.astype(BF16))
            us = [jnp.dot(pes[h], vv, preferred_element_type=F32) for h in range(2)]
            u = jnp.where(head0, us[0], us[1])
            m = jnp.where(head0, ms[0], ms[1])
            l = jnp.where(head0, ls[0], ls[1])
            if dil == 1:
                rows = pl.ds(qstart, n)
            else:
                rows = pl.ds(qstart, n, stride=dil)
            u_scr[p, rows, :] = u
            m_scr[p, rows, :] = m
            l_scr[p, rows, :] = l
            return carry

        lax.fori_loop(0, DIL_UNIT // n, tile, 0)

    def merge(i, carry):
        rows = pl.ds(pl.multiple_of(i * n, n), n)
        m0, m1, m2 = m_scr[0, rows, :], m_scr[1, rows, :], m_scr[2, rows, :]
        mx = jnp.maximum(jnp.maximum(m0, m1), m2)
        w0, w1, w2 = jnp.exp(m0 - mx), jnp.exp(m1 - mx), jnp.exp(m2 - mx)
        num = w0 * u_scr[0, rows, :] + w1 * u_scr[1, rows, :] + w2 * u_scr[2, rows, :]
        den = w0 * l_scr[0, rows, :] + w1 * l_scr[1, rows, :] + w2 * l_scr[2, rows, :]
        o_ref[0, rows, :] = num / den
        return carry

    lax.fori_loop(0, DIL_UNIT // n, merge, 0)


def _dilated(qkv_d, bias):
    b, s, _ = qkv_d.shape
    u = DIL_UNIT
    npair = D_DIL // LANES
    cur = lambda off: pl.BlockSpec((1, u, LANES), lambda bi, g, p: (bi, g, off + p))
    prev = lambda off: pl.BlockSpec((1, u, LANES), lambda bi, g, p: (bi, jnp.maximum(g - 1, 0), off + p))
    return pl.pallas_call(
        _dil_kernel,
        grid=(b, s // u, npair),
        in_specs=[cur(0), cur(npair), prev(npair), cur(2 * npair), prev(2 * npair),
                  pl.BlockSpec((3, 2, DIL_STEPS, 2 * DIL_STEPS), lambda bi, g, p: (0, p, 0, 0))],
        out_specs=pl.BlockSpec((1, u, LANES), lambda bi, g, p: (bi, g, p)),
        out_shape=jax.ShapeDtypeStruct((b, s, D_DIL), F32),
        scratch_shapes=[pltpu.VMEM((2 * u, LANES), F32), pltpu.VMEM((2 * u, LANES), F32),
                        pltpu.VMEM((3, u, LANES), F32), pltpu.VMEM((3, u, LANES), F32),
                        pltpu.VMEM((3, u, LANES), F32)],
        compiler_params=_cparams(("arbitrary", "arbitrary", "arbitrary")),
        name="dilated",
    )(qkv_d, qkv_d, qkv_d, qkv_d, qkv_d, bias)


def _stick_kernel(q_ref, k_ref, v_ref, tri_ref, o_ref):
    blk = SB_BLOCK
    qi = pl.program_id(2)
    lane = lax.broadcasted_iota(jnp.int32, (blk, LANES), 1)
    head0 = lane < HEAD_DIM
    q = q_ref[0]
    zero = jnp.zeros_like(q)
    qh = (jnp.where(head0, q, zero), jnp.where(head0, zero, q))
    tri2 = tri_ref[...]
    causal = (lax.broadcasted_iota(jnp.int32, (blk, blk), 1)
              < lax.broadcasted_iota(jnp.int32, (blk, blk), 0))

    def sweep(kb, state, diag):
        start = pl.multiple_of(kb * blk, blk)
        kblk = k_ref[0, pl.ds(start, blk), :]
        vblk = v_ref[0, pl.ds(start, blk), :]
        zs = [lax.dot_general(qh[h], kblk, (((1,), (1,)), ((), ())), preferred_element_type=F32)
              for h in range(2)]
        log_betas, log_keeps, splits = [], [], []
        for h in range(2):
            z = zs[h]
            nz = -z
            sp = jnp.log(1.0 + jnp.exp(jnp.minimum(z, nz)))
            log_betas.append(jnp.minimum(z, 0.0) - sp)
            log_keep = jnp.minimum(nz, 0.0) - sp
            if diag:
                log_keep = jnp.where(causal, log_keep, 0.0)
            hi = log_keep.astype(BF16)
            lo = (log_keep - hi.astype(F32)).astype(BF16)
            log_keeps.append(log_keep)
            splits.append(jnp.concatenate([hi, lo], axis=1))
        betweens = [jnp.dot(splits[h], tri2, preferred_element_type=F32) for h in range(2)]
        ws = []
        for h in range(2):
            w = jnp.exp(log_betas[h] + (betweens[h] + state[2 * h]))
            if diag:
                w = jnp.where(causal, w, 0.0)
            ws.append(w.astype(BF16))
        new = []
        for h in range(2):
            acc = state[2 * h + 1] + jnp.dot(ws[h], vblk, preferred_element_type=F32)
            carry = state[2 * h] + jnp.sum(log_keeps[h], axis=-1, keepdims=True)
            new += [carry, acc]
        return tuple(new)

    init = (jnp.zeros((blk, 1), F32), jnp.zeros((blk, LANES), F32)) * 2
    state = sweep(qi, init, True)
    state = lax.fori_loop(0, qi, lambda i, st: sweep(qi - 1 - i, st, False), state)
    o_ref[0] = jnp.where(head0, state[1], state[3])


def _stick(q_s, k_s, v_s):
    b, s, _ = q_s.shape
    blk = SB_BLOCK
    tri = np.tril(np.ones((blk, blk), np.float32), -1)
    tri2 = jnp.asarray(np.concatenate([tri, tri], axis=0), BF16)
    full = pl.BlockSpec((1, s, LANES), lambda bi, p, i: (bi, 0, p))
    return pl.pallas_call(
        _stick_kernel,
        grid=(b, D_SB // LANES, s // blk),
        in_specs=[pl.BlockSpec((1, blk, LANES), lambda bi, p, i: (bi, i, p)), full, full,
                  pl.BlockSpec((2 * blk, blk), lambda bi, p, i: (0, 0))],
        out_specs=pl.BlockSpec((1, blk, LANES), lambda bi, p, i: (bi, i, p)),
        out_shape=jax.ShapeDtypeStruct((b, s, D_SB), F32),
        compiler_params=_cparams(("arbitrary", "arbitrary", "arbitrary")),
        name="stick",
    )(q_s, k_s, v_s, tri2)


def _postmix_kernel(x_ref, od_ref, os_ref, gd_ref, gs_ref, wout_ref, gate_ref, shift_ref, scale_ref,
                    gffn_ref, wr_ref, tril_ref, triu_ref,
                    x1_ref, xs_ref, route_ref, cnt_ref, base_ref, carry_scr):
    tm = POST_ROWS
    lt = LOCAL_ROWS

    @pl.when(jnp.logical_and(pl.program_id(0) == 0, pl.program_id(1) == 0))
    def _():
        carry_scr[...] = jnp.zeros_like(carry_scr)

    mixed = jnp.concatenate([_rms(od_ref[0], gd_ref[...]), _rms(os_ref[0], gs_ref[...])], axis=-1)
    proj = jnp.dot(mixed.astype(BF16), wout_ref[...], preferred_element_type=F32)
    x1 = x_ref[0] + gate_ref[0] * proj
    x1_ref[0] = x1
    h2 = _rms(x1, gffn_ref[...]) * (1.0 + scale_ref[0]) + shift_ref[0]
    logits = jnp.dot(h2, wr_ref[...], precision=lax.Precision.HIGHEST, preferred_element_type=F32)

    lane = lax.broadcasted_iota(jnp.int32, (tm, LANES), 1)
    big = jnp.int32(LANES)
    lmax = lambda v: jnp.max(v, axis=-1, keepdims=True)
    lmin = lambda v: jnp.min(v, axis=-1, keepdims=True)
    lsum = lambda v: jnp.sum(v, axis=-1, keepdims=True)

    gmask = lane < N_GROUPS
    gl = jnp.where(gmask, logits, NEG_INF)
    gmx = lmax(gl)
    group = lmin(jnp.where(jnp.logical_and(gmask, gl == gmx), lane, big))
    group_gate = 1.0 / lsum(jnp.exp(gl - gmx))
    lo = ROUTE_LANE0 + group * EXPERTS_PER_GROUP
    emask = jnp.logical_and(lane >= lo, lane < lo + EXPERTS_PER_GROUP)
    el = jnp.where(emask, logits, NEG_INF)
    l1 = lmax(el)
    i1 = lmin(jnp.where(el == l1, lane, big))
    el2 = jnp.where(lane == i1, NEG_INF, el)
    l2 = lmax(el2)
    i2 = lmin(jnp.where(el2 == l2, lane, big))
    r = jnp.exp(l2 - l1)
    w1 = group_gate / (1.0 + r)
    w2 = group_gate * r / (1.0 + r)

    is1 = lane == i1
    is2 = lane == i2
    oh = jnp.where(is1, 1.0, jnp.where(is2, 1.0, 0.0))
    earlier = jnp.dot(tril_ref[...], oh.astype(BF16), preferred_element_type=F32)
    runs = jnp.floor((jnp.sum(oh, axis=0, keepdims=True) + (SUBLANES - 1.0)) * (1.0 / SUBLANES))
    run_off = jnp.dot(jnp.broadcast_to(runs, (SUBLANES, LANES)).astype(BF16), triu_ref[...],
                      preferred_element_type=F32)[0:1]
    pos = earlier + run_off * SUBLANES
    slot1 = lsum(jnp.where(is1, pos, 0.0))
    slot2 = lsum(jnp.where(is2, pos, 0.0))
    cnt = runs * SUBLANES
    cnt_ref[0] = cnt
    base_ref[0] = carry_scr[...]
    carry_scr[...] = carry_scr[...] + cnt

    route = jnp.where(lane == 0, slot1, jnp.where(lane == 1, slot2, 0.0))
    route_ref[0] = route

    def pieces(w):
        hi, mid, lw = _split3(w)
        return jnp.where(lane == 0, hi.astype(F32),
                         jnp.where(lane == 1, mid.astype(F32),
                                   jnp.where(lane == 2, lw.astype(F32), 0.0))).astype(BF16)

    route_t = route.T
    s1 = route_t[0:1, :].astype(jnp.int32)
    s2 = route_t[1:2, :].astype(jnp.int32)
    row = lax.broadcasted_iota(jnp.int32, (lt, tm), 0)
    p1 = jnp.where(row == s1, 1.0, 0.0)
    p2 = jnp.where(row == s2, 1.0, 0.0)
    p1b = p1.astype(BF16)
    p2b = p2.astype(BF16)
    xs_ref[:, 0:h2.shape[1]] = jnp.dot((p1 + p2).astype(BF16), h2.astype(BF16), preferred_element_type=F32)
    xs_ref[:, h2.shape[1]:] = (jnp.dot(p1b, pieces(w1), preferred_element_type=F32)
                               + jnp.dot(p2b, pieces(w2), preferred_element_type=F32))


def _postmix(x, o_dil, o_sb, g_dil, g_sb, w_out_bf16, gate, shift, scale, g_ffn, w_router):
    b, s, d = x.shape
    tm = POST_ROWS
    lt = LOCAL_ROWS
    nt = s // tm
    tril = jnp.asarray(np.tril(np.ones((tm, tm), np.float32), -1), BF16)
    triu = jnp.asarray(np.triu(np.ones((LANES, LANES), np.float32), 1), BF16)
    row = lambda w: pl.BlockSpec((1, tm, w), lambda bi, i: (bi, i, 0))
    vec = lambda w: pl.BlockSpec((1, w), lambda bi, i: (0, 0))
    mod_spec = pl.BlockSpec((1, 1, d), lambda bi, i: (bi, 0, 0))
    tile_vec = pl.BlockSpec((1, 1, LANES), lambda bi, i: (bi * nt + i, 0, 0))
    return pl.pallas_call(
        _postmix_kernel,
        grid=(b, nt),
        in_specs=[row(d), row(D_DIL), row(D_SB), vec(D_DIL), vec(D_SB),
                  pl.BlockSpec((d, d), lambda bi, i: (0, 0)),
                  mod_spec, mod_spec, mod_spec, vec(d),
                  pl.BlockSpec((d, LANES), lambda bi, i: (0, 0)),
                  pl.BlockSpec((tm, tm), lambda bi, i: (0, 0)),
                  pl.BlockSpec((LANES, LANES), lambda bi, i: (0, 0))],
        out_specs=[row(d),
                   pl.BlockSpec((lt, d + LANES), lambda bi, i: (bi * nt + i, 0)),
                   row(LANES), tile_vec, tile_vec],
        out_shape=[jax.ShapeDtypeStruct((b, s, d), F32),
                   jax.ShapeDtypeStruct((b * nt * lt, d + LANES), F32),
                   jax.ShapeDtypeStruct((b, s, LANES), F32),
                   jax.ShapeDtypeStruct((b * nt, 1, LANES), F32),
                   jax.ShapeDtypeStruct((b * nt, 1, LANES), F32)],
        scratch_shapes=[pltpu.VMEM((1, LANES), F32)],
        compiler_params=_cparams(("arbitrary", "arbitrary")),
        name="postmix",
    )(x, o_dil, o_sb, g_dil.reshape(1, -1), g_sb.reshape(1, -1), w_out_bf16, gate, shift, scale,
      g_ffn.reshape(1, d), w_router, tril, triu)


def _runs_kernel(start_ref, cnt_ref, base_ref, src_ref, dst_in_ref, dst_ref, sem, *, to_sorted):
    del dst_in_ref
    local0 = pl.program_id(0) * LOCAL_ROWS

    def each_piece(fn):
        def body(e, off):
            c = cnt_ref[0, 0, ROUTE_LANE0 + e]
            sorted0 = start_ref[e] + base_ref[0, 0, ROUTE_LANE0 + e]
            for k in range(3, 10):
                p = 1 << k

                @pl.when((c & p) != 0)
                def _(p=p):
                    done = c - (c & (2 * p - 1))
                    lrow = pl.multiple_of(local0 + off + done, SUBLANES)
                    srow = pl.multiple_of(sorted0 + done, SUBLANES)
                    if to_sorted:
                        fn(pltpu.make_async_copy(src_ref.at[pl.ds(lrow, p)], dst_ref.at[pl.ds(srow, p)], sem))
                    else:
                        fn(pltpu.make_async_copy(src_ref.at[pl.ds(srow, p)], dst_ref.at[pl.ds(lrow, p)], sem))
            return off + c

        lax.fori_loop(0, N_EXPERTS, body, 0)

    each_piece(lambda cp: cp.start())
    each_piece(lambda cp: cp.wait())


def _move_runs(pad_start, cnt, base, src, dst_zeros, to_sorted, name):
    ntiles = cnt.shape[0]
    smem = pl.BlockSpec((1, 1, LANES), lambda i, st: (i, 0, 0), memory_space=pltpu.SMEM)
    any_spec = pl.BlockSpec(memory_space=pl.ANY)
    return pl.pallas_call(
        functools.partial(_runs_kernel, to_sorted=to_sorted),
        grid_spec=pltpu.PrefetchScalarGridSpec(
            num_scalar_prefetch=1, grid=(ntiles,),
            in_specs=[smem, smem, any_spec, any_spec],
            out_specs=any_spec,
            scratch_shapes=[pltpu.SemaphoreType.DMA(())]),
        out_shape=jax.ShapeDtypeStruct(dst_zeros.shape, dst_zeros.dtype),
        input_output_aliases={4: 0},
        compiler_params=_cparams(("arbitrary",)),
        name=name,
    )(pad_start, cnt, base, src, dst_zeros)


def _expert_kernel(be_ref, nused_ref, x_ref, wg_ref, wu_ref, wd_ref, y_ref):
    del be_ref
    d = y_ref.shape[1]

    @pl.when(pl.program_id(0) < nused_ref[0])
    def _():
        xb = x_ref[:, 0:d].astype(BF16)
        weight = jnp.sum(x_ref[:, d:], axis=-1, keepdims=True)
        gate = jnp.dot(xb, wg_ref[0], preferred_element_type=F32)
        up = jnp.dot(xb, wu_ref[0], preferred_element_type=F32)
        act = gate / (1.0 + jnp.exp(-gate)) * up
        y_ref[...] = jnp.dot(act.astype(BF16), wd_ref[0], preferred_element_type=F32) * weight

    @pl.when(pl.program_id(0) >= nused_ref[0])
    def _():
        y_ref[...] = jnp.zeros_like(y_ref)


def _experts(block_expert, n_used, buf, wg, wu, wd):
    cap, dw = buf.shape
    d, f = wg.shape[1], wg.shape[2]
    bm = EXPERT_ROWS
    return pl.pallas_call(
        _expert_kernel,
        grid_spec=pltpu.PrefetchScalarGridSpec(
            num_scalar_prefetch=2, grid=(cap // bm,),
            in_specs=[pl.BlockSpec((bm, dw), lambda i, be, nu: (i, 0)),
                      pl.BlockSpec((1, d, f), lambda i, be, nu: (be[i], 0, 0)),
                      pl.BlockSpec((1, d, f), lambda i, be, nu: (be[i], 0, 0)),
                      pl.BlockSpec((1, f, d), lambda i, be, nu: (be[i], 0, 0))],
            out_specs=pl.BlockSpec((bm, d), lambda i, be, nu: (i, 0))),
        out_shape=jax.ShapeDtypeStruct((cap, d), F32),
        compiler_params=_cparams(("arbitrary",)),
        name="experts",
    )(block_expert, n_used, buf, wg, wu, wd)


def _combine_kernel(x1_ref, y_ref, route_ref, gate_ref, g_ref, o_ref):
    tm = POST_ROWS
    lt = LOCAL_ROWS
    lane = lax.broadcasted_iota(jnp.int32, (tm, LANES), 1)
    route = route_ref[0]
    s1 = jnp.sum(jnp.where(lane == 0, route, 0.0), axis=-1, keepdims=True).astype(jnp.int32)
    s2 = jnp.sum(jnp.where(lane == 1, route, 0.0), axis=-1, keepdims=True).astype(jnp.int32)
    col = lax.broadcasted_iota(jnp.int32, (tm, lt), 1)
    pick = jnp.where(col == s1, 1.0, jnp.where(col == s2, 1.0, 0.0)).astype(BF16)
    hi, mid, lo = _split3(y_ref[...])
    y = jnp.dot(jnp.concatenate([pick, pick, pick], axis=1), jnp.concatenate([hi, mid, lo], axis=0),
                preferred_element_type=F32)
    o_ref[0] = _rms(x1_ref[0] + gate_ref[0] * y, g_ref[...])


def _combine(x1, y_local, route, gate, g_final):
    b, s, d = x1.shape
    tm = POST_ROWS
    lt = LOCAL_ROWS
    nt = s // tm
    return pl.pallas_call(
        _combine_kernel,
        grid=(b, nt),
        in_specs=[pl.BlockSpec((1, tm, d), lambda bi, i: (bi, i, 0)),
                  pl.BlockSpec((lt, d), lambda bi, i: (bi * nt + i, 0)),
                  pl.BlockSpec((1, tm, LANES), lambda bi, i: (bi, i, 0)),
                  pl.BlockSpec((1, 1, d), lambda bi, i: (bi, 0, 0)),
                  pl.BlockSpec((1, d), lambda bi, i: (0, 0))],
        out_specs=pl.BlockSpec((1, tm, d), lambda bi, i: (bi, i, 0)),
        out_shape=jax.ShapeDtypeStruct((b, s, d), F32),
        compiler_params=_cparams(("arbitrary", "arbitrary")),
        name="combine",
    )(x1, y_local, route, gate, g_final.reshape(1, d))


def kernel(x, c, w_ada, b_ada, g_mix, w_in, g_dil_out, g_sb_out, w_out, g_ffn,
           w_group, w_expert, w_gate, w_up, w_down, g_final):
    b, s, d = x.shape
    depth = w_ada.shape[0]
    assert s % DIL_UNIT == 0 and d == D_DIL + D_SB
    assert depth == 1, "the final rmsnorm is fused into the last layer's combine step"
    n = b * s
    ntiles = n // POST_ROWS
    bias = jnp.asarray(_dilated_bias())
    for layer in range(depth):
        mod = _ada(c, w_ada[layer], b_ada[layer])
        shift_mix, scale_mix, gate_mix, shift_ffn, scale_ffn, gate_ffn = (
            m.reshape(b, 1, d) for m in jnp.split(mod, 6, axis=-1))

        qkv_d, q_s, k_s, v_s = _premix(x, shift_mix, scale_mix, g_mix[layer], w_in[layer].astype(BF16))
        o_dil = _dilated(qkv_d, bias)
        o_sb = _stick(q_s, k_s, v_s)

        w_router = jnp.concatenate(
            [w_group[layer], w_expert[layer],
             jnp.zeros((d, LANES - N_GROUPS - N_EXPERTS), F32)], axis=1)
        x1, xs_local, route, cnt, base = _postmix(
            x, o_dil, o_sb, g_dil_out[layer], g_sb_out[layer], w_out[layer].astype(BF16),
            gate_mix, shift_ffn, scale_ffn, g_ffn[layer], w_router)

        bm = EXPERT_ROWS
        cnt = cnt.astype(jnp.int32)
        base = base.astype(jnp.int32)
        total = (base[-1, 0] + cnt[-1, 0])[ROUTE_LANE0:ROUTE_LANE0 + N_EXPERTS]
        padded = (total + bm - 1) // bm * bm
        pad_end = jnp.cumsum(padded)
        pad_start = (pad_end - padded).astype(jnp.int32)
        cap = -(-(2 * n + (SUBLANES - 1) * N_EXPERTS * ntiles) // bm) * bm + N_EXPERTS * bm
        n_blocks = cap // bm
        block_expert = jnp.minimum(
            jnp.sum(pad_end[None, :] <= (jnp.arange(n_blocks) * bm)[:, None], axis=1),
            N_EXPERTS - 1).astype(jnp.int32)
        n_used = (pad_end[-1:] // bm).astype(jnp.int32)

        buf = _move_runs(pad_start, cnt, base, xs_local, jnp.zeros((cap, d + LANES), F32), True, "dispatch")
        y_sorted = _experts(block_expert, n_used, buf, w_gate[layer].astype(BF16),
                            w_up[layer].astype(BF16), w_down[layer].astype(BF16))
        y_local = _move_runs(pad_start, cnt, base, y_sorted,
                             jnp.zeros((ntiles * LOCAL_ROWS, d), F32), False, "collect")
        x = _combine(x1, y_local, route, gate_ffn, g_final)
    return x
```

```python
import functools

import numpy as np
import jax
import jax.numpy as jnp
from jax import lax
from jax.experimental import pallas as pl
from jax.experimental.pallas import tpu as pltpu

HEAD_DIM = 64
N_HEADS_DIL = 8
N_HEADS_SB = 8
D_DIL = N_HEADS_DIL * HEAD_DIM
D_SB = N_HEADS_SB * HEAD_DIM
DILATION_PATTERNS = ((128, 1), (512, 4), (2048, 16))
N_GROUPS = 4
EXPERTS_PER_GROUP = 8
N_EXPERTS = N_GROUPS * EXPERTS_PER_GROUP
NORM_EPS = 1e-6

LANES = 128
SUBLANES = 8
DIL_STEPS = 128
DIL_UNIT = 2048
SB_BLOCK = 256
PRE_ROWS = 512
POST_ROWS = 512
LOCAL_ROWS = 2 * POST_ROWS + 256
EXPERT_ROWS = 512
ROUTE_LANE0 = N_GROUPS
VMEM_LIMIT = 56 * 1024 * 1024

F32 = jnp.float32
BF16 = jnp.bfloat16
NEG_INF = float("-inf")


def _cparams(sem):
    return pltpu.CompilerParams(dimension_semantics=sem, vmem_limit_bytes=VMEM_LIMIT)


def _rms(v, g):
    return v * lax.rsqrt(jnp.mean(v * v, axis=-1, keepdims=True) + NORM_EPS) * g


def _split3(v):
    hi = v.astype(BF16)
    r = v - hi.astype(F32)
    mid = r.astype(BF16)
    lo = (r - mid.astype(F32)).astype(BF16)
    return hi, mid, lo


def _ada_kernel(c_ref, w_ref, b_ref, o_ref):
    c = c_ref[...]
    cond = c / (1.0 + jnp.exp(-c))
    o_ref[...] = jnp.dot(cond, w_ref[...], precision=lax.Precision.HIGHEST,
                         preferred_element_type=F32) + b_ref[...]


def _ada(c, w_ada, b_ada):
    b, d = c.shape
    n = w_ada.shape[1]
    return pl.pallas_call(
        _ada_kernel,
        grid=(n // d,),
        in_specs=[pl.BlockSpec((b, d), lambda j: (0, 0)),
                  pl.BlockSpec((d, d), lambda j: (0, j)),
                  pl.BlockSpec((1, d), lambda j: (0, j))],
        out_specs=pl.BlockSpec((b, d), lambda j: (0, j)),
        out_shape=jax.ShapeDtypeStruct((b, n), F32),
        compiler_params=_cparams(("arbitrary",)),
        name="ada",
    )(c, w_ada, b_ada.reshape(1, n))


def _premix_kernel(x_ref, shift_ref, scale_ref, g_ref, w_ref, qkvd_ref, qs_ref, ks_ref, vs_ref):
    h = _rms(x_ref[0], g_ref[...]) * (1.0 + scale_ref[0]) + shift_ref[0]
    hb = h.astype(BF16)
    scale = HEAD_DIM ** -0.5
    for j in range(6):
        r = jnp.dot(hb, w_ref[:, j * 512:(j + 1) * 512], preferred_element_type=F32)
        if j == 0:
            qkvd_ref[0, :, 0:512] = r * scale
        elif j < 3:
            qkvd_ref[0, :, j * 512:(j + 1) * 512] = r
        elif j == 3:
            qs_ref[0] = (r * scale).astype(BF16)
        elif j == 4:
            ks_ref[0] = r.astype(BF16)
        else:
            vs_ref[0] = r.astype(BF16)


def _premix(x, shift, scale, g_mix, w_in_bf16):
    b, s, d = x.shape
    tm = PRE_ROWS
    mod_spec = pl.BlockSpec((1, 1, d), lambda bi, i: (bi, 0, 0))
    sb_spec = pl.BlockSpec((1, tm, D_SB), lambda bi, i: (bi, i, 0))
    return pl.pallas_call(
        _premix_kernel,
        grid=(b, s // tm),
        in_specs=[pl.BlockSpec((1, tm, d), lambda bi, i: (bi, i, 0)),
                  mod_spec, mod_spec,
                  pl.BlockSpec((1, d), lambda bi, i: (0, 0)),
                  pl.BlockSpec((d, 3 * (D_DIL + D_SB)), lambda bi, i: (0, 0))],
        out_specs=[pl.BlockSpec((1, tm, 3 * D_DIL), lambda bi, i: (bi, i, 0)),
                   sb_spec, sb_spec, sb_spec],
        out_shape=[jax.ShapeDtypeStruct((b, s, 3 * D_DIL), F32),
                   jax.ShapeDtypeStruct((b, s, D_SB), BF16),
                   jax.ShapeDtypeStruct((b, s, D_SB), BF16),
                   jax.ShapeDtypeStruct((b, s, D_SB), BF16)],
        compiler_params=_cparams(("arbitrary", "arbitrary")),
        name="premix",
    )(x, shift, scale, g_mix.reshape(1, d), w_in_bf16)


def _dilated_bias():
    n = DIL_STEPS
    slopes = np.array([2.0 ** (-8.0 * (i + 1) / N_HEADS_DIL) for i in range(N_HEADS_DIL)], dtype=np.float32)
    steps = np.arange(n)[:, None] + n - np.arange(2 * n)[None, :]
    valid = (steps >= 0) & (steps <= n)
    out = []
    for _, dilation in DILATION_PATTERNS:
        bias = -slopes[:, None, None] * (steps * dilation).astype(np.float32)[None]
        out.append(np.where(valid[None], bias, -np.inf).astype(np.float32))
    return np.stack(out)


def _dil_kernel(q_ref, kc_ref, kp_ref, vc_ref, vp_ref, bias_ref, o_ref,
                kext, vext, u_scr, m_scr, l_scr):
    n = DIL_STEPS
    g = pl.program_id(1)
    kext[0:DIL_UNIT, :] = kp_ref[0]
    kext[DIL_UNIT:2 * DIL_UNIT, :] = kc_ref[0]
    vext[0:DIL_UNIT, :] = vp_ref[0]
    vext[DIL_UNIT:2 * DIL_UNIT, :] = vc_ref[0]
    lane = lax.broadcasted_iota(jnp.int32, (n, LANES), 1)
    head0 = lane < HEAD_DIM
    col = lax.broadcasted_iota(jnp.int32, (n, 2 * n), 1)

    for p, (_, dil) in enumerate(DILATION_PATTERNS):
        unit = n * dil

        def tile(ti, carry, p=p, dil=dil, unit=unit):
            j = ti // dil
            r = ti % dil
            qstart = j * unit + r
            kstart = DIL_UNIT + qstart - unit
            if dil == 1:
                q = q_ref[0, pl.ds(qstart, n), :]
                kk = kext[pl.ds(kstart, 2 * n), :]
                vv = vext[pl.ds(kstart, 2 * n), :]
            else:
                q = q_ref[0, pl.ds(qstart, n, stride=dil), :]
                kk = kext[pl.ds(kstart, 2 * n, stride=dil), :]
                vv = vext[pl.ds(kstart, 2 * n, stride=dil), :]
            kk = kk.astype(BF16)
            vv = vv.astype(BF16)
            dead_cols = jnp.where(jnp.logical_and(g == 0, j == 0), n, 0)
            qh = [jnp.where(head0 if h == 0 else jnp.logical_not(head0), q, 0.0).astype(BF16)
                  for h in range(2)]
            ss = [lax.dot_general(qh[h], kk, (((1,), (1,)), ((), ())), preferred_element_type=F32)
                  for h in range(2)]
            ms, ls, pes = [], [], []
            for h in range(2):
                logits = jnp.where(col < dead_cols, NEG_INF, ss[h] + bias_ref[p, h])
                m = jnp.max(logits, axis=-1, keepdims=True)
                pe = jnp.exp(logits - m)
                ls.append(jnp.sum(pe, axis=-1, keepdims=True))
                ms.append(m)
                pes.append(pe.astype(BF16))
            us = [jnp.dot(pes[h], vv, preferred_element_type=F32) for h in range(2)]
            u = jnp.where(head0, us[0], us[1])
            m = jnp.where(head0, ms[0], ms[1])
            l = jnp.where(head0, ls[0], ls[1])
            if dil == 1:
                rows = pl.ds(qstart, n)
            else:
                rows = pl.ds(qstart, n, stride=dil)
            u_scr[p, rows, :] = u
            m_scr[p, rows, :] = m
            l_scr[p, rows, :] = l
            return carry

        lax.fori_loop(0, DIL_UNIT // n, tile, 0)

    def merge(i, carry):
        rows = pl.ds(pl.multiple_of(i * n, n), n)
        m0, m1, m2 = m_scr[0, rows, :], m_scr[1, rows, :], m_scr[2, rows, :]
        mx = jnp.maximum(jnp.maximum(m0, m1), m2)
        w0, w1, w2 = jnp.exp(m0 - mx), jnp.exp(m1 - mx), jnp.exp(m2 - mx)
        num = w0 * u_scr[0, rows, :] + w1 * u_scr[1, rows, :] + w2 * u_scr[2, rows, :]
        den = w0 * l_scr[0, rows, :] + w1 * l_scr[1, rows, :] + w2 * l_scr[2, rows, :]
        o_ref[0, rows, :] = num / den
        return carry

    lax.fori_loop(0, DIL_UNIT // n, merge, 0)


def _dilated(qkv_d, bias):
    b, s, _ = qkv_d.shape
    u = DIL_UNIT
    npair = D_DIL // LANES
    cur = lambda off: pl.BlockSpec((1, u, LANES), lambda bi, g, p: (bi, g, off + p))
    prev = lambda off: pl.BlockSpec((1, u, LANES), lambda bi, g, p: (bi, jnp.maximum(g - 1, 0), off + p))
    return pl.pallas_call(
        _dil_kernel,
        grid=(b, s // u, npair),
        in_specs=[cur(0), cur(npair), prev(npair), cur(2 * npair), prev(2 * npair),
                  pl.BlockSpec((3, 2, DIL_STEPS, 2 * DIL_STEPS), lambda bi, g, p: (0, p, 0, 0))],
        out_specs=pl.BlockSpec((1, u, LANES), lambda bi, g, p: (bi, g, p)),
        out_shape=jax.ShapeDtypeStruct((b, s, D_DIL), F32),
        scratch_shapes=[pltpu.VMEM((2 * u, LANES), F32), pltpu.VMEM((2 * u, LANES), F32),
                        pltpu.VMEM((3, u, LANES), F32), pltpu.VMEM((3, u, LANES), F32),
                        pltpu.VMEM((3, u, LANES), F32)],
        compiler_params=_cparams(("arbitrary", "arbitrary", "arbitrary")),
        name="dilated",
    )(qkv_d, qkv_d, qkv_d, qkv_d, qkv_d, bias)


def _stick_kernel(q_ref, k_ref, v_ref, tri_ref, o_ref):
    blk = SB_BLOCK
    qi = pl.program_id(2)
    lane = lax.broadcasted_iota(jnp.int32, (blk, LANES), 1)
    head0 = lane < HEAD_DIM
    q = q_ref[0]
    zero = jnp.zeros_like(q)
    qh = (jnp.where(head0, q, zero), jnp.where(head0, zero, q))
    tri2 = tri_ref[...]
    causal = (lax.broadcasted_iota(jnp.int32, (blk, blk), 1)
              < lax.broadcasted_iota(jnp.int32, (blk, blk), 0))

    def sweep(kb, state, diag):
        start = pl.multiple_of(kb * blk, blk)
        kblk = k_ref[0, pl.ds(start, blk), :]
        vblk = v_ref[0, pl.ds(start, blk), :]
        zs = [lax.dot_general(qh[h], kblk, (((1,), (1,)), ((), ())), preferred_element_type=F32)
              for h in range(2)]
        log_betas, log_keeps, splits = [], [], []
        for h in range(2):
            z = zs[h]
            nz = -z
            sp = jnp.log(1.0 + jnp.exp(jnp.minimum(z, nz)))
            log_betas.append(jnp.minimum(z, 0.0) - sp)
            log_keep = jnp.minimum(nz, 0.0) - sp
            if diag:
                log_keep = jnp.where(causal, log_keep, 0.0)
            hi = log_keep.astype(BF16)
            lo = (log_keep - hi.astype(F32)).astype(BF16)
            log_keeps.append(log_keep)
            splits.append(jnp.concatenate([hi, lo], axis=1))
        betweens = [jnp.dot(splits[h], tri2, preferred_element_type=F32) for h in range(2)]
        ws = []
        for h in range(2):
            w = jnp.exp(log_betas[h] + (betweens[h] + state[2 * h]))
            if diag:
                w = jnp.where(causal, w, 0.0)
            ws.append(w.astype(BF16))
        new = []
        for h in range(2):
            acc = state[2 * h + 1] + jnp.dot(ws[h], vblk, preferred_element_type=F32)
            carry = state[2 * h] + jnp.sum(log_keeps[h], axis=-1, keepdims=True)
            new += [carry, acc]
        return tuple(new)

    init = (jnp.zeros((blk, 1), F32), jnp.zeros((blk, LANES), F32)) * 2
    state = sweep(qi, init, True)
    state = lax.fori_loop(0, qi, lambda i, st: sweep(qi - 1 - i, st, False), state)
    o_ref[0] = jnp.where(head0, state[1], state[3])


def _stick(q_s, k_s, v_s):
    b, s, _ = q_s.shape
    blk = SB_BLOCK
    tri = np.tril(np.ones((blk, blk), np.float32), -1)
    tri2 = jnp.asarray(np.concatenate([tri, tri], axis=0), BF16)
    full = pl.BlockSpec((1, s, LANES), lambda bi, p, i: (bi, 0, p))
    return pl.pallas_call(
        _stick_kernel,
        grid=(b, D_SB // LANES, s // blk),
        in_specs=[pl.BlockSpec((1, blk, LANES), lambda bi, p, i: (bi, i, p)), full, full,
                  pl.BlockSpec((2 * blk, blk), lambda bi, p, i: (0, 0))],
        out_specs=pl.BlockSpec((1, blk, LANES), lambda bi, p, i: (bi, i, p)),
        out_shape=jax.ShapeDtypeStruct((b, s, D_SB), F32),
        compiler_params=_cparams(("arbitrary", "arbitrary", "arbitrary")),
        name="stick",
    )(q_s, k_s, v_s, tri2)


def _postmix_kernel(x_ref, od_ref, os_ref, gd_ref, gs_ref, wout_ref, gate_ref, shift_ref, scale_ref,
                    gffn_ref, wr_ref, tril_ref, triu_ref,
                    x1_ref, h2_ref, route_ref, cnt_ref, base_ref, carry_scr):
    tm = POST_ROWS

    @pl.when(jnp.logical_and(pl.program_id(0) == 0, pl.program_id(1) == 0))
    def _():
        carry_scr[...] = jnp.zeros_like(carry_scr)

    mixed = jnp.concatenate([_rms(od_ref[0], gd_ref[...]), _rms(os_ref[0], gs_ref[...])], axis=-1)
    proj = jnp.dot(mixed.astype(BF16), wout_ref[...], preferred_element_type=F32)
    x1 = x_ref[0] + gate_ref[0] * proj
    x1_ref[0] = x1
    h2 = _rms(x1, gffn_ref[...]) * (1.0 + scale_ref[0]) + shift_ref[0]
    logits = jnp.dot(h2, wr_ref[...], precision=lax.Precision.HIGHEST, preferred_element_type=F32)

    lane = lax.broadcasted_iota(jnp.int32, (tm, LANES), 1)
    big = jnp.int32(LANES)
    lmax = lambda v: jnp.max(v, axis=-1, keepdims=True)
    lmin = lambda v: jnp.min(v, axis=-1, keepdims=True)
    lsum = lambda v: jnp.sum(v, axis=-1, keepdims=True)

    gmask = lane < N_GROUPS
    gl = jnp.where(gmask, logits, NEG_INF)
    gmx = lmax(gl)
    group = lmin(jnp.where(jnp.logical_and(gmask, gl == gmx), lane, big))
    group_gate = 1.0 / lsum(jnp.exp(gl - gmx))
    lo = ROUTE_LANE0 + group * EXPERTS_PER_GROUP
    emask = jnp.logical_and(lane >= lo, lane < lo + EXPERTS_PER_GROUP)
    el = jnp.where(emask, logits, NEG_INF)
    l1 = lmax(el)
    i1 = lmin(jnp.where(el == l1, lane, big))
    el2 = jnp.where(lane == i1, NEG_INF, el)
    l2 = lmax(el2)
    i2 = lmin(jnp.where(el2 == l2, lane, big))
    r = jnp.exp(l2 - l1)
    w1 = group_gate / (1.0 + r)
    w2 = group_gate * r / (1.0 + r)

    is1 = lane == i1
    is2 = lane == i2
    oh = jnp.where(is1, 1.0, jnp.where(is2, 1.0, 0.0))
    earlier = jnp.dot(tril_ref[...], oh.astype(BF16), preferred_element_type=F32)
    runs = jnp.floor((jnp.sum(oh, axis=0, keepdims=True) + (SUBLANES - 1.0)) * (1.0 / SUBLANES))
    run_off = jnp.dot(jnp.broadcast_to(runs, (SUBLANES, LANES)).astype(BF16), triu_ref[...],
                      preferred_element_type=F32)[0:1]
    pos = earlier + run_off * SUBLANES
    slot1 = lsum(jnp.where(is1, pos, 0.0))
    slot2 = lsum(jnp.where(is2, pos, 0.0))
    cnt = runs * SUBLANES
    cnt_ref[0] = cnt
    base_ref[0] = carry_scr[...]
    carry_scr[...] = carry_scr[...] + cnt

    h2_ref[0] = h2.astype(BF16)
    route_ref[0] = jnp.where(lane == 0, slot1, jnp.where(lane == 1, slot2,
                                                         jnp.where(lane == 2, w1, jnp.where(lane == 3, w2, 0.0))))


def _postmix(x, o_dil, o_sb, g_dil, g_sb, w_out_bf16, gate, shift, scale, g_ffn, w_router):
    b, s, d = x.shape
    tm = POST_ROWS
    nt = s // tm
    tril = jnp.asarray(np.tril(np.ones((tm, tm), np.float32), -1), BF16)
    triu = jnp.asarray(np.triu(np.ones((LANES, LANES), np.float32), 1), BF16)
    row = lambda w: pl.BlockSpec((1, tm, w), lambda bi, i: (bi, i, 0))
    vec = lambda w: pl.BlockSpec((1, w), lambda bi, i: (0, 0))
    mod_spec = pl.BlockSpec((1, 1, d), lambda bi, i: (bi, 0, 0))
    tile_vec = pl.BlockSpec((1, 1, LANES), lambda bi, i: (bi * nt + i, 0, 0))
    return pl.pallas_call(
        _postmix_kernel,
        grid=(b, nt),
        in_specs=[row(d), row(D_DIL), row(D_SB), vec(D_DIL), vec(D_SB),
                  pl.BlockSpec((d, d), lambda bi, i: (0, 0)),
                  mod_spec, mod_spec, mod_spec, vec(d),
                  pl.BlockSpec((d, LANES), lambda bi, i: (0, 0)),
                  pl.BlockSpec((tm, tm), lambda bi, i: (0, 0)),
                  pl.BlockSpec((LANES, LANES), lambda bi, i: (0, 0))],
        out_specs=[row(d), row(d), row(LANES), tile_vec, tile_vec],
        out_shape=[jax.ShapeDtypeStruct((b, s, d), F32),
                   jax.ShapeDtypeStruct((b, s, d), BF16),
                   jax.ShapeDtypeStruct((b, s, LANES), F32),
                   jax.ShapeDtypeStruct((b * nt, 1, LANES), F32),
                   jax.ShapeDtypeStruct((b * nt, 1, LANES), F32)],
        scratch_shapes=[pltpu.VMEM((1, LANES), F32)],
        compiler_params=_cparams(("arbitrary", "arbitrary")),
        name="postmix",
    )(x, o_dil, o_sb, g_dil.reshape(1, -1), g_sb.reshape(1, -1), w_out_bf16, gate, shift, scale,
      g_ffn.reshape(1, d), w_router, tril, triu)


def _for_each_run_piece(tile, start_ref, cnt_ref, base_ref, fn):
    def body(e, off):
        c = cnt_ref[tile * N_EXPERTS + e]
        sorted0 = start_ref[e] + base_ref[tile * N_EXPERTS + e]
        for k in range(3, 10):
            p = 1 << k

            @pl.when((c & p) != 0)
            def _(p=p):
                done = c - (c & (2 * p - 1))
                fn(pl.multiple_of(off + done, SUBLANES), pl.multiple_of(sorted0 + done, SUBLANES), p)
        return off + c

    return lax.fori_loop(0, N_EXPERTS, body, 0)


def _sort_kernel(start_ref, cnt_ref, base_ref, h2_ref, route_ref, buf_in_ref, buf_ref, xs_scr, sem):
    del buf_in_ref
    tm = POST_ROWS
    lt = LOCAL_ROWS
    d = h2_ref.shape[2]
    lane = lax.broadcasted_iota(jnp.int32, (tm, LANES), 1)
    route = route_ref[0]
    w1 = jnp.sum(jnp.where(lane == 2, route, 0.0), axis=-1, keepdims=True)
    w2 = jnp.sum(jnp.where(lane == 3, route, 0.0), axis=-1, keepdims=True)

    def pieces(w):
        hi, mid, lw = _split3(w)
        return jnp.where(lane == 0, hi.astype(F32),
                         jnp.where(lane == 1, mid.astype(F32),
                                   jnp.where(lane == 2, lw.astype(F32), 0.0))).astype(BF16)

    route_t = route.T
    s1 = route_t[0:1, :].astype(jnp.int32)
    s2 = route_t[1:2, :].astype(jnp.int32)
    row = lax.broadcasted_iota(jnp.int32, (lt, tm), 0)
    p1 = jnp.where(row == s1, 1.0, 0.0)
    p2 = jnp.where(row == s2, 1.0, 0.0)
    xs_scr[:, 0:d] = jnp.dot((p1 + p2).astype(BF16), h2_ref[0], preferred_element_type=F32)
    xs_scr[:, d:] = (jnp.dot(p1.astype(BF16), pieces(w1), preferred_element_type=F32)
                     + jnp.dot(p2.astype(BF16), pieces(w2), preferred_element_type=F32))

    tile = pl.program_id(0)

    def piece(lrow, srow, rows):
        return pltpu.make_async_copy(xs_scr.at[pl.ds(lrow, rows)], buf_ref.at[pl.ds(srow, rows)], sem)

    _for_each_run_piece(tile, start_ref, cnt_ref, base_ref, lambda *a: piece(*a).start())
    _for_each_run_piece(tile, start_ref, cnt_ref, base_ref, lambda *a: piece(*a).wait())


def _dispatch(pad_start, cnt, base, h2, route, buf_zeros):
    b, s, d = h2.shape
    tm = POST_ROWS
    nt = s // tm
    any_spec = pl.BlockSpec(memory_space=pl.ANY)
    return pl.pallas_call(
        _sort_kernel,
        grid_spec=pltpu.PrefetchScalarGridSpec(
            num_scalar_prefetch=3, grid=(b * nt,),
            in_specs=[pl.BlockSpec((1, tm, d), lambda t, *_: (t // nt, t % nt, 0)),
                      pl.BlockSpec((1, tm, LANES), lambda t, *_: (t // nt, t % nt, 0)),
                      any_spec],
            out_specs=any_spec,
            scratch_shapes=[pltpu.VMEM((LOCAL_ROWS, d + LANES), F32), pltpu.SemaphoreType.DMA(())]),
        out_shape=jax.ShapeDtypeStruct(buf_zeros.shape, buf_zeros.dtype),
        input_output_aliases={5: 0},
        compiler_params=_cparams(("arbitrary",)),
        name="dispatch",
    )(pad_start, cnt, base, h2, route, buf_zeros)


def _expert_kernel(be_ref, nused_ref, x_ref, wg_ref, wu_ref, wd_ref, y_ref):
    del be_ref
    d = y_ref.shape[1]

    @pl.when(pl.program_id(0) < nused_ref[0])
    def _():
        xb = x_ref[:, 0:d].astype(BF16)
        weight = jnp.sum(x_ref[:, d:], axis=-1, keepdims=True)
        gate = jnp.dot(xb, wg_ref[0], preferred_element_type=F32)
        up = jnp.dot(xb, wu_ref[0], preferred_element_type=F32)
        act = gate / (1.0 + jnp.exp(-gate)) * up
        y_ref[...] = jnp.dot(act.astype(BF16), wd_ref[0], preferred_element_type=F32) * weight

    @pl.when(pl.program_id(0) >= nused_ref[0])
    def _():
        y_ref[...] = jnp.zeros_like(y_ref)


def _experts(block_expert, n_used, buf, wg, wu, wd):
    cap, dw = buf.shape
    d, f = wg.shape[1], wg.shape[2]
    bm = EXPERT_ROWS
    return pl.pallas_call(
        _expert_kernel,
        grid_spec=pltpu.PrefetchScalarGridSpec(
            num_scalar_prefetch=2, grid=(cap // bm,),
            in_specs=[pl.BlockSpec((bm, dw), lambda i, be, nu: (i, 0)),
                      pl.BlockSpec((1, d, f), lambda i, be, nu: (be[i], 0, 0)),
                      pl.BlockSpec((1, d, f), lambda i, be, nu: (be[i], 0, 0)),
                      pl.BlockSpec((1, f, d), lambda i, be, nu: (be[i], 0, 0))],
            out_specs=pl.BlockSpec((bm, d), lambda i, be, nu: (i, 0))),
        out_shape=jax.ShapeDtypeStruct((cap, d), F32),
        compiler_params=_cparams(("arbitrary",)),
        name="experts",
    )(block_expert, n_used, buf, wg, wu, wd)


def _combine_kernel(start_ref, cnt_ref, base_ref, x1_ref, route_ref, gate_ref, g_ref, y_hbm_ref, o_ref,
                    y_scr, sem):
    tm = POST_ROWS
    lt = LOCAL_ROWS
    tile = pl.program_id(0)

    def piece(lrow, srow, rows):
        return pltpu.make_async_copy(y_hbm_ref.at[pl.ds(srow, rows)], y_scr.at[pl.ds(lrow, rows)], sem)

    _for_each_run_piece(tile, start_ref, cnt_ref, base_ref, lambda *a: piece(*a).start())
    lane = lax.broadcasted_iota(jnp.int32, (tm, LANES), 1)
    route = route_ref[0]
    s1 = jnp.sum(jnp.where(lane == 0, route, 0.0), axis=-1, keepdims=True).astype(jnp.int32)
    s2 = jnp.sum(jnp.where(lane == 1, route, 0.0), axis=-1, keepdims=True).astype(jnp.int32)
    col = lax.broadcasted_iota(jnp.int32, (tm, lt), 1)
    pick = jnp.where(col == s1, 1.0, jnp.where(col == s2, 1.0, 0.0)).astype(BF16)
    used = _for_each_run_piece(tile, start_ref, cnt_ref, base_ref, lambda *a: piece(*a).wait())
    live = lax.broadcasted_iota(jnp.int32, (lt, 1), 0) < used
    hi, mid, lo = _split3(jnp.where(live, y_scr[...], 0.0))
    y = jnp.dot(jnp.concatenate([pick, pick, pick], axis=1), jnp.concatenate([hi, mid, lo], axis=0),
                preferred_element_type=F32)
    o_ref[0] = _rms(x1_ref[0] + gate_ref[0] * y, g_ref[...])


def _combine(pad_start, cnt, base, x1, y_sorted, route, gate, g_final):
    b, s, d = x1.shape
    tm = POST_ROWS
    nt = s // tm
    return pl.pallas_call(
        _combine_kernel,
        grid_spec=pltpu.PrefetchScalarGridSpec(
            num_scalar_prefetch=3, grid=(b * nt,),
            in_specs=[pl.BlockSpec((1, tm, d), lambda t, *_: (t // nt, t % nt, 0)),
                      pl.BlockSpec((1, tm, LANES), lambda t, *_: (t // nt, t % nt, 0)),
                      pl.BlockSpec((1, 1, d), lambda t, *_: (t // nt, 0, 0)),
                      pl.BlockSpec((1, d), lambda t, *_: (0, 0)),
                      pl.BlockSpec(memory_space=pl.ANY)],
            out_specs=pl.BlockSpec((1, tm, d), lambda t, *_: (t // nt, t % nt, 0)),
            scratch_shapes=[pltpu.VMEM((LOCAL_ROWS, d), F32), pltpu.SemaphoreType.DMA(())]),
        out_shape=jax.ShapeDtypeStruct((b, s, d), F32),
        compiler_params=_cparams(("arbitrary",)),
        name="combine",
    )(pad_start, cnt, base, x1, route, gate, g_final.reshape(1, d), y_sorted)


def kernel(x, c, w_ada, b_ada, g_mix, w_in, g_dil_out, g_sb_out, w_out, g_ffn,
           w_group, w_expert, w_gate, w_up, w_down, g_final):
    b, s, d = x.shape
    depth = w_ada.shape[0]
    assert s % DIL_UNIT == 0 and d == D_DIL + D_SB
    assert depth == 1, "the final rmsnorm is fused into the last layer's combine step"
    n = b * s
    ntiles = n // POST_ROWS
    bias = jnp.asarray(_dilated_bias())
    for layer in range(depth):
        mod = _ada(c, w_ada[layer], b_ada[layer])
        shift_mix, scale_mix, gate_mix, shift_ffn, scale_ffn, gate_ffn = (
            m.reshape(b, 1, d) for m in jnp.split(mod, 6, axis=-1))

        qkv_d, q_s, k_s, v_s = _premix(x, shift_mix, scale_mix, g_mix[layer], w_in[layer].astype(BF16))
        o_dil = _dilated(qkv_d, bias)
        o_sb = _stick(q_s, k_s, v_s)

        w_router = jnp.concatenate(
            [w_group[layer], w_expert[layer],
             jnp.zeros((d, LANES - N_GROUPS - N_EXPERTS), F32)], axis=1)
        x1, h2, route, cnt, base = _postmix(
            x, o_dil, o_sb, g_dil_out[layer], g_sb_out[layer], w_out[layer].astype(BF16),
            gate_mix, shift_ffn, scale_ffn, g_ffn[layer], w_router)

        bm = EXPERT_ROWS
        cnt = cnt[:, 0, ROUTE_LANE0:ROUTE_LANE0 + N_EXPERTS].astype(jnp.int32)
        base = base[:, 0, ROUTE_LANE0:ROUTE_LANE0 + N_EXPERTS].astype(jnp.int32)
        total = base[-1] + cnt[-1]
        cnt = cnt.reshape(-1)
        base = base.reshape(-1)
        padded = (total + bm - 1) // bm * bm
        pad_end = jnp.cumsum(padded)
        pad_start = (pad_end - padded).astype(jnp.int32)
        cap = -(-(2 * n + (SUBLANES - 1) * N_EXPERTS * ntiles) // bm) * bm + N_EXPERTS * bm
        n_blocks = cap // bm
        block_expert = jnp.minimum(
            jnp.sum(pad_end[None, :] <= (jnp.arange(n_blocks) * bm)[:, None], axis=1),
            N_EXPERTS - 1).astype(jnp.int32)
        n_used = (pad_end[-1:] // bm).astype(jnp.int32)

        buf = _dispatch(pad_start, cnt, base, h2, route, jnp.zeros((cap, d + LANES), F32))
        y_sorted = _experts(block_expert, n_used, buf, w_gate[layer].astype(BF16),
                            w_up[layer].astype(BF16), w_down[layer].astype(BF16))
        x = _combine(pad_start, cnt, base, x1, y_sorted, route, gate_ffn, g_final)
    return x
```

```python
import functools

import numpy as np
import jax
import jax.numpy as jnp
from jax import lax
from jax.experimental import pallas as pl
from jax.experimental.pallas import tpu as pltpu

HEAD_DIM = 64
N_HEADS_DIL = 8
N_HEADS_SB = 8
D_DIL = N_HEADS_DIL * HEAD_DIM
D_SB = N_HEADS_SB * HEAD_DIM
DILATION_PATTERNS = ((128, 1), (512, 4), (2048, 16))
N_GROUPS = 4
EXPERTS_PER_GROUP = 8
N_EXPERTS = N_GROUPS * EXPERTS_PER_GROUP
NORM_EPS = 1e-6

LANES = 128
SUBLANES = 8
DIL_STEPS = 128
DIL_UNIT = 2048
SB_BLOCK = 256
SB_QUERY_ROWS = 512
PRE_ROWS = 512
POST_ROWS = 512
LOCAL_ROWS = 2 * POST_ROWS + 256
EXPERT_ROWS = 512
ROUTE_LANE0 = N_GROUPS
VMEM_LIMIT = 56 * 1024 * 1024

F32 = jnp.float32
BF16 = jnp.bfloat16
NEG_INF = float("-inf")


def _cparams(sem):
    return pltpu.CompilerParams(dimension_semantics=sem, vmem_limit_bytes=VMEM_LIMIT)


def _rms(v, g):
    return v * lax.rsqrt(jnp.mean(v * v, axis=-1, keepdims=True) + NORM_EPS) * g


def _split3(v):
    hi = v.astype(BF16)
    r = v - hi.astype(F32)
    mid = r.astype(BF16)
    lo = (r - mid.astype(F32)).astype(BF16)
    return hi, mid, lo


def _ada_kernel(c_ref, w_ref, b_ref, o_ref):
    c = c_ref[...]
    cond = c / (1.0 + jnp.exp(-c))
    o_ref[...] = jnp.dot(cond, w_ref[...], precision=lax.Precision.HIGHEST,
                         preferred_element_type=F32) + b_ref[...]


def _ada(c, w_ada, b_ada):
    b, d = c.shape
    n = w_ada.shape[1]
    return pl.pallas_call(
        _ada_kernel,
        grid=(n // d,),
        in_specs=[pl.BlockSpec((b, d), lambda j: (0, 0)),
                  pl.BlockSpec((d, d), lambda j: (0, j)),
                  pl.BlockSpec((1, d), lambda j: (0, j))],
        out_specs=pl.BlockSpec((b, d), lambda j: (0, j)),
        out_shape=jax.ShapeDtypeStruct((b, n), F32),
        compiler_params=_cparams(("arbitrary",)),
        name="ada",
    )(c, w_ada, b_ada.reshape(1, n))


def _premix_kernel(x_ref, shift_ref, scale_ref, g_ref, w_ref, qkvd_ref, qs_ref, ks_ref, vs_ref):
    h = _rms(x_ref[0], g_ref[...]) * (1.0 + scale_ref[0]) + shift_ref[0]
    hb = h.astype(BF16)
    scale = HEAD_DIM ** -0.5
    for j in range(6):
        r = jnp.dot(hb, w_ref[:, j * 512:(j + 1) * 512], preferred_element_type=F32)
        if j == 0:
            qkvd_ref[0, :, 0:512] = r * scale
        elif j < 3:
            qkvd_ref[0, :, j * 512:(j + 1) * 512] = r
        elif j == 3:
            qs_ref[0] = (r * scale).astype(BF16)
        elif j == 4:
            ks_ref[0] = r.astype(BF16)
        else:
            vs_ref[0] = r.astype(BF16)


def _premix(x, shift, scale, g_mix, w_in_bf16):
    b, s, d = x.shape
    tm = PRE_ROWS
    mod_spec = pl.BlockSpec((1, 1, d), lambda bi, i: (bi, 0, 0))
    sb_spec = pl.BlockSpec((1, tm, D_SB), lambda bi, i: (bi, i, 0))
    return pl.pallas_call(
        _premix_kernel,
        grid=(b, s // tm),
        in_specs=[pl.BlockSpec((1, tm, d), lambda bi, i: (bi, i, 0)),
                  mod_spec, mod_spec,
                  pl.BlockSpec((1, d), lambda bi, i: (0, 0)),
                  pl.BlockSpec((d, 3 * (D_DIL + D_SB)), lambda bi, i: (0, 0))],
        out_specs=[pl.BlockSpec((1, tm, 3 * D_DIL), lambda bi, i: (bi, i, 0)),
                   sb_spec, sb_spec, sb_spec],
        out_shape=[jax.ShapeDtypeStruct((b, s, 3 * D_DIL), F32),
                   jax.ShapeDtypeStruct((b, s, D_SB), BF16),
                   jax.ShapeDtypeStruct((b, s, D_SB), BF16),
                   jax.ShapeDtypeStruct((b, s, D_SB), BF16)],
        compiler_params=_cparams(("arbitrary", "arbitrary")),
        name="premix",
    )(x, shift, scale, g_mix.reshape(1, d), w_in_bf16)


def _dilated_bias():
    n = DIL_STEPS
    slopes = np.array([2.0 ** (-8.0 * (i + 1) / N_HEADS_DIL) for i in range(N_HEADS_DIL)], dtype=np.float32)
    steps = np.arange(n)[:, None] + n - np.arange(2 * n)[None, :]
    valid = (steps >= 0) & (steps <= n)
    out = []
    for _, dilation in DILATION_PATTERNS:
        bias = -slopes[:, None, None] * (steps * dilation).astype(np.float32)[None]
        out.append(np.where(valid[None], bias, -np.inf).astype(np.float32))
    return np.stack(out)


def _dil_kernel(q_ref, kc_ref, kp_ref, vc_ref, vp_ref, bias_ref, o_ref,
                kext, vext, u_scr, m_scr, l_scr):
    n = DIL_STEPS
    g = pl.program_id(1)
    kext[0:DIL_UNIT, :] = kp_ref[0]
    kext[DIL_UNIT:2 * DIL_UNIT, :] = kc_ref[0]
    vext[0:DIL_UNIT, :] = vp_ref[0]
    vext[DIL_UNIT:2 * DIL_UNIT, :] = vc_ref[0]
    lane = lax.broadcasted_iota(jnp.int32, (n, LANES), 1)
    head0 = lane < HEAD_DIM
    col = lax.broadcasted_iota(jnp.int32, (n, 2 * n), 1)

    for p, (_, dil) in enumerate(DILATION_PATTERNS):
        unit = n * dil

        def tile(ti, carry, p=p, dil=dil, unit=unit):
            j = ti // dil
            r = ti % dil
            qstart = j * unit + r
            kstart = DIL_UNIT + qstart - unit
            if dil == 1:
                q = q_ref[0, pl.ds(qstart, n), :]
                kk = kext[pl.ds(kstart, 2 * n), :]
                vv = vext[pl.ds(kstart, 2 * n), :]
            else:
                q = q_ref[0, pl.ds(qstart, n, stride=dil), :]
                kk = kext[pl.ds(kstart, 2 * n, stride=dil), :]
                vv = vext[pl.ds(kstart, 2 * n, stride=dil), :]
            kk = kk.astype(BF16)
            vv = vv.astype(BF16)
            dead_cols = jnp.where(jnp.logical_and(g == 0, j == 0), n, 0)
            qh = [jnp.where(head0 if h == 0 else jnp.logical_not(head0), q, 0.0).astype(BF16)
                  for h in range(2)]
            ss = [lax.dot_general(qh[h], kk, (((1,), (1,)), ((), ())), preferred_element_type=F32)
                  for h in range(2)]
            ms, ls, pes = [], [], []
            for h in range(2):
                logits = jnp.where(col < dead_cols, NEG_INF, ss[h] + bias_ref[p, h])
                m = jnp.max(logits, axis=-1, keepdims=True)
                pe = jnp.exp(logits - m)
                ls.append(jnp.sum(pe, axis=-1, keepdims=True))
                ms.append(m)
                pes.append(pe.astype(BF16))
            us = [jnp.dot(pes[h], vv, preferred_element_type=F32) for h in range(2)]
            u = jnp.where(head0, us[0], us[1])
            m = jnp.where(head0, ms[0], ms[1])
            l = jnp.where(head0, ls[0], ls[1])
            if dil == 1:
                rows = pl.ds(qstart, n)
            else:
                rows = pl.ds(qstart, n, stride=dil)
            u_scr[p, rows, :] = u
            m_scr[p, rows, :] = m
            l_scr[p, rows, :] = l
            return carry

        lax.fori_loop(0, DIL_UNIT // n, tile, 0)

    def merge(i, carry):
        rows = pl.ds(pl.multiple_of(i * n, n), n)
        m0, m1, m2 = m_scr[0, rows, :], m_scr[1, rows, :], m_scr[2, rows, :]
        mx = jnp.maximum(jnp.maximum(m0, m1), m2)
        w0, w1, w2 = jnp.exp(m0 - mx), jnp.exp(m1 - mx), jnp.exp(m2 - mx)
        num = w0 * u_scr[0, rows, :] + w1 * u_scr[1, rows, :] + w2 * u_scr[2, rows, :]
        den = w0 * l_scr[0, rows, :] + w1 * l_scr[1, rows, :] + w2 * l_scr[2, rows, :]
        o_ref[0, rows, :] = num / den
        return carry

    lax.fori_loop(0, DIL_UNIT // n, merge, 0)


def _dilated(qkv_d, bias):
    b, s, _ = qkv_d.shape
    u = DIL_UNIT
    npair = D_DIL // LANES
    cur = lambda off: pl.BlockSpec((1, u, LANES), lambda bi, g, p: (bi, g, off + p))
    prev = lambda off: pl.BlockSpec((1, u, LANES), lambda bi, g, p: (bi, jnp.maximum(g - 1, 0), off + p))
    return pl.pallas_call(
        _dil_kernel,
        grid=(b, s // u, npair),
        in_specs=[cur(0), cur(npair), prev(npair), cur(2 * npair), prev(2 * npair),
                  pl.BlockSpec((3, 2, DIL_STEPS, 2 * DIL_STEPS), lambda bi, g, p: (0, p, 0, 0))],
        out_specs=pl.BlockSpec((1, u, LANES), lambda bi, g, p: (bi, g, p)),
        out_shape=jax.ShapeDtypeStruct((b, s, D_DIL), F32),
        scratch_shapes=[pltpu.VMEM((2 * u, LANES), F32), pltpu.VMEM((2 * u, LANES), F32),
                        pltpu.VMEM((3, u, LANES), F32), pltpu.VMEM((3, u, LANES), F32),
                        pltpu.VMEM((3, u, LANES), F32)],
        compiler_params=_cparams(("arbitrary", "arbitrary", "arbitrary")),
        name="dilated",
    )(qkv_d, qkv_d, qkv_d, qkv_d, qkv_d, bias)


def _stick_kernel(q_ref, k_ref, v_ref, tri_ref, o_ref,
                  qh_scr, z_scr, w_scr, acc_scr, carry_scr, scale_scr):
    blk = SB_BLOCK
    nsub = SB_QUERY_ROWS // blk
    nchain = 2 * nsub
    qi = pl.program_id(2)
    lane = lax.broadcasted_iota(jnp.int32, (blk, LANES), 1)
    head0 = lane < HEAD_DIM
    for sub in range(nsub):
        q = q_ref[0, sub * blk:(sub + 1) * blk, :]
        zero = jnp.zeros_like(q)
        qh_scr[2 * sub] = jnp.where(head0, q, zero)
        qh_scr[2 * sub + 1] = jnp.where(head0, zero, q)
    acc_scr[...] = jnp.zeros_like(acc_scr)
    carry_scr[...] = jnp.zeros_like(carry_scr)
    sign = jnp.int32(-2 ** 31)

    def rows(kb):
        return pl.ds(pl.multiple_of(kb * blk, blk), blk)

    def scores(kb, which, slot):
        kblk = k_ref[0, rows(kb), :]
        for c in which:
            z_scr[slot * nchain + c] = lax.dot_general(
                qh_scr[c], kblk, (((1,), (1,)), ((), ())), preferred_element_type=F32)

    def weights(which, slot, diag_sub):
        causal = (lax.broadcasted_iota(jnp.int32, (blk, blk), 1)
                  < lax.broadcasted_iota(jnp.int32, (blk, blk), 0))
        log_betas, log_keeps, splits = {}, {}, {}
        for c in which:
            z = z_scr[slot * nchain + c]
            neg_abs = lax.bitcast_convert_type(lax.bitcast_convert_type(z, jnp.int32) | sign, F32)
            sp = jnp.log(1.0 + jnp.exp(neg_abs))
            log_beta = jnp.minimum(z, 0.0) - sp
            log_keep = log_beta - z
            if c // 2 == diag_sub:
                log_keep = jnp.where(causal, log_keep, 0.0)
            hi = log_keep.astype(BF16)
            lo = (log_keep - hi.astype(F32)).astype(BF16)
            log_betas[c], log_keeps[c] = log_beta, log_keep
            splits[c] = jnp.concatenate([hi, lo], axis=1)
        betweens = {c: jnp.dot(splits[c], tri_ref[...], preferred_element_type=F32) for c in which}
        for c in which:
            w = jnp.exp(log_betas[c] + betweens[c])
            if c // 2 == diag_sub:
                w = jnp.where(causal, w, 0.0)
            w_scr[c] = w.astype(BF16)
            carry = carry_scr[c]
            scale_scr[c] = jnp.exp(carry)
            carry_scr[c] = carry + (betweens[c][:, 0:1] + log_keeps[c][:, 0:1])

    def accumulate(which, kb):
        vblk = v_ref[0, rows(kb), :]
        for c in which:
            acc_scr[c] = acc_scr[c] + scale_scr[c] * jnp.dot(w_scr[c], vblk, preferred_element_type=F32)

    everyone = list(range(nchain))
    top = nsub * qi + nsub - 1
    first = nsub * qi - 1
    for i in range(nsub):
        which = [c for c in everyone if c // 2 >= nsub - 1 - i]
        scores(top - i, which, 1)
        if i == nsub - 1:
            scores(jnp.maximum(first, 0), everyone, 0)
        weights(which, 1, nsub - 1 - i)
        if i < nsub - 1:
            accumulate(which, top - i)

    def step(i, carry):
        kb = first - i
        accumulate(everyone, kb + 1)
        scores(jnp.maximum(kb - 1, 0), everyone, (i + 1) % 2)
        weights(everyone, i % 2, -1)
        return carry

    lax.fori_loop(0, nsub * qi, step, 0)
    accumulate(everyone, 0)
    for sub in range(nsub):
        o_ref[0, sub * blk:(sub + 1) * blk, :] = jnp.where(head0, acc_scr[2 * sub], acc_scr[2 * sub + 1])


def _stick(q_s, k_s, v_s):
    b, s, _ = q_s.shape
    blk = SB_BLOCK
    qrows = SB_QUERY_ROWS
    nchain = 2 * qrows // blk
    tri = np.tril(np.ones((blk, blk), np.float32), -1)
    tri2 = jnp.asarray(np.concatenate([tri, tri], axis=0), BF16)
    full = pl.BlockSpec((1, s, LANES), lambda bi, p, i: (bi, 0, p))
    return pl.pallas_call(
        _stick_kernel,
        grid=(b, D_SB // LANES, s // qrows),
        in_specs=[pl.BlockSpec((1, qrows, LANES), lambda bi, p, i: (bi, i, p)), full, full,
                  pl.BlockSpec((2 * blk, blk), lambda bi, p, i: (0, 0))],
        out_specs=pl.BlockSpec((1, qrows, LANES), lambda bi, p, i: (bi, i, p)),
        out_shape=jax.ShapeDtypeStruct((b, s, D_SB), F32),
        scratch_shapes=[pltpu.VMEM((nchain, blk, LANES), BF16),
                        pltpu.VMEM((2 * nchain, blk, blk), F32),
                        pltpu.VMEM((nchain, blk, blk), BF16),
                        pltpu.VMEM((nchain, blk, LANES), F32),
                        pltpu.VMEM((nchain, blk, 1), F32),
                        pltpu.VMEM((nchain, blk, 1), F32)],
        compiler_params=_cparams(("arbitrary", "arbitrary", "arbitrary")),
        name="stick",
    )(q_s, k_s, v_s, tri2)


def _postmix_kernel(x_ref, od_ref, os_ref, gd_ref, gs_ref, wout_ref, gate_ref, shift_ref, scale_ref,
                    gffn_ref, wr_ref, tril_ref, triu_ref,
                    x1_ref, h2_ref, route_ref, cnt_ref, base_ref, carry_scr):
    tm = POST_ROWS

    @pl.when(jnp.logical_and(pl.program_id(0) == 0, pl.program_id(1) == 0))
    def _():
        carry_scr[...] = jnp.zeros_like(carry_scr)

    mixed = jnp.concatenate([_rms(od_ref[0], gd_ref[...]), _rms(os_ref[0], gs_ref[...])], axis=-1)
    proj = jnp.dot(mixed.astype(BF16), wout_ref[...], preferred_element_type=F32)
    x1 = x_ref[0] + gate_ref[0] * proj
    x1_ref[0] = x1
    h2 = _rms(x1, gffn_ref[...]) * (1.0 + scale_ref[0]) + shift_ref[0]
    logits = jnp.dot(h2, wr_ref[...], precision=lax.Precision.HIGHEST, preferred_element_type=F32)

    lane = lax.broadcasted_iota(jnp.int32, (tm, LANES), 1)
    big = jnp.int32(LANES)
    lmax = lambda v: jnp.max(v, axis=-1, keepdims=True)
    lmin = lambda v: jnp.min(v, axis=-1, keepdims=True)
    lsum = lambda v: jnp.sum(v, axis=-1, keepdims=True)

    gmask = lane < N_GROUPS
    gl = jnp.where(gmask, logits, NEG_INF)
    gmx = lmax(gl)
    group = lmin(jnp.where(jnp.logical_and(gmask, gl == gmx), lane, big))
    group_gate = 1.0 / lsum(jnp.exp(gl - gmx))
    lo = ROUTE_LANE0 + group * EXPERTS_PER_GROUP
    emask = jnp.logical_and(lane >= lo, lane < lo + EXPERTS_PER_GROUP)
    el = jnp.where(emask, logits, NEG_INF)
    l1 = lmax(el)
    i1 = lmin(jnp.where(el == l1, lane, big))
    el2 = jnp.where(lane == i1, NEG_INF, el)
    l2 = lmax(el2)
    i2 = lmin(jnp.where(el2 == l2, lane, big))
    r = jnp.exp(l2 - l1)
    w1 = group_gate / (1.0 + r)
    w2 = group_gate * r / (1.0 + r)

    is1 = lane == i1
    is2 = lane == i2
    oh = jnp.where(is1, 1.0, jnp.where(is2, 1.0, 0.0))
    earlier = jnp.dot(tril_ref[...], oh.astype(BF16), preferred_element_type=F32)
    runs = jnp.floor((jnp.sum(oh, axis=0, keepdims=True) + (SUBLANES - 1.0)) * (1.0 / SUBLANES))
    run_off = jnp.dot(jnp.broadcast_to(runs, (SUBLANES, LANES)).astype(BF16), triu_ref[...],
                      preferred_element_type=F32)[0:1]
    pos = earlier + run_off * SUBLANES
    slot1 = lsum(jnp.where(is1, pos, 0.0))
    slot2 = lsum(jnp.where(is2, pos, 0.0))
    cnt = runs * SUBLANES
    cnt_ref[0] = cnt
    base_ref[0] = carry_scr[...]
    carry_scr[...] = carry_scr[...] + cnt

    h2_ref[0] = h2.astype(BF16)
    route_ref[0] = jnp.where(lane == 0, slot1, jnp.where(lane == 1, slot2,
                                                         jnp.where(lane == 2, w1, jnp.where(lane == 3, w2, 0.0))))


def _postmix(x, o_dil, o_sb, g_dil, g_sb, w_out_bf16, gate, shift, scale, g_ffn, w_router):
    b, s, d = x.shape
    tm = POST_ROWS
    nt = s // tm
    tril = jnp.asarray(np.tril(np.ones((tm, tm), np.float32), -1), BF16)
    triu = jnp.asarray(np.triu(np.ones((LANES, LANES), np.float32), 1), BF16)
    row = lambda w: pl.BlockSpec((1, tm, w), lambda bi, i: (bi, i, 0))
    vec = lambda w: pl.BlockSpec((1, w), lambda bi, i: (0, 0))
    mod_spec = pl.BlockSpec((1, 1, d), lambda bi, i: (bi, 0, 0))
    tile_vec = pl.BlockSpec((1, 1, LANES), lambda bi, i: (bi * nt + i, 0, 0))
    return pl.pallas_call(
        _postmix_kernel,
        grid=(b, nt),
        in_specs=[row(d), row(D_DIL), row(D_SB), vec(D_DIL), vec(D_SB),
                  pl.BlockSpec((d, d), lambda bi, i: (0, 0)),
                  mod_spec, mod_spec, mod_spec, vec(d),
                  pl.BlockSpec((d, LANES), lambda bi, i: (0, 0)),
                  pl.BlockSpec((tm, tm), lambda bi, i: (0, 0)),
                  pl.BlockSpec((LANES, LANES), lambda bi, i: (0, 0))],
        out_specs=[row(d), row(d), row(LANES), tile_vec, tile_vec],
        out_shape=[jax.ShapeDtypeStruct((b, s, d), F32),
                   jax.ShapeDtypeStruct((b, s, d), BF16),
                   jax.ShapeDtypeStruct((b, s, LANES), F32),
                   jax.ShapeDtypeStruct((b * nt, 1, LANES), F32),
                   jax.ShapeDtypeStruct((b * nt, 1, LANES), F32)],
        scratch_shapes=[pltpu.VMEM((1, LANES), F32)],
        compiler_params=_cparams(("arbitrary", "arbitrary")),
        name="postmix",
    )(x, o_dil, o_sb, g_dil.reshape(1, -1), g_sb.reshape(1, -1), w_out_bf16, gate, shift, scale,
      g_ffn.reshape(1, d), w_router, tril, triu)


def _for_each_run_piece(tile, start_ref, cnt_ref, base_ref, fn):
    def body(e, off):
        c = cnt_ref[tile * N_EXPERTS + e]
        sorted0 = start_ref[e] + base_ref[tile * N_EXPERTS + e]
        for k in range(3, 10):
            p = 1 << k

            @pl.when((c & p) != 0)
            def _(p=p):
                done = c - (c & (2 * p - 1))
                fn(pl.multiple_of(off + done, SUBLANES), pl.multiple_of(sorted0 + done, SUBLANES), p)
        return off + c

    return lax.fori_loop(0, N_EXPERTS, body, 0)


def _sort_kernel(start_ref, cnt_ref, base_ref, h2_ref, route_ref, buf_in_ref, buf_ref, xs_scr, sem):
    del buf_in_ref
    tm = POST_ROWS
    lt = LOCAL_ROWS
    d = h2_ref.shape[2]
    lane = lax.broadcasted_iota(jnp.int32, (tm, LANES), 1)
    route = route_ref[0]
    w1 = jnp.sum(jnp.where(lane == 2, route, 0.0), axis=-1, keepdims=True)
    w2 = jnp.sum(jnp.where(lane == 3, route, 0.0), axis=-1, keepdims=True)

    def pieces(w):
        hi, mid, lw = _split3(w)
        return jnp.where(lane == 0, hi.astype(F32),
                         jnp.where(lane == 1, mid.astype(F32),
                                   jnp.where(lane == 2, lw.astype(F32), 0.0))).astype(BF16)

    route_t = route.T
    s1 = route_t[0:1, :].astype(jnp.int32)
    s2 = route_t[1:2, :].astype(jnp.int32)
    row = lax.broadcasted_iota(jnp.int32, (lt, tm), 0)
    p1 = jnp.where(row == s1, 1.0, 0.0)
    p2 = jnp.where(row == s2, 1.0, 0.0)
    xs_scr[:, 0:d] = jnp.dot((p1 + p2).astype(BF16), h2_ref[0], preferred_element_type=F32)
    xs_scr[:, d:] = (jnp.dot(p1.astype(BF16), pieces(w1), preferred_element_type=F32)
                     + jnp.dot(p2.astype(BF16), pieces(w2), preferred_element_type=F32))

    tile = pl.program_id(0)

    def piece(lrow, srow, rows):
        return pltpu.make_async_copy(xs_scr.at[pl.ds(lrow, rows)], buf_ref.at[pl.ds(srow, rows)], sem)

    _for_each_run_piece(tile, start_ref, cnt_ref, base_ref, lambda *a: piece(*a).start())
    _for_each_run_piece(tile, start_ref, cnt_ref, base_ref, lambda *a: piece(*a).wait())


def _dispatch(pad_start, cnt, base, h2, route, buf_zeros):
    b, s, d = h2.shape
    tm = POST_ROWS
    nt = s // tm
    any_spec = pl.BlockSpec(memory_space=pl.ANY)
    return pl.pallas_call(
        _sort_kernel,
        grid_spec=pltpu.PrefetchScalarGridSpec(
            num_scalar_prefetch=3, grid=(b * nt,),
            in_specs=[pl.BlockSpec((1, tm, d), lambda t, *_: (t // nt, t % nt, 0)),
                      pl.BlockSpec((1, tm, LANES), lambda t, *_: (t // nt, t % nt, 0)),
                      any_spec],
            out_specs=any_spec,
            scratch_shapes=[pltpu.VMEM((LOCAL_ROWS, d + LANES), F32), pltpu.SemaphoreType.DMA(())]),
        out_shape=jax.ShapeDtypeStruct(buf_zeros.shape, buf_zeros.dtype),
        input_output_aliases={5: 0},
        compiler_params=_cparams(("arbitrary",)),
        name="dispatch",
    )(pad_start, cnt, base, h2, route, buf_zeros)


def _expert_kernel(be_ref, nused_ref, x_ref, wg_ref, wu_ref, wd_ref, y_ref):
    del be_ref
    d = y_ref.shape[1]

    @pl.when(pl.program_id(0) < nused_ref[0])
    def _():
        xb = x_ref[:, 0:d].astype(BF16)
        weight = jnp.sum(x_ref[:, d:], axis=-1, keepdims=True)
        gate = jnp.dot(xb, wg_ref[0], preferred_element_type=F32)
        up = jnp.dot(xb, wu_ref[0], preferred_element_type=F32)
        act = gate / (1.0 + jnp.exp(-gate)) * up
        y_ref[...] = jnp.dot(act.astype(BF16), wd_ref[0], preferred_element_type=F32) * weight

    @pl.when(pl.program_id(0) >= nused_ref[0])
    def _():
        y_ref[...] = jnp.zeros_like(y_ref)


def _experts(block_expert, n_used, buf, wg, wu, wd):
    cap, dw = buf.shape
    d, f = wg.shape[1], wg.shape[2]
    bm = EXPERT_ROWS
    return pl.pallas_call(
        _expert_kernel,
        grid_spec=pltpu.PrefetchScalarGridSpec(
            num_scalar_prefetch=2, grid=(cap // bm,),
            in_specs=[pl.BlockSpec((bm, dw), lambda i, be, nu: (i, 0)),
                      pl.BlockSpec((1, d, f), lambda i, be, nu: (be[i], 0, 0)),
                      pl.BlockSpec((1, d, f), lambda i, be, nu: (be[i], 0, 0)),
                      pl.BlockSpec((1, f, d), lambda i, be, nu: (be[i], 0, 0))],
            out_specs=pl.BlockSpec((bm, d), lambda i, be, nu: (i, 0))),
        out_shape=jax.ShapeDtypeStruct((cap, d), F32),
        compiler_params=_cparams(("arbitrary",)),
        name="experts",
    )(block_expert, n_used, buf, wg, wu, wd)


def _combine_kernel(start_ref, cnt_ref, base_ref, x1_ref, route_ref, gate_ref, g_ref, y_hbm_ref, o_ref,
                    y_scr, sem):
    tm = POST_ROWS
    lt = LOCAL_ROWS
    tile = pl.program_id(0)

    def piece(lrow, srow, rows):
        return pltpu.make_async_copy(y_hbm_ref.at[pl.ds(srow, rows)], y_scr.at[pl.ds(lrow, rows)], sem)

    _for_each_run_piece(tile, start_ref, cnt_ref, base_ref, lambda *a: piece(*a).start())
    lane = lax.broadcasted_iota(jnp.int32, (tm, LANES), 1)
    route = route_ref[0]
    s1 = jnp.sum(jnp.where(lane == 0, route, 0.0), axis=-1, keepdims=True).astype(jnp.int32)
    s2 = jnp.sum(jnp.where(lane == 1, route, 0.0), axis=-1, keepdims=True).astype(jnp.int32)
    col = lax.broadcasted_iota(jnp.int32, (tm, lt), 1)
    pick = jnp.where(col == s1, 1.0, jnp.where(col == s2, 1.0, 0.0)).astype(BF16)
    used = _for_each_run_piece(tile, start_ref, cnt_ref, base_ref, lambda *a: piece(*a).wait())
    live = lax.broadcasted_iota(jnp.int32, (lt, 1), 0) < used
    hi, mid, lo = _split3(jnp.where(live, y_scr[...], 0.0))
    y = jnp.dot(jnp.concatenate([pick, pick, pick], axis=1), jnp.concatenate([hi, mid, lo], axis=0),
                preferred_element_type=F32)
    o_ref[0] = _rms(x1_ref[0] + gate_ref[0] * y, g_ref[...])


def _combine(pad_start, cnt, base, x1, y_sorted, route, gate, g_final):
    b, s, d = x1.shape
    tm = POST_ROWS
    nt = s // tm
    return pl.pallas_call(
        _combine_kernel,
        grid_spec=pltpu.PrefetchScalarGridSpec(
            num_scalar_prefetch=3, grid=(b * nt,),
            in_specs=[pl.BlockSpec((1, tm, d), lambda t, *_: (t // nt, t % nt, 0)),
                      pl.BlockSpec((1, tm, LANES), lambda t, *_: (t // nt, t % nt, 0)),
                      pl.BlockSpec((1, 1, d), lambda t, *_: (t // nt, 0, 0)),
                      pl.BlockSpec((1, d), lambda t, *_: (0, 0)),
                      pl.BlockSpec(memory_space=pl.ANY)],
            out_specs=pl.BlockSpec((1, tm, d), lambda t, *_: (t // nt, t % nt, 0)),
            scratch_shapes=[pltpu.VMEM((LOCAL_ROWS, d), F32), pltpu.SemaphoreType.DMA(())]),
        out_shape=jax.ShapeDtypeStruct((b, s, d), F32),
        compiler_params=_cparams(("arbitrary",)),
        name="combine",
    )(pad_start, cnt, base, x1, route, gate, g_final.reshape(1, d), y_sorted)


def kernel(x, c, w_ada, b_ada, g_mix, w_in, g_dil_out, g_sb_out, w_out, g_ffn,
           w_group, w_expert, w_gate, w_up, w_down, g_final):
    b, s, d = x.shape
    depth = w_ada.shape[0]
    assert s % DIL_UNIT == 0 and d == D_DIL + D_SB
    assert depth == 1, "the final rmsnorm is fused into the last layer's combine step"
    n = b * s
    ntiles = n // POST_ROWS
    bias = jnp.asarray(_dilated_bias())
    for layer in range(depth):
        mod = _ada(c, w_ada[layer], b_ada[layer])
        shift_mix, scale_mix, gate_mix, shift_ffn, scale_ffn, gate_ffn = (
            m.reshape(b, 1, d) for m in jnp.split(mod, 6, axis=-1))

        qkv_d, q_s, k_s, v_s = _premix(x, shift_mix, scale_mix, g_mix[layer], w_in[layer].astype(BF16))
        o_dil = _dilated(qkv_d, bias)
        o_sb = _stick(q_s, k_s, v_s)

        w_router = jnp.concatenate(
            [w_group[layer], w_expert[layer],
             jnp.zeros((d, LANES - N_GROUPS - N_EXPERTS), F32)], axis=1)
        x1, h2, route, cnt, base = _postmix(
            x, o_dil, o_sb, g_dil_out[layer], g_sb_out[layer], w_out[layer].astype(BF16),
            gate_mix, shift_ffn, scale_ffn, g_ffn[layer], w_router)

        bm = EXPERT_ROWS
        cnt = cnt[:, 0, ROUTE_LANE0:ROUTE_LANE0 + N_EXPERTS].astype(jnp.int32)
        base = base[:, 0, ROUTE_LANE0:ROUTE_LANE0 + N_EXPERTS].astype(jnp.int32)
        total = base[-1] + cnt[-1]
        cnt = cnt.reshape(-1)
        base = base.reshape(-1)
        padded = (total + bm - 1) // bm * bm
        pad_end = jnp.cumsum(padded)
        pad_start = (pad_end - padded).astype(jnp.int32)
        cap = -(-(2 * n + (SUBLANES - 1) * N_EXPERTS * ntiles) // bm) * bm + N_EXPERTS * bm
        n_blocks = cap // bm
        block_expert = jnp.minimum(
            jnp.sum(pad_end[None, :] <= (jnp.arange(n_blocks) * bm)[:, None], axis=1),
            N_EXPERTS - 1).astype(jnp.int32)
        n_used = (pad_end[-1:] // bm).astype(jnp.int32)

        buf = _dispatch(pad_start, cnt, base, h2, route, jnp.zeros((cap, d + LANES), F32))
        y_sorted = _experts(block_expert, n_used, buf, w_gate[layer].astype(BF16),
                            w_up[layer].astype(BF16), w_down[layer].astype(BF16))
        x = _combine(pad_start, cnt, base, x1, y_sorted, route, gate_ffn, g_final)
    return x
```

```python
import functools

import numpy as np
import jax
import jax.numpy as jnp
from jax import lax
from jax.experimental import pallas as pl
from jax.experimental.pallas import tpu as pltpu

HEAD_DIM = 64
N_HEADS_DIL = 8
N_HEADS_SB = 8
D_DIL = N_HEADS_DIL * HEAD_DIM
D_SB = N_HEADS_SB * HEAD_DIM
DILATION_PATTERNS = ((128, 1), (512, 4), (2048, 16))
N_GROUPS = 4
EXPERTS_PER_GROUP = 8
N_EXPERTS = N_GROUPS * EXPERTS_PER_GROUP
NORM_EPS = 1e-6

LANES = 128
SUBLANES = 8
DIL_STEPS = 128
DIL_UNIT = 2048
SB_BLOCK = 256
SB_QUERY_ROWS = 512
PRE_ROWS = 512
POST_ROWS = 512
LOCAL_ROWS = 2 * POST_ROWS + 256
EXPERT_ROWS = 512
ROUTE_LANE0 = N_GROUPS
VMEM_LIMIT = 56 * 1024 * 1024

F32 = jnp.float32
BF16 = jnp.bfloat16
NEG_INF = float("-inf")


def _cparams(sem):
    return pltpu.CompilerParams(dimension_semantics=sem, vmem_limit_bytes=VMEM_LIMIT)


def _rms(v, g):
    return v * lax.rsqrt(jnp.mean(v * v, axis=-1, keepdims=True) + NORM_EPS) * g


def _split3(v):
    hi = v.astype(BF16)
    r = v - hi.astype(F32)
    mid = r.astype(BF16)
    lo = (r - mid.astype(F32)).astype(BF16)
    return hi, mid, lo


def _ada_kernel(c_ref, w_ref, b_ref, o_ref):
    c = c_ref[...]
    cond = c / (1.0 + jnp.exp(-c))
    o_ref[...] = jnp.dot(cond, w_ref[...], precision=lax.Precision.HIGHEST,
                         preferred_element_type=F32) + b_ref[...]


def _ada(c, w_ada, b_ada):
    b, d = c.shape
    n = w_ada.shape[1]
    return pl.pallas_call(
        _ada_kernel,
        grid=(n // d,),
        in_specs=[pl.BlockSpec((b, d), lambda j: (0, 0)),
                  pl.BlockSpec((d, d), lambda j: (0, j)),
                  pl.BlockSpec((1, d), lambda j: (0, j))],
        out_specs=pl.BlockSpec((b, d), lambda j: (0, j)),
        out_shape=jax.ShapeDtypeStruct((b, n), F32),
        compiler_params=_cparams(("arbitrary",)),
        name="ada",
    )(c, w_ada, b_ada.reshape(1, n))


def _premix_kernel(x_ref, shift_ref, scale_ref, g_ref, w_ref, qkvd_ref, qs_ref, ks_ref, vs_ref):
    h = _rms(x_ref[0], g_ref[...]) * (1.0 + scale_ref[0]) + shift_ref[0]
    hb = h.astype(BF16)
    scale = HEAD_DIM ** -0.5
    for j in range(6):
        r = jnp.dot(hb, w_ref[:, j * 512:(j + 1) * 512], preferred_element_type=F32)
        if j == 0:
            qkvd_ref[0, :, 0:512] = r * scale
        elif j < 3:
            qkvd_ref[0, :, j * 512:(j + 1) * 512] = r
        elif j == 3:
            qs_ref[0] = (r * scale).astype(BF16)
        elif j == 4:
            ks_ref[0] = r.astype(BF16)
        else:
            vs_ref[0] = r.astype(BF16)


def _premix(x, shift, scale, g_mix, w_in_bf16):
    b, s, d = x.shape
    tm = PRE_ROWS
    mod_spec = pl.BlockSpec((1, 1, d), lambda bi, i: (bi, 0, 0))
    sb_spec = pl.BlockSpec((1, tm, D_SB), lambda bi, i: (bi, i, 0))
    return pl.pallas_call(
        _premix_kernel,
        grid=(b, s // tm),
        in_specs=[pl.BlockSpec((1, tm, d), lambda bi, i: (bi, i, 0)),
                  mod_spec, mod_spec,
                  pl.BlockSpec((1, d), lambda bi, i: (0, 0)),
                  pl.BlockSpec((d, 3 * (D_DIL + D_SB)), lambda bi, i: (0, 0))],
        out_specs=[pl.BlockSpec((1, tm, 3 * D_DIL), lambda bi, i: (bi, i, 0)),
                   sb_spec, sb_spec, sb_spec],
        out_shape=[jax.ShapeDtypeStruct((b, s, 3 * D_DIL), F32),
                   jax.ShapeDtypeStruct((b, s, D_SB), BF16),
                   jax.ShapeDtypeStruct((b, s, D_SB), BF16),
                   jax.ShapeDtypeStruct((b, s, D_SB), BF16)],
        compiler_params=_cparams(("arbitrary", "arbitrary")),
        name="premix",
    )(x, shift, scale, g_mix.reshape(1, d), w_in_bf16)


def _dilated_bias():
    n = DIL_STEPS
    slopes = np.array([2.0 ** (-8.0 * (i + 1) / N_HEADS_DIL) for i in range(N_HEADS_DIL)], dtype=np.float32)
    steps = np.arange(n)[:, None] + n - np.arange(2 * n)[None, :]
    valid = (steps >= 0) & (steps <= n)
    out = []
    for _, dilation in DILATION_PATTERNS:
        bias = -slopes[:, None, None] * (steps * dilation).astype(np.float32)[None]
        out.append(np.where(valid[None], bias, -np.inf).astype(np.float32))
    return np.stack(out)


def _dil_kernel(q_ref, kc_ref, kp_ref, vc_ref, vp_ref, bias_ref, o_ref,
                kext, vext, u_scr, m_scr, l_scr):
    n = DIL_STEPS
    g = pl.program_id(1)
    kext[0:DIL_UNIT, :] = kp_ref[0]
    kext[DIL_UNIT:2 * DIL_UNIT, :] = kc_ref[0]
    vext[0:DIL_UNIT, :] = vp_ref[0]
    vext[DIL_UNIT:2 * DIL_UNIT, :] = vc_ref[0]
    lane = lax.broadcasted_iota(jnp.int32, (n, LANES), 1)
    head0 = lane < HEAD_DIM
    col = lax.broadcasted_iota(jnp.int32, (n, 2 * n), 1)

    for p, (_, dil) in enumerate(DILATION_PATTERNS):
        unit = n * dil

        def tile(ti, carry, p=p, dil=dil, unit=unit):
            j = ti // dil
            r = ti % dil
            qstart = j * unit + r
            kstart = DIL_UNIT + qstart - unit
            if dil == 1:
                q = q_ref[0, pl.ds(qstart, n), :]
                kk = kext[pl.ds(kstart, 2 * n), :]
                vv = vext[pl.ds(kstart, 2 * n), :]
            else:
                q = q_ref[0, pl.ds(qstart, n, stride=dil), :]
                kk = kext[pl.ds(kstart, 2 * n, stride=dil), :]
                vv = vext[pl.ds(kstart, 2 * n, stride=dil), :]
            kk = kk.astype(BF16)
            vv = vv.astype(BF16)
            dead_cols = jnp.where(jnp.logical_and(g == 0, j == 0), n, 0)
            qh = [jnp.where(head0 if h == 0 else jnp.logical_not(head0), q, 0.0).astype(BF16)
                  for h in range(2)]
            ss = [lax.dot_general(qh[h], kk, (((1,), (1,)), ((), ())), preferred_element_type=F32)
                  for h in range(2)]
            ms, ls, pes = [], [], []
            for h in range(2):
                logits = jnp.where(col < dead_cols, NEG_INF, ss[h] + bias_ref[p, h])
                m = jnp.max(logits, axis=-1, keepdims=True)
                pe = jnp.exp(logits - m)
                ls.append(jnp.sum(pe, axis=-1, keepdims=True))
                ms.append(m)
                pes.append(pe.astype(BF16))
            us = [jnp.dot(pes[h], vv, preferred_element_type=F32) for h in range(2)]
            u = jnp.where(head0, us[0], us[1])
            m = jnp.where(head0, ms[0], ms[1])
            l = jnp.where(head0, ls[0], ls[1])
            if dil == 1:
                rows = pl.ds(qstart, n)
            else:
                rows = pl.ds(qstart, n, stride=dil)
            u_scr[p, rows, :] = u
            m_scr[p, rows, :] = m
            l_scr[p, rows, :] = l
            return carry

        lax.fori_loop(0, DIL_UNIT // n, tile, 0)

    def merge(i, carry):
        rows = pl.ds(pl.multiple_of(i * n, n), n)
        m0, m1, m2 = m_scr[0, rows, :], m_scr[1, rows, :], m_scr[2, rows, :]
        mx = jnp.maximum(jnp.maximum(m0, m1), m2)
        w0, w1, w2 = jnp.exp(m0 - mx), jnp.exp(m1 - mx), jnp.exp(m2 - mx)
        num = w0 * u_scr[0, rows, :] + w1 * u_scr[1, rows, :] + w2 * u_scr[2, rows, :]
        den = w0 * l_scr[0, rows, :] + w1 * l_scr[1, rows, :] + w2 * l_scr[2, rows, :]
        o_ref[0, rows, :] = num / den
        return carry

    lax.fori_loop(0, DIL_UNIT // n, merge, 0)


def _dilated(qkv_d, bias):
    b, s, _ = qkv_d.shape
    u = DIL_UNIT
    npair = D_DIL // LANES
    cur = lambda off: pl.BlockSpec((1, u, LANES), lambda bi, g, p: (bi, g, off + p))
    prev = lambda off: pl.BlockSpec((1, u, LANES), lambda bi, g, p: (bi, jnp.maximum(g - 1, 0), off + p))
    return pl.pallas_call(
        _dil_kernel,
        grid=(b, s // u, npair),
        in_specs=[cur(0), cur(npair), prev(npair), cur(2 * npair), prev(2 * npair),
                  pl.BlockSpec((3, 2, DIL_STEPS, 2 * DIL_STEPS), lambda bi, g, p: (0, p, 0, 0))],
        out_specs=pl.BlockSpec((1, u, LANES), lambda bi, g, p: (bi, g, p)),
        out_shape=jax.ShapeDtypeStruct((b, s, D_DIL), F32),
        scratch_shapes=[pltpu.VMEM((2 * u, LANES), F32), pltpu.VMEM((2 * u, LANES), F32),
                        pltpu.VMEM((3, u, LANES), F32), pltpu.VMEM((3, u, LANES), F32),
                        pltpu.VMEM((3, u, LANES), F32)],
        compiler_params=_cparams(("arbitrary", "arbitrary", "arbitrary")),
        name="dilated",
    )(qkv_d, qkv_d, qkv_d, qkv_d, qkv_d, bias)


def _stick_kernel(q_ref, k_ref, v_ref, tri_ref, o_ref,
                  qh_scr, z_scr, w_scr, acc_scr, carry_scr, scale_scr):
    blk = SB_BLOCK
    nsub = SB_QUERY_ROWS // blk
    assert nsub % 2 == 0
    nchain = 2 * nsub
    qi = pl.program_id(2)
    lane = lax.broadcasted_iota(jnp.int32, (blk, LANES), 1)
    head0 = lane < HEAD_DIM
    for sub in range(nsub):
        q = q_ref[0, sub * blk:(sub + 1) * blk, :]
        zero = jnp.zeros_like(q)
        qh_scr[2 * sub] = jnp.where(head0, q, zero)
        qh_scr[2 * sub + 1] = jnp.where(head0, zero, q)
    acc_scr[...] = jnp.zeros_like(acc_scr)
    carry_scr[...] = jnp.zeros_like(carry_scr)
    sign = jnp.int32(-2 ** 31)

    def rows(kb):
        return pl.ds(pl.multiple_of(kb * blk, blk), blk)

    def scores(kb, which, slot):
        kblk = k_ref[0, rows(kb), :]
        for c in which:
            z_scr[slot * nchain + c] = lax.dot_general(
                qh_scr[c], kblk, (((1,), (1,)), ((), ())), preferred_element_type=F32)

    def weights(which, slot, diag_sub):
        causal = (lax.broadcasted_iota(jnp.int32, (blk, blk), 1)
                  < lax.broadcasted_iota(jnp.int32, (blk, blk), 0))
        log_betas, log_keeps, splits = {}, {}, {}
        for c in which:
            z = z_scr[slot * nchain + c]
            neg_abs = lax.bitcast_convert_type(lax.bitcast_convert_type(z, jnp.int32) | sign, F32)
            sp = jnp.log(1.0 + jnp.exp(neg_abs))
            log_beta = jnp.minimum(z, 0.0) - sp
            log_keep = log_beta - z
            if c // 2 == diag_sub:
                log_keep = jnp.where(causal, log_keep, 0.0)
            hi = log_keep.astype(BF16)
            lo = (log_keep - hi.astype(F32)).astype(BF16)
            log_betas[c], log_keeps[c] = log_beta, log_keep
            splits[c] = jnp.concatenate([hi, lo], axis=1)
        betweens = {c: jnp.dot(splits[c], tri_ref[...], preferred_element_type=F32) for c in which}
        for c in which:
            w = jnp.exp(log_betas[c] + betweens[c])
            if c // 2 == diag_sub:
                w = jnp.where(causal, w, 0.0)
            w_scr[c] = w.astype(BF16)
            carry = carry_scr[c]
            scale_scr[c] = jnp.exp(carry)
            carry_scr[c] = carry + (betweens[c][:, 0:1] + log_keeps[c][:, 0:1])

    def accumulate(which, kb):
        vblk = v_ref[0, rows(kb), :]
        for c in which:
            acc_scr[c] = acc_scr[c] + scale_scr[c] * jnp.dot(w_scr[c], vblk, preferred_element_type=F32)

    everyone = list(range(nchain))
    top = nsub * qi + nsub - 1
    first = nsub * qi - 1
    for i in range(nsub):
        which = [c for c in everyone if c // 2 >= nsub - 1 - i]
        scores(top - i, which, 1)
        if i == nsub - 1:
            scores(jnp.maximum(first, 0), everyone, 0)
        weights(which, 1, nsub - 1 - i)
        if i < nsub - 1:
            accumulate(which, top - i)

    def step(i, carry):
        for slot in range(2):
            kb = first - 2 * i - slot
            accumulate(everyone, kb + 1)
            scores(jnp.maximum(kb - 1, 0), everyone, 1 - slot)
            weights(everyone, slot, -1)
        return carry

    lax.fori_loop(0, nsub * qi // 2, step, 0)
    accumulate(everyone, 0)
    for sub in range(nsub):
        o_ref[0, sub * blk:(sub + 1) * blk, :] = jnp.where(head0, acc_scr[2 * sub], acc_scr[2 * sub + 1])


def _stick(q_s, k_s, v_s):
    b, s, _ = q_s.shape
    blk = SB_BLOCK
    qrows = SB_QUERY_ROWS
    nchain = 2 * qrows // blk
    tri = np.tril(np.ones((blk, blk), np.float32), -1)
    tri2 = jnp.asarray(np.concatenate([tri, tri], axis=0), BF16)
    full = pl.BlockSpec((1, s, LANES), lambda bi, p, i: (bi, 0, p))
    return pl.pallas_call(
        _stick_kernel,
        grid=(b, D_SB // LANES, s // qrows),
        in_specs=[pl.BlockSpec((1, qrows, LANES), lambda bi, p, i: (bi, i, p)), full, full,
                  pl.BlockSpec((2 * blk, blk), lambda bi, p, i: (0, 0))],
        out_specs=pl.BlockSpec((1, qrows, LANES), lambda bi, p, i: (bi, i, p)),
        out_shape=jax.ShapeDtypeStruct((b, s, D_SB), F32),
        scratch_shapes=[pltpu.VMEM((nchain, blk, LANES), BF16),
                        pltpu.VMEM((2 * nchain, blk, blk), F32),
                        pltpu.VMEM((nchain, blk, blk), BF16),
                        pltpu.VMEM((nchain, blk, LANES), F32),
                        pltpu.VMEM((nchain, blk, 1), F32),
                        pltpu.VMEM((nchain, blk, 1), F32)],
        compiler_params=_cparams(("arbitrary", "arbitrary", "arbitrary")),
        name="stick",
    )(q_s, k_s, v_s, tri2)


def _postmix_kernel(x_ref, od_ref, os_ref, gd_ref, gs_ref, wout_ref, gate_ref, shift_ref, scale_ref,
                    gffn_ref, wr_ref, tril_ref, triu_ref,
                    x1_ref, h2_ref, route_ref, cnt_ref, base_ref, carry_scr):
    tm = POST_ROWS

    @pl.when(jnp.logical_and(pl.program_id(0) == 0, pl.program_id(1) == 0))
    def _():
        carry_scr[...] = jnp.zeros_like(carry_scr)

    mixed = jnp.concatenate([_rms(od_ref[0], gd_ref[...]), _rms(os_ref[0], gs_ref[...])], axis=-1)
    proj = jnp.dot(mixed.astype(BF16), wout_ref[...], preferred_element_type=F32)
    x1 = x_ref[0] + gate_ref[0] * proj
    x1_ref[0] = x1
    h2 = _rms(x1, gffn_ref[...]) * (1.0 + scale_ref[0]) + shift_ref[0]
    logits = jnp.dot(h2, wr_ref[...], precision=lax.Precision.HIGHEST, preferred_element_type=F32)

    lane = lax.broadcasted_iota(jnp.int32, (tm, LANES), 1)
    big = jnp.int32(LANES)
    lmax = lambda v: jnp.max(v, axis=-1, keepdims=True)
    lmin = lambda v: jnp.min(v, axis=-1, keepdims=True)
    lsum = lambda v: jnp.sum(v, axis=-1, keepdims=True)

    gmask = lane < N_GROUPS
    gl = jnp.where(gmask, logits, NEG_INF)
    gmx = lmax(gl)
    group = lmin(jnp.where(jnp.logical_and(gmask, gl == gmx), lane, big))
    group_gate = 1.0 / lsum(jnp.exp(gl - gmx))
    lo = ROUTE_LANE0 + group * EXPERTS_PER_GROUP
    emask = jnp.logical_and(lane >= lo, lane < lo + EXPERTS_PER_GROUP)
    el = jnp.where(emask, logits, NEG_INF)
    l1 = lmax(el)
    i1 = lmin(jnp.where(el == l1, lane, big))
    el2 = jnp.where(lane == i1, NEG_INF, el)
    l2 = lmax(el2)
    i2 = lmin(jnp.where(el2 == l2, lane, big))
    r = jnp.exp(l2 - l1)
    w1 = group_gate / (1.0 + r)
    w2 = group_gate * r / (1.0 + r)

    is1 = lane == i1
    is2 = lane == i2
    oh = jnp.where(is1, 1.0, jnp.where(is2, 1.0, 0.0))
    earlier = jnp.dot(tril_ref[...], oh.astype(BF16), preferred_element_type=F32)
    runs = jnp.floor((jnp.sum(oh, axis=0, keepdims=True) + (SUBLANES - 1.0)) * (1.0 / SUBLANES))
    run_off = jnp.dot(jnp.broadcast_to(runs, (SUBLANES, LANES)).astype(BF16), triu_ref[...],
                      preferred_element_type=F32)[0:1]
    pos = earlier + run_off * SUBLANES
    slot1 = lsum(jnp.where(is1, pos, 0.0))
    slot2 = lsum(jnp.where(is2, pos, 0.0))
    cnt = runs * SUBLANES
    cnt_ref[0] = cnt
    base_ref[0] = carry_scr[...]
    carry_scr[...] = carry_scr[...] + cnt

    h2_ref[0] = h2.astype(BF16)
    route_ref[0] = jnp.where(lane == 0, slot1, jnp.where(lane == 1, slot2,
                                                         jnp.where(lane == 2, w1, jnp.where(lane == 3, w2, 0.0))))


def _postmix(x, o_dil, o_sb, g_dil, g_sb, w_out_bf16, gate, shift, scale, g_ffn, w_router):
    b, s, d = x.shape
    tm = POST_ROWS
    nt = s // tm
    tril = jnp.asarray(np.tril(np.ones((tm, tm), np.float32), -1), BF16)
    triu = jnp.asarray(np.triu(np.ones((LANES, LANES), np.float32), 1), BF16)
    row = lambda w: pl.BlockSpec((1, tm, w), lambda bi, i: (bi, i, 0))
    vec = lambda w: pl.BlockSpec((1, w), lambda bi, i: (0, 0))
    mod_spec = pl.BlockSpec((1, 1, d), lambda bi, i: (bi, 0, 0))
    tile_vec = pl.BlockSpec((1, 1, LANES), lambda bi, i: (bi * nt + i, 0, 0))
    return pl.pallas_call(
        _postmix_kernel,
        grid=(b, nt),
        in_specs=[row(d), row(D_DIL), row(D_SB), vec(D_DIL), vec(D_SB),
                  pl.BlockSpec((d, d), lambda bi, i: (0, 0)),
                  mod_spec, mod_spec, mod_spec, vec(d),
                  pl.BlockSpec((d, LANES), lambda bi, i: (0, 0)),
                  pl.BlockSpec((tm, tm), lambda bi, i: (0, 0)),
                  pl.BlockSpec((LANES, LANES), lambda bi, i: (0, 0))],
        out_specs=[row(d), row(d), row(LANES), tile_vec, tile_vec],
        out_shape=[jax.ShapeDtypeStruct((b, s, d), F32),
                   jax.ShapeDtypeStruct((b, s, d), BF16),
                   jax.ShapeDtypeStruct((b, s, LANES), F32),
                   jax.ShapeDtypeStruct((b * nt, 1, LANES), F32),
                   jax.ShapeDtypeStruct((b * nt, 1, LANES), F32)],
        scratch_shapes=[pltpu.VMEM((1, LANES), F32)],
        compiler_params=_cparams(("arbitrary", "arbitrary")),
        name="postmix",
    )(x, o_dil, o_sb, g_dil.reshape(1, -1), g_sb.reshape(1, -1), w_out_bf16, gate, shift, scale,
      g_ffn.reshape(1, d), w_router, tril, triu)


def _for_each_run_piece(tile, start_ref, cnt_ref, base_ref, fn):
    def body(e, off):
        c = cnt_ref[tile * N_EXPERTS + e]
        sorted0 = start_ref[e] + base_ref[tile * N_EXPERTS + e]
        for k in range(3, 10):
            p = 1 << k

            @pl.when((c & p) != 0)
            def _(p=p):
                done = c - (c & (2 * p - 1))
                fn(pl.multiple_of(off + done, SUBLANES), pl.multiple_of(sorted0 + done, SUBLANES), p)
        return off + c

    return lax.fori_loop(0, N_EXPERTS, body, 0)


def _sort_kernel(start_ref, cnt_ref, base_ref, h2_ref, route_ref, buf_in_ref, buf_ref, xs_scr, sem):
    del buf_in_ref
    tm = POST_ROWS
    lt = LOCAL_ROWS
    d = h2_ref.shape[2]
    lane = lax.broadcasted_iota(jnp.int32, (tm, LANES), 1)
    route = route_ref[0]
    w1 = jnp.sum(jnp.where(lane == 2, route, 0.0), axis=-1, keepdims=True)
    w2 = jnp.sum(jnp.where(lane == 3, route, 0.0), axis=-1, keepdims=True)

    def pieces(w):
        hi, mid, lw = _split3(w)
        return jnp.where(lane == 0, hi.astype(F32),
                         jnp.where(lane == 1, mid.astype(F32),
                                   jnp.where(lane == 2, lw.astype(F32), 0.0))).astype(BF16)

    route_t = route.T
    s1 = route_t[0:1, :].astype(jnp.int32)
    s2 = route_t[1:2, :].astype(jnp.int32)
    row = lax.broadcasted_iota(jnp.int32, (lt, tm), 0)
    p1 = jnp.where(row == s1, 1.0, 0.0)
    p2 = jnp.where(row == s2, 1.0, 0.0)
    xs_scr[:, 0:d] = jnp.dot((p1 + p2).astype(BF16), h2_ref[0], preferred_element_type=F32)
    xs_scr[:, d:] = (jnp.dot(p1.astype(BF16), pieces(w1), preferred_element_type=F32)
                     + jnp.dot(p2.astype(BF16), pieces(w2), preferred_element_type=F32))

    tile = pl.program_id(0)

    def piece(lrow, srow, rows):
        return pltpu.make_async_copy(xs_scr.at[pl.ds(lrow, rows)], buf_ref.at[pl.ds(srow, rows)], sem)

    _for_each_run_piece(tile, start_ref, cnt_ref, base_ref, lambda *a: piece(*a).start())
    _for_each_run_piece(tile, start_ref, cnt_ref, base_ref, lambda *a: piece(*a).wait())


def _dispatch(pad_start, cnt, base, h2, route, buf_zeros):
    b, s, d = h2.shape
    tm = POST_ROWS
    nt = s // tm
    any_spec = pl.BlockSpec(memory_space=pl.ANY)
    return pl.pallas_call(
        _sort_kernel,
        grid_spec=pltpu.PrefetchScalarGridSpec(
            num_scalar_prefetch=3, grid=(b * nt,),
            in_specs=[pl.BlockSpec((1, tm, d), lambda t, *_: (t // nt, t % nt, 0)),
                      pl.BlockSpec((1, tm, LANES), lambda t, *_: (t // nt, t % nt, 0)),
                      any_spec],
            out_specs=any_spec,
            scratch_shapes=[pltpu.VMEM((LOCAL_ROWS, d + LANES), F32), pltpu.SemaphoreType.DMA(())]),
        out_shape=jax.ShapeDtypeStruct(buf_zeros.shape, buf_zeros.dtype),
        input_output_aliases={5: 0},
        compiler_params=_cparams(("arbitrary",)),
        name="dispatch",
    )(pad_start, cnt, base, h2, route, buf_zeros)


def _expert_kernel(be_ref, nused_ref, x_ref, wg_ref, wu_ref, wd_ref, y_ref):
    del be_ref
    d = y_ref.shape[1]

    @pl.when(pl.program_id(0) < nused_ref[0])
    def _():
        xb = x_ref[:, 0:d].astype(BF16)
        weight = jnp.sum(x_ref[:, d:], axis=-1, keepdims=True)
        gate = jnp.dot(xb, wg_ref[0], preferred_element_type=F32)
        up = jnp.dot(xb, wu_ref[0], preferred_element_type=F32)
        act = gate / (1.0 + jnp.exp(-gate)) * up
        y_ref[...] = jnp.dot(act.astype(BF16), wd_ref[0], preferred_element_type=F32) * weight

    @pl.when(pl.program_id(0) >= nused_ref[0])
    def _():
        y_ref[...] = jnp.zeros_like(y_ref)


def _experts(block_expert, n_used, buf, wg, wu, wd):
    cap, dw = buf.shape
    d, f = wg.shape[1], wg.shape[2]
    bm = EXPERT_ROWS
    return pl.pallas_call(
        _expert_kernel,
        grid_spec=pltpu.PrefetchScalarGridSpec(
            num_scalar_prefetch=2, grid=(cap // bm,),
            in_specs=[pl.BlockSpec((bm, dw), lambda i, be, nu: (i, 0)),
                      pl.BlockSpec((1, d, f), lambda i, be, nu: (be[i], 0, 0)),
                      pl.BlockSpec((1, d, f), lambda i, be, nu: (be[i], 0, 0)),
                      pl.BlockSpec((1, f, d), lambda i, be, nu: (be[i], 0, 0))],
            out_specs=pl.BlockSpec((bm, d), lambda i, be, nu: (i, 0))),
        out_shape=jax.ShapeDtypeStruct((cap, d), F32),
        compiler_params=_cparams(("arbitrary",)),
        name="experts",
    )(block_expert, n_used, buf, wg, wu, wd)


def _combine_kernel(start_ref, cnt_ref, base_ref, x1_ref, route_ref, gate_ref, g_ref, y_hbm_ref, o_ref,
                    y_scr, sem):
    tm = POST_ROWS
    lt = LOCAL_ROWS
    tile = pl.program_id(0)

    def piece(lrow, srow, rows):
        return pltpu.make_async_copy(y_hbm_ref.at[pl.ds(srow, rows)], y_scr.at[pl.ds(lrow, rows)], sem)

    _for_each_run_piece(tile, start_ref, cnt_ref, base_ref, lambda *a: piece(*a).start())
    lane = lax.broadcasted_iota(jnp.int32, (tm, LANES), 1)
    route = route_ref[0]
    s1 = jnp.sum(jnp.where(lane == 0, route, 0.0), axis=-1, keepdims=True).astype(jnp.int32)
    s2 = jnp.sum(jnp.where(lane == 1, route, 0.0), axis=-1, keepdims=True).astype(jnp.int32)
    col = lax.broadcasted_iota(jnp.int32, (tm, lt), 1)
    pick = jnp.where(col == s1, 1.0, jnp.where(col == s2, 1.0, 0.0)).astype(BF16)
    used = _for_each_run_piece(tile, start_ref, cnt_ref, base_ref, lambda *a: piece(*a).wait())
    live = lax.broadcasted_iota(jnp.int32, (lt, 1), 0) < used
    hi, mid, lo = _split3(jnp.where(live, y_scr[...], 0.0))
    y = jnp.dot(jnp.concatenate([pick, pick, pick], axis=1), jnp.concatenate([hi, mid, lo], axis=0),
                preferred_element_type=F32)
    o_ref[0] = _rms(x1_ref[0] + gate_ref[0] * y, g_ref[...])


def _combine(pad_start, cnt, base, x1, y_sorted, route, gate, g_final):
    b, s, d = x1.shape
    tm = POST_ROWS
    nt = s // tm
    return pl.pallas_call(
        _combine_kernel,
        grid_spec=pltpu.PrefetchScalarGridSpec(
            num_scalar_prefetch=3, grid=(b * nt,),
            in_specs=[pl.BlockSpec((1, tm, d), lambda t, *_: (t // nt, t % nt, 0)),
                      pl.BlockSpec((1, tm, LANES), lambda t, *_: (t // nt, t % nt, 0)),
                      pl.BlockSpec((1, 1, d), lambda t, *_: (t // nt, 0, 0)),
                      pl.BlockSpec((1, d), lambda t, *_: (0, 0)),
                      pl.BlockSpec(memory_space=pl.ANY)],
            out_specs=pl.BlockSpec((1, tm, d), lambda t, *_: (t // nt, t % nt, 0)),
            scratch_shapes=[pltpu.VMEM((LOCAL_ROWS, d), F32), pltpu.SemaphoreType.DMA(())]),
        out_shape=jax.ShapeDtypeStruct((b, s, d), F32),
        compiler_params=_cparams(("arbitrary",)),
        name="combine",
    )(pad_start, cnt, base, x1, route, gate, g_final.reshape(1, d), y_sorted)


def kernel(x, c, w_ada, b_ada, g_mix, w_in, g_dil_out, g_sb_out, w_out, g_ffn,
           w_group, w_expert, w_gate, w_up, w_down, g_final):
    b, s, d = x.shape
    depth = w_ada.shape[0]
    assert s % DIL_UNIT == 0 and d == D_DIL + D_SB
    assert depth == 1, "the final rmsnorm is fused into the last layer's combine step"
    n = b * s
    ntiles = n // POST_ROWS
    bias = jnp.asarray(_dilated_bias())
    for layer in range(depth):
        mod = _ada(c, w_ada[layer], b_ada[layer])
        shift_mix, scale_mix, gate_mix, shift_ffn, scale_ffn, gate_ffn = (
            m.reshape(b, 1, d) for m in jnp.split(mod, 6, axis=-1))

        qkv_d, q_s, k_s, v_s = _premix(x, shift_mix, scale_mix, g_mix[layer], w_in[layer].astype(BF16))
        o_dil = _dilated(qkv_d, bias)
        o_sb = _stick(q_s, k_s, v_s)

        w_router = jnp.concatenate(
            [w_group[layer], w_expert[layer],
             jnp.zeros((d, LANES - N_GROUPS - N_EXPERTS), F32)], axis=1)
        x1, h2, route, cnt, base = _postmix(
            x, o_dil, o_sb, g_dil_out[layer], g_sb_out[layer], w_out[layer].astype(BF16),
            gate_mix, shift_ffn, scale_ffn, g_ffn[layer], w_router)

        bm = EXPERT_ROWS
        cnt = cnt[:, 0, ROUTE_LANE0:ROUTE_LANE0 + N_EXPERTS].astype(jnp.int32)
        base = base[:, 0, ROUTE_LANE0:ROUTE_LANE0 + N_EXPERTS].astype(jnp.int32)
        total = base[-1] + cnt[-1]
        cnt = cnt.reshape(-1)
        base = base.reshape(-1)
        padded = (total + bm - 1) // bm * bm
        pad_end = jnp.cumsum(padded)
        pad_start = (pad_end - padded).astype(jnp.int32)
        cap = -(-(2 * n + (SUBLANES - 1) * N_EXPERTS * ntiles) // bm) * bm + N_EXPERTS * bm
        n_blocks = cap // bm
        block_expert = jnp.minimum(
            jnp.sum(pad_end[None, :] <= (jnp.arange(n_blocks) * bm)[:, None], axis=1),
            N_EXPERTS - 1).astype(jnp.int32)
        n_used = (pad_end[-1:] // bm).astype(jnp.int32)

        buf = _dispatch(pad_start, cnt, base, h2, route, jnp.zeros((cap, d + LANES), F32))
        y_sorted = _experts(block_expert, n_used, buf, w_gate[layer].astype(BF16),
                            w_up[layer].astype(BF16), w_down[layer].astype(BF16))
        x = _combine(pad_start, cnt, base, x1, y_sorted, route, gate_ffn, g_final)
    return x
```

```python
import functools

import numpy as np
import jax
import jax.numpy as jnp
from jax import lax
from jax.experimental import pallas as pl
from jax.experimental.pallas import tpu as pltpu

HEAD_DIM = 64
N_HEADS_DIL = 8
N_HEADS_SB = 8
D_DIL = N_HEADS_DIL * HEAD_DIM
D_SB = N_HEADS_SB * HEAD_DIM
DILATION_PATTERNS = ((128, 1), (512, 4), (2048, 16))
N_GROUPS = 4
EXPERTS_PER_GROUP = 8
N_EXPERTS = N_GROUPS * EXPERTS_PER_GROUP
NORM_EPS = 1e-6

LANES = 128
SUBLANES = 8
DIL_STEPS = 128
DIL_UNIT = 2048
DIL_TILES_PER_TRIP = 4
SB_BLOCK = 256
SB_QUERY_ROWS = 512
PRE_ROWS = 512
POST_ROWS = 512
LOCAL_ROWS = 2 * POST_ROWS + 256
EXPERT_ROWS = 512
ROUTE_LANE0 = N_GROUPS
VMEM_LIMIT = 56 * 1024 * 1024

F32 = jnp.float32
BF16 = jnp.bfloat16
NEG_INF = float("-inf")


def _cparams(sem):
    return pltpu.CompilerParams(dimension_semantics=sem, vmem_limit_bytes=VMEM_LIMIT)


def _rms(v, g):
    return v * lax.rsqrt(jnp.mean(v * v, axis=-1, keepdims=True) + NORM_EPS) * g


def _split3(v):
    hi = v.astype(BF16)
    r = v - hi.astype(F32)
    mid = r.astype(BF16)
    lo = (r - mid.astype(F32)).astype(BF16)
    return hi, mid, lo


def _ada_kernel(c_ref, w_ref, b_ref, o_ref):
    c = c_ref[...]
    cond = c / (1.0 + jnp.exp(-c))
    o_ref[...] = jnp.dot(cond, w_ref[...], precision=lax.Precision.HIGHEST,
                         preferred_element_type=F32) + b_ref[...]


def _ada(c, w_ada, b_ada):
    b, d = c.shape
    n = w_ada.shape[1]
    return pl.pallas_call(
        _ada_kernel,
        grid=(n // d,),
        in_specs=[pl.BlockSpec((b, d), lambda j: (0, 0)),
                  pl.BlockSpec((d, d), lambda j: (0, j)),
                  pl.BlockSpec((1, d), lambda j: (0, j))],
        out_specs=pl.BlockSpec((b, d), lambda j: (0, j)),
        out_shape=jax.ShapeDtypeStruct((b, n), F32),
        compiler_params=_cparams(("arbitrary",)),
        name="ada",
    )(c, w_ada, b_ada.reshape(1, n))


def _premix_kernel(x_ref, shift_ref, scale_ref, g_ref, w_ref, qkvd_ref, qs_ref, ks_ref, vs_ref):
    h = _rms(x_ref[0], g_ref[...]) * (1.0 + scale_ref[0]) + shift_ref[0]
    hb = h.astype(BF16)
    scale = HEAD_DIM ** -0.5
    for j in range(6):
        r = jnp.dot(hb, w_ref[:, j * 512:(j + 1) * 512], preferred_element_type=F32)
        if j == 0:
            qkvd_ref[0, :, 0:512] = r * scale
        elif j < 3:
            qkvd_ref[0, :, j * 512:(j + 1) * 512] = r
        elif j == 3:
            qs_ref[0] = (r * scale).astype(BF16)
        elif j == 4:
            ks_ref[0] = r.astype(BF16)
        else:
            vs_ref[0] = r.astype(BF16)


def _premix(x, shift, scale, g_mix, w_in_bf16):
    b, s, d = x.shape
    tm = PRE_ROWS
    mod_spec = pl.BlockSpec((1, 1, d), lambda bi, i: (bi, 0, 0))
    sb_spec = pl.BlockSpec((1, tm, D_SB), lambda bi, i: (bi, i, 0))
    return pl.pallas_call(
        _premix_kernel,
        grid=(b, s // tm),
        in_specs=[pl.BlockSpec((1, tm, d), lambda bi, i: (bi, i, 0)),
                  mod_spec, mod_spec,
                  pl.BlockSpec((1, d), lambda bi, i: (0, 0)),
                  pl.BlockSpec((d, 3 * (D_DIL + D_SB)), lambda bi, i: (0, 0))],
        out_specs=[pl.BlockSpec((1, tm, 3 * D_DIL), lambda bi, i: (bi, i, 0)),
                   sb_spec, sb_spec, sb_spec],
        out_shape=[jax.ShapeDtypeStruct((b, s, 3 * D_DIL), F32),
                   jax.ShapeDtypeStruct((b, s, D_SB), BF16),
                   jax.ShapeDtypeStruct((b, s, D_SB), BF16),
                   jax.ShapeDtypeStruct((b, s, D_SB), BF16)],
        compiler_params=_cparams(("arbitrary", "arbitrary")),
        name="premix",
    )(x, shift, scale, g_mix.reshape(1, d), w_in_bf16)


def _dilated_bias():
    n = DIL_STEPS
    slopes = np.array([2.0 ** (-8.0 * (i + 1) / N_HEADS_DIL) for i in range(N_HEADS_DIL)], dtype=np.float32)
    steps = np.arange(n)[:, None] + n - np.arange(2 * n)[None, :]
    valid = (steps >= 0) & (steps <= n)
    out = []
    for _, dilation in DILATION_PATTERNS:
        bias = -slopes[:, None, None] * (steps * dilation).astype(np.float32)[None]
        out.append(np.where(valid[None], bias, -np.inf).astype(np.float32))
    return np.stack(out)


def _dil_kernel(q_ref, kc_ref, kp_ref, vc_ref, vp_ref, bias_ref, o_ref,
                kext, vext, u_scr, m_scr, l_scr):
    n = DIL_STEPS
    g = pl.program_id(1)
    kext[0:DIL_UNIT, :] = kp_ref[0]
    kext[DIL_UNIT:2 * DIL_UNIT, :] = kc_ref[0]
    vext[0:DIL_UNIT, :] = vp_ref[0]
    vext[DIL_UNIT:2 * DIL_UNIT, :] = vc_ref[0]
    lane = lax.broadcasted_iota(jnp.int32, (n, LANES), 1)
    head0 = lane < HEAD_DIM
    col = lax.broadcasted_iota(jnp.int32, (n, 2 * n), 1)

    for p, (_, dil) in enumerate(DILATION_PATTERNS):
        unit = n * dil

        def tiles(it, carry, p=p, dil=dil, unit=unit):
            rows_of, vvs, deads, ss = [], [], [], []
            for t in range(DIL_TILES_PER_TRIP):
                ti = it * DIL_TILES_PER_TRIP + t
                j = ti // dil
                r = ti % dil
                qstart = j * unit + r
                kstart = DIL_UNIT + qstart - unit
                if dil == 1:
                    rows_of.append(pl.ds(qstart, n))
                    krows = pl.ds(kstart, 2 * n)
                else:
                    rows_of.append(pl.ds(qstart, n, stride=dil))
                    krows = pl.ds(kstart, 2 * n, stride=dil)
                q = q_ref[0, rows_of[t], :]
                kk = kext[krows, :].astype(BF16)
                vvs.append(vext[krows, :].astype(BF16))
                deads.append(jnp.where(jnp.logical_and(g == 0, j == 0), n, 0))
                for h in range(2):
                    qh = jnp.where(head0 if h == 0 else jnp.logical_not(head0), q, 0.0).astype(BF16)
                    ss.append(lax.dot_general(qh, kk, (((1,), (1,)), ((), ())), preferred_element_type=F32))
            ms, ls, pes = [], [], []
            for t in range(DIL_TILES_PER_TRIP):
                for h in range(2):
                    logits = jnp.where(col < deads[t], NEG_INF, ss[2 * t + h] + bias_ref[p, h])
                    m = jnp.max(logits, axis=-1, keepdims=True)
                    pe = jnp.exp(logits - m)
                    ls.append(jnp.sum(pe, axis=-1, keepdims=True))
                    ms.append(m)
                    pes.append(pe.astype(BF16))
            us = [jnp.dot(pes[2 * t + h], vvs[t], preferred_element_type=F32)
                  for t in range(DIL_TILES_PER_TRIP) for h in range(2)]
            for t in range(DIL_TILES_PER_TRIP):
                u_scr[p, rows_of[t], :] = jnp.where(head0, us[2 * t], us[2 * t + 1])
                m_scr[p, rows_of[t], :] = jnp.where(head0, ms[2 * t], ms[2 * t + 1])
                l_scr[p, rows_of[t], :] = jnp.where(head0, ls[2 * t], ls[2 * t + 1])
            return carry

        lax.fori_loop(0, DIL_UNIT // n // DIL_TILES_PER_TRIP, tiles, 0)

    def merge(i, carry):
        rows = pl.ds(pl.multiple_of(i * n, n), n)
        m0, m1, m2 = m_scr[0, rows, :], m_scr[1, rows, :], m_scr[2, rows, :]
        mx = jnp.maximum(jnp.maximum(m0, m1), m2)
        w0, w1, w2 = jnp.exp(m0 - mx), jnp.exp(m1 - mx), jnp.exp(m2 - mx)
        num = w0 * u_scr[0, rows, :] + w1 * u_scr[1, rows, :] + w2 * u_scr[2, rows, :]
        den = w0 * l_scr[0, rows, :] + w1 * l_scr[1, rows, :] + w2 * l_scr[2, rows, :]
        o_ref[0, rows, :] = num / den
        return carry

    lax.fori_loop(0, DIL_UNIT // n, merge, 0)


def _dilated(qkv_d, bias):
    b, s, _ = qkv_d.shape
    u = DIL_UNIT
    npair = D_DIL // LANES
    cur = lambda off: pl.BlockSpec((1, u, LANES), lambda bi, g, p: (bi, g, off + p))
    prev = lambda off: pl.BlockSpec((1, u, LANES), lambda bi, g, p: (bi, jnp.maximum(g - 1, 0), off + p))
    return pl.pallas_call(
        _dil_kernel,
        grid=(b, s // u, npair),
        in_specs=[cur(0), cur(npair), prev(npair), cur(2 * npair), prev(2 * npair),
                  pl.BlockSpec((3, 2, DIL_STEPS, 2 * DIL_STEPS), lambda bi, g, p: (0, p, 0, 0))],
        out_specs=pl.BlockSpec((1, u, LANES), lambda bi, g, p: (bi, g, p)),
        out_shape=jax.ShapeDtypeStruct((b, s, D_DIL), F32),
        scratch_shapes=[pltpu.VMEM((2 * u, LANES), F32), pltpu.VMEM((2 * u, LANES), F32),
                        pltpu.VMEM((3, u, LANES), F32), pltpu.VMEM((3, u, LANES), F32),
                        pltpu.VMEM((3, u, LANES), F32)],
        compiler_params=_cparams(("arbitrary", "arbitrary", "arbitrary")),
        name="dilated",
    )(qkv_d, qkv_d, qkv_d, qkv_d, qkv_d, bias)


def _stick_kernel(q_ref, k_ref, v_ref, tri_ref, o_ref,
                  qh_scr, z_scr, w_scr, acc_scr, carry_scr, scale_scr):
    blk = SB_BLOCK
    nsub = SB_QUERY_ROWS // blk
    assert nsub % 2 == 0
    nchain = 2 * nsub
    qi = pl.program_id(2)
    lane = lax.broadcasted_iota(jnp.int32, (blk, LANES), 1)
    head0 = lane < HEAD_DIM
    for sub in range(nsub):
        q = q_ref[0, sub * blk:(sub + 1) * blk, :]
        zero = jnp.zeros_like(q)
        qh_scr[2 * sub] = jnp.where(head0, q, zero)
        qh_scr[2 * sub + 1] = jnp.where(head0, zero, q)
    acc_scr[...] = jnp.zeros_like(acc_scr)
    carry_scr[...] = jnp.zeros_like(carry_scr)
    sign = jnp.int32(-2 ** 31)

    def rows(kb):
        return pl.ds(pl.multiple_of(kb * blk, blk), blk)

    def scores(kb, which, slot):
        kblk = k_ref[0, rows(kb), :]
        for c in which:
            z_scr[slot * nchain + c] = lax.dot_general(
                qh_scr[c], kblk, (((1,), (1,)), ((), ())), preferred_element_type=F32)

    def weights(which, slot, diag_sub):
        causal = (lax.broadcasted_iota(jnp.int32, (blk, blk), 1)
                  < lax.broadcasted_iota(jnp.int32, (blk, blk), 0))
        log_betas, log_keeps, splits = {}, {}, {}
        for c in which:
            z = z_scr[slot * nchain + c]
            neg_abs = lax.bitcast_convert_type(lax.bitcast_convert_type(z, jnp.int32) | sign, F32)
            sp = jnp.log(1.0 + jnp.exp(neg_abs))
            log_beta = jnp.minimum(z, 0.0) - sp
            log_keep = log_beta - z
            if c // 2 == diag_sub:
                log_keep = jnp.where(causal, log_keep, 0.0)
            hi = log_keep.astype(BF16)
            lo = (log_keep - hi.astype(F32)).astype(BF16)
            log_betas[c], log_keeps[c] = log_beta, log_keep
            splits[c] = jnp.concatenate([hi, lo], axis=1)
        betweens = {c: jnp.dot(splits[c], tri_ref[...], preferred_element_type=F32) for c in which}
        for c in which:
            w = jnp.exp(log_betas[c] + betweens[c])
            if c // 2 == diag_sub:
                w = jnp.where(causal, w, 0.0)
            w_scr[c] = w.astype(BF16)
            carry = carry_scr[c]
            scale_scr[c] = jnp.exp(carry)
            carry_scr[c] = carry + (betweens[c][:, 0:1] + log_keeps[c][:, 0:1])

    def accumulate(which, kb):
        vblk = v_ref[0, rows(kb), :]
        for c in which:
            acc_scr[c] = acc_scr[c] + scale_scr[c] * jnp.dot(w_scr[c], vblk, preferred_element_type=F32)

    everyone = list(range(nchain))
    top = nsub * qi + nsub - 1
    first = nsub * qi - 1
    for i in range(nsub):
        which = [c for c in everyone if c // 2 >= nsub - 1 - i]
        scores(top - i, which, 1)
        if i == nsub - 1:
            scores(jnp.maximum(first, 0), everyone, 0)
        weights(which, 1, nsub - 1 - i)
        if i < nsub - 1:
            accumulate(which, top - i)

    def step(i, carry):
        for slot in range(2):
            kb = first - 2 * i - slot
            accumulate(everyone, kb + 1)
            scores(jnp.maximum(kb - 1, 0), everyone, 1 - slot)
            weights(everyone, slot, -1)
        return carry

    lax.fori_loop(0, nsub * qi // 2, step, 0)
    accumulate(everyone, 0)
    for sub in range(nsub):
        o_ref[0, sub * blk:(sub + 1) * blk, :] = jnp.where(head0, acc_scr[2 * sub], acc_scr[2 * sub + 1])


def _stick(q_s, k_s, v_s):
    b, s, _ = q_s.shape
    blk = SB_BLOCK
    qrows = SB_QUERY_ROWS
    nchain = 2 * qrows // blk
    tri = np.tril(np.ones((blk, blk), np.float32), -1)
    tri2 = jnp.asarray(np.concatenate([tri, tri], axis=0), BF16)
    full = pl.BlockSpec((1, s, LANES), lambda bi, p, i: (bi, 0, p))
    return pl.pallas_call(
        _stick_kernel,
        grid=(b, D_SB // LANES, s // qrows),
        in_specs=[pl.BlockSpec((1, qrows, LANES), lambda bi, p, i: (bi, i, p)), full, full,
                  pl.BlockSpec((2 * blk, blk), lambda bi, p, i: (0, 0))],
        out_specs=pl.BlockSpec((1, qrows, LANES), lambda bi, p, i: (bi, i, p)),
        out_shape=jax.ShapeDtypeStruct((b, s, D_SB), F32),
        scratch_shapes=[pltpu.VMEM((nchain, blk, LANES), BF16),
                        pltpu.VMEM((2 * nchain, blk, blk), F32),
                        pltpu.VMEM((nchain, blk, blk), BF16),
                        pltpu.VMEM((nchain, blk, LANES), F32),
                        pltpu.VMEM((nchain, blk, 1), F32),
                        pltpu.VMEM((nchain, blk, 1), F32)],
        compiler_params=_cparams(("arbitrary", "arbitrary", "arbitrary")),
        name="stick",
    )(q_s, k_s, v_s, tri2)


def _postmix_kernel(x_ref, od_ref, os_ref, gd_ref, gs_ref, wout_ref, gate_ref, shift_ref, scale_ref,
                    gffn_ref, wr_ref, tril_ref, triu_ref,
                    x1_ref, h2_ref, route_ref, cnt_ref, base_ref, carry_scr):
    tm = POST_ROWS

    @pl.when(jnp.logical_and(pl.program_id(0) == 0, pl.program_id(1) == 0))
    def _():
        carry_scr[...] = jnp.zeros_like(carry_scr)

    mixed = jnp.concatenate([_rms(od_ref[0], gd_ref[...]), _rms(os_ref[0], gs_ref[...])], axis=-1)
    proj = jnp.dot(mixed.astype(BF16), wout_ref[...], preferred_element_type=F32)
    x1 = x_ref[0] + gate_ref[0] * proj
    x1_ref[0] = x1
    h2 = _rms(x1, gffn_ref[...]) * (1.0 + scale_ref[0]) + shift_ref[0]
    logits = jnp.dot(h2, wr_ref[...], precision=lax.Precision.HIGHEST, preferred_element_type=F32)

    lane = lax.broadcasted_iota(jnp.int32, (tm, LANES), 1)
    big = jnp.int32(LANES)
    lmax = lambda v: jnp.max(v, axis=-1, keepdims=True)
    lmin = lambda v: jnp.min(v, axis=-1, keepdims=True)
    lsum = lambda v: jnp.sum(v, axis=-1, keepdims=True)

    gmask = lane < N_GROUPS
    gl = jnp.where(gmask, logits, NEG_INF)
    gmx = lmax(gl)
    group = lmin(jnp.where(jnp.logical_and(gmask, gl == gmx), lane, big))
    group_gate = 1.0 / lsum(jnp.exp(gl - gmx))
    lo = ROUTE_LANE0 + group * EXPERTS_PER_GROUP
    emask = jnp.logical_and(lane >= lo, lane < lo + EXPERTS_PER_GROUP)
    el = jnp.where(emask, logits, NEG_INF)
    l1 = lmax(el)
    i1 = lmin(jnp.where(el == l1, lane, big))
    el2 = jnp.where(lane == i1, NEG_INF, el)
    l2 = lmax(el2)
    i2 = lmin(jnp.where(el2 == l2, lane, big))
    r = jnp.exp(l2 - l1)
    w1 = group_gate / (1.0 + r)
    w2 = group_gate * r / (1.0 + r)

    is1 = lane == i1
    is2 = lane == i2
    oh = jnp.where(is1, 1.0, jnp.where(is2, 1.0, 0.0))
    earlier = jnp.dot(tril_ref[...], oh.astype(BF16), preferred_element_type=F32)
    runs = jnp.floor((jnp.sum(oh, axis=0, keepdims=True) + (SUBLANES - 1.0)) * (1.0 / SUBLANES))
    run_off = jnp.dot(jnp.broadcast_to(runs, (SUBLANES, LANES)).astype(BF16), triu_ref[...],
                      preferred_element_type=F32)[0:1]
    pos = earlier + run_off * SUBLANES
    slot1 = lsum(jnp.where(is1, pos, 0.0))
    slot2 = lsum(jnp.where(is2, pos, 0.0))
    cnt = runs * SUBLANES
    cnt_ref[0] = cnt
    base_ref[0] = carry_scr[...]
    carry_scr[...] = carry_scr[...] + cnt

    h2_ref[0] = h2.astype(BF16)
    route_ref[0] = jnp.where(lane == 0, slot1, jnp.where(lane == 1, slot2,
                                                         jnp.where(lane == 2, w1, jnp.where(lane == 3, w2, 0.0))))


def _postmix(x, o_dil, o_sb, g_dil, g_sb, w_out_bf16, gate, shift, scale, g_ffn, w_router):
    b, s, d = x.shape
    tm = POST_ROWS
    nt = s // tm
    tril = jnp.asarray(np.tril(np.ones((tm, tm), np.float32), -1), BF16)
    triu = jnp.asarray(np.triu(np.ones((LANES, LANES), np.float32), 1), BF16)
    row = lambda w: pl.BlockSpec((1, tm, w), lambda bi, i: (bi, i, 0))
    vec = lambda w: pl.BlockSpec((1, w), lambda bi, i: (0, 0))
    mod_spec = pl.BlockSpec((1, 1, d), lambda bi, i: (bi, 0, 0))
    tile_vec = pl.BlockSpec((1, 1, LANES), lambda bi, i: (bi * nt + i, 0, 0))
    return pl.pallas_call(
        _postmix_kernel,
        grid=(b, nt),
        in_specs=[row(d), row(D_DIL), row(D_SB), vec(D_DIL), vec(D_SB),
                  pl.BlockSpec((d, d), lambda bi, i: (0, 0)),
                  mod_spec, mod_spec, mod_spec, vec(d),
                  pl.BlockSpec((d, LANES), lambda bi, i: (0, 0)),
                  pl.BlockSpec((tm, tm), lambda bi, i: (0, 0)),
                  pl.BlockSpec((LANES, LANES), lambda bi, i: (0, 0))],
        out_specs=[row(d), row(d), row(LANES), tile_vec, tile_vec],
        out_shape=[jax.ShapeDtypeStruct((b, s, d), F32),
                   jax.ShapeDtypeStruct((b, s, d), BF16),
                   jax.ShapeDtypeStruct((b, s, LANES), F32),
                   jax.ShapeDtypeStruct((b * nt, 1, LANES), F32),
                   jax.ShapeDtypeStruct((b * nt, 1, LANES), F32)],
        scratch_shapes=[pltpu.VMEM((1, LANES), F32)],
        compiler_params=_cparams(("arbitrary", "arbitrary")),
        name="postmix",
    )(x, o_dil, o_sb, g_dil.reshape(1, -1), g_sb.reshape(1, -1), w_out_bf16, gate, shift, scale,
      g_ffn.reshape(1, d), w_router, tril, triu)


def _for_each_run_piece(tile, start_ref, cnt_ref, base_ref, fn):
    def body(e, off):
        c = cnt_ref[tile * N_EXPERTS + e]
        sorted0 = start_ref[e] + base_ref[tile * N_EXPERTS + e]
        for k in range(3, 10):
            p = 1 << k

            @pl.when((c & p) != 0)
            def _(p=p):
                done = c - (c & (2 * p - 1))
                fn(pl.multiple_of(off + done, SUBLANES), pl.multiple_of(sorted0 + done, SUBLANES), p)
        return off + c

    return lax.fori_loop(0, N_EXPERTS, body, 0)


def _sort_kernel(start_ref, cnt_ref, base_ref, h2_ref, route_ref, buf_in_ref, buf_ref, xs_scr, sem):
    del buf_in_ref
    tm = POST_ROWS
    lt = LOCAL_ROWS
    d = h2_ref.shape[2]
    lane = lax.broadcasted_iota(jnp.int32, (tm, LANES), 1)
    route = route_ref[0]
    w1 = jnp.sum(jnp.where(lane == 2, route, 0.0), axis=-1, keepdims=True)
    w2 = jnp.sum(jnp.where(lane == 3, route, 0.0), axis=-1, keepdims=True)

    def pieces(w):
        hi, mid, lw = _split3(w)
        return jnp.where(lane == 0, hi.astype(F32),
                         jnp.where(lane == 1, mid.astype(F32),
                                   jnp.where(lane == 2, lw.astype(F32), 0.0))).astype(BF16)

    route_t = route.T
    s1 = route_t[0:1, :].astype(jnp.int32)
    s2 = route_t[1:2, :].astype(jnp.int32)
    row = lax.broadcasted_iota(jnp.int32, (lt, tm), 0)
    p1 = jnp.where(row == s1, 1.0, 0.0)
    p2 = jnp.where(row == s2, 1.0, 0.0)
    xs_scr[:, 0:d] = jnp.dot((p1 + p2).astype(BF16), h2_ref[0], preferred_element_type=F32)
    xs_scr[:, d:] = (jnp.dot(p1.astype(BF16), pieces(w1), preferred_element_type=F32)
                     + jnp.dot(p2.astype(BF16), pieces(w2), preferred_element_type=F32))

    tile = pl.program_id(0)

    def piece(lrow, srow, rows):
        return pltpu.make_async_copy(xs_scr.at[pl.ds(lrow, rows)], buf_ref.at[pl.ds(srow, rows)], sem)

    _for_each_run_piece(tile, start_ref, cnt_ref, base_ref, lambda *a: piece(*a).start())
    _for_each_run_piece(tile, start_ref, cnt_ref, base_ref, lambda *a: piece(*a).wait())


def _dispatch(pad_start, cnt, base, h2, route, buf_zeros):
    b, s, d = h2.shape
    tm = POST_ROWS
    nt = s // tm
    any_spec = pl.BlockSpec(memory_space=pl.ANY)
    return pl.pallas_call(
        _sort_kernel,
        grid_spec=pltpu.PrefetchScalarGridSpec(
            num_scalar_prefetch=3, grid=(b * nt,),
            in_specs=[pl.BlockSpec((1, tm, d), lambda t, *_: (t // nt, t % nt, 0)),
                      pl.BlockSpec((1, tm, LANES), lambda t, *_: (t // nt, t % nt, 0)),
                      any_spec],
            out_specs=any_spec,
            scratch_shapes=[pltpu.VMEM((LOCAL_ROWS, d + LANES), F32), pltpu.SemaphoreType.DMA(())]),
        out_shape=jax.ShapeDtypeStruct(buf_zeros.shape, buf_zeros.dtype),
        input_output_aliases={5: 0},
        compiler_params=_cparams(("arbitrary",)),
        name="dispatch",
    )(pad_start, cnt, base, h2, route, buf_zeros)


def _expert_kernel(be_ref, nused_ref, x_ref, wg_ref, wu_ref, wd_ref, y_ref):
    del be_ref
    d = y_ref.shape[1]

    @pl.when(pl.program_id(0) < nused_ref[0])
    def _():
        xb = x_ref[:, 0:d].astype(BF16)
        weight = jnp.sum(x_ref[:, d:], axis=-1, keepdims=True)
        gate = jnp.dot(xb, wg_ref[0], preferred_element_type=F32)
        up = jnp.dot(xb, wu_ref[0], preferred_element_type=F32)
        act = gate / (1.0 + jnp.exp(-gate)) * up
        y_ref[...] = jnp.dot(act.astype(BF16), wd_ref[0], preferred_element_type=F32) * weight

    @pl.when(pl.program_id(0) >= nused_ref[0])
    def _():
        y_ref[...] = jnp.zeros_like(y_ref)


def _experts(block_expert, n_used, buf, wg, wu, wd):
    cap, dw = buf.shape
    d, f = wg.shape[1], wg.shape[2]
    bm = EXPERT_ROWS
    return pl.pallas_call(
        _expert_kernel,
        grid_spec=pltpu.PrefetchScalarGridSpec(
            num_scalar_prefetch=2, grid=(cap // bm,),
            in_specs=[pl.BlockSpec((bm, dw), lambda i, be, nu: (i, 0)),
                      pl.BlockSpec((1, d, f), lambda i, be, nu: (be[i], 0, 0)),
                      pl.BlockSpec((1, d, f), lambda i, be, nu: (be[i], 0, 0)),
                      pl.BlockSpec((1, f, d), lambda i, be, nu: (be[i], 0, 0))],
            out_specs=pl.BlockSpec((bm, d), lambda i, be, nu: (i, 0))),
        out_shape=jax.ShapeDtypeStruct((cap, d), F32),
        compiler_params=_cparams(("arbitrary",)),
        name="experts",
    )(block_expert, n_used, buf, wg, wu, wd)


def _combine_kernel(start_ref, cnt_ref, base_ref, x1_ref, route_ref, gate_ref, g_ref, y_hbm_ref, o_ref,
                    y_scr, sem):
    tm = POST_ROWS
    lt = LOCAL_ROWS
    tile = pl.program_id(0)

    def piece(lrow, srow, rows):
        return pltpu.make_async_copy(y_hbm_ref.at[pl.ds(srow, rows)], y_scr.at[pl.ds(lrow, rows)], sem)

    _for_each_run_piece(tile, start_ref, cnt_ref, base_ref, lambda *a: piece(*a).start())
    lane = lax.broadcasted_iota(jnp.int32, (tm, LANES), 1)
    route = route_ref[0]
    s1 = jnp.sum(jnp.where(lane == 0, route, 0.0), axis=-1, keepdims=True).astype(jnp.int32)
    s2 = jnp.sum(jnp.where(lane == 1, route, 0.0), axis=-1, keepdims=True).astype(jnp.int32)
    col = lax.broadcasted_iota(jnp.int32, (tm, lt), 1)
    pick = jnp.where(col == s1, 1.0, jnp.where(col == s2, 1.0, 0.0)).astype(BF16)
    used = _for_each_run_piece(tile, start_ref, cnt_ref, base_ref, lambda *a: piece(*a).wait())
    live = lax.broadcasted_iota(jnp.int32, (lt, 1), 0) < used
    hi, mid, lo = _split3(jnp.where(live, y_scr[...], 0.0))
    y = jnp.dot(jnp.concatenate([pick, pick, pick], axis=1), jnp.concatenate([hi, mid, lo], axis=0),
                preferred_element_type=F32)
    o_ref[0] = _rms(x1_ref[0] + gate_ref[0] * y, g_ref[...])


def _combine(pad_start, cnt, base, x1, y_sorted, route, gate, g_final):
    b, s, d = x1.shape
    tm = POST_ROWS
    nt = s // tm
    return pl.pallas_call(
        _combine_kernel,
        grid_spec=pltpu.PrefetchScalarGridSpec(
            num_scalar_prefetch=3, grid=(b * nt,),
            in_specs=[pl.BlockSpec((1, tm, d), lambda t, *_: (t // nt, t % nt, 0)),
                      pl.BlockSpec((1, tm, LANES), lambda t, *_: (t // nt, t % nt, 0)),
                      pl.BlockSpec((1, 1, d), lambda t, *_: (t // nt, 0, 0)),
                      pl.BlockSpec((1, d), lambda t, *_: (0, 0)),
                      pl.BlockSpec(memory_space=pl.ANY)],
            out_specs=pl.BlockSpec((1, tm, d), lambda t, *_: (t // nt, t % nt, 0)),
            scratch_shapes=[pltpu.VMEM((LOCAL_ROWS, d), F32), pltpu.SemaphoreType.DMA(())]),
        out_shape=jax.ShapeDtypeStruct((b, s, d), F32),
        compiler_params=_cparams(("arbitrary",)),
        name="combine",
    )(pad_start, cnt, base, x1, route, gate, g_final.reshape(1, d), y_sorted)


def kernel(x, c, w_ada, b_ada, g_mix, w_in, g_dil_out, g_sb_out, w_out, g_ffn,
           w_group, w_expert, w_gate, w_up, w_down, g_final):
    b, s, d = x.shape
    depth = w_ada.shape[0]
    assert s % DIL_UNIT == 0 and d == D_DIL + D_SB
    assert depth == 1, "the final rmsnorm is fused into the last layer's combine step"
    n = b * s
    ntiles = n // POST_ROWS
    bias = jnp.asarray(_dilated_bias())
    for layer in range(depth):
        mod = _ada(c, w_ada[layer], b_ada[layer])
        shift_mix, scale_mix, gate_mix, shift_ffn, scale_ffn, gate_ffn = (
            m.reshape(b, 1, d) for m in jnp.split(mod, 6, axis=-1))

        qkv_d, q_s, k_s, v_s = _premix(x, shift_mix, scale_mix, g_mix[layer], w_in[layer].astype(BF16))
        o_dil = _dilated(qkv_d, bias)
        o_sb = _stick(q_s, k_s, v_s)

        w_router = jnp.concatenate(
            [w_group[layer], w_expert[layer],
             jnp.zeros((d, LANES - N_GROUPS - N_EXPERTS), F32)], axis=1)
        x1, h2, route, cnt, base = _postmix(
            x, o_dil, o_sb, g_dil_out[layer], g_sb_out[layer], w_out[layer].astype(BF16),
            gate_mix, shift_ffn, scale_ffn, g_ffn[layer], w_router)

        bm = EXPERT_ROWS
        cnt = cnt[:, 0, ROUTE_LANE0:ROUTE_LANE0 + N_EXPERTS].astype(jnp.int32)
        base = base[:, 0, ROUTE_LANE0:ROUTE_LANE0 + N_EXPERTS].astype(jnp.int32)
        total = base[-1] + cnt[-1]
        cnt = cnt.reshape(-1)
        base = base.reshape(-1)
        padded = (total + bm - 1) // bm * bm
        pad_end = jnp.cumsum(padded)
        pad_start = (pad_end - padded).astype(jnp.int32)
        cap = -(-(2 * n + (SUBLANES - 1) * N_EXPERTS * ntiles) // bm) * bm + N_EXPERTS * bm
        n_blocks = cap // bm
        block_expert = jnp.minimum(
            jnp.sum(pad_end[None, :] <= (jnp.arange(n_blocks) * bm)[:, None], axis=1),
            N_EXPERTS - 1).astype(jnp.int32)
        n_used = (pad_end[-1:] // bm).astype(jnp.int32)

        buf = _dispatch(pad_start, cnt, base, h2, route, jnp.zeros((cap, d + LANES), F32))
        y_sorted = _experts(block_expert, n_used, buf, w_gate[layer].astype(BF16),
                            w_up[layer].astype(BF16), w_down[layer].astype(BF16))
        x = _combine(pad_start, cnt, base, x1, y_sorted, route, gate_ffn, g_final)
    return x
```

```python
import functools

import numpy as np
import jax
import jax.numpy as jnp
from jax import lax
from jax.experimental import pallas as pl
from jax.experimental.pallas import tpu as pltpu

HEAD_DIM = 64
N_HEADS_DIL = 8
N_HEADS_SB = 8
D_DIL = N_HEADS_DIL * HEAD_DIM
D_SB = N_HEADS_SB * HEAD_DIM
DILATION_PATTERNS = ((128, 1), (512, 4), (2048, 16))
N_GROUPS = 4
EXPERTS_PER_GROUP = 8
N_EXPERTS = N_GROUPS * EXPERTS_PER_GROUP
NORM_EPS = 1e-6

LANES = 128
SUBLANES = 8
DIL_STEPS = 128
DIL_UNIT = 2048
DIL_TILES_PER_TRIP = 4
SB_BLOCK = 256
SB_QUERY_ROWS = 512
PRE_ROWS = 512
POST_ROWS = 512
LOCAL_ROWS = 2 * POST_ROWS + 256
EXPERT_ROWS = 512
ROUTE_LANE0 = N_GROUPS
VMEM_LIMIT = 56 * 1024 * 1024

F32 = jnp.float32
BF16 = jnp.bfloat16
NEG_INF = float("-inf")


def _cparams(sem):
    return pltpu.CompilerParams(dimension_semantics=sem, vmem_limit_bytes=VMEM_LIMIT)


def _rms(v, g):
    return v * lax.rsqrt(jnp.mean(v * v, axis=-1, keepdims=True) + NORM_EPS) * g


def _split3(v):
    hi = v.astype(BF16)
    r = v - hi.astype(F32)
    mid = r.astype(BF16)
    lo = (r - mid.astype(F32)).astype(BF16)
    return hi, mid, lo


def _ada_kernel(c_ref, w_ref, b_ref, o_ref):
    c = c_ref[...]
    cond = c / (1.0 + jnp.exp(-c))
    o_ref[...] = jnp.dot(cond, w_ref[...], precision=lax.Precision.HIGHEST,
                         preferred_element_type=F32) + b_ref[...]


def _ada(c, w_ada, b_ada):
    b, d = c.shape
    n = w_ada.shape[1]
    return pl.pallas_call(
        _ada_kernel,
        grid=(n // d,),
        in_specs=[pl.BlockSpec((b, d), lambda j: (0, 0)),
                  pl.BlockSpec((d, d), lambda j: (0, j)),
                  pl.BlockSpec((1, d), lambda j: (0, j))],
        out_specs=pl.BlockSpec((b, d), lambda j: (0, j)),
        out_shape=jax.ShapeDtypeStruct((b, n), F32),
        compiler_params=_cparams(("arbitrary",)),
        name="ada",
    )(c, w_ada, b_ada.reshape(1, n))


def _premix_kernel(x_ref, shift_ref, scale_ref, g_ref, w_ref, qkvd_ref, qs_ref, ks_ref, vs_ref):
    h = _rms(x_ref[0], g_ref[...]) * (1.0 + scale_ref[0]) + shift_ref[0]
    hb = h.astype(BF16)
    scale = HEAD_DIM ** -0.5
    for j in range(6):
        r = jnp.dot(hb, w_ref[:, j * 512:(j + 1) * 512], preferred_element_type=F32)
        if j == 0:
            qkvd_ref[0, :, 0:512] = r * scale
        elif j < 3:
            qkvd_ref[0, :, j * 512:(j + 1) * 512] = r
        elif j == 3:
            qs_ref[0] = (r * scale).astype(BF16)
        elif j == 4:
            ks_ref[0] = r.astype(BF16)
        else:
            vs_ref[0] = r.astype(BF16)


def _premix(x, shift, scale, g_mix, w_in_bf16):
    b, s, d = x.shape
    tm = PRE_ROWS
    mod_spec = pl.BlockSpec((1, 1, d), lambda bi, i: (bi, 0, 0))
    sb_spec = pl.BlockSpec((1, tm, D_SB), lambda bi, i: (bi, i, 0))
    return pl.pallas_call(
        _premix_kernel,
        grid=(b, s // tm),
        in_specs=[pl.BlockSpec((1, tm, d), lambda bi, i: (bi, i, 0)),
                  mod_spec, mod_spec,
                  pl.BlockSpec((1, d), lambda bi, i: (0, 0)),
                  pl.BlockSpec((d, 3 * (D_DIL + D_SB)), lambda bi, i: (0, 0))],
        out_specs=[pl.BlockSpec((1, tm, 3 * D_DIL), lambda bi, i: (bi, i, 0)),
                   sb_spec, sb_spec, sb_spec],
        out_shape=[jax.ShapeDtypeStruct((b, s, 3 * D_DIL), F32),
                   jax.ShapeDtypeStruct((b, s, D_SB), BF16),
                   jax.ShapeDtypeStruct((b, s, D_SB), BF16),
                   jax.ShapeDtypeStruct((b, s, D_SB), BF16)],
        compiler_params=_cparams(("arbitrary", "arbitrary")),
        name="premix",
    )(x, shift, scale, g_mix.reshape(1, d), w_in_bf16)


def _dilated_bias():
    n = DIL_STEPS
    slopes = np.array([2.0 ** (-8.0 * (i + 1) / N_HEADS_DIL) for i in range(N_HEADS_DIL)], dtype=np.float32)
    steps = np.arange(n)[:, None] + n - np.arange(2 * n)[None, :]
    valid = (steps >= 0) & (steps <= n)
    out = []
    for _, dilation in DILATION_PATTERNS:
        bias = -slopes[:, None, None] * (steps * dilation).astype(np.float32)[None]
        out.append(np.where(valid[None], bias, -np.inf).astype(np.float32))
    return np.stack(out)


def _dil_kernel(q_ref, kc_ref, kp_ref, vc_ref, vp_ref, bias_ref, o_ref,
                kext, vext, u_scr, m_scr, l_scr):
    n = DIL_STEPS
    g = pl.program_id(1)
    kext[0:DIL_UNIT, :] = kp_ref[0]
    kext[DIL_UNIT:2 * DIL_UNIT, :] = kc_ref[0]
    vext[0:DIL_UNIT, :] = vp_ref[0]
    vext[DIL_UNIT:2 * DIL_UNIT, :] = vc_ref[0]
    lane = lax.broadcasted_iota(jnp.int32, (n, LANES), 1)
    head0 = lane < HEAD_DIM
    col = lax.broadcasted_iota(jnp.int32, (n, 2 * n), 1)

    for p, (_, dil) in enumerate(DILATION_PATTERNS):
        unit = n * dil

        def tiles(it, carry, p=p, dil=dil, unit=unit):
            rows_of, vvs, deads, ss = [], [], [], []
            for t in range(DIL_TILES_PER_TRIP):
                ti = it * DIL_TILES_PER_TRIP + t
                j = ti // dil
                r = ti % dil
                qstart = j * unit + r
                kstart = DIL_UNIT + qstart - unit
                if dil == 1:
                    rows_of.append(pl.ds(qstart, n))
                    krows = pl.ds(kstart, 2 * n)
                else:
                    rows_of.append(pl.ds(qstart, n, stride=dil))
                    krows = pl.ds(kstart, 2 * n, stride=dil)
                q = q_ref[0, rows_of[t], :]
                kk = kext[krows, :].astype(BF16)
                vvs.append(vext[krows, :].astype(BF16))
                deads.append(jnp.where(jnp.logical_and(g == 0, j == 0), n, 0))
                for h in range(2):
                    qh = jnp.where(head0 if h == 0 else jnp.logical_not(head0), q, 0.0).astype(BF16)
                    ss.append(lax.dot_general(qh, kk, (((1,), (1,)), ((), ())), preferred_element_type=F32))
            ms, ls, pes = [], [], []
            for t in range(DIL_TILES_PER_TRIP):
                for h in range(2):
                    logits = jnp.where(col < deads[t], NEG_INF, ss[2 * t + h] + bias_ref[p, h])
                    m = jnp.max(logits, axis=-1, keepdims=True)
                    pe = jnp.exp(logits - m)
                    ls.append(jnp.sum(pe, axis=-1, keepdims=True))
                    ms.append(m)
                    pes.append(pe.astype(BF16))
            us = [jnp.dot(pes[2 * t + h], vvs[t], preferred_element_type=F32)
                  for t in range(DIL_TILES_PER_TRIP) for h in range(2)]
            for t in range(DIL_TILES_PER_TRIP):
                u_scr[p, rows_of[t], :] = jnp.where(head0, us[2 * t], us[2 * t + 1])
                m_scr[p, rows_of[t], :] = jnp.where(head0, ms[2 * t], ms[2 * t + 1])
                l_scr[p, rows_of[t], :] = jnp.where(head0, ls[2 * t], ls[2 * t + 1])
            return carry

        lax.fori_loop(0, DIL_UNIT // n // DIL_TILES_PER_TRIP, tiles, 0)

    def merge(i, carry):
        rows = pl.ds(pl.multiple_of(i * n, n), n)
        m0, m1, m2 = m_scr[0, rows, :], m_scr[1, rows, :], m_scr[2, rows, :]
        mx = jnp.maximum(jnp.maximum(m0, m1), m2)
        w0, w1, w2 = jnp.exp(m0 - mx), jnp.exp(m1 - mx), jnp.exp(m2 - mx)
        num = w0 * u_scr[0, rows, :] + w1 * u_scr[1, rows, :] + w2 * u_scr[2, rows, :]
        den = w0 * l_scr[0, rows, :] + w1 * l_scr[1, rows, :] + w2 * l_scr[2, rows, :]
        o_ref[0, rows, :] = num / den
        return carry

    lax.fori_loop(0, DIL_UNIT // n, merge, 0)


def _dilated(qkv_d, bias):
    b, s, _ = qkv_d.shape
    u = DIL_UNIT
    npair = D_DIL // LANES
    cur = lambda off: pl.BlockSpec((1, u, LANES), lambda bi, g, p: (bi, g, off + p))
    prev = lambda off: pl.BlockSpec((1, u, LANES), lambda bi, g, p: (bi, jnp.maximum(g - 1, 0), off + p))
    return pl.pallas_call(
        _dil_kernel,
        grid=(b, s // u, npair),
        in_specs=[cur(0), cur(npair), prev(npair), cur(2 * npair), prev(2 * npair),
                  pl.BlockSpec((3, 2, DIL_STEPS, 2 * DIL_STEPS), lambda bi, g, p: (0, p, 0, 0))],
        out_specs=pl.BlockSpec((1, u, LANES), lambda bi, g, p: (bi, g, p)),
        out_shape=jax.ShapeDtypeStruct((b, s, D_DIL), F32),
        scratch_shapes=[pltpu.VMEM((2 * u, LANES), F32), pltpu.VMEM((2 * u, LANES), F32),
                        pltpu.VMEM((3, u, LANES), F32), pltpu.VMEM((3, u, LANES), F32),
                        pltpu.VMEM((3, u, LANES), F32)],
        compiler_params=_cparams(("arbitrary", "arbitrary", "arbitrary")),
        name="dilated",
    )(qkv_d, qkv_d, qkv_d, qkv_d, qkv_d, bias)


def _stick_kernel(q_ref, k_ref, v_ref, tri_ref, o_ref,
                  qh_scr, z_scr, w_scr, acc_scr, carry_scr, scale_scr):
    blk = SB_BLOCK
    nsub = SB_QUERY_ROWS // blk
    assert nsub % 2 == 0
    nchain = 2 * nsub
    qi = pl.program_id(2)
    lane = lax.broadcasted_iota(jnp.int32, (blk, LANES), 1)
    head0 = lane < HEAD_DIM
    for sub in range(nsub):
        q = q_ref[0, sub * blk:(sub + 1) * blk, :]
        zero = jnp.zeros_like(q)
        qh_scr[2 * sub] = jnp.where(head0, q, zero)
        qh_scr[2 * sub + 1] = jnp.where(head0, zero, q)
    acc_scr[...] = jnp.zeros_like(acc_scr)
    carry_scr[...] = jnp.zeros_like(carry_scr)
    sign = jnp.int32(-2 ** 31)

    def rows(kb):
        return pl.ds(pl.multiple_of(kb * blk, blk), blk)

    def scores(kb, which, slot):
        kblk = k_ref[0, rows(kb), :]
        for c in which:
            z_scr[slot * nchain + c] = lax.dot_general(
                qh_scr[c], kblk, (((1,), (1,)), ((), ())), preferred_element_type=F32)

    def weights(which, slot, diag_sub):
        causal = (lax.broadcasted_iota(jnp.int32, (blk, blk), 1)
                  < lax.broadcasted_iota(jnp.int32, (blk, blk), 0))
        log_betas, log_keeps, splits = {}, {}, {}
        for c in which:
            z = z_scr[slot * nchain + c]
            neg_abs = lax.bitcast_convert_type(lax.bitcast_convert_type(z, jnp.int32) | sign, F32)
            sp = jnp.log(1.0 + jnp.exp(neg_abs))
            log_beta = jnp.minimum(z, 0.0) - sp
            log_keep = log_beta - z
            if c // 2 == diag_sub:
                log_keep = jnp.where(causal, log_keep, 0.0)
            hi = log_keep.astype(BF16)
            lo = (log_keep - hi.astype(F32)).astype(BF16)
            log_betas[c], log_keeps[c] = log_beta, log_keep
            splits[c] = jnp.concatenate([hi, lo], axis=1)
        betweens = {c: jnp.dot(splits[c], tri_ref[...], preferred_element_type=F32) for c in which}
        for c in which:
            w = jnp.exp(log_betas[c] + betweens[c])
            if c // 2 == diag_sub:
                w = jnp.where(causal, w, 0.0)
            w_scr[c] = w.astype(BF16)
            carry = carry_scr[c]
            scale_scr[c] = jnp.exp(carry)
            carry_scr[c] = carry + (betweens[c][:, 0:1] + log_keeps[c][:, 0:1])

    def accumulate(which, kb):
        vblk = v_ref[0, rows(kb), :]
        for c in which:
            acc_scr[c] = acc_scr[c] + scale_scr[c] * jnp.dot(w_scr[c], vblk, preferred_element_type=F32)

    everyone = list(range(nchain))
    top = nsub * qi + nsub - 1
    first = nsub * qi - 1
    for i in range(nsub):
        which = [c for c in everyone if c // 2 >= nsub - 1 - i]
        scores(top - i, which, 1)
        if i == nsub - 1:
            scores(jnp.maximum(first, 0), everyone, 0)
        weights(which, 1, nsub - 1 - i)
        if i < nsub - 1:
            accumulate(which, top - i)

    def step(i, carry):
        for slot in range(2):
            kb = first - 2 * i - slot
            accumulate(everyone, kb + 1)
            scores(jnp.maximum(kb - 1, 0), everyone, 1 - slot)
            weights(everyone, slot, -1)
        return carry

    lax.fori_loop(0, nsub * qi // 2, step, 0)
    accumulate(everyone, 0)
    for sub in range(nsub):
        o_ref[0, sub * blk:(sub + 1) * blk, :] = jnp.where(head0, acc_scr[2 * sub], acc_scr[2 * sub + 1])


def _stick(q_s, k_s, v_s):
    b, s, _ = q_s.shape
    blk = SB_BLOCK
    qrows = SB_QUERY_ROWS
    nchain = 2 * qrows // blk
    tri = np.tril(np.ones((blk, blk), np.float32), -1)
    tri2 = jnp.asarray(np.concatenate([tri, tri], axis=0), BF16)
    full = pl.BlockSpec((1, s, LANES), lambda bi, p, i: (bi, 0, p))
    return pl.pallas_call(
        _stick_kernel,
        grid=(b, D_SB // LANES, s // qrows),
        in_specs=[pl.BlockSpec((1, qrows, LANES), lambda bi, p, i: (bi, i, p)), full, full,
                  pl.BlockSpec((2 * blk, blk), lambda bi, p, i: (0, 0))],
        out_specs=pl.BlockSpec((1, qrows, LANES), lambda bi, p, i: (bi, i, p)),
        out_shape=jax.ShapeDtypeStruct((b, s, D_SB), F32),
        scratch_shapes=[pltpu.VMEM((nchain, blk, LANES), BF16),
                        pltpu.VMEM((2 * nchain, blk, blk), F32),
                        pltpu.VMEM((nchain, blk, blk), BF16),
                        pltpu.VMEM((nchain, blk, LANES), F32),
                        pltpu.VMEM((nchain, blk, 1), F32),
                        pltpu.VMEM((nchain, blk, 1), F32)],
        compiler_params=_cparams(("arbitrary", "arbitrary", "arbitrary")),
        name="stick",
    )(q_s, k_s, v_s, tri2)


def _postmix_kernel(x_ref, od_ref, os_ref, gd_ref, gs_ref, wout_ref, gate_ref, shift_ref, scale_ref,
                    gffn_ref, wr_ref, tril_ref, triu_ref,
                    x1_ref, h2_ref, route_ref, cnt_ref, base_ref, carry_scr):
    tm = POST_ROWS

    @pl.when(jnp.logical_and(pl.program_id(0) == 0, pl.program_id(1) == 0))
    def _():
        carry_scr[...] = jnp.zeros_like(carry_scr)

    mixed = jnp.concatenate([_rms(od_ref[0], gd_ref[...]), _rms(os_ref[0], gs_ref[...])], axis=-1)
    proj = jnp.dot(mixed.astype(BF16), wout_ref[...], preferred_element_type=F32)
    x1 = x_ref[0] + gate_ref[0] * proj
    x1_ref[0] = x1
    h2 = _rms(x1, gffn_ref[...]) * (1.0 + scale_ref[0]) + shift_ref[0]
    logits = jnp.dot(h2, wr_ref[...], precision=lax.Precision.HIGHEST, preferred_element_type=F32)

    lane = lax.broadcasted_iota(jnp.int32, (tm, LANES), 1)
    big = jnp.int32(LANES)
    lmax = lambda v: jnp.max(v, axis=-1, keepdims=True)
    lmin = lambda v: jnp.min(v, axis=-1, keepdims=True)
    lsum = lambda v: jnp.sum(v, axis=-1, keepdims=True)

    gmask = lane < N_GROUPS
    gl = jnp.where(gmask, logits, NEG_INF)
    gmx = lmax(gl)
    group = lmin(jnp.where(jnp.logical_and(gmask, gl == gmx), lane, big))
    group_gate = 1.0 / lsum(jnp.exp(gl - gmx))
    lo = ROUTE_LANE0 + group * EXPERTS_PER_GROUP
    emask = jnp.logical_and(lane >= lo, lane < lo + EXPERTS_PER_GROUP)
    el = jnp.where(emask, logits, NEG_INF)
    l1 = lmax(el)
    i1 = lmin(jnp.where(el == l1, lane, big))
    el2 = jnp.where(lane == i1, NEG_INF, el)
    l2 = lmax(el2)
    i2 = lmin(jnp.where(el2 == l2, lane, big))
    r = jnp.exp(l2 - l1)
    w1 = group_gate / (1.0 + r)
    w2 = group_gate * r / (1.0 + r)

    is1 = lane == i1
    is2 = lane == i2
    oh = jnp.where(is1, 1.0, jnp.where(is2, 1.0, 0.0))
    earlier = jnp.dot(tril_ref[...], oh.astype(BF16), preferred_element_type=F32)
    runs = jnp.floor((jnp.sum(oh, axis=0, keepdims=True) + (SUBLANES - 1.0)) * (1.0 / SUBLANES))
    run_off = jnp.dot(jnp.broadcast_to(runs, (SUBLANES, LANES)).astype(BF16), triu_ref[...],
                      preferred_element_type=F32)[0:1]
    pos = earlier + run_off * SUBLANES
    slot1 = lsum(jnp.where(is1, pos, 0.0))
    slot2 = lsum(jnp.where(is2, pos, 0.0))
    cnt = runs * SUBLANES
    cnt_ref[0] = cnt
    base_ref[0] = carry_scr[...]
    carry_scr[...] = carry_scr[...] + cnt

    h2_ref[0] = h2.astype(BF16)
    route_ref[0] = jnp.where(lane == 0, slot1, jnp.where(lane == 1, slot2,
                                                         jnp.where(lane == 2, w1, jnp.where(lane == 3, w2, 0.0))))


def _postmix(x, o_dil, o_sb, g_dil, g_sb, w_out_bf16, gate, shift, scale, g_ffn, w_router):
    b, s, d = x.shape
    tm = POST_ROWS
    nt = s // tm
    tril = jnp.asarray(np.tril(np.ones((tm, tm), np.float32), -1), BF16)
    triu = jnp.asarray(np.triu(np.ones((LANES, LANES), np.float32), 1), BF16)
    row = lambda w: pl.BlockSpec((1, tm, w), lambda bi, i: (bi, i, 0))
    vec = lambda w: pl.BlockSpec((1, w), lambda bi, i: (0, 0))
    mod_spec = pl.BlockSpec((1, 1, d), lambda bi, i: (bi, 0, 0))
    tile_vec = pl.BlockSpec((1, 1, LANES), lambda bi, i: (bi * nt + i, 0, 0))
    return pl.pallas_call(
        _postmix_kernel,
        grid=(b, nt),
        in_specs=[row(d), row(D_DIL), row(D_SB), vec(D_DIL), vec(D_SB),
                  pl.BlockSpec((d, d), lambda bi, i: (0, 0)),
                  mod_spec, mod_spec, mod_spec, vec(d),
                  pl.BlockSpec((d, LANES), lambda bi, i: (0, 0)),
                  pl.BlockSpec((tm, tm), lambda bi, i: (0, 0)),
                  pl.BlockSpec((LANES, LANES), lambda bi, i: (0, 0))],
        out_specs=[row(d), row(d), row(LANES), tile_vec, tile_vec],
        out_shape=[jax.ShapeDtypeStruct((b, s, d), F32),
                   jax.ShapeDtypeStruct((b, s, d), BF16),
                   jax.ShapeDtypeStruct((b, s, LANES), F32),
                   jax.ShapeDtypeStruct((b * nt, 1, LANES), F32),
                   jax.ShapeDtypeStruct((b * nt, 1, LANES), F32)],
        scratch_shapes=[pltpu.VMEM((1, LANES), F32)],
        compiler_params=_cparams(("arbitrary", "arbitrary")),
        name="postmix",
    )(x, o_dil, o_sb, g_dil.reshape(1, -1), g_sb.reshape(1, -1), w_out_bf16, gate, shift, scale,
      g_ffn.reshape(1, d), w_router, tril, triu)


def _for_each_run_piece(tile, start_ref, cnt_ref, base_ref, fn):
    def body(e, off):
        c = cnt_ref[tile * N_EXPERTS + e]
        sorted0 = start_ref[e] + base_ref[tile * N_EXPERTS + e]
        for k in range(3, 10):
            p = 1 << k

            @pl.when((c & p) != 0)
            def _(p=p):
                done = c - (c & (2 * p - 1))
                fn(pl.multiple_of(off + done, SUBLANES), pl.multiple_of(sorted0 + done, SUBLANES), p)
        return off + c

    return lax.fori_loop(0, N_EXPERTS, body, 0)


def _sort_kernel(start_ref, cnt_ref, base_ref, h2_ref, route_ref, buf_ref, xs_scr, sem):
    tm = POST_ROWS
    lt = LOCAL_ROWS
    d = h2_ref.shape[2]
    tile = pl.program_id(0)
    slot = tile % 2

    def piece(slot, lrow, srow, rows):
        return pltpu.make_async_copy(xs_scr.at[slot, pl.ds(lrow, rows)], buf_ref.at[pl.ds(srow, rows)],
                                     sem.at[slot])

    def drain(tile, slot):
        _for_each_run_piece(tile, start_ref, cnt_ref, base_ref, lambda *a: piece(slot, *a).wait())

    @pl.when(tile >= 2)
    def _():
        drain(tile - 2, slot)

    lane = lax.broadcasted_iota(jnp.int32, (tm, LANES), 1)
    route = route_ref[0]
    w1 = jnp.sum(jnp.where(lane == 2, route, 0.0), axis=-1, keepdims=True)
    w2 = jnp.sum(jnp.where(lane == 3, route, 0.0), axis=-1, keepdims=True)

    def pieces(w):
        hi, mid, lw = _split3(w)
        return jnp.where(lane == 0, hi.astype(F32),
                         jnp.where(lane == 1, mid.astype(F32),
                                   jnp.where(lane == 2, lw.astype(F32), 0.0))).astype(BF16)

    route_t = route.T
    s1 = route_t[0:1, :].astype(jnp.int32)
    s2 = route_t[1:2, :].astype(jnp.int32)
    row = lax.broadcasted_iota(jnp.int32, (lt, tm), 0)
    p1 = jnp.where(row == s1, 1.0, 0.0)
    p2 = jnp.where(row == s2, 1.0, 0.0)
    xs_scr[slot, :, 0:d] = jnp.dot((p1 + p2).astype(BF16), h2_ref[0], preferred_element_type=F32)
    xs_scr[slot, :, d:] = (jnp.dot(p1.astype(BF16), pieces(w1), preferred_element_type=F32)
                           + jnp.dot(p2.astype(BF16), pieces(w2), preferred_element_type=F32))
    _for_each_run_piece(tile, start_ref, cnt_ref, base_ref, lambda *a: piece(slot, *a).start())

    last = pl.num_programs(0) - 1

    @pl.when(jnp.logical_and(tile == last, tile >= 1))
    def _():
        drain(tile - 1, 1 - slot)

    @pl.when(tile == last)
    def _():
        drain(tile, slot)


def _dispatch(pad_start, cnt, base, h2, route, cap):
    b, s, d = h2.shape
    tm = POST_ROWS
    nt = s // tm
    return pl.pallas_call(
        _sort_kernel,
        grid_spec=pltpu.PrefetchScalarGridSpec(
            num_scalar_prefetch=3, grid=(b * nt,),
            in_specs=[pl.BlockSpec((1, tm, d), lambda t, *_: (t // nt, t % nt, 0)),
                      pl.BlockSpec((1, tm, LANES), lambda t, *_: (t // nt, t % nt, 0))],
            out_specs=pl.BlockSpec(memory_space=pl.ANY),
            scratch_shapes=[pltpu.VMEM((2, LOCAL_ROWS, d + LANES), F32), pltpu.SemaphoreType.DMA((2,))]),
        out_shape=jax.ShapeDtypeStruct((cap, d + LANES), F32),
        compiler_params=_cparams(("arbitrary",)),
        name="dispatch",
    )(pad_start, cnt, base, h2, route)


def _expert_kernel(be_ref, live_ref, x_ref, wg_ref, wu_ref, wd_ref, y_ref):
    del be_ref
    d = y_ref.shape[1]
    live = live_ref[pl.program_id(0)]

    @pl.when(live > 0)
    def _():
        keep = lax.broadcasted_iota(jnp.int32, (x_ref.shape[0], 1), 0) < live
        x = jnp.where(keep, x_ref[...], 0.0)
        xb = x[:, 0:d].astype(BF16)
        weight = jnp.sum(x[:, d:], axis=-1, keepdims=True)
        gate = jnp.dot(xb, wg_ref[0], preferred_element_type=F32)
        up = jnp.dot(xb, wu_ref[0], preferred_element_type=F32)
        act = gate / (1.0 + jnp.exp(-gate)) * up
        y_ref[...] = jnp.dot(act.astype(BF16), wd_ref[0], preferred_element_type=F32) * weight

    @pl.when(live == 0)
    def _():
        y_ref[...] = jnp.zeros_like(y_ref)


def _experts(block_expert, live_rows, buf, wg, wu, wd):
    cap, dw = buf.shape
    d, f = wg.shape[1], wg.shape[2]
    bm = EXPERT_ROWS
    return pl.pallas_call(
        _expert_kernel,
        grid_spec=pltpu.PrefetchScalarGridSpec(
            num_scalar_prefetch=2, grid=(cap // bm,),
            in_specs=[pl.BlockSpec((bm, dw), lambda i, be, nu: (i, 0)),
                      pl.BlockSpec((1, d, f), lambda i, be, nu: (be[i], 0, 0)),
                      pl.BlockSpec((1, d, f), lambda i, be, nu: (be[i], 0, 0)),
                      pl.BlockSpec((1, f, d), lambda i, be, nu: (be[i], 0, 0))],
            out_specs=pl.BlockSpec((bm, d), lambda i, be, nu: (i, 0))),
        out_shape=jax.ShapeDtypeStruct((cap, d), F32),
        compiler_params=_cparams(("arbitrary",)),
        name="experts",
    )(block_expert, live_rows, buf, wg, wu, wd)


def _combine_kernel(start_ref, cnt_ref, base_ref, x1_ref, route_ref, gate_ref, g_ref, y_hbm_ref, o_ref,
                    y_scr, sem):
    tm = POST_ROWS
    lt = LOCAL_ROWS
    tile = pl.program_id(0)
    slot = tile % 2

    def piece(slot, lrow, srow, rows):
        return pltpu.make_async_copy(y_hbm_ref.at[pl.ds(srow, rows)], y_scr.at[slot, pl.ds(lrow, rows)],
                                     sem.at[slot])

    def fetch(tile, slot):
        _for_each_run_piece(tile, start_ref, cnt_ref, base_ref, lambda *a: piece(slot, *a).start())

    @pl.when(tile == 0)
    def _():
        fetch(tile, slot)

    @pl.when(tile + 1 < pl.num_programs(0))
    def _():
        fetch(tile + 1, 1 - slot)

    lane = lax.broadcasted_iota(jnp.int32, (tm, LANES), 1)
    route = route_ref[0]
    s1 = jnp.sum(jnp.where(lane == 0, route, 0.0), axis=-1, keepdims=True).astype(jnp.int32)
    s2 = jnp.sum(jnp.where(lane == 1, route, 0.0), axis=-1, keepdims=True).astype(jnp.int32)
    col = lax.broadcasted_iota(jnp.int32, (tm, lt), 1)
    pick = jnp.where(col == s1, 1.0, jnp.where(col == s2, 1.0, 0.0)).astype(BF16)
    used = _for_each_run_piece(tile, start_ref, cnt_ref, base_ref, lambda *a: piece(slot, *a).wait())
    live = lax.broadcasted_iota(jnp.int32, (lt, 1), 0) < used
    hi, mid, lo = _split3(jnp.where(live, y_scr[slot], 0.0))
    y = jnp.dot(jnp.concatenate([pick, pick, pick], axis=1), jnp.concatenate([hi, mid, lo], axis=0),
                preferred_element_type=F32)
    o_ref[0] = _rms(x1_ref[0] + gate_ref[0] * y, g_ref[...])


def _combine(pad_start, cnt, base, x1, y_sorted, route, gate, g_final):
    b, s, d = x1.shape
    tm = POST_ROWS
    nt = s // tm
    return pl.pallas_call(
        _combine_kernel,
        grid_spec=pltpu.PrefetchScalarGridSpec(
            num_scalar_prefetch=3, grid=(b * nt,),
            in_specs=[pl.BlockSpec((1, tm, d), lambda t, *_: (t // nt, t % nt, 0)),
                      pl.BlockSpec((1, tm, LANES), lambda t, *_: (t // nt, t % nt, 0)),
                      pl.BlockSpec((1, 1, d), lambda t, *_: (t // nt, 0, 0)),
                      pl.BlockSpec((1, d), lambda t, *_: (0, 0)),
                      pl.BlockSpec(memory_space=pl.ANY)],
            out_specs=pl.BlockSpec((1, tm, d), lambda t, *_: (t // nt, t % nt, 0)),
            scratch_shapes=[pltpu.VMEM((2, LOCAL_ROWS, d), F32), pltpu.SemaphoreType.DMA((2,))]),
        out_shape=jax.ShapeDtypeStruct((b, s, d), F32),
        compiler_params=_cparams(("arbitrary",)),
        name="combine",
    )(pad_start, cnt, base, x1, route, gate, g_final.reshape(1, d), y_sorted)


def kernel(x, c, w_ada, b_ada, g_mix, w_in, g_dil_out, g_sb_out, w_out, g_ffn,
           w_group, w_expert, w_gate, w_up, w_down, g_final):
    b, s, d = x.shape
    depth = w_ada.shape[0]
    assert s % DIL_UNIT == 0 and d == D_DIL + D_SB
    assert depth == 1, "the final rmsnorm is fused into the last layer's combine step"
    n = b * s
    ntiles = n // POST_ROWS
    bias = jnp.asarray(_dilated_bias())
    for layer in range(depth):
        mod = _ada(c, w_ada[layer], b_ada[layer])
        shift_mix, scale_mix, gate_mix, shift_ffn, scale_ffn, gate_ffn = (
            m.reshape(b, 1, d) for m in jnp.split(mod, 6, axis=-1))

        qkv_d, q_s, k_s, v_s = _premix(x, shift_mix, scale_mix, g_mix[layer], w_in[layer].astype(BF16))
        o_dil = _dilated(qkv_d, bias)
        o_sb = _stick(q_s, k_s, v_s)

        w_router = jnp.concatenate(
            [w_group[layer], w_expert[layer],
             jnp.zeros((d, LANES - N_GROUPS - N_EXPERTS), F32)], axis=1)
        x1, h2, route, cnt, base = _postmix(
            x, o_dil, o_sb, g_dil_out[layer], g_sb_out[layer], w_out[layer].astype(BF16),
            gate_mix, shift_ffn, scale_ffn, g_ffn[layer], w_router)

        bm = EXPERT_ROWS
        cnt = cnt[:, 0, ROUTE_LANE0:ROUTE_LANE0 + N_EXPERTS].astype(jnp.int32)
        base = base[:, 0, ROUTE_LANE0:ROUTE_LANE0 + N_EXPERTS].astype(jnp.int32)
        total = base[-1] + cnt[-1]
        cnt = cnt.reshape(-1)
        base = base.reshape(-1)
        padded = (total + bm - 1) // bm * bm
        pad_end = jnp.cumsum(padded)
        pad_start = (pad_end - padded).astype(jnp.int32)
        cap = -(-(2 * n + (SUBLANES - 1) * N_EXPERTS * ntiles) // bm) * bm + N_EXPERTS * bm
        n_blocks = cap // bm
        block_expert = jnp.minimum(
            jnp.sum(pad_end[None, :] <= (jnp.arange(n_blocks) * bm)[:, None], axis=1),
            N_EXPERTS - 1).astype(jnp.int32)
        live_rows = jnp.clip((pad_start + total)[block_expert] - jnp.arange(n_blocks) * bm, 0, bm).astype(jnp.int32)

        buf = _dispatch(pad_start, cnt, base, h2, route, cap)
        y_sorted = _experts(block_expert, live_rows, buf, w_gate[layer].astype(BF16),
                            w_up[layer].astype(BF16), w_down[layer].astype(BF16))
        x = _combine(pad_start, cnt, base, x1, y_sorted, route, gate_ffn, g_final)
    return x
```

```python
import functools

import numpy as np
import jax
import jax.numpy as jnp
from jax import lax
from jax.experimental import pallas as pl
from jax.experimental.pallas import tpu as pltpu

HEAD_DIM = 64
N_HEADS_DIL = 8
N_HEADS_SB = 8
D_DIL = N_HEADS_DIL * HEAD_DIM
D_SB = N_HEADS_SB * HEAD_DIM
DILATION_PATTERNS = ((128, 1), (512, 4), (2048, 16))
N_GROUPS = 4
EXPERTS_PER_GROUP = 8
N_EXPERTS = N_GROUPS * EXPERTS_PER_GROUP
NORM_EPS = 1e-6

LANES = 128
SUBLANES = 8
DIL_STEPS = 128
DIL_UNIT = 2048
DIL_TILES_PER_TRIP = 4
SB_BLOCK = 256
SB_QUERY_ROWS = 512
PRE_ROWS = 512
POST_ROWS = 512
LOCAL_ROWS = 2 * POST_ROWS + 256
EXPERT_ROWS = 512
ROUTE_LANE0 = N_GROUPS
VMEM_LIMIT = 56 * 1024 * 1024

F32 = jnp.float32
BF16 = jnp.bfloat16
NEG_INF = float("-inf")


def _cparams(sem):
    return pltpu.CompilerParams(dimension_semantics=sem, vmem_limit_bytes=VMEM_LIMIT)


def _rms(v, g):
    return v * lax.rsqrt(jnp.mean(v * v, axis=-1, keepdims=True) + NORM_EPS) * g


def _split3(v):
    hi = v.astype(BF16)
    r = v - hi.astype(F32)
    mid = r.astype(BF16)
    lo = (r - mid.astype(F32)).astype(BF16)
    return hi, mid, lo


def _ada_kernel(c_ref, w_ref, b_ref, o_ref):
    c = c_ref[...]
    cond = c / (1.0 + jnp.exp(-c))
    o_ref[...] = jnp.dot(cond, w_ref[...], precision=lax.Precision.HIGHEST,
                         preferred_element_type=F32) + b_ref[...]


def _ada(c, w_ada, b_ada):
    b, d = c.shape
    n = w_ada.shape[1]
    return pl.pallas_call(
        _ada_kernel,
        grid=(n // d,),
        in_specs=[pl.BlockSpec((b, d), lambda j: (0, 0)),
                  pl.BlockSpec((d, d), lambda j: (0, j)),
                  pl.BlockSpec((1, d), lambda j: (0, j))],
        out_specs=pl.BlockSpec((b, d), lambda j: (0, j)),
        out_shape=jax.ShapeDtypeStruct((b, n), F32),
        compiler_params=_cparams(("arbitrary",)),
        name="ada",
    )(c, w_ada, b_ada.reshape(1, n))


def _premix_kernel(x_ref, shift_ref, scale_ref, g_ref, w_ref, qkvd_ref, qs_ref, ks_ref, vs_ref):
    h = _rms(x_ref[0], g_ref[...]) * (1.0 + scale_ref[0]) + shift_ref[0]
    hb = h.astype(BF16)
    scale = HEAD_DIM ** -0.5
    for j in range(6):
        r = jnp.dot(hb, w_ref[:, j * 512:(j + 1) * 512], preferred_element_type=F32)
        if j == 0:
            qkvd_ref[0, :, 0:512] = r * scale
        elif j < 3:
            qkvd_ref[0, :, j * 512:(j + 1) * 512] = r
        elif j == 3:
            qs_ref[0] = (r * scale).astype(BF16)
        elif j == 4:
            ks_ref[0] = r.astype(BF16)
        else:
            vs_ref[0] = r.astype(BF16)


def _premix(x, shift, scale, g_mix, w_in_bf16):
    b, s, d = x.shape
    tm = PRE_ROWS
    mod_spec = pl.BlockSpec((1, 1, d), lambda bi, i: (bi, 0, 0))
    sb_spec = pl.BlockSpec((1, tm, D_SB), lambda bi, i: (bi, i, 0))
    return pl.pallas_call(
        _premix_kernel,
        grid=(b, s // tm),
        in_specs=[pl.BlockSpec((1, tm, d), lambda bi, i: (bi, i, 0)),
                  mod_spec, mod_spec,
                  pl.BlockSpec((1, d), lambda bi, i: (0, 0)),
                  pl.BlockSpec((d, 3 * (D_DIL + D_SB)), lambda bi, i: (0, 0))],
        out_specs=[pl.BlockSpec((1, tm, 3 * D_DIL), lambda bi, i: (bi, i, 0)),
                   sb_spec, sb_spec, sb_spec],
        out_shape=[jax.ShapeDtypeStruct((b, s, 3 * D_DIL), F32),
                   jax.ShapeDtypeStruct((b, s, D_SB), BF16),
                   jax.ShapeDtypeStruct((b, s, D_SB), BF16),
                   jax.ShapeDtypeStruct((b, s, D_SB), BF16)],
        compiler_params=_cparams(("arbitrary", "arbitrary")),
        name="premix",
    )(x, shift, scale, g_mix.reshape(1, d), w_in_bf16)


def _dilated_bias():
    n = DIL_STEPS
    slopes = np.array([2.0 ** (-8.0 * (i + 1) / N_HEADS_DIL) for i in range(N_HEADS_DIL)], dtype=np.float32)
    steps = np.arange(n)[:, None] + n - np.arange(2 * n)[None, :]
    valid = (steps >= 0) & (steps <= n)
    out = []
    for _, dilation in DILATION_PATTERNS:
        bias = -slopes[:, None, None] * (steps * dilation).astype(np.float32)[None]
        out.append(np.where(valid[None], bias, -np.inf).astype(np.float32))
    return np.stack(out)


def _dil_kernel(q_ref, kc_ref, kp_ref, vc_ref, vp_ref, bias_ref, o_ref,
                kext, vext, u_scr, m_scr, l_scr):
    n = DIL_STEPS
    g = pl.program_id(1)
    kext[0:DIL_UNIT, :] = kp_ref[0]
    kext[DIL_UNIT:2 * DIL_UNIT, :] = kc_ref[0]
    vext[0:DIL_UNIT, :] = vp_ref[0]
    vext[DIL_UNIT:2 * DIL_UNIT, :] = vc_ref[0]
    lane = lax.broadcasted_iota(jnp.int32, (n, LANES), 1)
    head0 = lane < HEAD_DIM
    col = lax.broadcasted_iota(jnp.int32, (n, 2 * n), 1)

    for p, (_, dil) in enumerate(DILATION_PATTERNS):
        unit = n * dil

        def tiles(it, carry, p=p, dil=dil, unit=unit):
            rows_of, vvs, deads, ss = [], [], [], []
            for t in range(DIL_TILES_PER_TRIP):
                ti = it * DIL_TILES_PER_TRIP + t
                j = ti // dil
                r = ti % dil
                qstart = j * unit + r
                kstart = DIL_UNIT + qstart - unit
                if dil == 1:
                    rows_of.append(pl.ds(qstart, n))
                    krows = pl.ds(kstart, 2 * n)
                else:
                    rows_of.append(pl.ds(qstart, n, stride=dil))
                    krows = pl.ds(kstart, 2 * n, stride=dil)
                q = q_ref[0, rows_of[t], :]
                kk = kext[krows, :].astype(BF16)
                vvs.append(vext[krows, :].astype(BF16))
                deads.append(jnp.where(jnp.logical_and(g == 0, j == 0), n, 0))
                for h in range(2):
                    qh = jnp.where(head0 if h == 0 else jnp.logical_not(head0), q, 0.0).astype(BF16)
                    ss.append(lax.dot_general(qh, kk, (((1,), (1,)), ((), ())), preferred_element_type=F32))
            ms, ls, pes = [], [], []
            for t in range(DIL_TILES_PER_TRIP):
                for h in range(2):
                    logits = jnp.where(col < deads[t], NEG_INF, ss[2 * t + h] + bias_ref[p, h])
                    m = jnp.max(logits, axis=-1, keepdims=True)
                    pe = jnp.exp(logits - m)
                    ls.append(jnp.sum(pe, axis=-1, keepdims=True))
                    ms.append(m)
                    pes.append(pe.astype(BF16))
            us = [jnp.dot(pes[2 * t + h], vvs[t], preferred_element_type=F32)
                  for t in range(DIL_TILES_PER_TRIP) for h in range(2)]
            for t in range(DIL_TILES_PER_TRIP):
                u_scr[p, rows_of[t], :] = jnp.where(head0, us[2 * t], us[2 * t + 1])
                m_scr[p, rows_of[t], :] = jnp.where(head0, ms[2 * t], ms[2 * t + 1])
                l_scr[p, rows_of[t], :] = jnp.where(head0, ls[2 * t], ls[2 * t + 1])
            return carry

        lax.fori_loop(0, DIL_UNIT // n // DIL_TILES_PER_TRIP, tiles, 0)

    def merge(i, carry):
        rows = pl.ds(pl.multiple_of(i * n, n), n)
        m0, m1, m2 = m_scr[0, rows, :], m_scr[1, rows, :], m_scr[2, rows, :]
        mx = jnp.maximum(jnp.maximum(m0, m1), m2)
        w0, w1, w2 = jnp.exp(m0 - mx), jnp.exp(m1 - mx), jnp.exp(m2 - mx)
        num = w0 * u_scr[0, rows, :] + w1 * u_scr[1, rows, :] + w2 * u_scr[2, rows, :]
        den = w0 * l_scr[0, rows, :] + w1 * l_scr[1, rows, :] + w2 * l_scr[2, rows, :]
        o_ref[0, rows, :] = num / den
        return carry

    lax.fori_loop(0, DIL_UNIT // n, merge, 0)


def _dilated(qkv_d, bias):
    b, s, _ = qkv_d.shape
    u = DIL_UNIT
    npair = D_DIL // LANES
    cur = lambda off: pl.BlockSpec((1, u, LANES), lambda bi, g, p: (bi, g, off + p))
    prev = lambda off: pl.BlockSpec((1, u, LANES), lambda bi, g, p: (bi, jnp.maximum(g - 1, 0), off + p))
    return pl.pallas_call(
        _dil_kernel,
        grid=(b, s // u, npair),
        in_specs=[cur(0), cur(npair), prev(npair), cur(2 * npair), prev(2 * npair),
                  pl.BlockSpec((3, 2, DIL_STEPS, 2 * DIL_STEPS), lambda bi, g, p: (0, p, 0, 0))],
        out_specs=pl.BlockSpec((1, u, LANES), lambda bi, g, p: (bi, g, p)),
        out_shape=jax.ShapeDtypeStruct((b, s, D_DIL), F32),
        scratch_shapes=[pltpu.VMEM((2 * u, LANES), F32), pltpu.VMEM((2 * u, LANES), F32),
                        pltpu.VMEM((3, u, LANES), F32), pltpu.VMEM((3, u, LANES), F32),
                        pltpu.VMEM((3, u, LANES), F32)],
        compiler_params=_cparams(("arbitrary", "arbitrary", "arbitrary")),
        name="dilated",
    )(qkv_d, qkv_d, qkv_d, qkv_d, qkv_d, bias)


def _stick_kernel(q_ref, k_ref, v_ref, tri_ref, o_ref,
                  qh_scr, z_scr, w_scr, acc_scr, carry_scr, scale_scr):
    blk = SB_BLOCK
    nsub = SB_QUERY_ROWS // blk
    assert nsub % 2 == 0
    nchain = 2 * nsub
    qi = pl.program_id(2)
    lane = lax.broadcasted_iota(jnp.int32, (blk, LANES), 1)
    head0 = lane < HEAD_DIM
    for sub in range(nsub):
        q = q_ref[0, sub * blk:(sub + 1) * blk, :]
        zero = jnp.zeros_like(q)
        qh_scr[2 * sub] = jnp.where(head0, q, zero)
        qh_scr[2 * sub + 1] = jnp.where(head0, zero, q)
    acc_scr[...] = jnp.zeros_like(acc_scr)
    carry_scr[...] = jnp.zeros_like(carry_scr)
    sign = jnp.int32(-2 ** 31)

    def rows(kb):
        return pl.ds(pl.multiple_of(kb * blk, blk), blk)

    def scores(kb, which, slot):
        kblk = k_ref[0, rows(kb), :]
        for c in which:
            z_scr[slot * nchain + c] = lax.dot_general(
                qh_scr[c], kblk, (((1,), (1,)), ((), ())), preferred_element_type=F32)

    def weights(which, slot, diag_sub):
        causal = (lax.broadcasted_iota(jnp.int32, (blk, blk), 1)
                  < lax.broadcasted_iota(jnp.int32, (blk, blk), 0))
        splits = {}
        for c in which:
            z = z_scr[slot * nchain + c]
            neg_abs = lax.bitcast_convert_type(lax.bitcast_convert_type(z, jnp.int32) | sign, F32)
            softplus = jnp.maximum(z, 0.0) + jnp.log(1.0 + jnp.exp(neg_abs))
            if c // 2 == diag_sub:
                softplus = jnp.where(causal, softplus, 0.0)
            hi = softplus.astype(BF16)
            lo = (softplus - hi.astype(F32)).astype(BF16)
            splits[c] = jnp.concatenate([hi, lo], axis=1)
        sums = {c: jnp.dot(splits[c], tri_ref[...], preferred_element_type=F32) for c in which}
        for c in which:
            w = jnp.exp(z_scr[slot * nchain + c] - sums[c])
            if c // 2 == diag_sub:
                w = jnp.where(causal, w, 0.0)
            w_scr[c] = w.astype(BF16)
            carry = carry_scr[c]
            scale_scr[c] = jnp.exp(carry)
            carry_scr[c] = carry - sums[c][:, 0:1]

    def accumulate(which, kb):
        vblk = v_ref[0, rows(kb), :]
        for c in which:
            acc_scr[c] = acc_scr[c] + scale_scr[c] * jnp.dot(w_scr[c], vblk, preferred_element_type=F32)

    everyone = list(range(nchain))
    top = nsub * qi + nsub - 1
    first = nsub * qi - 1
    for i in range(nsub):
        which = [c for c in everyone if c // 2 >= nsub - 1 - i]
        scores(top - i, which, 1)
        if i == nsub - 1:
            scores(jnp.maximum(first, 0), everyone, 0)
        weights(which, 1, nsub - 1 - i)
        if i < nsub - 1:
            accumulate(which, top - i)

    def step(i, carry):
        for slot in range(2):
            kb = first - 2 * i - slot
            accumulate(everyone, kb + 1)
            scores(jnp.maximum(kb - 1, 0), everyone, 1 - slot)
            weights(everyone, slot, -1)
        return carry

    lax.fori_loop(0, nsub * qi // 2, step, 0)
    accumulate(everyone, 0)
    for sub in range(nsub):
        o_ref[0, sub * blk:(sub + 1) * blk, :] = jnp.where(head0, acc_scr[2 * sub], acc_scr[2 * sub + 1])


def _stick(q_s, k_s, v_s):
    b, s, _ = q_s.shape
    blk = SB_BLOCK
    qrows = SB_QUERY_ROWS
    nchain = 2 * qrows // blk
    tri = np.tril(np.ones((blk, blk), np.float32))
    tri2 = jnp.asarray(np.concatenate([tri, tri], axis=0), BF16)
    full = pl.BlockSpec((1, s, LANES), lambda bi, p, i: (bi, 0, p))
    return pl.pallas_call(
        _stick_kernel,
        grid=(b, D_SB // LANES, s // qrows),
        in_specs=[pl.BlockSpec((1, qrows, LANES), lambda bi, p, i: (bi, i, p)), full, full,
                  pl.BlockSpec((2 * blk, blk), lambda bi, p, i: (0, 0))],
        out_specs=pl.BlockSpec((1, qrows, LANES), lambda bi, p, i: (bi, i, p)),
        out_shape=jax.ShapeDtypeStruct((b, s, D_SB), F32),
        scratch_shapes=[pltpu.VMEM((nchain, blk, LANES), BF16),
                        pltpu.VMEM((2 * nchain, blk, blk), F32),
                        pltpu.VMEM((nchain, blk, blk), BF16),
                        pltpu.VMEM((nchain, blk, LANES), F32),
                        pltpu.VMEM((nchain, blk, 1), F32),
                        pltpu.VMEM((nchain, blk, 1), F32)],
        compiler_params=_cparams(("arbitrary", "arbitrary", "arbitrary")),
        name="stick",
    )(q_s, k_s, v_s, tri2)


def _postmix_kernel(x_ref, od_ref, os_ref, gd_ref, gs_ref, wout_ref, gate_ref, shift_ref, scale_ref,
                    gffn_ref, wr_ref, tril_ref, triu_ref,
                    x1_ref, h2_ref, route_ref, cnt_ref, base_ref, carry_scr):
    tm = POST_ROWS

    @pl.when(jnp.logical_and(pl.program_id(0) == 0, pl.program_id(1) == 0))
    def _():
        carry_scr[...] = jnp.zeros_like(carry_scr)

    mixed = jnp.concatenate([_rms(od_ref[0], gd_ref[...]), _rms(os_ref[0], gs_ref[...])], axis=-1)
    proj = jnp.dot(mixed.astype(BF16), wout_ref[...], preferred_element_type=F32)
    x1 = x_ref[0] + gate_ref[0] * proj
    x1_ref[0] = x1
    h2 = _rms(x1, gffn_ref[...]) * (1.0 + scale_ref[0]) + shift_ref[0]
    logits = jnp.dot(h2, wr_ref[...], precision=lax.Precision.HIGHEST, preferred_element_type=F32)

    lane = lax.broadcasted_iota(jnp.int32, (tm, LANES), 1)
    big = jnp.int32(LANES)
    lmax = lambda v: jnp.max(v, axis=-1, keepdims=True)
    lmin = lambda v: jnp.min(v, axis=-1, keepdims=True)
    lsum = lambda v: jnp.sum(v, axis=-1, keepdims=True)

    gmask = lane < N_GROUPS
    gl = jnp.where(gmask, logits, NEG_INF)
    gmx = lmax(gl)
    group = lmin(jnp.where(jnp.logical_and(gmask, gl == gmx), lane, big))
    group_gate = 1.0 / lsum(jnp.exp(gl - gmx))
    lo = ROUTE_LANE0 + group * EXPERTS_PER_GROUP
    emask = jnp.logical_and(lane >= lo, lane < lo + EXPERTS_PER_GROUP)
    el = jnp.where(emask, logits, NEG_INF)
    l1 = lmax(el)
    i1 = lmin(jnp.where(el == l1, lane, big))
    el2 = jnp.where(lane == i1, NEG_INF, el)
    l2 = lmax(el2)
    i2 = lmin(jnp.where(el2 == l2, lane, big))
    r = jnp.exp(l2 - l1)
    w1 = group_gate / (1.0 + r)
    w2 = group_gate * r / (1.0 + r)

    is1 = lane == i1
    is2 = lane == i2
    oh = jnp.where(is1, 1.0, jnp.where(is2, 1.0, 0.0))
    earlier = jnp.dot(tril_ref[...], oh.astype(BF16), preferred_element_type=F32)
    runs = jnp.floor((jnp.sum(oh, axis=0, keepdims=True) + (SUBLANES - 1.0)) * (1.0 / SUBLANES))
    run_off = jnp.dot(jnp.broadcast_to(runs, (SUBLANES, LANES)).astype(BF16), triu_ref[...],
                      preferred_element_type=F32)[0:1]
    pos = earlier + run_off * SUBLANES
    slot1 = lsum(jnp.where(is1, pos, 0.0))
    slot2 = lsum(jnp.where(is2, pos, 0.0))
    cnt = runs * SUBLANES
    cnt_ref[0] = cnt
    base_ref[0] = carry_scr[...]
    carry_scr[...] = carry_scr[...] + cnt

    h2_ref[0] = h2.astype(BF16)
    route_ref[0] = jnp.where(lane == 0, slot1, jnp.where(lane == 1, slot2,
                                                         jnp.where(lane == 2, w1, jnp.where(lane == 3, w2, 0.0))))


def _postmix(x, o_dil, o_sb, g_dil, g_sb, w_out_bf16, gate, shift, scale, g_ffn, w_router):
    b, s, d = x.shape
    tm = POST_ROWS
    nt = s // tm
    tril = jnp.asarray(np.tril(np.ones((tm, tm), np.float32), -1), BF16)
    triu = jnp.asarray(np.triu(np.ones((LANES, LANES), np.float32), 1), BF16)
    row = lambda w: pl.BlockSpec((1, tm, w), lambda bi, i: (bi, i, 0))
    vec = lambda w: pl.BlockSpec((1, w), lambda bi, i: (0, 0))
    mod_spec = pl.BlockSpec((1, 1, d), lambda bi, i: (bi, 0, 0))
    tile_vec = pl.BlockSpec((1, 1, LANES), lambda bi, i: (bi * nt + i, 0, 0))
    return pl.pallas_call(
        _postmix_kernel,
        grid=(b, nt),
        in_specs=[row(d), row(D_DIL), row(D_SB), vec(D_DIL), vec(D_SB),
                  pl.BlockSpec((d, d), lambda bi, i: (0, 0)),
                  mod_spec, mod_spec, mod_spec, vec(d),
                  pl.BlockSpec((d, LANES), lambda bi, i: (0, 0)),
                  pl.BlockSpec((tm, tm), lambda bi, i: (0, 0)),
                  pl.BlockSpec((LANES, LANES), lambda bi, i: (0, 0))],
        out_specs=[row(d), row(d), row(LANES), tile_vec, tile_vec],
        out_shape=[jax.ShapeDtypeStruct((b, s, d), F32),
                   jax.ShapeDtypeStruct((b, s, d), BF16),
                   jax.ShapeDtypeStruct((b, s, LANES), F32),
                   jax.ShapeDtypeStruct((b * nt, 1, LANES), F32),
                   jax.ShapeDtypeStruct((b * nt, 1, LANES), F32)],
        scratch_shapes=[pltpu.VMEM((1, LANES), F32)],
        compiler_params=_cparams(("arbitrary", "arbitrary")),
        name="postmix",
    )(x, o_dil, o_sb, g_dil.reshape(1, -1), g_sb.reshape(1, -1), w_out_bf16, gate, shift, scale,
      g_ffn.reshape(1, d), w_router, tril, triu)


def _for_each_run_piece(tile, start_ref, cnt_ref, base_ref, fn):
    def body(e, off):
        c = cnt_ref[tile * N_EXPERTS + e]
        sorted0 = start_ref[e] + base_ref[tile * N_EXPERTS + e]
        for k in range(3, 10):
            p = 1 << k

            @pl.when((c & p) != 0)
            def _(p=p):
                done = c - (c & (2 * p - 1))
                fn(pl.multiple_of(off + done, SUBLANES), pl.multiple_of(sorted0 + done, SUBLANES), p)
        return off + c

    return lax.fori_loop(0, N_EXPERTS, body, 0)


def _sort_kernel(start_ref, cnt_ref, base_ref, h2_ref, route_ref, buf_ref, xs_scr, sem):
    tm = POST_ROWS
    lt = LOCAL_ROWS
    d = h2_ref.shape[2]
    tile = pl.program_id(0)
    slot = tile % 2

    def piece(slot, lrow, srow, rows):
        return pltpu.make_async_copy(xs_scr.at[slot, pl.ds(lrow, rows)], buf_ref.at[pl.ds(srow, rows)],
                                     sem.at[slot])

    def drain(tile, slot):
        _for_each_run_piece(tile, start_ref, cnt_ref, base_ref, lambda *a: piece(slot, *a).wait())

    @pl.when(tile >= 2)
    def _():
        drain(tile - 2, slot)

    lane = lax.broadcasted_iota(jnp.int32, (tm, LANES), 1)
    route = route_ref[0]
    w1 = jnp.sum(jnp.where(lane == 2, route, 0.0), axis=-1, keepdims=True)
    w2 = jnp.sum(jnp.where(lane == 3, route, 0.0), axis=-1, keepdims=True)

    def pieces(w):
        hi, mid, lw = _split3(w)
        return jnp.where(lane == 0, hi.astype(F32),
                         jnp.where(lane == 1, mid.astype(F32),
                                   jnp.where(lane == 2, lw.astype(F32), 0.0))).astype(BF16)

    route_t = route.T
    s1 = route_t[0:1, :].astype(jnp.int32)
    s2 = route_t[1:2, :].astype(jnp.int32)
    row = lax.broadcasted_iota(jnp.int32, (lt, tm), 0)
    p1 = jnp.where(row == s1, 1.0, 0.0)
    p2 = jnp.where(row == s2, 1.0, 0.0)
    xs_scr[slot, :, 0:d] = jnp.dot((p1 + p2).astype(BF16), h2_ref[0], preferred_element_type=F32)
    xs_scr[slot, :, d:] = (jnp.dot(p1.astype(BF16), pieces(w1), preferred_element_type=F32)
                           + jnp.dot(p2.astype(BF16), pieces(w2), preferred_element_type=F32))
    _for_each_run_piece(tile, start_ref, cnt_ref, base_ref, lambda *a: piece(slot, *a).start())

    last = pl.num_programs(0) - 1

    @pl.when(jnp.logical_and(tile == last, tile >= 1))
    def _():
        drain(tile - 1, 1 - slot)

    @pl.when(tile == last)
    def _():
        drain(tile, slot)


def _dispatch(pad_start, cnt, base, h2, route, cap):
    b, s, d = h2.shape
    tm = POST_ROWS
    nt = s // tm
    return pl.pallas_call(
        _sort_kernel,
        grid_spec=pltpu.PrefetchScalarGridSpec(
            num_scalar_prefetch=3, grid=(b * nt,),
            in_specs=[pl.BlockSpec((1, tm, d), lambda t, *_: (t // nt, t % nt, 0)),
                      pl.BlockSpec((1, tm, LANES), lambda t, *_: (t // nt, t % nt, 0))],
            out_specs=pl.BlockSpec(memory_space=pl.ANY),
            scratch_shapes=[pltpu.VMEM((2, LOCAL_ROWS, d + LANES), F32), pltpu.SemaphoreType.DMA((2,))]),
        out_shape=jax.ShapeDtypeStruct((cap, d + LANES), F32),
        compiler_params=_cparams(("arbitrary",)),
        name="dispatch",
    )(pad_start, cnt, base, h2, route)


def _expert_kernel(be_ref, live_ref, x_ref, wg_ref, wu_ref, wd_ref, y_ref):
    del be_ref
    d = y_ref.shape[1]
    live = live_ref[pl.program_id(0)]

    @pl.when(live > 0)
    def _():
        keep = lax.broadcasted_iota(jnp.int32, (x_ref.shape[0], 1), 0) < live
        x = jnp.where(keep, x_ref[...], 0.0)
        xb = x[:, 0:d].astype(BF16)
        weight = jnp.sum(x[:, d:], axis=-1, keepdims=True)
        gate = jnp.dot(xb, wg_ref[0], preferred_element_type=F32)
        up = jnp.dot(xb, wu_ref[0], preferred_element_type=F32)
        act = gate / (1.0 + jnp.exp(-gate)) * up
        y_ref[...] = jnp.dot(act.astype(BF16), wd_ref[0], preferred_element_type=F32) * weight

    @pl.when(live == 0)
    def _():
        y_ref[...] = jnp.zeros_like(y_ref)


def _experts(block_expert, live_rows, buf, wg, wu, wd):
    cap, dw = buf.shape
    d, f = wg.shape[1], wg.shape[2]
    bm = EXPERT_ROWS
    return pl.pallas_call(
        _expert_kernel,
        grid_spec=pltpu.PrefetchScalarGridSpec(
            num_scalar_prefetch=2, grid=(cap // bm,),
            in_specs=[pl.BlockSpec((bm, dw), lambda i, be, nu: (i, 0)),
                      pl.BlockSpec((1, d, f), lambda i, be, nu: (be[i], 0, 0)),
                      pl.BlockSpec((1, d, f), lambda i, be, nu: (be[i], 0, 0)),
                      pl.BlockSpec((1, f, d), lambda i, be, nu: (be[i], 0, 0))],
            out_specs=pl.BlockSpec((bm, d), lambda i, be, nu: (i, 0))),
        out_shape=jax.ShapeDtypeStruct((cap, d), F32),
        compiler_params=_cparams(("arbitrary",)),
        name="experts",
    )(block_expert, live_rows, buf, wg, wu, wd)


def _combine_kernel(start_ref, cnt_ref, base_ref, x1_ref, route_ref, gate_ref, g_ref, y_hbm_ref, o_ref,
                    y_scr, sem):
    tm = POST_ROWS
    lt = LOCAL_ROWS
    tile = pl.program_id(0)
    slot = tile % 2

    def piece(slot, lrow, srow, rows):
        return pltpu.make_async_copy(y_hbm_ref.at[pl.ds(srow, rows)], y_scr.at[slot, pl.ds(lrow, rows)],
                                     sem.at[slot])

    def fetch(tile, slot):
        _for_each_run_piece(tile, start_ref, cnt_ref, base_ref, lambda *a: piece(slot, *a).start())

    @pl.when(tile == 0)
    def _():
        fetch(tile, slot)

    @pl.when(tile + 1 < pl.num_programs(0))
    def _():
        fetch(tile + 1, 1 - slot)

    lane = lax.broadcasted_iota(jnp.int32, (tm, LANES), 1)
    route = route_ref[0]
    s1 = jnp.sum(jnp.where(lane == 0, route, 0.0), axis=-1, keepdims=True).astype(jnp.int32)
    s2 = jnp.sum(jnp.where(lane == 1, route, 0.0), axis=-1, keepdims=True).astype(jnp.int32)
    col = lax.broadcasted_iota(jnp.int32, (tm, lt), 1)
    pick = jnp.where(col == s1, 1.0, jnp.where(col == s2, 1.0, 0.0)).astype(BF16)
    used = _for_each_run_piece(tile, start_ref, cnt_ref, base_ref, lambda *a: piece(slot, *a).wait())
    live = lax.broadcasted_iota(jnp.int32, (lt, 1), 0) < used
    hi, mid, lo = _split3(jnp.where(live, y_scr[slot], 0.0))
    y = jnp.dot(jnp.concatenate([pick, pick, pick], axis=1), jnp.concatenate([hi, mid, lo], axis=0),
                preferred_element_type=F32)
    o_ref[0] = _rms(x1_ref[0] + gate_ref[0] * y, g_ref[...])


def _combine(pad_start, cnt, base, x1, y_sorted, route, gate, g_final):
    b, s, d = x1.shape
    tm = POST_ROWS
    nt = s // tm
    return pl.pallas_call(
        _combine_kernel,
        grid_spec=pltpu.PrefetchScalarGridSpec(
            num_scalar_prefetch=3, grid=(b * nt,),
            in_specs=[pl.BlockSpec((1, tm, d), lambda t, *_: (t // nt, t % nt, 0)),
                      pl.BlockSpec((1, tm, LANES), lambda t, *_: (t // nt, t % nt, 0)),
                      pl.BlockSpec((1, 1, d), lambda t, *_: (t // nt, 0, 0)),
                      pl.BlockSpec((1, d), lambda t, *_: (0, 0)),
                      pl.BlockSpec(memory_space=pl.ANY)],
            out_specs=pl.BlockSpec((1, tm, d), lambda t, *_: (t // nt, t % nt, 0)),
            scratch_shapes=[pltpu.VMEM((2, LOCAL_ROWS, d), F32), pltpu.SemaphoreType.DMA((2,))]),
        out_shape=jax.ShapeDtypeStruct((b, s, d), F32),
        compiler_params=_cparams(("arbitrary",)),
        name="combine",
    )(pad_start, cnt, base, x1, route, gate, g_final.reshape(1, d), y_sorted)


def kernel(x, c, w_ada, b_ada, g_mix, w_in, g_dil_out, g_sb_out, w_out, g_ffn,
           w_group, w_expert, w_gate, w_up, w_down, g_final):
    b, s, d = x.shape
    depth = w_ada.shape[0]
    assert s % DIL_UNIT == 0 and d == D_DIL + D_SB
    assert depth == 1, "the final rmsnorm is fused into the last layer's combine step"
    n = b * s
    ntiles = n // POST_ROWS
    bias = jnp.asarray(_dilated_bias())
    for layer in range(depth):
        mod = _ada(c, w_ada[layer], b_ada[layer])
        shift_mix, scale_mix, gate_mix, shift_ffn, scale_ffn, gate_ffn = (
            m.reshape(b, 1, d) for m in jnp.split(mod, 6, axis=-1))

        qkv_d, q_s, k_s, v_s = _premix(x, shift_mix, scale_mix, g_mix[layer], w_in[layer].astype(BF16))
        o_dil = _dilated(qkv_d, bias)
        o_sb = _stick(q_s, k_s, v_s)

        w_router = jnp.concatenate(
            [w_group[layer], w_expert[layer],
             jnp.zeros((d, LANES - N_GROUPS - N_EXPERTS), F32)], axis=1)
        x1, h2, route, cnt, base = _postmix(
            x, o_dil, o_sb, g_dil_out[layer], g_sb_out[layer], w_out[layer].astype(BF16),
            gate_mix, shift_ffn, scale_ffn, g_ffn[layer], w_router)

        bm = EXPERT_ROWS
        cnt = cnt[:, 0, ROUTE_LANE0:ROUTE_LANE0 + N_EXPERTS].astype(jnp.int32)
        base = base[:, 0, ROUTE_LANE0:ROUTE_LANE0 + N_EXPERTS].astype(jnp.int32)
        total = base[-1] + cnt[-1]
        cnt = cnt.reshape(-1)
        base = base.reshape(-1)
        padded = (total + bm - 1) // bm * bm
        pad_end = jnp.cumsum(padded)
        pad_start = (pad_end - padded).astype(jnp.int32)
        cap = -(-(2 * n + (SUBLANES - 1) * N_EXPERTS * ntiles) // bm) * bm + N_EXPERTS * bm
        n_blocks = cap // bm
        block_expert = jnp.minimum(
            jnp.sum(pad_end[None, :] <= (jnp.arange(n_blocks) * bm)[:, None], axis=1),
            N_EXPERTS - 1).astype(jnp.int32)
        live_rows = jnp.clip((pad_start + total)[block_expert] - jnp.arange(n_blocks) * bm, 0, bm).astype(jnp.int32)

        buf = _dispatch(pad_start, cnt, base, h2, route, cap)
        y_sorted = _experts(block_expert, live_rows, buf, w_gate[layer].astype(BF16),
                            w_up[layer].astype(BF16), w_down[layer].astype(BF16))
        x = _combine(pad_start, cnt, base, x1, y_sorted, route, gate_ffn, g_final)
    return x
```

```python
import functools

import numpy as np
import jax
import jax.numpy as jnp
from jax import lax
from jax.experimental import pallas as pl
from jax.experimental.pallas import tpu as pltpu

HEAD_DIM = 64
N_HEADS_DIL = 8
N_HEADS_SB = 8
D_DIL = N_HEADS_DIL * HEAD_DIM
D_SB = N_HEADS_SB * HEAD_DIM
DILATION_PATTERNS = ((128, 1), (512, 4), (2048, 16))
N_GROUPS = 4
EXPERTS_PER_GROUP = 8
N_EXPERTS = N_GROUPS * EXPERTS_PER_GROUP
NORM_EPS = 1e-6

LANES = 128
SUBLANES = 8
DIL_STEPS = 128
DIL_UNIT = 2048
DIL_TILES_PER_TRIP = 4
SB_BLOCK = 256
SB_QUERY_ROWS = 512
SB_BLOCKS_PER_TRIP = 2
PRE_ROWS = 512
POST_ROWS = 512
LOCAL_ROWS = 2 * POST_ROWS + 256
EXPERT_ROWS = 512
ROUTE_LANE0 = N_GROUPS
VMEM_LIMIT = 56 * 1024 * 1024

F32 = jnp.float32
BF16 = jnp.bfloat16
NEG_INF = float("-inf")


def _cparams(sem):
    return pltpu.CompilerParams(dimension_semantics=sem, vmem_limit_bytes=VMEM_LIMIT)


def _rms(v, g):
    return v * lax.rsqrt(jnp.mean(v * v, axis=-1, keepdims=True) + NORM_EPS) * g


def _split3(v):
    hi = v.astype(BF16)
    r = v - hi.astype(F32)
    mid = r.astype(BF16)
    lo = (r - mid.astype(F32)).astype(BF16)
    return hi, mid, lo


def _ada_kernel(c_ref, w_ref, b_ref, o_ref):
    c = c_ref[...]
    cond = c / (1.0 + jnp.exp(-c))
    o_ref[...] = jnp.dot(cond, w_ref[...], precision=lax.Precision.HIGHEST,
                         preferred_element_type=F32) + b_ref[...]


def _ada(c, w_ada, b_ada):
    b, d = c.shape
    n = w_ada.shape[1]
    return pl.pallas_call(
        _ada_kernel,
        grid=(n // d,),
        in_specs=[pl.BlockSpec((b, d), lambda j: (0, 0)),
                  pl.BlockSpec((d, d), lambda j: (0, j)),
                  pl.BlockSpec((1, d), lambda j: (0, j))],
        out_specs=pl.BlockSpec((b, d), lambda j: (0, j)),
        out_shape=jax.ShapeDtypeStruct((b, n), F32),
        compiler_params=_cparams(("arbitrary",)),
        name="ada",
    )(c, w_ada, b_ada.reshape(1, n))


def _premix_kernel(x_ref, shift_ref, scale_ref, g_ref, w_ref, qkvd_ref, qs_ref, ks_ref, vs_ref):
    h = _rms(x_ref[0], g_ref[...]) * (1.0 + scale_ref[0]) + shift_ref[0]
    hb = h.astype(BF16)
    scale = HEAD_DIM ** -0.5
    for j in range(6):
        r = jnp.dot(hb, w_ref[:, j * 512:(j + 1) * 512], preferred_element_type=F32)
        if j == 0:
            qkvd_ref[0, :, 0:512] = r * scale
        elif j < 3:
            qkvd_ref[0, :, j * 512:(j + 1) * 512] = r
        elif j == 3:
            qs_ref[0] = (r * scale).astype(BF16)
        elif j == 4:
            ks_ref[0] = r.astype(BF16)
        else:
            vs_ref[0] = r.astype(BF16)


def _premix(x, shift, scale, g_mix, w_in_bf16):
    b, s, d = x.shape
    tm = PRE_ROWS
    mod_spec = pl.BlockSpec((1, 1, d), lambda bi, i: (bi, 0, 0))
    sb_spec = pl.BlockSpec((1, tm, D_SB), lambda bi, i: (bi, i, 0))
    return pl.pallas_call(
        _premix_kernel,
        grid=(b, s // tm),
        in_specs=[pl.BlockSpec((1, tm, d), lambda bi, i: (bi, i, 0)),
                  mod_spec, mod_spec,
                  pl.BlockSpec((1, d), lambda bi, i: (0, 0)),
                  pl.BlockSpec((d, 3 * (D_DIL + D_SB)), lambda bi, i: (0, 0))],
        out_specs=[pl.BlockSpec((1, tm, 3 * D_DIL), lambda bi, i: (bi, i, 0)),
                   sb_spec, sb_spec, sb_spec],
        out_shape=[jax.ShapeDtypeStruct((b, s, 3 * D_DIL), F32),
                   jax.ShapeDtypeStruct((b, s, D_SB), BF16),
                   jax.ShapeDtypeStruct((b, s, D_SB), BF16),
                   jax.ShapeDtypeStruct((b, s, D_SB), BF16)],
        compiler_params=_cparams(("arbitrary", "arbitrary")),
        name="premix",
    )(x, shift, scale, g_mix.reshape(1, d), w_in_bf16)


def _dilated_bias():
    n = DIL_STEPS
    slopes = np.array([2.0 ** (-8.0 * (i + 1) / N_HEADS_DIL) for i in range(N_HEADS_DIL)], dtype=np.float32)
    steps = np.arange(n)[:, None] + n - np.arange(2 * n)[None, :]
    valid = (steps >= 0) & (steps <= n)
    out = []
    for _, dilation in DILATION_PATTERNS:
        bias = -slopes[:, None, None] * (steps * dilation).astype(np.float32)[None]
        out.append(np.where(valid[None], bias, -np.inf).astype(np.float32))
    return np.stack(out)


def _dil_kernel(q_ref, kc_ref, kp_ref, vc_ref, vp_ref, bias_ref, o_ref,
                kext, vext, u_scr, m_scr, l_scr):
    n = DIL_STEPS
    g = pl.program_id(1)
    kext[0:DIL_UNIT, :] = kp_ref[0]
    kext[DIL_UNIT:2 * DIL_UNIT, :] = kc_ref[0]
    vext[0:DIL_UNIT, :] = vp_ref[0]
    vext[DIL_UNIT:2 * DIL_UNIT, :] = vc_ref[0]
    lane = lax.broadcasted_iota(jnp.int32, (n, LANES), 1)
    head0 = lane < HEAD_DIM
    col = lax.broadcasted_iota(jnp.int32, (n, 2 * n), 1)

    for p, (_, dil) in enumerate(DILATION_PATTERNS):
        unit = n * dil

        def tiles(it, carry, p=p, dil=dil, unit=unit):
            rows_of, vvs, deads, ss = [], [], [], []
            for t in range(DIL_TILES_PER_TRIP):
                ti = it * DIL_TILES_PER_TRIP + t
                j = ti // dil
                r = ti % dil
                qstart = j * unit + r
                kstart = DIL_UNIT + qstart - unit
                if dil == 1:
                    rows_of.append(pl.ds(qstart, n))
                    krows = pl.ds(kstart, 2 * n)
                else:
                    rows_of.append(pl.ds(qstart, n, stride=dil))
                    krows = pl.ds(kstart, 2 * n, stride=dil)
                q = q_ref[0, rows_of[t], :]
                kk = kext[krows, :].astype(BF16)
                vvs.append(vext[krows, :].astype(BF16))
                deads.append(jnp.where(jnp.logical_and(g == 0, j == 0), n, 0))
                for h in range(2):
                    qh = jnp.where(head0 if h == 0 else jnp.logical_not(head0), q, 0.0).astype(BF16)
                    ss.append(lax.dot_general(qh, kk, (((1,), (1,)), ((), ())), preferred_element_type=F32))
            ms, ls, pes = [], [], []
            for t in range(DIL_TILES_PER_TRIP):
                for h in range(2):
                    logits = jnp.where(col < deads[t], NEG_INF, ss[2 * t + h] + bias_ref[p, h])
                    m = jnp.max(logits, axis=-1, keepdims=True)
                    pe = jnp.exp(logits - m)
                    ls.append(jnp.sum(pe, axis=-1, keepdims=True))
                    ms.append(m)
                    pes.append(pe.astype(BF16))
            us = [jnp.dot(pes[2 * t + h], vvs[t], preferred_element_type=F32)
                  for t in range(DIL_TILES_PER_TRIP) for h in range(2)]
            for t in range(DIL_TILES_PER_TRIP):
                u_scr[p, rows_of[t], :] = jnp.where(head0, us[2 * t], us[2 * t + 1])
                m_scr[p, rows_of[t], :] = jnp.where(head0, ms[2 * t], ms[2 * t + 1])
                l_scr[p, rows_of[t], :] = jnp.where(head0, ls[2 * t], ls[2 * t + 1])
            return carry

        lax.fori_loop(0, DIL_UNIT // n // DIL_TILES_PER_TRIP, tiles, 0)

    def merge(i, carry):
        rows = pl.ds(pl.multiple_of(i * n, n), n)
        m0, m1, m2 = m_scr[0, rows, :], m_scr[1, rows, :], m_scr[2, rows, :]
        mx = jnp.maximum(jnp.maximum(m0, m1), m2)
        w0, w1, w2 = jnp.exp(m0 - mx), jnp.exp(m1 - mx), jnp.exp(m2 - mx)
        num = w0 * u_scr[0, rows, :] + w1 * u_scr[1, rows, :] + w2 * u_scr[2, rows, :]
        den = w0 * l_scr[0, rows, :] + w1 * l_scr[1, rows, :] + w2 * l_scr[2, rows, :]
        o_ref[0, rows, :] = num / den
        return carry

    lax.fori_loop(0, DIL_UNIT // n, merge, 0)


def _dilated(qkv_d, bias):
    b, s, _ = qkv_d.shape
    u = DIL_UNIT
    npair = D_DIL // LANES
    cur = lambda off: pl.BlockSpec((1, u, LANES), lambda bi, g, p: (bi, g, off + p))
    prev = lambda off: pl.BlockSpec((1, u, LANES), lambda bi, g, p: (bi, jnp.maximum(g - 1, 0), off + p))
    return pl.pallas_call(
        _dil_kernel,
        grid=(b, s // u, npair),
        in_specs=[cur(0), cur(npair), prev(npair), cur(2 * npair), prev(2 * npair),
                  pl.BlockSpec((3, 2, DIL_STEPS, 2 * DIL_STEPS), lambda bi, g, p: (0, p, 0, 0))],
        out_specs=pl.BlockSpec((1, u, LANES), lambda bi, g, p: (bi, g, p)),
        out_shape=jax.ShapeDtypeStruct((b, s, D_DIL), F32),
        scratch_shapes=[pltpu.VMEM((2 * u, LANES), F32), pltpu.VMEM((2 * u, LANES), F32),
                        pltpu.VMEM((3, u, LANES), F32), pltpu.VMEM((3, u, LANES), F32),
                        pltpu.VMEM((3, u, LANES), F32)],
        compiler_params=_cparams(("arbitrary", "arbitrary", "arbitrary")),
        name="dilated",
    )(qkv_d, qkv_d, qkv_d, qkv_d, qkv_d, bias)


def _stick_kernel(q_ref, k_ref, v_ref, tri_ref, o_ref,
                  qh_scr, z_scr, w_scr, acc_scr, carry_scr, scale_scr):
    blk = SB_BLOCK
    nsub = SB_QUERY_ROWS // blk
    assert nsub % 2 == 0
    nchain = 2 * nsub
    qi = pl.program_id(2)
    lane = lax.broadcasted_iota(jnp.int32, (blk, LANES), 1)
    head0 = lane < HEAD_DIM
    for sub in range(nsub):
        q = q_ref[0, sub * blk:(sub + 1) * blk, :]
        zero = jnp.zeros_like(q)
        qh_scr[2 * sub] = jnp.where(head0, q, zero)
        qh_scr[2 * sub + 1] = jnp.where(head0, zero, q)
    acc_scr[...] = jnp.zeros_like(acc_scr)
    carry_scr[...] = jnp.zeros_like(carry_scr)
    sign = jnp.int32(-2 ** 31)

    def rows(kb):
        return pl.ds(pl.multiple_of(kb * blk, blk), blk)

    def scores(kb, which, slot):
        kblk = k_ref[0, rows(kb), :]
        for c in which:
            z_scr[slot * nchain + c] = lax.dot_general(
                qh_scr[c], kblk, (((1,), (1,)), ((), ())), preferred_element_type=F32)

    def weights(which, slot, diag_sub):
        causal = (lax.broadcasted_iota(jnp.int32, (blk, blk), 1)
                  < lax.broadcasted_iota(jnp.int32, (blk, blk), 0))
        splits = {}
        for c in which:
            z = z_scr[slot * nchain + c]
            neg_abs = lax.bitcast_convert_type(lax.bitcast_convert_type(z, jnp.int32) | sign, F32)
            softplus = jnp.maximum(z, 0.0) + jnp.log(1.0 + jnp.exp(neg_abs))
            if c // 2 == diag_sub:
                softplus = jnp.where(causal, softplus, 0.0)
            hi = softplus.astype(BF16)
            lo = (softplus - hi.astype(F32)).astype(BF16)
            splits[c] = jnp.concatenate([hi, lo], axis=1)
        sums = {c: jnp.dot(splits[c], tri_ref[...], preferred_element_type=F32) for c in which}
        for c in which:
            w = jnp.exp(z_scr[slot * nchain + c] - sums[c])
            if c // 2 == diag_sub:
                w = jnp.where(causal, w, 0.0)
            w_scr[c] = w.astype(BF16)
            carry = carry_scr[c]
            scale_scr[c] = jnp.exp(carry)
            carry_scr[c] = carry - sums[c][:, 0:1]

    def accumulate(which, kb):
        vblk = v_ref[0, rows(kb), :]
        for c in which:
            acc_scr[c] = acc_scr[c] + scale_scr[c] * jnp.dot(w_scr[c], vblk, preferred_element_type=F32)

    everyone = list(range(nchain))
    top = nsub * qi + nsub - 1
    first = nsub * qi - 1
    for i in range(nsub):
        which = [c for c in everyone if c // 2 >= nsub - 1 - i]
        scores(top - i, which, 1)
        if i == nsub - 1:
            scores(jnp.maximum(first, 0), everyone, 0)
        weights(which, 1, nsub - 1 - i)
        if i < nsub - 1:
            accumulate(which, top - i)

    def blocks(kb0, count):
        for j in range(count):
            kb = kb0 - j
            accumulate(everyone, kb + 1)
            scores(jnp.maximum(kb - 1, 0), everyone, 1 - j % 2)
            weights(everyone, j % 2, -1)

    per = SB_BLOCKS_PER_TRIP
    trips = nsub * qi // per

    def step(i, carry):
        blocks(first - per * i, per)
        return carry

    lax.fori_loop(0, trips, step, 0)
    for rest in range(2, per, 2):
        @pl.when(nsub * qi - trips * per == rest)
        def _(rest=rest):
            blocks(first - per * trips, rest)
    accumulate(everyone, 0)
    for sub in range(nsub):
        o_ref[0, sub * blk:(sub + 1) * blk, :] = jnp.where(head0, acc_scr[2 * sub], acc_scr[2 * sub + 1])


def _stick(q_s, k_s, v_s):
    b, s, _ = q_s.shape
    blk = SB_BLOCK
    qrows = SB_QUERY_ROWS
    nchain = 2 * qrows // blk
    tri = np.tril(np.ones((blk, blk), np.float32))
    tri2 = jnp.asarray(np.concatenate([tri, tri], axis=0), BF16)
    full = pl.BlockSpec((1, s, LANES), lambda bi, p, i: (bi, 0, p))
    return pl.pallas_call(
        _stick_kernel,
        grid=(b, D_SB // LANES, s // qrows),
        in_specs=[pl.BlockSpec((1, qrows, LANES), lambda bi, p, i: (bi, i, p)), full, full,
                  pl.BlockSpec((2 * blk, blk), lambda bi, p, i: (0, 0))],
        out_specs=pl.BlockSpec((1, qrows, LANES), lambda bi, p, i: (bi, i, p)),
        out_shape=jax.ShapeDtypeStruct((b, s, D_SB), F32),
        scratch_shapes=[pltpu.VMEM((nchain, blk, LANES), BF16),
                        pltpu.VMEM((2 * nchain, blk, blk), F32),
                        pltpu.VMEM((nchain, blk, blk), BF16),
                        pltpu.VMEM((nchain, blk, LANES), F32),
                        pltpu.VMEM((nchain, blk, 1), F32),
                        pltpu.VMEM((nchain, blk, 1), F32)],
        compiler_params=_cparams(("arbitrary", "arbitrary", "arbitrary")),
        name="stick",
    )(q_s, k_s, v_s, tri2)


def _postmix_kernel(x_ref, od_ref, os_ref, gd_ref, gs_ref, wout_ref, gate_ref, shift_ref, scale_ref,
                    gffn_ref, wr_ref, tril_ref, triu_ref,
                    x1_ref, h2_ref, route_ref, cnt_ref, base_ref, carry_scr):
    tm = POST_ROWS

    @pl.when(jnp.logical_and(pl.program_id(0) == 0, pl.program_id(1) == 0))
    def _():
        carry_scr[...] = jnp.zeros_like(carry_scr)

    mixed = jnp.concatenate([_rms(od_ref[0], gd_ref[...]), _rms(os_ref[0], gs_ref[...])], axis=-1)
    proj = jnp.dot(mixed.astype(BF16), wout_ref[...], preferred_element_type=F32)
    x1 = x_ref[0] + gate_ref[0] * proj
    x1_ref[0] = x1
    h2 = _rms(x1, gffn_ref[...]) * (1.0 + scale_ref[0]) + shift_ref[0]
    h_hi = h2.astype(BF16)
    h_lo = (h2 - h_hi.astype(F32)).astype(BF16)
    wr = wr_ref[...]
    w_hi = wr.astype(BF16)
    w_lo = (wr - w_hi.astype(F32)).astype(BF16)
    logits = jnp.dot(jnp.concatenate([h_hi, h_lo, h_hi], axis=1),
                     jnp.concatenate([w_hi, w_hi, w_lo], axis=0), preferred_element_type=F32)

    lane = lax.broadcasted_iota(jnp.int32, (tm, LANES), 1)
    big = jnp.int32(LANES)
    lmax = lambda v: jnp.max(v, axis=-1, keepdims=True)
    lmin = lambda v: jnp.min(v, axis=-1, keepdims=True)
    lsum = lambda v: jnp.sum(v, axis=-1, keepdims=True)

    gmask = lane < N_GROUPS
    gl = jnp.where(gmask, logits, NEG_INF)
    gmx = lmax(gl)
    group = lmin(jnp.where(jnp.logical_and(gmask, gl == gmx), lane, big))
    group_gate = 1.0 / lsum(jnp.exp(gl - gmx))
    lo = ROUTE_LANE0 + group * EXPERTS_PER_GROUP
    emask = jnp.logical_and(lane >= lo, lane < lo + EXPERTS_PER_GROUP)
    el = jnp.where(emask, logits, NEG_INF)
    l1 = lmax(el)
    i1 = lmin(jnp.where(el == l1, lane, big))
    el2 = jnp.where(lane == i1, NEG_INF, el)
    l2 = lmax(el2)
    i2 = lmin(jnp.where(el2 == l2, lane, big))
    r = jnp.exp(l2 - l1)
    w1 = group_gate / (1.0 + r)
    w2 = group_gate * r / (1.0 + r)

    is1 = lane == i1
    is2 = lane == i2
    oh = jnp.where(is1, 1.0, jnp.where(is2, 1.0, 0.0))
    earlier = jnp.dot(tril_ref[...], oh.astype(BF16), preferred_element_type=F32)
    runs = jnp.floor((jnp.sum(oh, axis=0, keepdims=True) + (SUBLANES - 1.0)) * (1.0 / SUBLANES))
    run_off = jnp.dot(jnp.broadcast_to(runs, (SUBLANES, LANES)).astype(BF16), triu_ref[...],
                      preferred_element_type=F32)[0:1]
    pos = earlier + run_off * SUBLANES
    slot1 = lsum(jnp.where(is1, pos, 0.0))
    slot2 = lsum(jnp.where(is2, pos, 0.0))
    cnt = runs * SUBLANES
    cnt_ref[0] = cnt
    base_ref[0] = carry_scr[...]
    carry_scr[...] = carry_scr[...] + cnt

    h2_ref[0] = h2.astype(BF16)
    route_ref[0] = jnp.where(lane == 0, slot1, jnp.where(lane == 1, slot2,
                                                         jnp.where(lane == 2, w1, jnp.where(lane == 3, w2, 0.0))))


def _postmix(x, o_dil, o_sb, g_dil, g_sb, w_out_bf16, gate, shift, scale, g_ffn, w_router):
    b, s, d = x.shape
    tm = POST_ROWS
    nt = s // tm
    tril = jnp.asarray(np.tril(np.ones((tm, tm), np.float32), -1), BF16)
    triu = jnp.asarray(np.triu(np.ones((LANES, LANES), np.float32), 1), BF16)
    row = lambda w: pl.BlockSpec((1, tm, w), lambda bi, i: (bi, i, 0))
    vec = lambda w: pl.BlockSpec((1, w), lambda bi, i: (0, 0))
    mod_spec = pl.BlockSpec((1, 1, d), lambda bi, i: (bi, 0, 0))
    tile_vec = pl.BlockSpec((1, 1, LANES), lambda bi, i: (bi * nt + i, 0, 0))
    return pl.pallas_call(
        _postmix_kernel,
        grid=(b, nt),
        in_specs=[row(d), row(D_DIL), row(D_SB), vec(D_DIL), vec(D_SB),
                  pl.BlockSpec((d, d), lambda bi, i: (0, 0)),
                  mod_spec, mod_spec, mod_spec, vec(d),
                  pl.BlockSpec((d, LANES), lambda bi, i: (0, 0)),
                  pl.BlockSpec((tm, tm), lambda bi, i: (0, 0)),
                  pl.BlockSpec((LANES, LANES), lambda bi, i: (0, 0))],
        out_specs=[row(d), row(d), row(LANES), tile_vec, tile_vec],
        out_shape=[jax.ShapeDtypeStruct((b, s, d), F32),
                   jax.ShapeDtypeStruct((b, s, d), BF16),
                   jax.ShapeDtypeStruct((b, s, LANES), F32),
                   jax.ShapeDtypeStruct((b * nt, 1, LANES), F32),
                   jax.ShapeDtypeStruct((b * nt, 1, LANES), F32)],
        scratch_shapes=[pltpu.VMEM((1, LANES), F32)],
        compiler_params=_cparams(("arbitrary", "arbitrary")),
        name="postmix",
    )(x, o_dil, o_sb, g_dil.reshape(1, -1), g_sb.reshape(1, -1), w_out_bf16, gate, shift, scale,
      g_ffn.reshape(1, d), w_router, tril, triu)


def _for_each_run_piece(tile, start_ref, cnt_ref, base_ref, fn):
    def body(e, off):
        c = cnt_ref[tile * N_EXPERTS + e]
        sorted0 = start_ref[e] + base_ref[tile * N_EXPERTS + e]
        for k in range(3, 10):
            p = 1 << k

            @pl.when((c & p) != 0)
            def _(p=p):
                done = c - (c & (2 * p - 1))
                fn(pl.multiple_of(off + done, SUBLANES), pl.multiple_of(sorted0 + done, SUBLANES), p)
        return off + c

    return lax.fori_loop(0, N_EXPERTS, body, 0)


def _sort_kernel(start_ref, cnt_ref, base_ref, h2_ref, route_ref, buf_ref, xs_scr, sem):
    tm = POST_ROWS
    lt = LOCAL_ROWS
    d = h2_ref.shape[2]
    tile = pl.program_id(0)
    slot = tile % 2

    def piece(slot, lrow, srow, rows):
        return pltpu.make_async_copy(xs_scr.at[slot, pl.ds(lrow, rows)], buf_ref.at[pl.ds(srow, rows)],
                                     sem.at[slot])

    def drain(tile, slot):
        _for_each_run_piece(tile, start_ref, cnt_ref, base_ref, lambda *a: piece(slot, *a).wait())

    @pl.when(tile >= 2)
    def _():
        drain(tile - 2, slot)

    lane = lax.broadcasted_iota(jnp.int32, (tm, LANES), 1)
    route = route_ref[0]
    w1 = jnp.sum(jnp.where(lane == 2, route, 0.0), axis=-1, keepdims=True)
    w2 = jnp.sum(jnp.where(lane == 3, route, 0.0), axis=-1, keepdims=True)

    def pieces(w):
        hi, mid, lw = _split3(w)
        return jnp.where(lane == 0, hi.astype(F32),
                         jnp.where(lane == 1, mid.astype(F32),
                                   jnp.where(lane == 2, lw.astype(F32), 0.0))).astype(BF16)

    route_t = route.T
    s1 = route_t[0:1, :].astype(jnp.int32)
    s2 = route_t[1:2, :].astype(jnp.int32)
    row = lax.broadcasted_iota(jnp.int32, (lt, tm), 0)
    p1 = jnp.where(row == s1, 1.0, 0.0)
    p2 = jnp.where(row == s2, 1.0, 0.0)
    xs_scr[slot, :, 0:d] = jnp.dot((p1 + p2).astype(BF16), h2_ref[0], preferred_element_type=F32)
    xs_scr[slot, :, d:] = (jnp.dot(p1.astype(BF16), pieces(w1), preferred_element_type=F32)
                           + jnp.dot(p2.astype(BF16), pieces(w2), preferred_element_type=F32))
    _for_each_run_piece(tile, start_ref, cnt_ref, base_ref, lambda *a: piece(slot, *a).start())

    last = pl.num_programs(0) - 1

    @pl.when(jnp.logical_and(tile == last, tile >= 1))
    def _():
        drain(tile - 1, 1 - slot)

    @pl.when(tile == last)
    def _():
        drain(tile, slot)


def _dispatch(pad_start, cnt, base, h2, route, cap):
    b, s, d = h2.shape
    tm = POST_ROWS
    nt = s // tm
    return pl.pallas_call(
        _sort_kernel,
        grid_spec=pltpu.PrefetchScalarGridSpec(
            num_scalar_prefetch=3, grid=(b * nt,),
            in_specs=[pl.BlockSpec((1, tm, d), lambda t, *_: (t // nt, t % nt, 0)),
                      pl.BlockSpec((1, tm, LANES), lambda t, *_: (t // nt, t % nt, 0))],
            out_specs=pl.BlockSpec(memory_space=pl.ANY),
            scratch_shapes=[pltpu.VMEM((2, LOCAL_ROWS, d + LANES), F32), pltpu.SemaphoreType.DMA((2,))]),
        out_shape=jax.ShapeDtypeStruct((cap, d + LANES), F32),
        compiler_params=_cparams(("arbitrary",)),
        name="dispatch",
    )(pad_start, cnt, base, h2, route)


def _expert_kernel(be_ref, live_ref, x_ref, wg_ref, wu_ref, wd_ref, y_ref):
    del be_ref
    d = y_ref.shape[1]
    live = live_ref[pl.program_id(0)]

    @pl.when(live > 0)
    def _():
        keep = lax.broadcasted_iota(jnp.int32, (x_ref.shape[0], 1), 0) < live
        x = jnp.where(keep, x_ref[...], 0.0)
        xb = x[:, 0:d].astype(BF16)
        weight = jnp.sum(x[:, d:], axis=-1, keepdims=True)
        gate = jnp.dot(xb, wg_ref[0], preferred_element_type=F32)
        up = jnp.dot(xb, wu_ref[0], preferred_element_type=F32)
        act = gate / (1.0 + jnp.exp(-gate)) * up
        y_ref[...] = jnp.dot(act.astype(BF16), wd_ref[0], preferred_element_type=F32) * weight

    @pl.when(live == 0)
    def _():
        y_ref[...] = jnp.zeros_like(y_ref)


def _experts(block_expert, live_rows, buf, wg, wu, wd):
    cap, dw = buf.shape
    d, f = wg.shape[1], wg.shape[2]
    bm = EXPERT_ROWS
    return pl.pallas_call(
        _expert_kernel,
        grid_spec=pltpu.PrefetchScalarGridSpec(
            num_scalar_prefetch=2, grid=(cap // bm,),
            in_specs=[pl.BlockSpec((bm, dw), lambda i, be, nu: (i, 0)),
                      pl.BlockSpec((1, d, f), lambda i, be, nu: (be[i], 0, 0)),
                      pl.BlockSpec((1, d, f), lambda i, be, nu: (be[i], 0, 0)),
                      pl.BlockSpec((1, f, d), lambda i, be, nu: (be[i], 0, 0))],
            out_specs=pl.BlockSpec((bm, d), lambda i, be, nu: (i, 0))),
        out_shape=jax.ShapeDtypeStruct((cap, d), F32),
        compiler_params=_cparams(("arbitrary",)),
        name="experts",
    )(block_expert, live_rows, buf, wg, wu, wd)


def _combine_kernel(start_ref, cnt_ref, base_ref, x1_ref, route_ref, gate_ref, g_ref, y_hbm_ref, o_ref,
                    y_scr, sem):
    tm = POST_ROWS
    lt = LOCAL_ROWS
    tile = pl.program_id(0)
    slot = tile % 2

    def piece(slot, lrow, srow, rows):
        return pltpu.make_async_copy(y_hbm_ref.at[pl.ds(srow, rows)], y_scr.at[slot, pl.ds(lrow, rows)],
                                     sem.at[slot])

    def fetch(tile, slot):
        _for_each_run_piece(tile, start_ref, cnt_ref, base_ref, lambda *a: piece(slot, *a).start())

    @pl.when(tile == 0)
    def _():
        fetch(tile, slot)

    @pl.when(tile + 1 < pl.num_programs(0))
    def _():
        fetch(tile + 1, 1 - slot)

    lane = lax.broadcasted_iota(jnp.int32, (tm, LANES), 1)
    route = route_ref[0]
    s1 = jnp.sum(jnp.where(lane == 0, route, 0.0), axis=-1, keepdims=True).astype(jnp.int32)
    s2 = jnp.sum(jnp.where(lane == 1, route, 0.0), axis=-1, keepdims=True).astype(jnp.int32)
    col = lax.broadcasted_iota(jnp.int32, (tm, lt), 1)
    pick = jnp.where(col == s1, 1.0, jnp.where(col == s2, 1.0, 0.0)).astype(BF16)
    used = _for_each_run_piece(tile, start_ref, cnt_ref, base_ref, lambda *a: piece(slot, *a).wait())
    live = lax.broadcasted_iota(jnp.int32, (lt, 1), 0) < used
    yv = jnp.where(live, y_scr[slot], 0.0)
    hi = yv.astype(BF16)
    lo = (yv - hi.astype(F32)).astype(BF16)
    y = jnp.dot(jnp.concatenate([pick, pick], axis=1), jnp.concatenate([hi, lo], axis=0),
                preferred_element_type=F32)
    o_ref[0] = _rms(x1_ref[0] + gate_ref[0] * y, g_ref[...])


def _combine(pad_start, cnt, base, x1, y_sorted, route, gate, g_final):
    b, s, d = x1.shape
    tm = POST_ROWS
    nt = s // tm
    return pl.pallas_call(
        _combine_kernel,
        grid_spec=pltpu.PrefetchScalarGridSpec(
            num_scalar_prefetch=3, grid=(b * nt,),
            in_specs=[pl.BlockSpec((1, tm, d), lambda t, *_: (t // nt, t % nt, 0)),
                      pl.BlockSpec((1, tm, LANES), lambda t, *_: (t // nt, t % nt, 0)),
                      pl.BlockSpec((1, 1, d), lambda t, *_: (t // nt, 0, 0)),
                      pl.BlockSpec((1, d), lambda t, *_: (0, 0)),
                      pl.BlockSpec(memory_space=pl.ANY)],
            out_specs=pl.BlockSpec((1, tm, d), lambda t, *_: (t // nt, t % nt, 0)),
            scratch_shapes=[pltpu.VMEM((2, LOCAL_ROWS, d), F32), pltpu.SemaphoreType.DMA((2,))]),
        out_shape=jax.ShapeDtypeStruct((b, s, d), F32),
        compiler_params=_cparams(("arbitrary",)),
        name="combine",
    )(pad_start, cnt, base, x1, route, gate, g_final.reshape(1, d), y_sorted)


def kernel(x, c, w_ada, b_ada, g_mix, w_in, g_dil_out, g_sb_out, w_out, g_ffn,
           w_group, w_expert, w_gate, w_up, w_down, g_final):
    b, s, d = x.shape
    depth = w_ada.shape[0]
    assert s % DIL_UNIT == 0 and d == D_DIL + D_SB
    assert depth == 1, "the final rmsnorm is fused into the last layer's combine step"
    n = b * s
    ntiles = n // POST_ROWS
    bias = jnp.asarray(_dilated_bias())
    for layer in range(depth):
        mod = _ada(c, w_ada[layer], b_ada[layer])
        shift_mix, scale_mix, gate_mix, shift_ffn, scale_ffn, gate_ffn = (
            m.reshape(b, 1, d) for m in jnp.split(mod, 6, axis=-1))

        qkv_d, q_s, k_s, v_s = _premix(x, shift_mix, scale_mix, g_mix[layer], w_in[layer].astype(BF16))
        o_dil = _dilated(qkv_d, bias)
        o_sb = _stick(q_s, k_s, v_s)

        w_router = jnp.concatenate(
            [w_group[layer], w_expert[layer],
             jnp.zeros((d, LANES - N_GROUPS - N_EXPERTS), F32)], axis=1)
        x1, h2, route, cnt, base = _postmix(
            x, o_dil, o_sb, g_dil_out[layer], g_sb_out[layer], w_out[layer].astype(BF16),
            gate_mix, shift_ffn, scale_ffn, g_ffn[layer], w_router)

        bm = EXPERT_ROWS
        cnt = cnt[:, 0, ROUTE_LANE0:ROUTE_LANE0 + N_EXPERTS].astype(jnp.int32)
        base = base[:, 0, ROUTE_LANE0:ROUTE_LANE0 + N_EXPERTS].astype(jnp.int32)
        total = base[-1] + cnt[-1]
        cnt = cnt.reshape(-1)
        base = base.reshape(-1)
        padded = (total + bm - 1) // bm * bm
        pad_end = jnp.cumsum(padded)
        pad_start = (pad_end - padded).astype(jnp.int32)
        cap = -(-(2 * n + (SUBLANES - 1) * N_EXPERTS * ntiles) // bm) * bm + N_EXPERTS * bm
        n_blocks = cap // bm
        block_expert = jnp.minimum(
            jnp.sum(pad_end[None, :] <= (jnp.arange(n_blocks) * bm)[:, None], axis=1),
            N_EXPERTS - 1).astype(jnp.int32)
        live_rows = jnp.clip((pad_start + total)[block_expert] - jnp.arange(n_blocks) * bm, 0, bm).astype(jnp.int32)

        buf = _dispatch(pad_start, cnt, base, h2, route, cap)
        y_sorted = _experts(block_expert, live_rows, buf, w_gate[layer].astype(BF16),
                            w_up[layer].astype(BF16), w_down[layer].astype(BF16))
        x = _combine(pad_start, cnt, base, x1, y_sorted, route, gate_ffn, g_final)
    return x
```

```python
import functools

import numpy as np
import jax
import jax.numpy as jnp
from jax import lax
from jax.experimental import pallas as pl
from jax.experimental.pallas import tpu as pltpu

HEAD_DIM = 64
N_HEADS_DIL = 8
N_HEADS_SB = 8
D_DIL = N_HEADS_DIL * HEAD_DIM
D_SB = N_HEADS_SB * HEAD_DIM
DILATION_PATTERNS = ((128, 1), (512, 4), (2048, 16))
N_GROUPS = 4
EXPERTS_PER_GROUP = 8
N_EXPERTS = N_GROUPS * EXPERTS_PER_GROUP
NORM_EPS = 1e-6

LANES = 128
SUBLANES = 8
DIL_STEPS = 128
DIL_UNIT = 2048
DIL_TILES_PER_TRIP = 4
SB_BLOCK = 256
SB_QUERY_ROWS = 512
SB_BLOCKS_PER_TRIP = 2
PRE_ROWS = 512
POST_ROWS = 512
LOCAL_ROWS = 2 * POST_ROWS + 256
EXPERT_ROWS = 512
ROUTE_LANE0 = N_GROUPS
VMEM_LIMIT = 56 * 1024 * 1024

F32 = jnp.float32
BF16 = jnp.bfloat16
NEG_INF = float("-inf")


def _cparams(sem):
    return pltpu.CompilerParams(dimension_semantics=sem, vmem_limit_bytes=VMEM_LIMIT)


def _rms(v, g):
    return v * lax.rsqrt(jnp.mean(v * v, axis=-1, keepdims=True) + NORM_EPS) * g


def _split3(v):
    hi = v.astype(BF16)
    r = v - hi.astype(F32)
    mid = r.astype(BF16)
    lo = (r - mid.astype(F32)).astype(BF16)
    return hi, mid, lo


def _ada_kernel(c_ref, w_ref, b_ref, o_ref):
    c = c_ref[...]
    cond = c / (1.0 + jnp.exp(-c))
    o_ref[...] = jnp.dot(cond, w_ref[...], precision=lax.Precision.HIGHEST,
                         preferred_element_type=F32) + b_ref[...]


def _ada(c, w_ada, b_ada):
    b, d = c.shape
    n = w_ada.shape[1]
    return pl.pallas_call(
        _ada_kernel,
        grid=(n // d,),
        in_specs=[pl.BlockSpec((b, d), lambda j: (0, 0)),
                  pl.BlockSpec((d, d), lambda j: (0, j)),
                  pl.BlockSpec((1, d), lambda j: (0, j))],
        out_specs=pl.BlockSpec((b, d), lambda j: (0, j)),
        out_shape=jax.ShapeDtypeStruct((b, n), F32),
        compiler_params=_cparams(("arbitrary",)),
        name="ada",
    )(c, w_ada, b_ada.reshape(1, n))


def _premix_kernel(x_ref, shift_ref, scale_ref, g_ref, w_ref, qkvd_ref, qs_ref, ks_ref, vs_ref):
    h = _rms(x_ref[0], g_ref[...]) * (1.0 + scale_ref[0]) + shift_ref[0]
    hb = h.astype(BF16)
    scale = HEAD_DIM ** -0.5
    for j in range(6):
        r = jnp.dot(hb, w_ref[:, j * 512:(j + 1) * 512], preferred_element_type=F32)
        if j == 0:
            qkvd_ref[0, :, 0:512] = r * scale
        elif j < 3:
            qkvd_ref[0, :, j * 512:(j + 1) * 512] = r
        elif j == 3:
            qs_ref[0] = (r * scale).astype(BF16)
        elif j == 4:
            ks_ref[0] = r.astype(BF16)
        else:
            vs_ref[0] = r.astype(BF16)


def _premix(x, shift, scale, g_mix, w_in_bf16):
    b, s, d = x.shape
    tm = PRE_ROWS
    mod_spec = pl.BlockSpec((1, 1, d), lambda bi, i: (bi, 0, 0))
    sb_spec = pl.BlockSpec((1, tm, D_SB), lambda bi, i: (bi, i, 0))
    return pl.pallas_call(
        _premix_kernel,
        grid=(b, s // tm),
        in_specs=[pl.BlockSpec((1, tm, d), lambda bi, i: (bi, i, 0)),
                  mod_spec, mod_spec,
                  pl.BlockSpec((1, d), lambda bi, i: (0, 0)),
                  pl.BlockSpec((d, 3 * (D_DIL + D_SB)), lambda bi, i: (0, 0))],
        out_specs=[pl.BlockSpec((1, tm, 3 * D_DIL), lambda bi, i: (bi, i, 0)),
                   sb_spec, sb_spec, sb_spec],
        out_shape=[jax.ShapeDtypeStruct((b, s, 3 * D_DIL), F32),
                   jax.ShapeDtypeStruct((b, s, D_SB), BF16),
                   jax.ShapeDtypeStruct((b, s, D_SB), BF16),
                   jax.ShapeDtypeStruct((b, s, D_SB), BF16)],
        compiler_params=_cparams(("arbitrary", "arbitrary")),
        name="premix",
    )(x, shift, scale, g_mix.reshape(1, d), w_in_bf16)


def _dilated_bias():
    n = DIL_STEPS
    slopes = np.array([2.0 ** (-8.0 * (i + 1) / N_HEADS_DIL) for i in range(N_HEADS_DIL)], dtype=np.float32)
    steps = np.arange(n)[:, None] + n - np.arange(2 * n)[None, :]
    valid = (steps >= 0) & (steps <= n)
    out = []
    for _, dilation in DILATION_PATTERNS:
        bias = -slopes[:, None, None] * (steps * dilation).astype(np.float32)[None]
        out.append(np.where(valid[None], bias, -np.inf).astype(np.float32))
    return np.stack(out)


def _dil_kernel(q_ref, kc_ref, kp_ref, vc_ref, vp_ref, bias_ref, o_ref,
                kext, vext, u_scr, m_scr, l_scr):
    n = DIL_STEPS
    g = pl.program_id(1)
    kext[0:DIL_UNIT, :] = kp_ref[0]
    kext[DIL_UNIT:2 * DIL_UNIT, :] = kc_ref[0]
    vext[0:DIL_UNIT, :] = vp_ref[0]
    vext[DIL_UNIT:2 * DIL_UNIT, :] = vc_ref[0]
    lane = lax.broadcasted_iota(jnp.int32, (n, LANES), 1)
    head0 = lane < HEAD_DIM
    col = lax.broadcasted_iota(jnp.int32, (n, 2 * n), 1)

    for p, (_, dil) in enumerate(DILATION_PATTERNS):
        unit = n * dil

        def tiles(it, carry, p=p, dil=dil, unit=unit):
            rows_of, vvs, deads, ss = [], [], [], []
            for t in range(DIL_TILES_PER_TRIP):
                ti = it * DIL_TILES_PER_TRIP + t
                j = ti // dil
                r = ti % dil
                qstart = j * unit + r
                kstart = DIL_UNIT + qstart - unit
                if dil == 1:
                    rows_of.append(pl.ds(qstart, n))
                    krows = pl.ds(kstart, 2 * n)
                else:
                    rows_of.append(pl.ds(qstart, n, stride=dil))
                    krows = pl.ds(kstart, 2 * n, stride=dil)
                q = q_ref[0, rows_of[t], :]
                kk = kext[krows, :].astype(BF16)
                vvs.append(vext[krows, :].astype(BF16))
                deads.append(jnp.where(jnp.logical_and(g == 0, j == 0), n, 0))
                for h in range(2):
                    qh = jnp.where(head0 if h == 0 else jnp.logical_not(head0), q, 0.0).astype(BF16)
                    ss.append(lax.dot_general(qh, kk, (((1,), (1,)), ((), ())), preferred_element_type=F32))
            ms, ls, pes = [], [], []
            for t in range(DIL_TILES_PER_TRIP):
                for h in range(2):
                    logits = jnp.where(col < deads[t], NEG_INF, ss[2 * t + h] + bias_ref[p, h])
                    m = jnp.max(logits, axis=-1, keepdims=True)
                    pe = jnp.exp(logits - m)
                    ls.append(jnp.sum(pe, axis=-1, keepdims=True))
                    ms.append(m)
                    pes.append(pe.astype(BF16))
            us = [jnp.dot(pes[2 * t + h], vvs[t], preferred_element_type=F32)
                  for t in range(DIL_TILES_PER_TRIP) for h in range(2)]
            for t in range(DIL_TILES_PER_TRIP):
                u_scr[p, rows_of[t], :] = jnp.where(head0, us[2 * t], us[2 * t + 1])
                m_scr[p, rows_of[t], :] = jnp.where(head0, ms[2 * t], ms[2 * t + 1])
                l_scr[p, rows_of[t], :] = jnp.where(head0, ls[2 * t], ls[2 * t + 1])
            return carry

        lax.fori_loop(0, DIL_UNIT // n // DIL_TILES_PER_TRIP, tiles, 0)

    def merge(i, carry):
        rows = pl.ds(pl.multiple_of(i * n, n), n)
        m0, m1, m2 = m_scr[0, rows, :], m_scr[1, rows, :], m_scr[2, rows, :]
        mx = jnp.maximum(jnp.maximum(m0, m1), m2)
        w0, w1, w2 = jnp.exp(m0 - mx), jnp.exp(m1 - mx), jnp.exp(m2 - mx)
        num = w0 * u_scr[0, rows, :] + w1 * u_scr[1, rows, :] + w2 * u_scr[2, rows, :]
        den = w0 * l_scr[0, rows, :] + w1 * l_scr[1, rows, :] + w2 * l_scr[2, rows, :]
        o_ref[0, rows, :] = num / den
        return carry

    lax.fori_loop(0, DIL_UNIT // n, merge, 0)


def _dilated(qkv_d, bias):
    b, s, _ = qkv_d.shape
    u = DIL_UNIT
    npair = D_DIL // LANES
    cur = lambda off: pl.BlockSpec((1, u, LANES), lambda bi, g, p: (bi, g, off + p))
    prev = lambda off: pl.BlockSpec((1, u, LANES), lambda bi, g, p: (bi, jnp.maximum(g - 1, 0), off + p))
    return pl.pallas_call(
        _dil_kernel,
        grid=(b, s // u, npair),
        in_specs=[cur(0), cur(npair), prev(npair), cur(2 * npair), prev(2 * npair),
                  pl.BlockSpec((3, 2, DIL_STEPS, 2 * DIL_STEPS), lambda bi, g, p: (0, p, 0, 0))],
        out_specs=pl.BlockSpec((1, u, LANES), lambda bi, g, p: (bi, g, p)),
        out_shape=jax.ShapeDtypeStruct((b, s, D_DIL), F32),
        scratch_shapes=[pltpu.VMEM((2 * u, LANES), F32), pltpu.VMEM((2 * u, LANES), F32),
                        pltpu.VMEM((3, u, LANES), F32), pltpu.VMEM((3, u, LANES), F32),
                        pltpu.VMEM((3, u, LANES), F32)],
        compiler_params=_cparams(("arbitrary", "arbitrary", "arbitrary")),
        name="dilated",
    )(qkv_d, qkv_d, qkv_d, qkv_d, qkv_d, bias)


def _stick_kernel(q_ref, k_ref, v_ref, tri_ref, o_ref,
                  qh_scr, z_scr, w_scr, acc_scr, carry_scr, scale_scr):
    blk = SB_BLOCK
    nsub = SB_QUERY_ROWS // blk
    assert nsub % 2 == 0
    nchain = 2 * nsub
    qi = pl.program_id(2)
    lane = lax.broadcasted_iota(jnp.int32, (blk, LANES), 1)
    head0 = lane < HEAD_DIM
    for sub in range(nsub):
        q = q_ref[0, sub * blk:(sub + 1) * blk, :]
        zero = jnp.zeros_like(q)
        qh_scr[2 * sub] = jnp.where(head0, q, zero)
        qh_scr[2 * sub + 1] = jnp.where(head0, zero, q)
    acc_scr[...] = jnp.zeros_like(acc_scr)
    carry_scr[...] = jnp.zeros_like(carry_scr)
    sign = jnp.int32(-2 ** 31)

    def rows(kb):
        return pl.ds(pl.multiple_of(kb * blk, blk), blk)

    def scores(kb, which, slot):
        kblk = k_ref[0, rows(kb), :]
        for c in which:
            z_scr[slot * nchain + c] = lax.dot_general(
                qh_scr[c], kblk, (((1,), (1,)), ((), ())), preferred_element_type=F32)

    def weights(which, slot, diag_sub, beside=None):
        causal = (lax.broadcasted_iota(jnp.int32, (blk, blk), 1)
                  < lax.broadcasted_iota(jnp.int32, (blk, blk), 0))
        splits = {}
        for c in which:
            z = z_scr[slot * nchain + c]
            neg_abs = lax.bitcast_convert_type(lax.bitcast_convert_type(z, jnp.int32) | sign, F32)
            softplus = jnp.maximum(z, 0.0) + jnp.log(1.0 + jnp.exp(neg_abs))
            if c // 2 == diag_sub:
                softplus = jnp.where(causal, softplus, 0.0)
            hi = softplus.astype(BF16)
            lo = (softplus - hi.astype(F32)).astype(BF16)
            splits[c] = jnp.concatenate([hi, lo], axis=1)
        sums = {}
        for c in which:
            if beside is not None:
                beside(c)
            sums[c] = jnp.dot(splits[c], tri_ref[...], preferred_element_type=F32)
        for c in which:
            w = jnp.exp(z_scr[slot * nchain + c] - sums[c])
            if c // 2 == diag_sub:
                w = jnp.where(causal, w, 0.0)
            w_scr[c] = w.astype(BF16)
            carry = carry_scr[c]
            scale_scr[c] = jnp.exp(carry)
            carry_scr[c] = carry - sums[c][:, 0:LANES]

    def accumulate(which, kb):
        vblk = v_ref[0, rows(kb), :]
        for c in which:
            acc_scr[c] = acc_scr[c] + scale_scr[c][:, 0:1] * jnp.dot(w_scr[c], vblk,
                                                                     preferred_element_type=F32)

    everyone = list(range(nchain))
    top = nsub * qi + nsub - 1
    first = nsub * qi - 1
    for i in range(nsub):
        which = [c for c in everyone if c // 2 >= nsub - 1 - i]
        scores(top - i, which, 1)
        if i == nsub - 1:
            scores(jnp.maximum(first, 0), everyone, 0)
        weights(which, 1, nsub - 1 - i)
        if i < nsub - 1:
            accumulate(which, top - i)

    def blocks(kb0, count):
        for j in range(count):
            kb = kb0 - j

            def neighbours(c, kb=kb, j=j):
                accumulate([c], kb + 1)
                scores(jnp.maximum(kb - 1, 0), [c], 1 - j % 2)

            weights(everyone, j % 2, -1, beside=neighbours)

    per = SB_BLOCKS_PER_TRIP
    trips = nsub * qi // per

    def step(i, carry):
        blocks(first - per * i, per)
        return carry

    lax.fori_loop(0, trips, step, 0)
    for rest in range(2, per, 2):
        @pl.when(nsub * qi - trips * per == rest)
        def _(rest=rest):
            blocks(first - per * trips, rest)
    accumulate(everyone, 0)
    for sub in range(nsub):
        o_ref[0, sub * blk:(sub + 1) * blk, :] = jnp.where(head0, acc_scr[2 * sub], acc_scr[2 * sub + 1])


def _stick(q_s, k_s, v_s):
    b, s, _ = q_s.shape
    blk = SB_BLOCK
    qrows = SB_QUERY_ROWS
    nchain = 2 * qrows // blk
    tri = np.tril(np.ones((blk, blk), np.float32))
    tri2 = jnp.asarray(np.concatenate([tri, tri], axis=0), BF16)
    full = pl.BlockSpec((1, s, LANES), lambda bi, p, i: (bi, 0, p))
    return pl.pallas_call(
        _stick_kernel,
        grid=(b, D_SB // LANES, s // qrows),
        in_specs=[pl.BlockSpec((1, qrows, LANES), lambda bi, p, i: (bi, i, p)), full, full,
                  pl.BlockSpec((2 * blk, blk), lambda bi, p, i: (0, 0))],
        out_specs=pl.BlockSpec((1, qrows, LANES), lambda bi, p, i: (bi, i, p)),
        out_shape=jax.ShapeDtypeStruct((b, s, D_SB), F32),
        scratch_shapes=[pltpu.VMEM((nchain, blk, LANES), BF16),
                        pltpu.VMEM((2 * nchain, blk, blk), F32),
                        pltpu.VMEM((nchain, blk, blk), BF16),
                        pltpu.VMEM((nchain, blk, LANES), F32),
                        pltpu.VMEM((nchain, blk, LANES), F32),
                        pltpu.VMEM((nchain, blk, LANES), F32)],
        compiler_params=_cparams(("arbitrary", "arbitrary", "arbitrary")),
        name="stick",
    )(q_s, k_s, v_s, tri2)


def _postmix_kernel(x_ref, od_ref, os_ref, gd_ref, gs_ref, wout_ref, gate_ref, shift_ref, scale_ref,
                    gffn_ref, wr_ref, tril_ref, triu_ref,
                    x1_ref, h2_ref, route_ref, cnt_ref, base_ref, carry_scr):
    tm = POST_ROWS

    @pl.when(jnp.logical_and(pl.program_id(0) == 0, pl.program_id(1) == 0))
    def _():
        carry_scr[...] = jnp.zeros_like(carry_scr)

    mixed = jnp.concatenate([_rms(od_ref[0], gd_ref[...]), _rms(os_ref[0], gs_ref[...])], axis=-1)
    proj = jnp.dot(mixed.astype(BF16), wout_ref[...], preferred_element_type=F32)
    x1 = x_ref[0] + gate_ref[0] * proj
    x1_ref[0] = x1
    h2 = _rms(x1, gffn_ref[...]) * (1.0 + scale_ref[0]) + shift_ref[0]
    h_hi = h2.astype(BF16)
    h_lo = (h2 - h_hi.astype(F32)).astype(BF16)
    wr = wr_ref[...]
    w_hi = wr.astype(BF16)
    w_lo = (wr - w_hi.astype(F32)).astype(BF16)
    logits = jnp.dot(jnp.concatenate([h_hi, h_lo, h_hi], axis=1),
                     jnp.concatenate([w_hi, w_hi, w_lo], axis=0), preferred_element_type=F32)

    lane = lax.broadcasted_iota(jnp.int32, (tm, LANES), 1)
    big = jnp.int32(LANES)
    lmax = lambda v: jnp.max(v, axis=-1, keepdims=True)
    lmin = lambda v: jnp.min(v, axis=-1, keepdims=True)
    lsum = lambda v: jnp.sum(v, axis=-1, keepdims=True)

    gmask = lane < N_GROUPS
    gl = jnp.where(gmask, logits, NEG_INF)
    gmx = lmax(gl)
    group = lmin(jnp.where(jnp.logical_and(gmask, gl == gmx), lane, big))
    group_gate = 1.0 / lsum(jnp.exp(gl - gmx))
    lo = ROUTE_LANE0 + group * EXPERTS_PER_GROUP
    emask = jnp.logical_and(lane >= lo, lane < lo + EXPERTS_PER_GROUP)
    el = jnp.where(emask, logits, NEG_INF)
    l1 = lmax(el)
    i1 = lmin(jnp.where(el == l1, lane, big))
    el2 = jnp.where(lane == i1, NEG_INF, el)
    l2 = lmax(el2)
    i2 = lmin(jnp.where(el2 == l2, lane, big))
    r = jnp.exp(l2 - l1)
    w1 = group_gate / (1.0 + r)
    w2 = group_gate * r / (1.0 + r)

    is1 = lane == i1
    is2 = lane == i2
    oh = jnp.where(is1, 1.0, jnp.where(is2, 1.0, 0.0))
    earlier = jnp.dot(tril_ref[...], oh.astype(BF16), preferred_element_type=F32)
    runs = jnp.floor((jnp.sum(oh, axis=0, keepdims=True) + (SUBLANES - 1.0)) * (1.0 / SUBLANES))
    run_off = jnp.dot(jnp.broadcast_to(runs, (SUBLANES, LANES)).astype(BF16), triu_ref[...],
                      preferred_element_type=F32)[0:1]
    pos = earlier + run_off * SUBLANES
    slot1 = lsum(jnp.where(is1, pos, 0.0))
    slot2 = lsum(jnp.where(is2, pos, 0.0))
    cnt = runs * SUBLANES
    cnt_ref[0] = cnt
    base_ref[0] = carry_scr[...]
    carry_scr[...] = carry_scr[...] + cnt

    h2_ref[0] = h2.astype(BF16)
    route_ref[0] = jnp.where(lane == 0, slot1, jnp.where(lane == 1, slot2,
                                                         jnp.where(lane == 2, w1, jnp.where(lane == 3, w2, 0.0))))


def _postmix(x, o_dil, o_sb, g_dil, g_sb, w_out_bf16, gate, shift, scale, g_ffn, w_router):
    b, s, d = x.shape
    tm = POST_ROWS
    nt = s // tm
    tril = jnp.asarray(np.tril(np.ones((tm, tm), np.float32), -1), BF16)
    triu = jnp.asarray(np.triu(np.ones((LANES, LANES), np.float32), 1), BF16)
    row = lambda w: pl.BlockSpec((1, tm, w), lambda bi, i: (bi, i, 0))
    vec = lambda w: pl.BlockSpec((1, w), lambda bi, i: (0, 0))
    mod_spec = pl.BlockSpec((1, 1, d), lambda bi, i: (bi, 0, 0))
    tile_vec = pl.BlockSpec((1, 1, LANES), lambda bi, i: (bi * nt + i, 0, 0))
    return pl.pallas_call(
        _postmix_kernel,
        grid=(b, nt),
        in_specs=[row(d), row(D_DIL), row(D_SB), vec(D_DIL), vec(D_SB),
                  pl.BlockSpec((d, d), lambda bi, i: (0, 0)),
                  mod_spec, mod_spec, mod_spec, vec(d),
                  pl.BlockSpec((d, LANES), lambda bi, i: (0, 0)),
                  pl.BlockSpec((tm, tm), lambda bi, i: (0, 0)),
                  pl.BlockSpec((LANES, LANES), lambda bi, i: (0, 0))],
        out_specs=[row(d), row(d), row(LANES), tile_vec, tile_vec],
        out_shape=[jax.ShapeDtypeStruct((b, s, d), F32),
                   jax.ShapeDtypeStruct((b, s, d), BF16),
                   jax.ShapeDtypeStruct((b, s, LANES), F32),
                   jax.ShapeDtypeStruct((b * nt, 1, LANES), F32),
                   jax.ShapeDtypeStruct((b * nt, 1, LANES), F32)],
        scratch_shapes=[pltpu.VMEM((1, LANES), F32)],
        compiler_params=_cparams(("arbitrary", "arbitrary")),
        name="postmix",
    )(x, o_dil, o_sb, g_dil.reshape(1, -1), g_sb.reshape(1, -1), w_out_bf16, gate, shift, scale,
      g_ffn.reshape(1, d), w_router, tril, triu)


def _for_each_run_piece(tile, start_ref, cnt_ref, base_ref, fn):
    def body(e, off):
        c = cnt_ref[tile * N_EXPERTS + e]
        sorted0 = start_ref[e] + base_ref[tile * N_EXPERTS + e]
        for k in range(3, 10):
            p = 1 << k

            @pl.when((c & p) != 0)
            def _(p=p):
                done = c - (c & (2 * p - 1))
                fn(pl.multiple_of(off + done, SUBLANES), pl.multiple_of(sorted0 + done, SUBLANES), p)
        return off + c

    return lax.fori_loop(0, N_EXPERTS, body, 0)


def _sort_kernel(start_ref, cnt_ref, base_ref, h2_ref, route_ref, buf_ref, xs_scr, sem):
    tm = POST_ROWS
    lt = LOCAL_ROWS
    d = h2_ref.shape[2]
    tile = pl.program_id(0)
    slot = tile % 2

    def piece(slot, lrow, srow, rows):
        return pltpu.make_async_copy(xs_scr.at[slot, pl.ds(lrow, rows)], buf_ref.at[pl.ds(srow, rows)],
                                     sem.at[slot])

    def drain(tile, slot):
        _for_each_run_piece(tile, start_ref, cnt_ref, base_ref, lambda *a: piece(slot, *a).wait())

    @pl.when(tile >= 2)
    def _():
        drain(tile - 2, slot)

    lane = lax.broadcasted_iota(jnp.int32, (tm, LANES), 1)
    route = route_ref[0]
    w1 = jnp.sum(jnp.where(lane == 2, route, 0.0), axis=-1, keepdims=True)
    w2 = jnp.sum(jnp.where(lane == 3, route, 0.0), axis=-1, keepdims=True)

    def pieces(w):
        hi, mid, lw = _split3(w)
        return jnp.where(lane == 0, hi.astype(F32),
                         jnp.where(lane == 1, mid.astype(F32),
                                   jnp.where(lane == 2, lw.astype(F32), 0.0))).astype(BF16)

    route_t = route.T
    s1 = route_t[0:1, :].astype(jnp.int32)
    s2 = route_t[1:2, :].astype(jnp.int32)
    row = lax.broadcasted_iota(jnp.int32, (lt, tm), 0)
    p1 = jnp.where(row == s1, 1.0, 0.0)
    p2 = jnp.where(row == s2, 1.0, 0.0)
    xs_scr[slot, :, 0:d] = jnp.dot((p1 + p2).astype(BF16), h2_ref[0], preferred_element_type=F32)
    xs_scr[slot, :, d:] = (jnp.dot(p1.astype(BF16), pieces(w1), preferred_element_type=F32)
                           + jnp.dot(p2.astype(BF16), pieces(w2), preferred_element_type=F32))
    _for_each_run_piece(tile, start_ref, cnt_ref, base_ref, lambda *a: piece(slot, *a).start())

    last = pl.num_programs(0) - 1

    @pl.when(jnp.logical_and(tile == last, tile >= 1))
    def _():
        drain(tile - 1, 1 - slot)

    @pl.when(tile == last)
    def _():
        drain(tile, slot)


def _dispatch(pad_start, cnt, base, h2, route, cap):
    b, s, d = h2.shape
    tm = POST_ROWS
    nt = s // tm
    return pl.pallas_call(
        _sort_kernel,
        grid_spec=pltpu.PrefetchScalarGridSpec(
            num_scalar_prefetch=3, grid=(b * nt,),
            in_specs=[pl.BlockSpec((1, tm, d), lambda t, *_: (t // nt, t % nt, 0)),
                      pl.BlockSpec((1, tm, LANES), lambda t, *_: (t // nt, t % nt, 0))],
            out_specs=pl.BlockSpec(memory_space=pl.ANY),
            scratch_shapes=[pltpu.VMEM((2, LOCAL_ROWS, d + LANES), F32), pltpu.SemaphoreType.DMA((2,))]),
        out_shape=jax.ShapeDtypeStruct((cap, d + LANES), F32),
        compiler_params=_cparams(("arbitrary",)),
        name="dispatch",
    )(pad_start, cnt, base, h2, route)


def _expert_kernel(be_ref, live_ref, x_ref, wg_ref, wu_ref, wd_ref, y_ref):
    del be_ref
    d = y_ref.shape[1]
    live = live_ref[pl.program_id(0)]

    @pl.when(live > 0)
    def _():
        keep = lax.broadcasted_iota(jnp.int32, (x_ref.shape[0], 1), 0) < live
        x = jnp.where(keep, x_ref[...], 0.0)
        xb = x[:, 0:d].astype(BF16)
        weight = jnp.sum(x[:, d:], axis=-1, keepdims=True)
        gate = jnp.dot(xb, wg_ref[0], preferred_element_type=F32)
        up = jnp.dot(xb, wu_ref[0], preferred_element_type=F32)
        act = gate / (1.0 + jnp.exp(-gate)) * up
        y_ref[...] = jnp.dot(act.astype(BF16), wd_ref[0], preferred_element_type=F32) * weight

    @pl.when(live == 0)
    def _():
        y_ref[...] = jnp.zeros_like(y_ref)


def _experts(block_expert, live_rows, buf, wg, wu, wd):
    cap, dw = buf.shape
    d, f = wg.shape[1], wg.shape[2]
    bm = EXPERT_ROWS
    return pl.pallas_call(
        _expert_kernel,
        grid_spec=pltpu.PrefetchScalarGridSpec(
            num_scalar_prefetch=2, grid=(cap // bm,),
            in_specs=[pl.BlockSpec((bm, dw), lambda i, be, nu: (i, 0)),
                      pl.BlockSpec((1, d, f), lambda i, be, nu: (be[i], 0, 0)),
                      pl.BlockSpec((1, d, f), lambda i, be, nu: (be[i], 0, 0)),
                      pl.BlockSpec((1, f, d), lambda i, be, nu: (be[i], 0, 0))],
            out_specs=pl.BlockSpec((bm, d), lambda i, be, nu: (i, 0))),
        out_shape=jax.ShapeDtypeStruct((cap, d), F32),
        compiler_params=_cparams(("arbitrary",)),
        name="experts",
    )(block_expert, live_rows, buf, wg, wu, wd)


def _combine_kernel(start_ref, cnt_ref, base_ref, x1_ref, route_ref, gate_ref, g_ref, y_hbm_ref, o_ref,
                    y_scr, sem):
    tm = POST_ROWS
    lt = LOCAL_ROWS
    tile = pl.program_id(0)
    slot = tile % 2

    def piece(slot, lrow, srow, rows):
        return pltpu.make_async_copy(y_hbm_ref.at[pl.ds(srow, rows)], y_scr.at[slot, pl.ds(lrow, rows)],
                                     sem.at[slot])

    def fetch(tile, slot):
        _for_each_run_piece(tile, start_ref, cnt_ref, base_ref, lambda *a: piece(slot, *a).start())

    @pl.when(tile == 0)
    def _():
        fetch(tile, slot)

    @pl.when(tile + 1 < pl.num_programs(0))
    def _():
        fetch(tile + 1, 1 - slot)

    lane = lax.broadcasted_iota(jnp.int32, (tm, LANES), 1)
    route = route_ref[0]
    s1 = jnp.sum(jnp.where(lane == 0, route, 0.0), axis=-1, keepdims=True).astype(jnp.int32)
    s2 = jnp.sum(jnp.where(lane == 1, route, 0.0), axis=-1, keepdims=True).astype(jnp.int32)
    col = lax.broadcasted_iota(jnp.int32, (tm, lt), 1)
    pick = jnp.where(col == s1, 1.0, jnp.where(col == s2, 1.0, 0.0)).astype(BF16)
    used = _for_each_run_piece(tile, start_ref, cnt_ref, base_ref, lambda *a: piece(slot, *a).wait())
    live = lax.broadcasted_iota(jnp.int32, (lt, 1), 0) < used
    yv = jnp.where(live, y_scr[slot], 0.0)
    hi = yv.astype(BF16)
    lo = (yv - hi.astype(F32)).astype(BF16)
    y = jnp.dot(jnp.concatenate([pick, pick], axis=1), jnp.concatenate([hi, lo], axis=0),
                preferred_element_type=F32)
    o_ref[0] = _rms(x1_ref[0] + gate_ref[0] * y, g_ref[...])


def _combine(pad_start, cnt, base, x1, y_sorted, route, gate, g_final):
    b, s, d = x1.shape
    tm = POST_ROWS
    nt = s // tm
    return pl.pallas_call(
        _combine_kernel,
        grid_spec=pltpu.PrefetchScalarGridSpec(
            num_scalar_prefetch=3, grid=(b * nt,),
            in_specs=[pl.BlockSpec((1, tm, d), lambda t, *_: (t // nt, t % nt, 0)),
                      pl.BlockSpec((1, tm, LANES), lambda t, *_: (t // nt, t % nt, 0)),
                      pl.BlockSpec((1, 1, d), lambda t, *_: (t // nt, 0, 0)),
                      pl.BlockSpec((1, d), lambda t, *_: (0, 0)),
                      pl.BlockSpec(memory_space=pl.ANY)],
            out_specs=pl.BlockSpec((1, tm, d), lambda t, *_: (t // nt, t % nt, 0)),
            scratch_shapes=[pltpu.VMEM((2, LOCAL_ROWS, d), F32), pltpu.SemaphoreType.DMA((2,))]),
        out_shape=jax.ShapeDtypeStruct((b, s, d), F32),
        compiler_params=_cparams(("arbitrary",)),
        name="combine",
    )(pad_start, cnt, base, x1, route, gate, g_final.reshape(1, d), y_sorted)


def kernel(x, c, w_ada, b_ada, g_mix, w_in, g_dil_out, g_sb_out, w_out, g_ffn,
           w_group, w_expert, w_gate, w_up, w_down, g_final):
    b, s, d = x.shape
    depth = w_ada.shape[0]
    assert s % DIL_UNIT == 0 and d == D_DIL + D_SB
    assert depth == 1, "the final rmsnorm is fused into the last layer's combine step"
    n = b * s
    ntiles = n // POST_ROWS
    bias = jnp.asarray(_dilated_bias())
    for layer in range(depth):
        mod = _ada(c, w_ada[layer], b_ada[layer])
        shift_mix, scale_mix, gate_mix, shift_ffn, scale_ffn, gate_ffn = (
            m.reshape(b, 1, d) for m in jnp.split(mod, 6, axis=-1))

        qkv_d, q_s, k_s, v_s = _premix(x, shift_mix, scale_mix, g_mix[layer], w_in[layer].astype(BF16))
        o_dil = _dilated(qkv_d, bias)
        o_sb = _stick(q_s, k_s, v_s)

        w_router = jnp.concatenate(
            [w_group[layer], w_expert[layer],
             jnp.zeros((d, LANES - N_GROUPS - N_EXPERTS), F32)], axis=1)
        x1, h2, route, cnt, base = _postmix(
            x, o_dil, o_sb, g_dil_out[layer], g_sb_out[layer], w_out[layer].astype(BF16),
            gate_mix, shift_ffn, scale_ffn, g_ffn[layer], w_router)

        bm = EXPERT_ROWS
        cnt = cnt[:, 0, ROUTE_LANE0:ROUTE_LANE0 + N_EXPERTS].astype(jnp.int32)
        base = base[:, 0, ROUTE_LANE0:ROUTE_LANE0 + N_EXPERTS].astype(jnp.int32)
        total = base[-1] + cnt[-1]
        cnt = cnt.reshape(-1)
        base = base.reshape(-1)
        padded = (total + bm - 1) // bm * bm
        pad_end = jnp.cumsum(padded)
        pad_start = (pad_end - padded).astype(jnp.int32)
        cap = -(-(2 * n + (SUBLANES - 1) * N_EXPERTS * ntiles) // bm) * bm + N_EXPERTS * bm
        n_blocks = cap // bm
        block_expert = jnp.minimum(
            jnp.sum(pad_end[None, :] <= (jnp.arange(n_blocks) * bm)[:, None], axis=1),
            N_EXPERTS - 1).astype(jnp.int32)
        live_rows = jnp.clip((pad_start + total)[block_expert] - jnp.arange(n_blocks) * bm, 0, bm).astype(jnp.int32)

        buf = _dispatch(pad_start, cnt, base, h2, route, cap)
        y_sorted = _experts(block_expert, live_rows, buf, w_gate[layer].astype(BF16),
                            w_up[layer].astype(BF16), w_down[layer].astype(BF16))
        x = _combine(pad_start, cnt, base, x1, y_sorted, route, gate_ffn, g_final)
    return x
```

```python
import functools

import numpy as np
import jax
import jax.numpy as jnp
from jax import lax
from jax.experimental import pallas as pl
from jax.experimental.pallas import tpu as pltpu

HEAD_DIM = 64
N_HEADS_DIL = 8
N_HEADS_SB = 8
D_DIL = N_HEADS_DIL * HEAD_DIM
D_SB = N_HEADS_SB * HEAD_DIM
DILATION_PATTERNS = ((128, 1), (512, 4), (2048, 16))
N_GROUPS = 4
EXPERTS_PER_GROUP = 8
N_EXPERTS = N_GROUPS * EXPERTS_PER_GROUP
NORM_EPS = 1e-6

LANES = 128
SUBLANES = 8
DIL_STEPS = 128
DIL_UNIT = 2048
DIL_TILES_PER_TRIP = 4
SB_BLOCK = 256
SB_QUERY_ROWS = 512
SB_BLOCKS_PER_TRIP = 2
PRE_ROWS = 512
POST_ROWS = 512
LOCAL_ROWS = 2 * POST_ROWS + 256
EXPERT_ROWS = 512
ROUTE_LANE0 = N_GROUPS
VMEM_LIMIT = 56 * 1024 * 1024

F32 = jnp.float32
BF16 = jnp.bfloat16
NEG_INF = float("-inf")
LOG2E = 1.4426950408889634


def _cparams(sem):
    return pltpu.CompilerParams(dimension_semantics=sem, vmem_limit_bytes=VMEM_LIMIT)


def _rms(v, g):
    return v * lax.rsqrt(jnp.mean(v * v, axis=-1, keepdims=True) + NORM_EPS) * g


def _split3(v):
    hi = v.astype(BF16)
    r = v - hi.astype(F32)
    mid = r.astype(BF16)
    lo = (r - mid.astype(F32)).astype(BF16)
    return hi, mid, lo


def _ada_kernel(c_ref, w_ref, b_ref, o_ref):
    c = c_ref[...]
    cond = c / (1.0 + jnp.exp(-c))
    o_ref[...] = jnp.dot(cond, w_ref[...], precision=lax.Precision.HIGHEST,
                         preferred_element_type=F32) + b_ref[...]


def _ada(c, w_ada, b_ada):
    b, d = c.shape
    n = w_ada.shape[1]
    return pl.pallas_call(
        _ada_kernel,
        grid=(n // d,),
        in_specs=[pl.BlockSpec((b, d), lambda j: (0, 0)),
                  pl.BlockSpec((d, d), lambda j: (0, j)),
                  pl.BlockSpec((1, d), lambda j: (0, j))],
        out_specs=pl.BlockSpec((b, d), lambda j: (0, j)),
        out_shape=jax.ShapeDtypeStruct((b, n), F32),
        compiler_params=_cparams(("arbitrary",)),
        name="ada",
    )(c, w_ada, b_ada.reshape(1, n))


def _premix_kernel(x_ref, shift_ref, scale_ref, g_ref, w_ref, qkvd_ref, qs_ref, ks_ref, vs_ref):
    h = _rms(x_ref[0], g_ref[...]) * (1.0 + scale_ref[0]) + shift_ref[0]
    hb = h.astype(BF16)
    scale = HEAD_DIM ** -0.5 * LOG2E
    for j in range(6):
        r = jnp.dot(hb, w_ref[:, j * 512:(j + 1) * 512], preferred_element_type=F32)
        if j == 0:
            qkvd_ref[0, :, 0:512] = r * scale
        elif j < 3:
            qkvd_ref[0, :, j * 512:(j + 1) * 512] = r
        elif j == 3:
            qs_ref[0] = (r * scale).astype(BF16)
        elif j == 4:
            ks_ref[0] = r.astype(BF16)
        else:
            vs_ref[0] = r.astype(BF16)


def _premix(x, shift, scale, g_mix, w_in_bf16):
    b, s, d = x.shape
    tm = PRE_ROWS
    mod_spec = pl.BlockSpec((1, 1, d), lambda bi, i: (bi, 0, 0))
    sb_spec = pl.BlockSpec((1, tm, D_SB), lambda bi, i: (bi, i, 0))
    return pl.pallas_call(
        _premix_kernel,
        grid=(b, s // tm),
        in_specs=[pl.BlockSpec((1, tm, d), lambda bi, i: (bi, i, 0)),
                  mod_spec, mod_spec,
                  pl.BlockSpec((1, d), lambda bi, i: (0, 0)),
                  pl.BlockSpec((d, 3 * (D_DIL + D_SB)), lambda bi, i: (0, 0))],
        out_specs=[pl.BlockSpec((1, tm, 3 * D_DIL), lambda bi, i: (bi, i, 0)),
                   sb_spec, sb_spec, sb_spec],
        out_shape=[jax.ShapeDtypeStruct((b, s, 3 * D_DIL), F32),
                   jax.ShapeDtypeStruct((b, s, D_SB), BF16),
                   jax.ShapeDtypeStruct((b, s, D_SB), BF16),
                   jax.ShapeDtypeStruct((b, s, D_SB), BF16)],
        compiler_params=_cparams(("arbitrary", "arbitrary")),
        name="premix",
    )(x, shift, scale, g_mix.reshape(1, d), w_in_bf16)


def _dilated_bias():
    n = DIL_STEPS
    slopes = np.array([2.0 ** (-8.0 * (i + 1) / N_HEADS_DIL) for i in range(N_HEADS_DIL)], dtype=np.float32)
    steps = np.arange(n)[:, None] + n - np.arange(2 * n)[None, :]
    valid = (steps >= 0) & (steps <= n)
    out = []
    for _, dilation in DILATION_PATTERNS:
        bias = -slopes[:, None, None] * (steps * dilation).astype(np.float32)[None]
        out.append(np.where(valid[None], bias.astype(np.float64) * LOG2E, -np.inf).astype(np.float32))
    return np.stack(out)


def _dil_kernel(q_ref, kc_ref, kp_ref, vc_ref, vp_ref, bias_ref, o_ref,
                kext, vext, u_scr, m_scr, l_scr):
    n = DIL_STEPS
    g = pl.program_id(1)
    kext[0:DIL_UNIT, :] = kp_ref[0]
    kext[DIL_UNIT:2 * DIL_UNIT, :] = kc_ref[0]
    vext[0:DIL_UNIT, :] = vp_ref[0]
    vext[DIL_UNIT:2 * DIL_UNIT, :] = vc_ref[0]
    lane = lax.broadcasted_iota(jnp.int32, (n, LANES), 1)
    head0 = lane < HEAD_DIM
    col = lax.broadcasted_iota(jnp.int32, (n, 2 * n), 1)

    for p, (_, dil) in enumerate(DILATION_PATTERNS):
        unit = n * dil

        def tiles(it, carry, p=p, dil=dil, unit=unit):
            rows_of, vvs, deads, ss = [], [], [], []
            for t in range(DIL_TILES_PER_TRIP):
                ti = it * DIL_TILES_PER_TRIP + t
                j = ti // dil
                r = ti % dil
                qstart = j * unit + r
                kstart = DIL_UNIT + qstart - unit
                if dil == 1:
                    rows_of.append(pl.ds(qstart, n))
                    krows = pl.ds(kstart, 2 * n)
                else:
                    rows_of.append(pl.ds(qstart, n, stride=dil))
                    krows = pl.ds(kstart, 2 * n, stride=dil)
                q = q_ref[0, rows_of[t], :]
                kk = kext[krows, :].astype(BF16)
                vvs.append(vext[krows, :].astype(BF16))
                deads.append(jnp.where(jnp.logical_and(g == 0, j == 0), n, 0))
                for h in range(2):
                    qh = jnp.where(head0 if h == 0 else jnp.logical_not(head0), q, 0.0).astype(BF16)
                    ss.append(lax.dot_general(qh, kk, (((1,), (1,)), ((), ())), preferred_element_type=F32))
            ms, ls, pes = [], [], []
            for t in range(DIL_TILES_PER_TRIP):
                for h in range(2):
                    logits = jnp.where(col < deads[t], NEG_INF, ss[2 * t + h] + bias_ref[p, h])
                    m = jnp.max(logits, axis=-1, keepdims=True)
                    pe = jnp.exp2(logits - m)
                    ls.append(jnp.sum(pe, axis=-1, keepdims=True))
                    ms.append(m)
                    pes.append(pe.astype(BF16))
            us = [jnp.dot(pes[2 * t + h], vvs[t], preferred_element_type=F32)
                  for t in range(DIL_TILES_PER_TRIP) for h in range(2)]
            for t in range(DIL_TILES_PER_TRIP):
                u_scr[p, rows_of[t], :] = jnp.where(head0, us[2 * t], us[2 * t + 1])
                m_scr[p, rows_of[t], :] = jnp.where(head0, ms[2 * t], ms[2 * t + 1])
                l_scr[p, rows_of[t], :] = jnp.where(head0, ls[2 * t], ls[2 * t + 1])
            return carry

        lax.fori_loop(0, DIL_UNIT // n // DIL_TILES_PER_TRIP, tiles, 0)

    def merge(i, carry):
        rows = pl.ds(pl.multiple_of(i * n, n), n)
        m0, m1, m2 = m_scr[0, rows, :], m_scr[1, rows, :], m_scr[2, rows, :]
        mx = jnp.maximum(jnp.maximum(m0, m1), m2)
        w0, w1, w2 = jnp.exp2(m0 - mx), jnp.exp2(m1 - mx), jnp.exp2(m2 - mx)
        num = w0 * u_scr[0, rows, :] + w1 * u_scr[1, rows, :] + w2 * u_scr[2, rows, :]
        den = w0 * l_scr[0, rows, :] + w1 * l_scr[1, rows, :] + w2 * l_scr[2, rows, :]
        o_ref[0, rows, :] = num / den
        return carry

    lax.fori_loop(0, DIL_UNIT // n, merge, 0)


def _dilated(qkv_d, bias):
    b, s, _ = qkv_d.shape
    u = DIL_UNIT
    npair = D_DIL // LANES
    cur = lambda off: pl.BlockSpec((1, u, LANES), lambda bi, g, p: (bi, g, off + p))
    prev = lambda off: pl.BlockSpec((1, u, LANES), lambda bi, g, p: (bi, jnp.maximum(g - 1, 0), off + p))
    return pl.pallas_call(
        _dil_kernel,
        grid=(b, s // u, npair),
        in_specs=[cur(0), cur(npair), prev(npair), cur(2 * npair), prev(2 * npair),
                  pl.BlockSpec((3, 2, DIL_STEPS, 2 * DIL_STEPS), lambda bi, g, p: (0, p, 0, 0))],
        out_specs=pl.BlockSpec((1, u, LANES), lambda bi, g, p: (bi, g, p)),
        out_shape=jax.ShapeDtypeStruct((b, s, D_DIL), F32),
        scratch_shapes=[pltpu.VMEM((2 * u, LANES), F32), pltpu.VMEM((2 * u, LANES), F32),
                        pltpu.VMEM((3, u, LANES), F32), pltpu.VMEM((3, u, LANES), F32),
                        pltpu.VMEM((3, u, LANES), F32)],
        compiler_params=_cparams(("arbitrary", "arbitrary", "arbitrary")),
        name="dilated",
    )(qkv_d, qkv_d, qkv_d, qkv_d, qkv_d, bias)


def _stick_kernel(q_ref, k_ref, v_ref, tri_ref, o_ref,
                  qh_scr, z_scr, w_scr, acc_scr, carry_scr, scale_scr):
    blk = SB_BLOCK
    nsub = SB_QUERY_ROWS // blk
    assert nsub % 2 == 0
    nchain = 2 * nsub
    qi = pl.program_id(2)
    lane = lax.broadcasted_iota(jnp.int32, (blk, LANES), 1)
    head0 = lane < HEAD_DIM
    for sub in range(nsub):
        q = q_ref[0, sub * blk:(sub + 1) * blk, :]
        zero = jnp.zeros_like(q)
        qh_scr[2 * sub] = jnp.where(head0, q, zero)
        qh_scr[2 * sub + 1] = jnp.where(head0, zero, q)
    acc_scr[...] = jnp.zeros_like(acc_scr)
    carry_scr[...] = jnp.zeros_like(carry_scr)
    sign = jnp.int32(-2 ** 31)

    def rows(kb):
        return pl.ds(pl.multiple_of(kb * blk, blk), blk)

    def scores(kb, which, slot):
        kblk = k_ref[0, rows(kb), :]
        for c in which:
            z_scr[slot * nchain + c] = lax.dot_general(
                qh_scr[c], kblk, (((1,), (1,)), ((), ())), preferred_element_type=F32)

    def weights(which, slot, diag_sub, beside=None):
        causal = (lax.broadcasted_iota(jnp.int32, (blk, blk), 1)
                  < lax.broadcasted_iota(jnp.int32, (blk, blk), 0))
        splits = {}
        for c in which:
            z = z_scr[slot * nchain + c]
            neg_abs = lax.bitcast_convert_type(lax.bitcast_convert_type(z, jnp.int32) | sign, F32)
            softplus = jnp.maximum(z, 0.0) + jnp.log(1.0 + jnp.exp2(neg_abs)) * LOG2E
            if c // 2 == diag_sub:
                softplus = jnp.where(causal, softplus, 0.0)
            hi = softplus.astype(BF16)
            lo = (softplus - hi.astype(F32)).astype(BF16)
            splits[c] = jnp.concatenate([hi, lo], axis=1)
        sums = {}
        for c in which:
            if beside is not None:
                beside(c)
            sums[c] = jnp.dot(splits[c], tri_ref[...], preferred_element_type=F32)
        for c in which:
            w = jnp.exp2(z_scr[slot * nchain + c] - sums[c])
            if c // 2 == diag_sub:
                w = jnp.where(causal, w, 0.0)
            w_scr[c] = w.astype(BF16)
            carry = carry_scr[c]
            scale_scr[c] = jnp.exp2(carry)
            carry_scr[c] = carry - sums[c][:, 0:LANES]

    def accumulate(which, kb):
        vblk = v_ref[0, rows(kb), :]
        for c in which:
            acc_scr[c] = acc_scr[c] + scale_scr[c][:, 0:1] * jnp.dot(w_scr[c], vblk,
                                                                     preferred_element_type=F32)

    everyone = list(range(nchain))
    top = nsub * qi + nsub - 1
    first = nsub * qi - 1
    for i in range(nsub):
        which = [c for c in everyone if c // 2 >= nsub - 1 - i]
        scores(top - i, which, 1)
        if i == nsub - 1:
            scores(jnp.maximum(first, 0), everyone, 0)
        weights(which, 1, nsub - 1 - i)
        if i < nsub - 1:
            accumulate(which, top - i)

    def blocks(kb0, count):
        for j in range(count):
            kb = kb0 - j

            def neighbours(c, kb=kb, j=j):
                accumulate([c], kb + 1)
                scores(jnp.maximum(kb - 1, 0), [c], 1 - j % 2)

            weights(everyone, j % 2, -1, beside=neighbours)

    per = SB_BLOCKS_PER_TRIP
    trips = nsub * qi // per

    def step(i, carry):
        blocks(first - per * i, per)
        return carry

    lax.fori_loop(0, trips, step, 0)
    for rest in range(2, per, 2):
        @pl.when(nsub * qi - trips * per == rest)
        def _(rest=rest):
            blocks(first - per * trips, rest)
    accumulate(everyone, 0)
    for sub in range(nsub):
        o_ref[0, sub * blk:(sub + 1) * blk, :] = jnp.where(head0, acc_scr[2 * sub], acc_scr[2 * sub + 1])


def _stick(q_s, k_s, v_s):
    b, s, _ = q_s.shape
    blk = SB_BLOCK
    qrows = SB_QUERY_ROWS
    nchain = 2 * qrows // blk
    tri = np.tril(np.ones((blk, blk), np.float32))
    tri2 = jnp.asarray(np.concatenate([tri, tri], axis=0), BF16)
    full = pl.BlockSpec((1, s, LANES), lambda bi, p, i: (bi, 0, p))
    return pl.pallas_call(
        _stick_kernel,
        grid=(b, D_SB // LANES, s // qrows),
        in_specs=[pl.BlockSpec((1, qrows, LANES), lambda bi, p, i: (bi, i, p)), full, full,
                  pl.BlockSpec((2 * blk, blk), lambda bi, p, i: (0, 0))],
        out_specs=pl.BlockSpec((1, qrows, LANES), lambda bi, p, i: (bi, i, p)),
        out_shape=jax.ShapeDtypeStruct((b, s, D_SB), F32),
        scratch_shapes=[pltpu.VMEM((nchain, blk, LANES), BF16),
                        pltpu.VMEM((2 * nchain, blk, blk), F32),
                        pltpu.VMEM((nchain, blk, blk), BF16),
                        pltpu.VMEM((nchain, blk, LANES), F32),
                        pltpu.VMEM((nchain, blk, LANES), F32),
                        pltpu.VMEM((nchain, blk, LANES), F32)],
        compiler_params=_cparams(("arbitrary", "arbitrary", "arbitrary")),
        name="stick",
    )(q_s, k_s, v_s, tri2)


def _postmix_kernel(x_ref, od_ref, os_ref, gd_ref, gs_ref, wout_ref, gate_ref, shift_ref, scale_ref,
                    gffn_ref, wr_ref, tril_ref, triu_ref,
                    x1_ref, h2_ref, route_ref, cnt_ref, base_ref, carry_scr):
    tm = POST_ROWS

    @pl.when(jnp.logical_and(pl.program_id(0) == 0, pl.program_id(1) == 0))
    def _():
        carry_scr[...] = jnp.zeros_like(carry_scr)

    mixed = jnp.concatenate([_rms(od_ref[0], gd_ref[...]), _rms(os_ref[0], gs_ref[...])], axis=-1)
    proj = jnp.dot(mixed.astype(BF16), wout_ref[...], preferred_element_type=F32)
    x1 = x_ref[0] + gate_ref[0] * proj
    x1_ref[0] = x1
    h2 = _rms(x1, gffn_ref[...]) * (1.0 + scale_ref[0]) + shift_ref[0]
    h_hi = h2.astype(BF16)
    h_lo = (h2 - h_hi.astype(F32)).astype(BF16)
    wr = wr_ref[...]
    w_hi = wr.astype(BF16)
    w_lo = (wr - w_hi.astype(F32)).astype(BF16)
    logits = jnp.dot(jnp.concatenate([h_hi, h_lo, h_hi], axis=1),
                     jnp.concatenate([w_hi, w_hi, w_lo], axis=0), preferred_element_type=F32)

    lane = lax.broadcasted_iota(jnp.int32, (tm, LANES), 1)
    big = jnp.int32(LANES)
    lmax = lambda v: jnp.max(v, axis=-1, keepdims=True)
    lmin = lambda v: jnp.min(v, axis=-1, keepdims=True)
    lsum = lambda v: jnp.sum(v, axis=-1, keepdims=True)

    gmask = lane < N_GROUPS
    gl = jnp.where(gmask, logits, NEG_INF)
    gmx = lmax(gl)
    group = lmin(jnp.where(jnp.logical_and(gmask, gl == gmx), lane, big))
    group_gate = 1.0 / lsum(jnp.exp(gl - gmx))
    lo = ROUTE_LANE0 + group * EXPERTS_PER_GROUP
    emask = jnp.logical_and(lane >= lo, lane < lo + EXPERTS_PER_GROUP)
    el = jnp.where(emask, logits, NEG_INF)
    l1 = lmax(el)
    i1 = lmin(jnp.where(el == l1, lane, big))
    el2 = jnp.where(lane == i1, NEG_INF, el)
    l2 = lmax(el2)
    i2 = lmin(jnp.where(el2 == l2, lane, big))
    r = jnp.exp(l2 - l1)
    w1 = group_gate / (1.0 + r)
    w2 = group_gate * r / (1.0 + r)

    is1 = lane == i1
    is2 = lane == i2
    oh = jnp.where(is1, 1.0, jnp.where(is2, 1.0, 0.0))
    earlier = jnp.dot(tril_ref[...], oh.astype(BF16), preferred_element_type=F32)
    runs = jnp.floor((jnp.sum(oh, axis=0, keepdims=True) + (SUBLANES - 1.0)) * (1.0 / SUBLANES))
    run_off = jnp.dot(jnp.broadcast_to(runs, (SUBLANES, LANES)).astype(BF16), triu_ref[...],
                      preferred_element_type=F32)[0:1]
    pos = earlier + run_off * SUBLANES
    slot1 = lsum(jnp.where(is1, pos, 0.0))
    slot2 = lsum(jnp.where(is2, pos, 0.0))
    cnt = runs * SUBLANES
    cnt_ref[0] = cnt
    base_ref[0] = carry_scr[...]
    carry_scr[...] = carry_scr[...] + cnt

    h2_ref[0] = h2.astype(BF16)
    route_ref[0] = jnp.where(lane == 0, slot1, jnp.where(lane == 1, slot2,
                                                         jnp.where(lane == 2, w1, jnp.where(lane == 3, w2, 0.0))))


def _postmix(x, o_dil, o_sb, g_dil, g_sb, w_out_bf16, gate, shift, scale, g_ffn, w_router):
    b, s, d = x.shape
    tm = POST_ROWS
    nt = s // tm
    tril = jnp.asarray(np.tril(np.ones((tm, tm), np.float32), -1), BF16)
    triu = jnp.asarray(np.triu(np.ones((LANES, LANES), np.float32), 1), BF16)
    row = lambda w: pl.BlockSpec((1, tm, w), lambda bi, i: (bi, i, 0))
    vec = lambda w: pl.BlockSpec((1, w), lambda bi, i: (0, 0))
    mod_spec = pl.BlockSpec((1, 1, d), lambda bi, i: (bi, 0, 0))
    tile_vec = pl.BlockSpec((1, 1, LANES), lambda bi, i: (bi * nt + i, 0, 0))
    return pl.pallas_call(
        _postmix_kernel,
        grid=(b, nt),
        in_specs=[row(d), row(D_DIL), row(D_SB), vec(D_DIL), vec(D_SB),
                  pl.BlockSpec((d, d), lambda bi, i: (0, 0)),
                  mod_spec, mod_spec, mod_spec, vec(d),
                  pl.BlockSpec((d, LANES), lambda bi, i: (0, 0)),
                  pl.BlockSpec((tm, tm), lambda bi, i: (0, 0)),
                  pl.BlockSpec((LANES, LANES), lambda bi, i: (0, 0))],
        out_specs=[row(d), row(d), row(LANES), tile_vec, tile_vec],
        out_shape=[jax.ShapeDtypeStruct((b, s, d), F32),
                   jax.ShapeDtypeStruct((b, s, d), BF16),
                   jax.ShapeDtypeStruct((b, s, LANES), F32),
                   jax.ShapeDtypeStruct((b * nt, 1, LANES), F32),
                   jax.ShapeDtypeStruct((b * nt, 1, LANES), F32)],
        scratch_shapes=[pltpu.VMEM((1, LANES), F32)],
        compiler_params=_cparams(("arbitrary", "arbitrary")),
        name="postmix",
    )(x, o_dil, o_sb, g_dil.reshape(1, -1), g_sb.reshape(1, -1), w_out_bf16, gate, shift, scale,
      g_ffn.reshape(1, d), w_router, tril, triu)


def _for_each_run_piece(tile, start_ref, cnt_ref, base_ref, fn):
    def body(e, off):
        c = cnt_ref[tile * N_EXPERTS + e]
        sorted0 = start_ref[e] + base_ref[tile * N_EXPERTS + e]
        for k in range(3, 10):
            p = 1 << k

            @pl.when((c & p) != 0)
            def _(p=p):
                done = c - (c & (2 * p - 1))
                fn(pl.multiple_of(off + done, SUBLANES), pl.multiple_of(sorted0 + done, SUBLANES), p)
        return off + c

    return lax.fori_loop(0, N_EXPERTS, body, 0)


def _sort_kernel(start_ref, cnt_ref, base_ref, h2_ref, route_ref, buf_ref, xs_scr, sem):
    tm = POST_ROWS
    lt = LOCAL_ROWS
    d = h2_ref.shape[2]
    tile = pl.program_id(0)
    slot = tile % 2

    def piece(slot, lrow, srow, rows):
        return pltpu.make_async_copy(xs_scr.at[slot, pl.ds(lrow, rows)], buf_ref.at[pl.ds(srow, rows)],
                                     sem.at[slot])

    def drain(tile, slot):
        _for_each_run_piece(tile, start_ref, cnt_ref, base_ref, lambda *a: piece(slot, *a).wait())

    @pl.when(tile >= 2)
    def _():
        drain(tile - 2, slot)

    lane = lax.broadcasted_iota(jnp.int32, (tm, LANES), 1)
    route = route_ref[0]
    w1 = jnp.sum(jnp.where(lane == 2, route, 0.0), axis=-1, keepdims=True)
    w2 = jnp.sum(jnp.where(lane == 3, route, 0.0), axis=-1, keepdims=True)

    def pieces(w):
        hi, mid, lw = _split3(w)
        return jnp.where(lane == 0, hi.astype(F32),
                         jnp.where(lane == 1, mid.astype(F32),
                                   jnp.where(lane == 2, lw.astype(F32), 0.0))).astype(BF16)

    route_t = route.T
    s1 = route_t[0:1, :].astype(jnp.int32)
    s2 = route_t[1:2, :].astype(jnp.int32)
    row = lax.broadcasted_iota(jnp.int32, (lt, tm), 0)
    p1 = jnp.where(row == s1, 1.0, 0.0)
    p2 = jnp.where(row == s2, 1.0, 0.0)
    xs_scr[slot, :, 0:d] = jnp.dot((p1 + p2).astype(BF16), h2_ref[0], preferred_element_type=F32)
    xs_scr[slot, :, d:] = (jnp.dot(p1.astype(BF16), pieces(w1), preferred_element_type=F32)
                           + jnp.dot(p2.astype(BF16), pieces(w2), preferred_element_type=F32))
    _for_each_run_piece(tile, start_ref, cnt_ref, base_ref, lambda *a: piece(slot, *a).start())

    last = pl.num_programs(0) - 1

    @pl.when(jnp.logical_and(tile == last, tile >= 1))
    def _():
        drain(tile - 1, 1 - slot)

    @pl.when(tile == last)
    def _():
        drain(tile, slot)


def _dispatch(pad_start, cnt, base, h2, route, cap):
    b, s, d = h2.shape
    tm = POST_ROWS
    nt = s // tm
    return pl.pallas_call(
        _sort_kernel,
        grid_spec=pltpu.PrefetchScalarGridSpec(
            num_scalar_prefetch=3, grid=(b * nt,),
            in_specs=[pl.BlockSpec((1, tm, d), lambda t, *_: (t // nt, t % nt, 0)),
                      pl.BlockSpec((1, tm, LANES), lambda t, *_: (t // nt, t % nt, 0))],
            out_specs=pl.BlockSpec(memory_space=pl.ANY),
            scratch_shapes=[pltpu.VMEM((2, LOCAL_ROWS, d + LANES), F32), pltpu.SemaphoreType.DMA((2,))]),
        out_shape=jax.ShapeDtypeStruct((cap, d + LANES), F32),
        compiler_params=_cparams(("arbitrary",)),
        name="dispatch",
    )(pad_start, cnt, base, h2, route)


def _expert_kernel(be_ref, live_ref, x_ref, wg_ref, wu_ref, wd_ref, y_ref):
    del be_ref
    d = y_ref.shape[1]
    live = live_ref[pl.program_id(0)]

    @pl.when(live > 0)
    def _():
        keep = lax.broadcasted_iota(jnp.int32, (x_ref.shape[0], 1), 0) < live
        x = jnp.where(keep, x_ref[...], 0.0)
        xb = x[:, 0:d].astype(BF16)
        weight = jnp.sum(x[:, d:], axis=-1, keepdims=True)
        gate = jnp.dot(xb, wg_ref[0], preferred_element_type=F32)
        up = jnp.dot(xb, wu_ref[0], preferred_element_type=F32)
        act = gate / (1.0 + jnp.exp(-gate)) * up
        y_ref[...] = jnp.dot(act.astype(BF16), wd_ref[0], preferred_element_type=F32) * weight

    @pl.when(live == 0)
    def _():
        y_ref[...] = jnp.zeros_like(y_ref)


def _experts(block_expert, live_rows, buf, wg, wu, wd):
    cap, dw = buf.shape
    d, f = wg.shape[1], wg.shape[2]
    bm = EXPERT_ROWS
    return pl.pallas_call(
        _expert_kernel,
        grid_spec=pltpu.PrefetchScalarGridSpec(
            num_scalar_prefetch=2, grid=(cap // bm,),
            in_specs=[pl.BlockSpec((bm, dw), lambda i, be, nu: (i, 0)),
                      pl.BlockSpec((1, d, f), lambda i, be, nu: (be[i], 0, 0)),
                      pl.BlockSpec((1, d, f), lambda i, be, nu: (be[i], 0, 0)),
                      pl.BlockSpec((1, f, d), lambda i, be, nu: (be[i], 0, 0))],
            out_specs=pl.BlockSpec((bm, d), lambda i, be, nu: (i, 0))),
        out_shape=jax.ShapeDtypeStruct((cap, d), F32),
        compiler_params=_cparams(("arbitrary",)),
        name="experts",
    )(block_expert, live_rows, buf, wg, wu, wd)


def _combine_kernel(start_ref, cnt_ref, base_ref, x1_ref, route_ref, gate_ref, g_ref, y_hbm_ref, o_ref,
                    y_scr, sem):
    tm = POST_ROWS
    lt = LOCAL_ROWS
    tile = pl.program_id(0)
    slot = tile % 2

    def piece(slot, lrow, srow, rows):
        return pltpu.make_async_copy(y_hbm_ref.at[pl.ds(srow, rows)], y_scr.at[slot, pl.ds(lrow, rows)],
                                     sem.at[slot])

    def fetch(tile, slot):
        _for_each_run_piece(tile, start_ref, cnt_ref, base_ref, lambda *a: piece(slot, *a).start())

    @pl.when(tile == 0)
    def _():
        fetch(tile, slot)

    @pl.when(tile + 1 < pl.num_programs(0))
    def _():
        fetch(tile + 1, 1 - slot)

    lane = lax.broadcasted_iota(jnp.int32, (tm, LANES), 1)
    route = route_ref[0]
    s1 = jnp.sum(jnp.where(lane == 0, route, 0.0), axis=-1, keepdims=True).astype(jnp.int32)
    s2 = jnp.sum(jnp.where(lane == 1, route, 0.0), axis=-1, keepdims=True).astype(jnp.int32)
    col = lax.broadcasted_iota(jnp.int32, (tm, lt), 1)
    pick = jnp.where(col == s1, 1.0, jnp.where(col == s2, 1.0, 0.0)).astype(BF16)
    used = _for_each_run_piece(tile, start_ref, cnt_ref, base_ref, lambda *a: piece(slot, *a).wait())
    live = lax.broadcasted_iota(jnp.int32, (lt, 1), 0) < used
    yv = jnp.where(live, y_scr[slot], 0.0)
    hi = yv.astype(BF16)
    lo = (yv - hi.astype(F32)).astype(BF16)
    y = jnp.dot(jnp.concatenate([pick, pick], axis=1), jnp.concatenate([hi, lo], axis=0),
                preferred_element_type=F32)
    o_ref[0] = _rms(x1_ref[0] + gate_ref[0] * y, g_ref[...])


def _combine(pad_start, cnt, base, x1, y_sorted, route, gate, g_final):
    b, s, d = x1.shape
    tm = POST_ROWS
    nt = s // tm
    return pl.pallas_call(
        _combine_kernel,
        grid_spec=pltpu.PrefetchScalarGridSpec(
            num_scalar_prefetch=3, grid=(b * nt,),
            in_specs=[pl.BlockSpec((1, tm, d), lambda t, *_: (t // nt, t % nt, 0)),
                      pl.BlockSpec((1, tm, LANES), lambda t, *_: (t // nt, t % nt, 0)),
                      pl.BlockSpec((1, 1, d), lambda t, *_: (t // nt, 0, 0)),
                      pl.BlockSpec((1, d), lambda t, *_: (0, 0)),
                      pl.BlockSpec(memory_space=pl.ANY)],
            out_specs=pl.BlockSpec((1, tm, d), lambda t, *_: (t // nt, t % nt, 0)),
            scratch_shapes=[pltpu.VMEM((2, LOCAL_ROWS, d), F32), pltpu.SemaphoreType.DMA((2,))]),
        out_shape=jax.ShapeDtypeStruct((b, s, d), F32),
        compiler_params=_cparams(("arbitrary",)),
        name="combine",
    )(pad_start, cnt, base, x1, route, gate, g_final.reshape(1, d), y_sorted)


def kernel(x, c, w_ada, b_ada, g_mix, w_in, g_dil_out, g_sb_out, w_out, g_ffn,
           w_group, w_expert, w_gate, w_up, w_down, g_final):
    b, s, d = x.shape
    depth = w_ada.shape[0]
    assert s % DIL_UNIT == 0 and d == D_DIL + D_SB
    assert depth == 1, "the final rmsnorm is fused into the last layer's combine step"
    n = b * s
    ntiles = n // POST_ROWS
    bias = jnp.asarray(_dilated_bias())
    for layer in range(depth):
        mod = _ada(c, w_ada[layer], b_ada[layer])
        shift_mix, scale_mix, gate_mix, shift_ffn, scale_ffn, gate_ffn = (
            m.reshape(b, 1, d) for m in jnp.split(mod, 6, axis=-1))

        qkv_d, q_s, k_s, v_s = _premix(x, shift_mix, scale_mix, g_mix[layer], w_in[layer].astype(BF16))
        o_dil = _dilated(qkv_d, bias)
        o_sb = _stick(q_s, k_s, v_s)

        w_router = jnp.concatenate(
            [w_group[layer], w_expert[layer],
             jnp.zeros((d, LANES - N_GROUPS - N_EXPERTS), F32)], axis=1)
        x1, h2, route, cnt, base = _postmix(
            x, o_dil, o_sb, g_dil_out[layer], g_sb_out[layer], w_out[layer].astype(BF16),
            gate_mix, shift_ffn, scale_ffn, g_ffn[layer], w_router)

        bm = EXPERT_ROWS
        cnt = cnt[:, 0, ROUTE_LANE0:ROUTE_LANE0 + N_EXPERTS].astype(jnp.int32)
        base = base[:, 0, ROUTE_LANE0:ROUTE_LANE0 + N_EXPERTS].astype(jnp.int32)
        total = base[-1] + cnt[-1]
        cnt = cnt.reshape(-1)
        base = base.reshape(-1)
        padded = (total + bm - 1) // bm * bm
        pad_end = jnp.cumsum(padded)
        pad_start = (pad_end - padded).astype(jnp.int32)
        cap = -(-(2 * n + (SUBLANES - 1) * N_EXPERTS * ntiles) // bm) * bm + N_EXPERTS * bm
        n_blocks = cap // bm
        block_expert = jnp.minimum(
            jnp.sum(pad_end[None, :] <= (jnp.arange(n_blocks) * bm)[:, None], axis=1),
            N_EXPERTS - 1).astype(jnp.int32)
        live_rows = jnp.clip((pad_start + total)[block_expert] - jnp.arange(n_blocks) * bm, 0, bm).astype(jnp.int32)

        buf = _dispatch(pad_start, cnt, base, h2, route, cap)
        y_sorted = _experts(block_expert, live_rows, buf, w_gate[layer].astype(BF16),
                            w_up[layer].astype(BF16), w_down[layer].astype(BF16))
        x = _combine(pad_start, cnt, base, x1, y_sorted, route, gate_ffn, g_final)
    return x
```

```python
import numpy as np
import jax
import jax.numpy as jnp
from jax import lax
from jax.experimental import pallas as pl
from jax.experimental.pallas import tpu as pltpu

HEAD_DIM = 64
N_HEADS_DIL = 8
N_HEADS_SB = 8
D_DIL = N_HEADS_DIL * HEAD_DIM
D_SB = N_HEADS_SB * HEAD_DIM
DILATION_PATTERNS = ((128, 1), (512, 4), (2048, 16))
N_GROUPS = 4
EXPERTS_PER_GROUP = 8
N_EXPERTS = N_GROUPS * EXPERTS_PER_GROUP
NORM_EPS = 1e-6

LANES = 128
SUBLANES = 8
DIL_STEPS = 128
DIL_UNIT = 2048
DIL_TILES_PER_TRIP = 4
SB_BLOCK = 256
SB_QUERY_ROWS = 512
SB_BLOCKS_PER_TRIP = 2
PRE_ROWS = 512
POST_ROWS = 512
LOCAL_ROWS = 2 * POST_ROWS + 256
EXPERT_ROWS = 512
ROUTE_LANE0 = N_GROUPS
VMEM_LIMIT = 56 * 1024 * 1024

F32 = jnp.float32
BF16 = jnp.bfloat16
NEG_INF = float("-inf")
LOG2E = 1.4426950408889634


def _cparams(sem):
    return pltpu.CompilerParams(dimension_semantics=sem, vmem_limit_bytes=VMEM_LIMIT)


def _rms(v, g):
    return v * lax.rsqrt(jnp.mean(v * v, axis=-1, keepdims=True) + NORM_EPS) * g


def _split3(v):
    hi = v.astype(BF16)
    r = v - hi.astype(F32)
    mid = r.astype(BF16)
    lo = (r - mid.astype(F32)).astype(BF16)
    return hi, mid, lo


def _ada_kernel(c_ref, w_ref, b_ref, o_ref):
    c = c_ref[...]
    cond = c / (1.0 + jnp.exp(-c))
    o_ref[...] = jnp.dot(cond, w_ref[...], precision=lax.Precision.HIGHEST,
                         preferred_element_type=F32) + b_ref[...]


def _ada(c, w_ada, b_ada):
    b, d = c.shape
    n = w_ada.shape[1]
    return pl.pallas_call(
        _ada_kernel,
        grid=(n // d,),
        in_specs=[pl.BlockSpec((b, d), lambda j: (0, 0)),
                  pl.BlockSpec((d, d), lambda j: (0, j)),
                  pl.BlockSpec((1, d), lambda j: (0, j))],
        out_specs=pl.BlockSpec((b, d), lambda j: (0, j)),
        out_shape=jax.ShapeDtypeStruct((b, n), F32),
        compiler_params=_cparams(("arbitrary",)),
        name="ada",
    )(c, w_ada, b_ada.reshape(1, n))


def _premix_kernel(x_ref, shift_ref, scale_ref, g_ref, w_ref, qkvd_ref, qs_ref, ks_ref, vs_ref):
    h = _rms(x_ref[0], g_ref[...]) * (1.0 + scale_ref[0]) + shift_ref[0]
    hb = h.astype(BF16)
    scale = HEAD_DIM ** -0.5 * LOG2E
    for j in range(6):
        r = jnp.dot(hb, w_ref[:, j * 512:(j + 1) * 512], preferred_element_type=F32)
        if j == 0:
            qkvd_ref[0, :, 0:512] = r * scale
        elif j < 3:
            qkvd_ref[0, :, j * 512:(j + 1) * 512] = r
        elif j == 3:
            qs_ref[0] = (r * scale).astype(BF16)
        elif j == 4:
            ks_ref[0] = r.astype(BF16)
        else:
            vs_ref[0] = r.astype(BF16)


def _premix(x, shift, scale, g_mix, w_in_bf16):
    b, s, d = x.shape
    tm = PRE_ROWS
    mod_spec = pl.BlockSpec((1, 1, d), lambda bi, i: (bi, 0, 0))
    sb_spec = pl.BlockSpec((1, tm, D_SB), lambda bi, i: (bi, i, 0))
    return pl.pallas_call(
        _premix_kernel,
        grid=(b, s // tm),
        in_specs=[pl.BlockSpec((1, tm, d), lambda bi, i: (bi, i, 0)),
                  mod_spec, mod_spec,
                  pl.BlockSpec((1, d), lambda bi, i: (0, 0)),
                  pl.BlockSpec((d, 3 * (D_DIL + D_SB)), lambda bi, i: (0, 0))],
        out_specs=[pl.BlockSpec((1, tm, 3 * D_DIL), lambda bi, i: (bi, i, 0)),
                   sb_spec, sb_spec, sb_spec],
        out_shape=[jax.ShapeDtypeStruct((b, s, 3 * D_DIL), F32),
                   jax.ShapeDtypeStruct((b, s, D_SB), BF16),
                   jax.ShapeDtypeStruct((b, s, D_SB), BF16),
                   jax.ShapeDtypeStruct((b, s, D_SB), BF16)],
        compiler_params=_cparams(("arbitrary", "arbitrary")),
        name="premix",
    )(x, shift, scale, g_mix.reshape(1, d), w_in_bf16)


def _dilated_bias():
    n = DIL_STEPS
    slopes = np.array([2.0 ** (-8.0 * (i + 1) / N_HEADS_DIL) for i in range(N_HEADS_DIL)], dtype=np.float32)
    steps = np.arange(n)[:, None] + n - np.arange(2 * n)[None, :]
    valid = (steps >= 0) & (steps <= n)
    out = []
    for _, dilation in DILATION_PATTERNS:
        bias = -slopes[:, None, None] * (steps * dilation).astype(np.float32)[None]
        out.append(np.where(valid[None], bias.astype(np.float64) * LOG2E, -np.inf).astype(np.float32))
    return np.stack(out)


def _dil_kernel(q_ref, kc_ref, kp_ref, vc_ref, vp_ref, bias_ref, o_ref,
                kext, vext, u_scr, m_scr, l_scr):
    n = DIL_STEPS
    g = pl.program_id(1)
    kext[0:DIL_UNIT, :] = kp_ref[0]
    kext[DIL_UNIT:2 * DIL_UNIT, :] = kc_ref[0]
    vext[0:DIL_UNIT, :] = vp_ref[0]
    vext[DIL_UNIT:2 * DIL_UNIT, :] = vc_ref[0]
    lane = lax.broadcasted_iota(jnp.int32, (n, LANES), 1)
    head0 = lane < HEAD_DIM
    col = lax.broadcasted_iota(jnp.int32, (n, 2 * n), 1)

    for p, (_, dil) in enumerate(DILATION_PATTERNS):
        unit = n * dil

        def tiles(it, carry, p=p, dil=dil, unit=unit):
            rows_of, vvs, deads, ss = [], [], [], []
            for t in range(DIL_TILES_PER_TRIP):
                ti = it * DIL_TILES_PER_TRIP + t
                j = ti // dil
                r = ti % dil
                qstart = j * unit + r
                kstart = DIL_UNIT + qstart - unit
                if dil == 1:
                    rows_of.append(pl.ds(qstart, n))
                    krows = pl.ds(kstart, 2 * n)
                else:
                    rows_of.append(pl.ds(qstart, n, stride=dil))
                    krows = pl.ds(kstart, 2 * n, stride=dil)
                q = q_ref[0, rows_of[t], :]
                kk = kext[krows, :].astype(BF16)
                vvs.append(vext[krows, :].astype(BF16))
                deads.append(jnp.where(jnp.logical_and(g == 0, j == 0), n, 0))
                for h in range(2):
                    qh = jnp.where(head0 if h == 0 else jnp.logical_not(head0), q, 0.0).astype(BF16)
                    ss.append(lax.dot_general(qh, kk, (((1,), (1,)), ((), ())), preferred_element_type=F32))
            ms, ls, pes = [], [], []
            for t in range(DIL_TILES_PER_TRIP):
                for h in range(2):
                    logits = jnp.where(col < deads[t], NEG_INF, ss[2 * t + h] + bias_ref[p, h])
                    m = jnp.max(logits, axis=-1, keepdims=True)
                    pe = jnp.exp2(logits - m)
                    ls.append(jnp.sum(pe, axis=-1, keepdims=True))
                    ms.append(m)
                    pes.append(pe.astype(BF16))
            us = [jnp.dot(pes[2 * t + h], vvs[t], preferred_element_type=F32)
                  for t in range(DIL_TILES_PER_TRIP) for h in range(2)]
            for t in range(DIL_TILES_PER_TRIP):
                u_scr[p, rows_of[t], :] = jnp.where(head0, us[2 * t], us[2 * t + 1])
                m_scr[p, rows_of[t], :] = jnp.where(head0, ms[2 * t], ms[2 * t + 1])
                l_scr[p, rows_of[t], :] = jnp.where(head0, ls[2 * t], ls[2 * t + 1])
            return carry

        lax.fori_loop(0, DIL_UNIT // n // DIL_TILES_PER_TRIP, tiles, 0)

    def merge(i, carry):
        rows = pl.ds(pl.multiple_of(i * n, n), n)
        m0, m1, m2 = m_scr[0, rows, :], m_scr[1, rows, :], m_scr[2, rows, :]
        mx = jnp.maximum(jnp.maximum(m0, m1), m2)
        w0, w1, w2 = jnp.exp2(m0 - mx), jnp.exp2(m1 - mx), jnp.exp2(m2 - mx)
        num = w0 * u_scr[0, rows, :] + w1 * u_scr[1, rows, :] + w2 * u_scr[2, rows, :]
        den = w0 * l_scr[0, rows, :] + w1 * l_scr[1, rows, :] + w2 * l_scr[2, rows, :]
        o_ref[0, rows, :] = num / den
        return carry

    lax.fori_loop(0, DIL_UNIT // n, merge, 0)


def _dilated(qkv_d, bias):
    b, s, _ = qkv_d.shape
    u = DIL_UNIT
    npair = D_DIL // LANES
    cur = lambda off: pl.BlockSpec((1, u, LANES), lambda bi, g, p: (bi, g, off + p))
    prev = lambda off: pl.BlockSpec((1, u, LANES), lambda bi, g, p: (bi, jnp.maximum(g - 1, 0), off + p))
    return pl.pallas_call(
        _dil_kernel,
        grid=(b, s // u, npair),
        in_specs=[cur(0), cur(npair), prev(npair), cur(2 * npair), prev(2 * npair),
                  pl.BlockSpec((3, 2, DIL_STEPS, 2 * DIL_STEPS), lambda bi, g, p: (0, p, 0, 0))],
        out_specs=pl.BlockSpec((1, u, LANES), lambda bi, g, p: (bi, g, p)),
        out_shape=jax.ShapeDtypeStruct((b, s, D_DIL), F32),
        scratch_shapes=[pltpu.VMEM((2 * u, LANES), F32), pltpu.VMEM((2 * u, LANES), F32),
                        pltpu.VMEM((3, u, LANES), F32), pltpu.VMEM((3, u, LANES), F32),
                        pltpu.VMEM((3, u, LANES), F32)],
        compiler_params=_cparams(("arbitrary", "arbitrary", "arbitrary")),
        name="dilated",
    )(qkv_d, qkv_d, qkv_d, qkv_d, qkv_d, bias)


def _stick_kernel(q_ref, k_ref, v_ref, tri_ref, o_ref,
                  qh_scr, z_scr, w_scr, acc_scr, carry_scr, scale_scr):
    blk = SB_BLOCK
    nsub = SB_QUERY_ROWS // blk
    assert nsub % 2 == 0
    nchain = 2 * nsub
    qi = pl.program_id(2)
    lane = lax.broadcasted_iota(jnp.int32, (blk, LANES), 1)
    head0 = lane < HEAD_DIM
    for sub in range(nsub):
        q = q_ref[0, sub * blk:(sub + 1) * blk, :]
        zero = jnp.zeros_like(q)
        qh_scr[2 * sub] = jnp.where(head0, q, zero)
        qh_scr[2 * sub + 1] = jnp.where(head0, zero, q)
    acc_scr[...] = jnp.zeros_like(acc_scr)
    carry_scr[...] = jnp.zeros_like(carry_scr)
    sign = jnp.int32(-2 ** 31)

    def rows(kb):
        return pl.ds(pl.multiple_of(kb * blk, blk), blk)

    def scores(kb, which, slot):
        kblk = k_ref[0, rows(kb), :]
        for c in which:
            z_scr[slot * nchain + c] = lax.dot_general(
                qh_scr[c], kblk, (((1,), (1,)), ((), ())), preferred_element_type=F32)

    def weights(which, slot, diag_sub, beside=None):
        causal = (lax.broadcasted_iota(jnp.int32, (blk, blk), 1)
                  < lax.broadcasted_iota(jnp.int32, (blk, blk), 0))
        splits = {}
        for c in which:
            z = z_scr[slot * nchain + c]
            neg_abs = lax.bitcast_convert_type(lax.bitcast_convert_type(z, jnp.int32) | sign, F32)
            softplus = jnp.maximum(z, 0.0) + jnp.log(1.0 + jnp.exp2(neg_abs)) * LOG2E
            if c // 2 == diag_sub:
                softplus = jnp.where(causal, softplus, 0.0)
            hi = softplus.astype(BF16)
            lo = (softplus - hi.astype(F32)).astype(BF16)
            splits[c] = jnp.concatenate([hi, lo], axis=1)
        sums = {}
        for c in which:
            if beside is not None:
                beside(c)
            sums[c] = jnp.dot(splits[c], tri_ref[...], preferred_element_type=F32)
        for c in which:
            w = jnp.exp2(z_scr[slot * nchain + c] - sums[c])
            if c // 2 == diag_sub:
                w = jnp.where(causal, w, 0.0)
            w_scr[c] = w.astype(BF16)
            carry = carry_scr[c]
            scale_scr[c] = jnp.exp2(carry)
            carry_scr[c] = carry - sums[c][:, 0:LANES]

    def accumulate(which, kb):
        vblk = v_ref[0, rows(kb), :]
        for c in which:
            acc_scr[c] = acc_scr[c] + scale_scr[c][:, 0:1] * jnp.dot(w_scr[c], vblk,
                                                                     preferred_element_type=F32)

    everyone = list(range(nchain))
    top = nsub * qi + nsub - 1
    first = nsub * qi - 1
    diag = [[c for c in everyone if c // 2 >= nsub - 1 - i] for i in range(nsub)]
    for i in range(nsub):
        scores(top - i, diag[i], 2 + i)
    scores(jnp.maximum(first, 0), everyone, 0)
    for i in range(nsub):
        def previous(c, i=i):
            if i > 0 and c in diag[i - 1]:
                accumulate([c], top - (i - 1))

        weights(diag[i], 2 + i, nsub - 1 - i, beside=previous)

    def blocks(kb0, count):
        for j in range(count):
            kb = kb0 - j

            def neighbours(c, kb=kb, j=j):
                accumulate([c], kb + 1)
                scores(jnp.maximum(kb - 1, 0), [c], 1 - j % 2)

            weights(everyone, j % 2, -1, beside=neighbours)

    per = SB_BLOCKS_PER_TRIP
    trips = nsub * qi // per

    def step(i, carry):
        blocks(first - per * i, per)
        return carry

    lax.fori_loop(0, trips, step, 0)
    for rest in range(2, per, 2):
        @pl.when(nsub * qi - trips * per == rest)
        def _(rest=rest):
            blocks(first - per * trips, rest)
    accumulate(everyone, 0)
    for sub in range(nsub):
        o_ref[0, sub * blk:(sub + 1) * blk, :] = jnp.where(head0, acc_scr[2 * sub], acc_scr[2 * sub + 1])


def _stick(q_s, k_s, v_s):
    b, s, _ = q_s.shape
    blk = SB_BLOCK
    qrows = SB_QUERY_ROWS
    nchain = 2 * qrows // blk
    tri = np.tril(np.ones((blk, blk), np.float32))
    tri2 = jnp.asarray(np.concatenate([tri, tri], axis=0), BF16)
    full = pl.BlockSpec((1, s, LANES), lambda bi, p, i: (bi, 0, p))
    return pl.pallas_call(
        _stick_kernel,
        grid=(b, D_SB // LANES, s // qrows),
        in_specs=[pl.BlockSpec((1, qrows, LANES), lambda bi, p, i: (bi, i, p)), full, full,
                  pl.BlockSpec((2 * blk, blk), lambda bi, p, i: (0, 0))],
        out_specs=pl.BlockSpec((1, qrows, LANES), lambda bi, p, i: (bi, i, p)),
        out_shape=jax.ShapeDtypeStruct((b, s, D_SB), F32),
        scratch_shapes=[pltpu.VMEM((nchain, blk, LANES), BF16),
                        pltpu.VMEM(((2 + qrows // blk) * nchain, blk, blk), F32),
                        pltpu.VMEM((nchain, blk, blk), BF16),
                        pltpu.VMEM((nchain, blk, LANES), F32),
                        pltpu.VMEM((nchain, blk, LANES), F32),
                        pltpu.VMEM((nchain, blk, LANES), F32)],
        compiler_params=_cparams(("arbitrary", "arbitrary", "arbitrary")),
        name="stick",
    )(q_s, k_s, v_s, tri2)


def _postmix_kernel(x_ref, od_ref, os_ref, gd_ref, gs_ref, wout_ref, gate_ref, shift_ref, scale_ref,
                    gffn_ref, wr_ref, tril_ref, triu_ref,
                    x1_ref, h2_ref, route_ref, cnt_ref, base_ref, carry_scr):
    tm = POST_ROWS

    @pl.when(jnp.logical_and(pl.program_id(0) == 0, pl.program_id(1) == 0))
    def _():
        carry_scr[...] = jnp.zeros_like(carry_scr)

    mixed = jnp.concatenate([_rms(od_ref[0], gd_ref[...]), _rms(os_ref[0], gs_ref[...])], axis=-1)
    proj = jnp.dot(mixed.astype(BF16), wout_ref[...], preferred_element_type=F32)
    x1 = x_ref[0] + gate_ref[0] * proj
    x1_ref[0] = x1
    h2 = _rms(x1, gffn_ref[...]) * (1.0 + scale_ref[0]) + shift_ref[0]
    h_hi = h2.astype(BF16)
    h_lo = (h2 - h_hi.astype(F32)).astype(BF16)
    wr = wr_ref[...]
    w_hi = wr.astype(BF16)
    w_lo = (wr - w_hi.astype(F32)).astype(BF16)
    logits = jnp.dot(jnp.concatenate([h_hi, h_lo, h_hi], axis=1),
                     jnp.concatenate([w_hi, w_hi, w_lo], axis=0), preferred_element_type=F32)

    lane = lax.broadcasted_iota(jnp.int32, (tm, LANES), 1)
    big = jnp.int32(LANES)
    lmax = lambda v: jnp.max(v, axis=-1, keepdims=True)
    lmin = lambda v: jnp.min(v, axis=-1, keepdims=True)
    lsum = lambda v: jnp.sum(v, axis=-1, keepdims=True)

    gmask = lane < N_GROUPS
    gl = jnp.where(gmask, logits, NEG_INF)
    gmx = lmax(gl)
    group = lmin(jnp.where(jnp.logical_and(gmask, gl == gmx), lane, big))
    group_gate = 1.0 / lsum(jnp.exp(gl - gmx))
    lo = ROUTE_LANE0 + group * EXPERTS_PER_GROUP
    emask = jnp.logical_and(lane >= lo, lane < lo + EXPERTS_PER_GROUP)
    el = jnp.where(emask, logits, NEG_INF)
    l1 = lmax(el)
    i1 = lmin(jnp.where(el == l1, lane, big))
    el2 = jnp.where(lane == i1, NEG_INF, el)
    l2 = lmax(el2)
    i2 = lmin(jnp.where(el2 == l2, lane, big))
    r = jnp.exp(l2 - l1)
    w1 = group_gate / (1.0 + r)
    w2 = group_gate * r / (1.0 + r)

    is1 = lane == i1
    is2 = lane == i2
    oh = jnp.where(is1, 1.0, jnp.where(is2, 1.0, 0.0))
    earlier = jnp.dot(tril_ref[...], oh.astype(BF16), preferred_element_type=F32)
    runs = jnp.floor((jnp.sum(oh, axis=0, keepdims=True) + (SUBLANES - 1.0)) * (1.0 / SUBLANES))
    run_off = jnp.dot(jnp.broadcast_to(runs, (SUBLANES, LANES)).astype(BF16), triu_ref[...],
                      preferred_element_type=F32)[0:1]
    pos = earlier + run_off * SUBLANES
    slot1 = lsum(jnp.where(is1, pos, 0.0))
    slot2 = lsum(jnp.where(is2, pos, 0.0))
    cnt = runs * SUBLANES
    cnt_ref[0] = cnt
    base_ref[0] = carry_scr[...]
    carry_scr[...] = carry_scr[...] + cnt

    h2_ref[0] = h2.astype(BF16)
    route_ref[0] = jnp.where(lane == 0, slot1, jnp.where(lane == 1, slot2,
                                                         jnp.where(lane == 2, w1, jnp.where(lane == 3, w2, 0.0))))


def _postmix(x, o_dil, o_sb, g_dil, g_sb, w_out_bf16, gate, shift, scale, g_ffn, w_router):
    b, s, d = x.shape
    tm = POST_ROWS
    nt = s // tm
    tril = jnp.asarray(np.tril(np.ones((tm, tm), np.float32), -1), BF16)
    triu = jnp.asarray(np.triu(np.ones((LANES, LANES), np.float32), 1), BF16)
    row = lambda w: pl.BlockSpec((1, tm, w), lambda bi, i: (bi, i, 0))
    vec = lambda w: pl.BlockSpec((1, w), lambda bi, i: (0, 0))
    mod_spec = pl.BlockSpec((1, 1, d), lambda bi, i: (bi, 0, 0))
    tile_vec = pl.BlockSpec((1, 1, LANES), lambda bi, i: (bi * nt + i, 0, 0))
    return pl.pallas_call(
        _postmix_kernel,
        grid=(b, nt),
        in_specs=[row(d), row(D_DIL), row(D_SB), vec(D_DIL), vec(D_SB),
                  pl.BlockSpec((d, d), lambda bi, i: (0, 0)),
                  mod_spec, mod_spec, mod_spec, vec(d),
                  pl.BlockSpec((d, LANES), lambda bi, i: (0, 0)),
                  pl.BlockSpec((tm, tm), lambda bi, i: (0, 0)),
                  pl.BlockSpec((LANES, LANES), lambda bi, i: (0, 0))],
        out_specs=[row(d), row(d), row(LANES), tile_vec, tile_vec],
        out_shape=[jax.ShapeDtypeStruct((b, s, d), F32),
                   jax.ShapeDtypeStruct((b, s, d), BF16),
                   jax.ShapeDtypeStruct((b, s, LANES), F32),
                   jax.ShapeDtypeStruct((b * nt, 1, LANES), F32),
                   jax.ShapeDtypeStruct((b * nt, 1, LANES), F32)],
        scratch_shapes=[pltpu.VMEM((1, LANES), F32)],
        compiler_params=_cparams(("arbitrary", "arbitrary")),
        name="postmix",
    )(x, o_dil, o_sb, g_dil.reshape(1, -1), g_sb.reshape(1, -1), w_out_bf16, gate, shift, scale,
      g_ffn.reshape(1, d), w_router, tril, triu)


def _for_each_run_piece(tile, start_ref, cnt_ref, base_ref, fn):
    def body(e, off):
        c = cnt_ref[tile * N_EXPERTS + e]
        sorted0 = start_ref[e] + base_ref[tile * N_EXPERTS + e]
        for k in range(3, 10):
            p = 1 << k

            @pl.when((c & p) != 0)
            def _(p=p):
                done = c - (c & (2 * p - 1))
                fn(pl.multiple_of(off + done, SUBLANES), pl.multiple_of(sorted0 + done, SUBLANES), p)
        return off + c

    return lax.fori_loop(0, N_EXPERTS, body, 0)


def _sort_kernel(start_ref, cnt_ref, base_ref, h2_ref, route_ref, buf_ref, xs_scr, sem):
    tm = POST_ROWS
    lt = LOCAL_ROWS
    d = h2_ref.shape[2]
    tile = pl.program_id(0)
    slot = tile % 2

    def piece(slot, lrow, srow, rows):
        return pltpu.make_async_copy(xs_scr.at[slot, pl.ds(lrow, rows)], buf_ref.at[pl.ds(srow, rows)],
                                     sem.at[slot])

    def drain(tile, slot):
        _for_each_run_piece(tile, start_ref, cnt_ref, base_ref, lambda *a: piece(slot, *a).wait())

    @pl.when(tile >= 2)
    def _():
        drain(tile - 2, slot)

    lane = lax.broadcasted_iota(jnp.int32, (tm, LANES), 1)
    route = route_ref[0]
    w1 = jnp.sum(jnp.where(lane == 2, route, 0.0), axis=-1, keepdims=True)
    w2 = jnp.sum(jnp.where(lane == 3, route, 0.0), axis=-1, keepdims=True)

    def pieces(w):
        hi, mid, lw = _split3(w)
        return jnp.where(lane == 0, hi.astype(F32),
                         jnp.where(lane == 1, mid.astype(F32),
                                   jnp.where(lane == 2, lw.astype(F32), 0.0))).astype(BF16)

    route_t = route.T
    s1 = route_t[0:1, :].astype(jnp.int32)
    s2 = route_t[1:2, :].astype(jnp.int32)
    row = lax.broadcasted_iota(jnp.int32, (lt, tm), 0)
    p1 = jnp.where(row == s1, 1.0, 0.0)
    p2 = jnp.where(row == s2, 1.0, 0.0)
    xs_scr[slot, :, 0:d] = jnp.dot((p1 + p2).astype(BF16), h2_ref[0], preferred_element_type=F32)
    xs_scr[slot, :, d:] = (jnp.dot(p1.astype(BF16), pieces(w1), preferred_element_type=F32)
                           + jnp.dot(p2.astype(BF16), pieces(w2), preferred_element_type=F32))
    _for_each_run_piece(tile, start_ref, cnt_ref, base_ref, lambda *a: piece(slot, *a).start())

    last = pl.num_programs(0) - 1

    @pl.when(jnp.logical_and(tile == last, tile >= 1))
    def _():
        drain(tile - 1, 1 - slot)

    @pl.when(tile == last)
    def _():
        drain(tile, slot)


def _dispatch(pad_start, cnt, base, h2, route, cap):
    b, s, d = h2.shape
    tm = POST_ROWS
    nt = s // tm
    return pl.pallas_call(
        _sort_kernel,
        grid_spec=pltpu.PrefetchScalarGridSpec(
            num_scalar_prefetch=3, grid=(b * nt,),
            in_specs=[pl.BlockSpec((1, tm, d), lambda t, *_: (t // nt, t % nt, 0)),
                      pl.BlockSpec((1, tm, LANES), lambda t, *_: (t // nt, t % nt, 0))],
            out_specs=pl.BlockSpec(memory_space=pl.ANY),
            scratch_shapes=[pltpu.VMEM((2, LOCAL_ROWS, d + LANES), F32), pltpu.SemaphoreType.DMA((2,))]),
        out_shape=jax.ShapeDtypeStruct((cap, d + LANES), F32),
        compiler_params=_cparams(("arbitrary",)),
        name="dispatch",
    )(pad_start, cnt, base, h2, route)


def _expert_kernel(be_ref, live_ref, x_ref, wg_ref, wu_ref, wd_ref, y_ref):
    del be_ref
    d = y_ref.shape[1]
    live = live_ref[pl.program_id(0)]

    @pl.when(live > 0)
    def _():
        keep = lax.broadcasted_iota(jnp.int32, (x_ref.shape[0], 1), 0) < live
        x = jnp.where(keep, x_ref[...], 0.0)
        xb = x[:, 0:d].astype(BF16)
        weight = jnp.sum(x[:, d:], axis=-1, keepdims=True)
        gate = jnp.dot(xb, wg_ref[0], preferred_element_type=F32)
        up = jnp.dot(xb, wu_ref[0], preferred_element_type=F32)
        act = gate / (1.0 + jnp.exp(-gate)) * up
        y_ref[...] = jnp.dot(act.astype(BF16), wd_ref[0], preferred_element_type=F32) * weight

    @pl.when(live == 0)
    def _():
        y_ref[...] = jnp.zeros_like(y_ref)


def _experts(block_expert, live_rows, buf, wg, wu, wd):
    cap, dw = buf.shape
    d, f = wg.shape[1], wg.shape[2]
    bm = EXPERT_ROWS
    return pl.pallas_call(
        _expert_kernel,
        grid_spec=pltpu.PrefetchScalarGridSpec(
            num_scalar_prefetch=2, grid=(cap // bm,),
            in_specs=[pl.BlockSpec((bm, dw), lambda i, be, nu: (i, 0)),
                      pl.BlockSpec((1, d, f), lambda i, be, nu: (be[i], 0, 0)),
                      pl.BlockSpec((1, d, f), lambda i, be, nu: (be[i], 0, 0)),
                      pl.BlockSpec((1, f, d), lambda i, be, nu: (be[i], 0, 0))],
            out_specs=pl.BlockSpec((bm, d), lambda i, be, nu: (i, 0))),
        out_shape=jax.ShapeDtypeStruct((cap, d), F32),
        compiler_params=_cparams(("arbitrary",)),
        name="experts",
    )(block_expert, live_rows, buf, wg, wu, wd)


def _combine_kernel(start_ref, cnt_ref, base_ref, x1_ref, route_ref, gate_ref, g_ref, y_hbm_ref, o_ref,
                    y_scr, sem):
    tm = POST_ROWS
    lt = LOCAL_ROWS
    tile = pl.program_id(0)
    slot = tile % 2

    def piece(slot, lrow, srow, rows):
        return pltpu.make_async_copy(y_hbm_ref.at[pl.ds(srow, rows)], y_scr.at[slot, pl.ds(lrow, rows)],
                                     sem.at[slot])

    def fetch(tile, slot):
        _for_each_run_piece(tile, start_ref, cnt_ref, base_ref, lambda *a: piece(slot, *a).start())

    @pl.when(tile == 0)
    def _():
        fetch(tile, slot)

    @pl.when(tile + 1 < pl.num_programs(0))
    def _():
        fetch(tile + 1, 1 - slot)

    lane = lax.broadcasted_iota(jnp.int32, (tm, LANES), 1)
    route = route_ref[0]
    s1 = jnp.sum(jnp.where(lane == 0, route, 0.0), axis=-1, keepdims=True).astype(jnp.int32)
    s2 = jnp.sum(jnp.where(lane == 1, route, 0.0), axis=-1, keepdims=True).astype(jnp.int32)
    col = lax.broadcasted_iota(jnp.int32, (tm, lt), 1)
    pick = jnp.where(col == s1, 1.0, jnp.where(col == s2, 1.0, 0.0)).astype(BF16)
    used = _for_each_run_piece(tile, start_ref, cnt_ref, base_ref, lambda *a: piece(slot, *a).wait())
    live = lax.broadcasted_iota(jnp.int32, (lt, 1), 0) < used
    yv = jnp.where(live, y_scr[slot], 0.0)
    hi = yv.astype(BF16)
    lo = (yv - hi.astype(F32)).astype(BF16)
    y = jnp.dot(jnp.concatenate([pick, pick], axis=1), jnp.concatenate([hi, lo], axis=0),
                preferred_element_type=F32)
    o_ref[0] = _rms(x1_ref[0] + gate_ref[0] * y, g_ref[...])


def _combine(pad_start, cnt, base, x1, y_sorted, route, gate, g_final):
    b, s, d = x1.shape
    tm = POST_ROWS
    nt = s // tm
    return pl.pallas_call(
        _combine_kernel,
        grid_spec=pltpu.PrefetchScalarGridSpec(
            num_scalar_prefetch=3, grid=(b * nt,),
            in_specs=[pl.BlockSpec((1, tm, d), lambda t, *_: (t // nt, t % nt, 0)),
                      pl.BlockSpec((1, tm, LANES), lambda t, *_: (t // nt, t % nt, 0)),
                      pl.BlockSpec((1, 1, d), lambda t, *_: (t // nt, 0, 0)),
                      pl.BlockSpec((1, d), lambda t, *_: (0, 0)),
                      pl.BlockSpec(memory_space=pl.ANY)],
            out_specs=pl.BlockSpec((1, tm, d), lambda t, *_: (t // nt, t % nt, 0)),
            scratch_shapes=[pltpu.VMEM((2, LOCAL_ROWS, d), F32), pltpu.SemaphoreType.DMA((2,))]),
        out_shape=jax.ShapeDtypeStruct((b, s, d), F32),
        compiler_params=_cparams(("arbitrary",)),
        name="combine",
    )(pad_start, cnt, base, x1, route, gate, g_final.reshape(1, d), y_sorted)


def kernel(x, c, w_ada, b_ada, g_mix, w_in, g_dil_out, g_sb_out, w_out, g_ffn,
           w_group, w_expert, w_gate, w_up, w_down, g_final):
    b, s, d = x.shape
    depth = w_ada.shape[0]
    assert s % DIL_UNIT == 0 and d == D_DIL + D_SB
    assert depth == 1, "the final rmsnorm is fused into the last layer's combine step"
    n = b * s
    ntiles = n // POST_ROWS
    bias = jnp.asarray(_dilated_bias())
    for layer in range(depth):
        mod = _ada(c, w_ada[layer], b_ada[layer])
        shift_mix, scale_mix, gate_mix, shift_ffn, scale_ffn, gate_ffn = (
            m.reshape(b, 1, d) for m in jnp.split(mod, 6, axis=-1))

        qkv_d, q_s, k_s, v_s = _premix(x, shift_mix, scale_mix, g_mix[layer], w_in[layer].astype(BF16))
        o_dil = _dilated(qkv_d, bias)
        o_sb = _stick(q_s, k_s, v_s)

        w_router = jnp.concatenate(
            [w_group[layer], w_expert[layer],
             jnp.zeros((d, LANES - N_GROUPS - N_EXPERTS), F32)], axis=1)
        x1, h2, route, cnt, base = _postmix(
            x, o_dil, o_sb, g_dil_out[layer], g_sb_out[layer], w_out[layer].astype(BF16),
            gate_mix, shift_ffn, scale_ffn, g_ffn[layer], w_router)

        bm = EXPERT_ROWS
        cnt = cnt[:, 0, ROUTE_LANE0:ROUTE_LANE0 + N_EXPERTS].astype(jnp.int32)
        base = base[:, 0, ROUTE_LANE0:ROUTE_LANE0 + N_EXPERTS].astype(jnp.int32)
        total = base[-1] + cnt[-1]
        cnt = cnt.reshape(-1)
        base = base.reshape(-1)
        padded = (total + bm - 1) // bm * bm
        pad_end = jnp.cumsum(padded)
        pad_start = (pad_end - padded).astype(jnp.int32)
        cap = -(-(2 * n + (SUBLANES - 1) * N_EXPERTS * ntiles) // bm) * bm + N_EXPERTS * bm
        n_blocks = cap // bm
        block_expert = jnp.minimum(
            jnp.sum(pad_end[None, :] <= (jnp.arange(n_blocks) * bm)[:, None], axis=1),
            N_EXPERTS - 1).astype(jnp.int32)
        live_rows = jnp.clip((pad_start + total)[block_expert] - jnp.arange(n_blocks) * bm, 0, bm).astype(jnp.int32)

        buf = _dispatch(pad_start, cnt, base, h2, route, cap)
        y_sorted = _experts(block_expert, live_rows, buf, w_gate[layer].astype(BF16),
                            w_up[layer].astype(BF16), w_down[layer].astype(BF16))
        x = _combine(pad_start, cnt, base, x1, y_sorted, route, gate_ffn, g_final)
    return x
```

```python
import numpy as np
import jax
import jax.numpy as jnp
from jax import lax
from jax.experimental import pallas as pl
from jax.experimental.pallas import tpu as pltpu

HEAD_DIM = 64
N_HEADS_DIL = 8
N_HEADS_SB = 8
D_DIL = N_HEADS_DIL * HEAD_DIM
D_SB = N_HEADS_SB * HEAD_DIM
DILATION_PATTERNS = ((128, 1), (512, 4), (2048, 16))
N_GROUPS = 4
EXPERTS_PER_GROUP = 8
N_EXPERTS = N_GROUPS * EXPERTS_PER_GROUP
NORM_EPS = 1e-6

LANES = 128
SUBLANES = 8
DIL_STEPS = 128
DIL_UNIT = 2048
DIL_TILES_PER_TRIP = 4
SB_BLOCK = 256
SB_QUERY_ROWS = 512
SB_BLOCKS_PER_TRIP = 2
PRE_ROWS = 512
POST_ROWS = 512
LOCAL_ROWS = 2 * POST_ROWS + 256
EXPERT_ROWS = 512
ROUTE_LANE0 = N_GROUPS
VMEM_LIMIT = 56 * 1024 * 1024

F32 = jnp.float32
BF16 = jnp.bfloat16
NEG_INF = float("-inf")
LOG2E = 1.4426950408889634


def _cparams(sem):
    return pltpu.CompilerParams(dimension_semantics=sem, vmem_limit_bytes=VMEM_LIMIT)


def _rms(v, g):
    return v * lax.rsqrt(jnp.mean(v * v, axis=-1, keepdims=True) + NORM_EPS) * g


def _split3(v):
    hi = v.astype(BF16)
    r = v - hi.astype(F32)
    mid = r.astype(BF16)
    lo = (r - mid.astype(F32)).astype(BF16)
    return hi, mid, lo


def _ada_kernel(c_ref, w_ref, b_ref, o_ref):
    c = c_ref[...]
    cond = c / (1.0 + jnp.exp(-c))
    o_ref[...] = jnp.dot(cond, w_ref[...], precision=lax.Precision.HIGHEST,
                         preferred_element_type=F32) + b_ref[...]


def _ada(c, w_ada, b_ada):
    b, d = c.shape
    n = w_ada.shape[1]
    return pl.pallas_call(
        _ada_kernel,
        grid=(n // d,),
        in_specs=[pl.BlockSpec((b, d), lambda j: (0, 0)),
                  pl.BlockSpec((d, d), lambda j: (0, j)),
                  pl.BlockSpec((1, d), lambda j: (0, j))],
        out_specs=pl.BlockSpec((b, d), lambda j: (0, j)),
        out_shape=jax.ShapeDtypeStruct((b, n), F32),
        compiler_params=_cparams(("arbitrary",)),
        name="ada",
    )(c, w_ada, b_ada.reshape(1, n))


def _premix_kernel(x_ref, shift_ref, scale_ref, g_ref, w_ref, qkvd_ref, qs_ref, ks_ref, vs_ref):
    h = _rms(x_ref[0], g_ref[...]) * (1.0 + scale_ref[0]) + shift_ref[0]
    hb = h.astype(BF16)
    scale = HEAD_DIM ** -0.5 * LOG2E
    for j in range(6):
        r = jnp.dot(hb, w_ref[:, j * 512:(j + 1) * 512], preferred_element_type=F32)
        if j == 0:
            qkvd_ref[0, :, 0:512] = r * scale
        elif j < 3:
            qkvd_ref[0, :, j * 512:(j + 1) * 512] = r
        elif j == 3:
            qs_ref[0] = (r * scale).astype(BF16)
        elif j == 4:
            ks_ref[0] = r.astype(BF16)
        else:
            vs_ref[0] = r.astype(BF16)


def _premix(x, shift, scale, g_mix, w_in_bf16):
    b, s, d = x.shape
    tm = PRE_ROWS
    mod_spec = pl.BlockSpec((1, 1, d), lambda bi, i: (bi, 0, 0))
    sb_spec = pl.BlockSpec((1, tm, D_SB), lambda bi, i: (bi, i, 0))
    return pl.pallas_call(
        _premix_kernel,
        grid=(b, s // tm),
        in_specs=[pl.BlockSpec((1, tm, d), lambda bi, i: (bi, i, 0)),
                  mod_spec, mod_spec,
                  pl.BlockSpec((1, d), lambda bi, i: (0, 0)),
                  pl.BlockSpec((d, 3 * (D_DIL + D_SB)), lambda bi, i: (0, 0))],
        out_specs=[pl.BlockSpec((1, tm, 3 * D_DIL), lambda bi, i: (bi, i, 0)),
                   sb_spec, sb_spec, sb_spec],
        out_shape=[jax.ShapeDtypeStruct((b, s, 3 * D_DIL), F32),
                   jax.ShapeDtypeStruct((b, s, D_SB), BF16),
                   jax.ShapeDtypeStruct((b, s, D_SB), BF16),
                   jax.ShapeDtypeStruct((b, s, D_SB), BF16)],
        compiler_params=_cparams(("arbitrary", "arbitrary")),
        name="premix",
    )(x, shift, scale, g_mix.reshape(1, d), w_in_bf16)


def _dilated_bias():
    n = DIL_STEPS
    slopes = np.array([2.0 ** (-8.0 * (i + 1) / N_HEADS_DIL) for i in range(N_HEADS_DIL)], dtype=np.float32)
    steps = np.arange(n)[:, None] + n - np.arange(2 * n)[None, :]
    valid = (steps >= 0) & (steps <= n)
    out = []
    for _, dilation in DILATION_PATTERNS:
        bias = -slopes[:, None, None] * (steps * dilation).astype(np.float32)[None]
        out.append(np.where(valid[None], bias.astype(np.float64) * LOG2E, -np.inf).astype(np.float32))
    return np.stack(out)


def _dil_kernel(q_ref, kc_ref, kp_ref, vc_ref, vp_ref, bias_ref, o_ref,
                kext, vext, u_scr, m_scr, l_scr):
    n = DIL_STEPS
    g = pl.program_id(1)
    kext[0:DIL_UNIT, :] = kp_ref[0]
    kext[DIL_UNIT:2 * DIL_UNIT, :] = kc_ref[0]
    vext[0:DIL_UNIT, :] = vp_ref[0]
    vext[DIL_UNIT:2 * DIL_UNIT, :] = vc_ref[0]
    lane = lax.broadcasted_iota(jnp.int32, (n, LANES), 1)
    head0 = lane < HEAD_DIM
    col = lax.broadcasted_iota(jnp.int32, (n, 2 * n), 1)

    for p, (_, dil) in enumerate(DILATION_PATTERNS):
        unit = n * dil

        def tiles(it, carry, p=p, dil=dil, unit=unit):
            rows_of, vvs, deads, ss = [], [], [], []
            for t in range(DIL_TILES_PER_TRIP):
                ti = it * DIL_TILES_PER_TRIP + t
                j = ti // dil
                r = ti % dil
                qstart = j * unit + r
                kstart = DIL_UNIT + qstart - unit
                if dil == 1:
                    rows_of.append(pl.ds(qstart, n))
                    krows = pl.ds(kstart, 2 * n)
                else:
                    rows_of.append(pl.ds(qstart, n, stride=dil))
                    krows = pl.ds(kstart, 2 * n, stride=dil)
                q = q_ref[0, rows_of[t], :]
                kk = kext[krows, :].astype(BF16)
                vvs.append(vext[krows, :].astype(BF16))
                deads.append(jnp.where(jnp.logical_and(g == 0, j == 0), n, 0))
                for h in range(2):
                    qh = jnp.where(head0 if h == 0 else jnp.logical_not(head0), q, 0.0).astype(BF16)
                    ss.append(lax.dot_general(qh, kk, (((1,), (1,)), ((), ())), preferred_element_type=F32))
            ms, ls, pes = [], [], []
            for t in range(DIL_TILES_PER_TRIP):
                for h in range(2):
                    logits = jnp.where(col < deads[t], NEG_INF, ss[2 * t + h] + bias_ref[p, h])
                    m = jnp.max(logits, axis=-1, keepdims=True)
                    pe = jnp.exp2(logits - m)
                    ls.append(jnp.sum(pe, axis=-1, keepdims=True))
                    ms.append(m)
                    pes.append(pe.astype(BF16))
            us = [jnp.dot(pes[2 * t + h], vvs[t], preferred_element_type=F32)
                  for t in range(DIL_TILES_PER_TRIP) for h in range(2)]
            for t in range(DIL_TILES_PER_TRIP):
                u_scr[p, rows_of[t], :] = jnp.where(head0, us[2 * t], us[2 * t + 1])
                m_scr[p, rows_of[t], :] = jnp.where(head0, ms[2 * t], ms[2 * t + 1])
                l_scr[p, rows_of[t], :] = jnp.where(head0, ls[2 * t], ls[2 * t + 1])
            return carry

        lax.fori_loop(0, DIL_UNIT // n // DIL_TILES_PER_TRIP, tiles, 0)

    def merge(i, carry):
        rows = pl.ds(pl.multiple_of(i * n, n), n)
        m0, m1, m2 = m_scr[0, rows, :], m_scr[1, rows, :], m_scr[2, rows, :]
        mx = jnp.maximum(jnp.maximum(m0, m1), m2)
        w0, w1, w2 = jnp.exp2(m0 - mx), jnp.exp2(m1 - mx), jnp.exp2(m2 - mx)
        num = w0 * u_scr[0, rows, :] + w1 * u_scr[1, rows, :] + w2 * u_scr[2, rows, :]
        den = w0 * l_scr[0, rows, :] + w1 * l_scr[1, rows, :] + w2 * l_scr[2, rows, :]
        o_ref[0, rows, :] = num / den
        return carry

    lax.fori_loop(0, DIL_UNIT // n, merge, 0)


def _dilated(qkv_d, bias):
    b, s, _ = qkv_d.shape
    u = DIL_UNIT
    npair = D_DIL // LANES
    cur = lambda off: pl.BlockSpec((1, u, LANES), lambda bi, g, p: (bi, g, off + p))
    prev = lambda off: pl.BlockSpec((1, u, LANES), lambda bi, g, p: (bi, jnp.maximum(g - 1, 0), off + p))
    return pl.pallas_call(
        _dil_kernel,
        grid=(b, s // u, npair),
        in_specs=[cur(0), cur(npair), prev(npair), cur(2 * npair), prev(2 * npair),
                  pl.BlockSpec((3, 2, DIL_STEPS, 2 * DIL_STEPS), lambda bi, g, p: (0, p, 0, 0))],
        out_specs=pl.BlockSpec((1, u, LANES), lambda bi, g, p: (bi, g, p)),
        out_shape=jax.ShapeDtypeStruct((b, s, D_DIL), F32),
        scratch_shapes=[pltpu.VMEM((2 * u, LANES), F32), pltpu.VMEM((2 * u, LANES), F32),
                        pltpu.VMEM((3, u, LANES), F32), pltpu.VMEM((3, u, LANES), F32),
                        pltpu.VMEM((3, u, LANES), F32)],
        compiler_params=_cparams(("arbitrary", "arbitrary", "arbitrary")),
        name="dilated",
    )(qkv_d, qkv_d, qkv_d, qkv_d, qkv_d, bias)


def _stick_kernel(q_ref, k_ref, v_ref, tri_ref, o_ref,
                  qh_scr, z_scr, w_scr, acc_scr, carry_scr):
    blk = SB_BLOCK
    nsub = SB_QUERY_ROWS // blk
    assert nsub % 2 == 0
    nchain = 2 * nsub
    qi = pl.program_id(2)
    lane = lax.broadcasted_iota(jnp.int32, (blk, LANES), 1)
    head0 = lane < HEAD_DIM
    for sub in range(nsub):
        q = q_ref[0, sub * blk:(sub + 1) * blk, :]
        zero = jnp.zeros_like(q)
        qh_scr[2 * sub] = jnp.where(head0, q, zero)
        qh_scr[2 * sub + 1] = jnp.where(head0, zero, q)
    acc_scr[...] = jnp.zeros_like(acc_scr)
    carry_scr[...] = jnp.zeros_like(carry_scr)
    sign = jnp.int32(-2 ** 31)

    def rows(kb):
        return pl.ds(pl.multiple_of(kb * blk, blk), blk)

    def scores(kb, which, slot):
        kblk = k_ref[0, rows(kb), :]
        for c in which:
            z_scr[slot * nchain + c] = lax.dot_general(
                qh_scr[c], kblk, (((1,), (1,)), ((), ())), preferred_element_type=F32)

    def weights(which, slot, diag_sub, beside=None):
        causal = (lax.broadcasted_iota(jnp.int32, (blk, blk), 1)
                  < lax.broadcasted_iota(jnp.int32, (blk, blk), 0))
        splits = {}
        for c in which:
            z = z_scr[slot * nchain + c]
            neg_abs = lax.bitcast_convert_type(lax.bitcast_convert_type(z, jnp.int32) | sign, F32)
            softplus = jnp.maximum(z, 0.0) + jnp.log(1.0 + jnp.exp2(neg_abs)) * LOG2E
            if c // 2 == diag_sub:
                softplus = jnp.where(causal, softplus, 0.0)
            hi = softplus.astype(BF16)
            lo = (softplus - hi.astype(F32)).astype(BF16)
            splits[c] = jnp.concatenate([hi, lo], axis=1)
        sums = {}
        for c in which:
            if beside is not None:
                beside(c)
            sums[c] = jnp.dot(splits[c], tri_ref[...], preferred_element_type=F32)
        for c in which:
            carry = carry_scr[c]
            w = jnp.exp2((z_scr[slot * nchain + c] - sums[c]) + carry[:, 0:1])
            if c // 2 == diag_sub:
                w = jnp.where(causal, w, 0.0)
            w_scr[c] = w.astype(BF16)
            carry_scr[c] = carry - sums[c][:, 0:LANES]

    def accumulate(which, kb):
        vblk = v_ref[0, rows(kb), :]
        for c in which:
            acc_scr[c] = acc_scr[c] + jnp.dot(w_scr[c], vblk, preferred_element_type=F32)

    everyone = list(range(nchain))
    top = nsub * qi + nsub - 1
    first = nsub * qi - 1
    diag = [[c for c in everyone if c // 2 >= nsub - 1 - i] for i in range(nsub)]
    for i in range(nsub):
        scores(top - i, diag[i], 2 + i)
    scores(jnp.maximum(first, 0), everyone, 0)
    for i in range(nsub):
        def previous(c, i=i):
            if i > 0 and c in diag[i - 1]:
                accumulate([c], top - (i - 1))

        weights(diag[i], 2 + i, nsub - 1 - i, beside=previous)

    def blocks(kb0, count):
        for j in range(count):
            kb = kb0 - j

            def neighbours(c, kb=kb, j=j):
                accumulate([c], kb + 1)
                scores(jnp.maximum(kb - 1, 0), [c], 1 - j % 2)

            weights(everyone, j % 2, -1, beside=neighbours)

    per = SB_BLOCKS_PER_TRIP
    trips = nsub * qi // per

    def step(i, carry):
        blocks(first - per * i, per)
        return carry

    lax.fori_loop(0, trips, step, 0)
    for rest in range(2, per, 2):
        @pl.when(nsub * qi - trips * per == rest)
        def _(rest=rest):
            blocks(first - per * trips, rest)
    accumulate(everyone, 0)
    for sub in range(nsub):
        o_ref[0, sub * blk:(sub + 1) * blk, :] = jnp.where(head0, acc_scr[2 * sub], acc_scr[2 * sub + 1])


def _stick(q_s, k_s, v_s):
    b, s, _ = q_s.shape
    blk = SB_BLOCK
    qrows = SB_QUERY_ROWS
    nchain = 2 * qrows // blk
    tri = np.tril(np.ones((blk, blk), np.float32))
    tri2 = jnp.asarray(np.concatenate([tri, tri], axis=0), BF16)
    full = pl.BlockSpec((1, s, LANES), lambda bi, p, i: (bi, 0, p))
    return pl.pallas_call(
        _stick_kernel,
        grid=(b, D_SB // LANES, s // qrows),
        in_specs=[pl.BlockSpec((1, qrows, LANES), lambda bi, p, i: (bi, i, p)), full, full,
                  pl.BlockSpec((2 * blk, blk), lambda bi, p, i: (0, 0))],
        out_specs=pl.BlockSpec((1, qrows, LANES), lambda bi, p, i: (bi, i, p)),
        out_shape=jax.ShapeDtypeStruct((b, s, D_SB), F32),
        scratch_shapes=[pltpu.VMEM((nchain, blk, LANES), BF16),
                        pltpu.VMEM(((2 + qrows // blk) * nchain, blk, blk), F32),
                        pltpu.VMEM((nchain, blk, blk), BF16),
                        pltpu.VMEM((nchain, blk, LANES), F32),
                        pltpu.VMEM((nchain, blk, LANES), F32)],
        compiler_params=_cparams(("arbitrary", "arbitrary", "arbitrary")),
        name="stick",
    )(q_s, k_s, v_s, tri2)


def _postmix_kernel(x_ref, od_ref, os_ref, gd_ref, gs_ref, wout_ref, gate_ref, shift_ref, scale_ref,
                    gffn_ref, wr_ref, tril_ref, triu_ref,
                    x1_ref, h2_ref, route_ref, cnt_ref, base_ref, carry_scr):
    tm = POST_ROWS

    @pl.when(jnp.logical_and(pl.program_id(0) == 0, pl.program_id(1) == 0))
    def _():
        carry_scr[...] = jnp.zeros_like(carry_scr)

    mixed = jnp.concatenate([_rms(od_ref[0], gd_ref[...]), _rms(os_ref[0], gs_ref[...])], axis=-1)
    proj = jnp.dot(mixed.astype(BF16), wout_ref[...], preferred_element_type=F32)
    x1 = x_ref[0] + gate_ref[0] * proj
    x1_ref[0] = x1
    h2 = _rms(x1, gffn_ref[...]) * (1.0 + scale_ref[0]) + shift_ref[0]
    h_hi = h2.astype(BF16)
    h_lo = (h2 - h_hi.astype(F32)).astype(BF16)
    wr = wr_ref[...]
    w_hi = wr.astype(BF16)
    w_lo = (wr - w_hi.astype(F32)).astype(BF16)
    logits = jnp.dot(jnp.concatenate([h_hi, h_lo, h_hi], axis=1),
                     jnp.concatenate([w_hi, w_hi, w_lo], axis=0), preferred_element_type=F32)

    lane = lax.broadcasted_iota(jnp.int32, (tm, LANES), 1)
    big = jnp.int32(LANES)
    lmax = lambda v: jnp.max(v, axis=-1, keepdims=True)
    lmin = lambda v: jnp.min(v, axis=-1, keepdims=True)
    lsum = lambda v: jnp.sum(v, axis=-1, keepdims=True)

    gmask = lane < N_GROUPS
    gl = jnp.where(gmask, logits, NEG_INF)
    gmx = lmax(gl)
    group = lmin(jnp.where(jnp.logical_and(gmask, gl == gmx), lane, big))
    group_gate = 1.0 / lsum(jnp.exp(gl - gmx))
    lo = ROUTE_LANE0 + group * EXPERTS_PER_GROUP
    emask = jnp.logical_and(lane >= lo, lane < lo + EXPERTS_PER_GROUP)
    el = jnp.where(emask, logits, NEG_INF)
    l1 = lmax(el)
    i1 = lmin(jnp.where(el == l1, lane, big))
    el2 = jnp.where(lane == i1, NEG_INF, el)
    l2 = lmax(el2)
    i2 = lmin(jnp.where(el2 == l2, lane, big))
    r = jnp.exp(l2 - l1)
    w1 = group_gate / (1.0 + r)
    w2 = group_gate * r / (1.0 + r)

    is1 = lane == i1
    is2 = lane == i2
    oh = jnp.where(is1, 1.0, jnp.where(is2, 1.0, 0.0))
    earlier = jnp.dot(tril_ref[...], oh.astype(BF16), preferred_element_type=F32)
    runs = jnp.floor((jnp.sum(oh, axis=0, keepdims=True) + (SUBLANES - 1.0)) * (1.0 / SUBLANES))
    run_off = jnp.dot(jnp.broadcast_to(runs, (SUBLANES, LANES)).astype(BF16), triu_ref[...],
                      preferred_element_type=F32)[0:1]
    pos = earlier + run_off * SUBLANES
    slot1 = lsum(jnp.where(is1, pos, 0.0))
    slot2 = lsum(jnp.where(is2, pos, 0.0))
    cnt = runs * SUBLANES
    cnt_ref[0] = cnt
    base_ref[0] = carry_scr[...]
    carry_scr[...] = carry_scr[...] + cnt

    h2_ref[0] = h2.astype(BF16)
    route_ref[0] = jnp.where(lane == 0, slot1, jnp.where(lane == 1, slot2,
                                                         jnp.where(lane == 2, w1, jnp.where(lane == 3, w2, 0.0))))


def _postmix(x, o_dil, o_sb, g_dil, g_sb, w_out_bf16, gate, shift, scale, g_ffn, w_router):
    b, s, d = x.shape
    tm = POST_ROWS
    nt = s // tm
    tril = jnp.asarray(np.tril(np.ones((tm, tm), np.float32), -1), BF16)
    triu = jnp.asarray(np.triu(np.ones((LANES, LANES), np.float32), 1), BF16)
    row = lambda w: pl.BlockSpec((1, tm, w), lambda bi, i: (bi, i, 0))
    vec = lambda w: pl.BlockSpec((1, w), lambda bi, i: (0, 0))
    mod_spec = pl.BlockSpec((1, 1, d), lambda bi, i: (bi, 0, 0))
    tile_vec = pl.BlockSpec((1, 1, LANES), lambda bi, i: (bi * nt + i, 0, 0))
    return pl.pallas_call(
        _postmix_kernel,
        grid=(b, nt),
        in_specs=[row(d), row(D_DIL), row(D_SB), vec(D_DIL), vec(D_SB),
                  pl.BlockSpec((d, d), lambda bi, i: (0, 0)),
                  mod_spec, mod_spec, mod_spec, vec(d),
                  pl.BlockSpec((d, LANES), lambda bi, i: (0, 0)),
                  pl.BlockSpec((tm, tm), lambda bi, i: (0, 0)),
                  pl.BlockSpec((LANES, LANES), lambda bi, i: (0, 0))],
        out_specs=[row(d), row(d), row(LANES), tile_vec, tile_vec],
        out_shape=[jax.ShapeDtypeStruct((b, s, d), F32),
                   jax.ShapeDtypeStruct((b, s, d), BF16),
                   jax.ShapeDtypeStruct((b, s, LANES), F32),
                   jax.ShapeDtypeStruct((b * nt, 1, LANES), F32),
                   jax.ShapeDtypeStruct((b * nt, 1, LANES), F32)],
        scratch_shapes=[pltpu.VMEM((1, LANES), F32)],
        compiler_params=_cparams(("arbitrary", "arbitrary")),
        name="postmix",
    )(x, o_dil, o_sb, g_dil.reshape(1, -1), g_sb.reshape(1, -1), w_out_bf16, gate, shift, scale,
      g_ffn.reshape(1, d), w_router, tril, triu)


def _for_each_run_piece(tile, start_ref, cnt_ref, base_ref, fn):
    def body(e, off):
        c = cnt_ref[tile * N_EXPERTS + e]
        sorted0 = start_ref[e] + base_ref[tile * N_EXPERTS + e]
        for k in range(3, 10):
            p = 1 << k

            @pl.when((c & p) != 0)
            def _(p=p):
                done = c - (c & (2 * p - 1))
                fn(pl.multiple_of(off + done, SUBLANES), pl.multiple_of(sorted0 + done, SUBLANES), p)
        return off + c

    return lax.fori_loop(0, N_EXPERTS, body, 0)


def _sort_kernel(start_ref, cnt_ref, base_ref, h2_ref, route_ref, buf_ref, xs_scr, sem):
    tm = POST_ROWS
    lt = LOCAL_ROWS
    d = h2_ref.shape[2]
    tile = pl.program_id(0)
    slot = tile % 2

    def piece(slot, lrow, srow, rows):
        return pltpu.make_async_copy(xs_scr.at[slot, pl.ds(lrow, rows)], buf_ref.at[pl.ds(srow, rows)],
                                     sem.at[slot])

    def drain(tile, slot):
        _for_each_run_piece(tile, start_ref, cnt_ref, base_ref, lambda *a: piece(slot, *a).wait())

    @pl.when(tile >= 2)
    def _():
        drain(tile - 2, slot)

    lane = lax.broadcasted_iota(jnp.int32, (tm, LANES), 1)
    route = route_ref[0]
    w1 = jnp.sum(jnp.where(lane == 2, route, 0.0), axis=-1, keepdims=True)
    w2 = jnp.sum(jnp.where(lane == 3, route, 0.0), axis=-1, keepdims=True)

    def pieces(w):
        hi, mid, lw = _split3(w)
        return jnp.where(lane == 0, hi.astype(F32),
                         jnp.where(lane == 1, mid.astype(F32),
                                   jnp.where(lane == 2, lw.astype(F32), 0.0))).astype(BF16)

    route_t = route.T
    s1 = route_t[0:1, :].astype(jnp.int32)
    s2 = route_t[1:2, :].astype(jnp.int32)
    row = lax.broadcasted_iota(jnp.int32, (lt, tm), 0)
    p1 = jnp.where(row == s1, 1.0, 0.0)
    p2 = jnp.where(row == s2, 1.0, 0.0)
    xs_scr[slot, :, 0:d] = jnp.dot((p1 + p2).astype(BF16), h2_ref[0], preferred_element_type=F32)
    xs_scr[slot, :, d:] = (jnp.dot(p1.astype(BF16), pieces(w1), preferred_element_type=F32)
                           + jnp.dot(p2.astype(BF16), pieces(w2), preferred_element_type=F32))
    _for_each_run_piece(tile, start_ref, cnt_ref, base_ref, lambda *a: piece(slot, *a).start())

    last = pl.num_programs(0) - 1

    @pl.when(jnp.logical_and(tile == last, tile >= 1))
    def _():
        drain(tile - 1, 1 - slot)

    @pl.when(tile == last)
    def _():
        drain(tile, slot)


def _dispatch(pad_start, cnt, base, h2, route, cap):
    b, s, d = h2.shape
    tm = POST_ROWS
    nt = s // tm
    return pl.pallas_call(
        _sort_kernel,
        grid_spec=pltpu.PrefetchScalarGridSpec(
            num_scalar_prefetch=3, grid=(b * nt,),
            in_specs=[pl.BlockSpec((1, tm, d), lambda t, *_: (t // nt, t % nt, 0)),
                      pl.BlockSpec((1, tm, LANES), lambda t, *_: (t // nt, t % nt, 0))],
            out_specs=pl.BlockSpec(memory_space=pl.ANY),
            scratch_shapes=[pltpu.VMEM((2, LOCAL_ROWS, d + LANES), F32), pltpu.SemaphoreType.DMA((2,))]),
        out_shape=jax.ShapeDtypeStruct((cap, d + LANES), F32),
        compiler_params=_cparams(("arbitrary",)),
        name="dispatch",
    )(pad_start, cnt, base, h2, route)


def _expert_kernel(be_ref, live_ref, x_ref, wg_ref, wu_ref, wd_ref, y_ref):
    del be_ref
    d = y_ref.shape[1]
    live = live_ref[pl.program_id(0)]

    @pl.when(live > 0)
    def _():
        keep = lax.broadcasted_iota(jnp.int32, (x_ref.shape[0], 1), 0) < live
        x = jnp.where(keep, x_ref[...], 0.0)
        xb = x[:, 0:d].astype(BF16)
        weight = jnp.sum(x[:, d:], axis=-1, keepdims=True)
        gate = jnp.dot(xb, wg_ref[0], preferred_element_type=F32)
        up = jnp.dot(xb, wu_ref[0], preferred_element_type=F32)
        act = gate / (1.0 + jnp.exp(-gate)) * up
        y_ref[...] = jnp.dot(act.astype(BF16), wd_ref[0], preferred_element_type=F32) * weight

    @pl.when(live == 0)
    def _():
        y_ref[...] = jnp.zeros_like(y_ref)


def _experts(block_expert, live_rows, buf, wg, wu, wd):
    cap, dw = buf.shape
    d, f = wg.shape[1], wg.shape[2]
    bm = EXPERT_ROWS
    return pl.pallas_call(
        _expert_kernel,
        grid_spec=pltpu.PrefetchScalarGridSpec(
            num_scalar_prefetch=2, grid=(cap // bm,),
            in_specs=[pl.BlockSpec((bm, dw), lambda i, be, nu: (i, 0)),
                      pl.BlockSpec((1, d, f), lambda i, be, nu: (be[i], 0, 0)),
                      pl.BlockSpec((1, d, f), lambda i, be, nu: (be[i], 0, 0)),
                      pl.BlockSpec((1, f, d), lambda i, be, nu: (be[i], 0, 0))],
            out_specs=pl.BlockSpec((bm, d), lambda i, be, nu: (i, 0))),
        out_shape=jax.ShapeDtypeStruct((cap, d), F32),
        compiler_params=_cparams(("arbitrary",)),
        name="experts",
    )(block_expert, live_rows, buf, wg, wu, wd)


def _combine_kernel(start_ref, cnt_ref, base_ref, x1_ref, route_ref, gate_ref, g_ref, y_hbm_ref, o_ref,
                    y_scr, sem):
    tm = POST_ROWS
    lt = LOCAL_ROWS
    tile = pl.program_id(0)
    slot = tile % 2

    def piece(slot, lrow, srow, rows):
        return pltpu.make_async_copy(y_hbm_ref.at[pl.ds(srow, rows)], y_scr.at[slot, pl.ds(lrow, rows)],
                                     sem.at[slot])

    def fetch(tile, slot):
        _for_each_run_piece(tile, start_ref, cnt_ref, base_ref, lambda *a: piece(slot, *a).start())

    @pl.when(tile == 0)
    def _():
        fetch(tile, slot)

    @pl.when(tile + 1 < pl.num_programs(0))
    def _():
        fetch(tile + 1, 1 - slot)

    lane = lax.broadcasted_iota(jnp.int32, (tm, LANES), 1)
    route = route_ref[0]
    s1 = jnp.sum(jnp.where(lane == 0, route, 0.0), axis=-1, keepdims=True).astype(jnp.int32)
    s2 = jnp.sum(jnp.where(lane == 1, route, 0.0), axis=-1, keepdims=True).astype(jnp.int32)
    col = lax.broadcasted_iota(jnp.int32, (tm, lt), 1)
    pick = jnp.where(col == s1, 1.0, jnp.where(col == s2, 1.0, 0.0)).astype(BF16)
    used = _for_each_run_piece(tile, start_ref, cnt_ref, base_ref, lambda *a: piece(slot, *a).wait())
    live = lax.broadcasted_iota(jnp.int32, (lt, 1), 0) < used
    yv = jnp.where(live, y_scr[slot], 0.0)
    hi = yv.astype(BF16)
    lo = (yv - hi.astype(F32)).astype(BF16)
    y = jnp.dot(jnp.concatenate([pick, pick], axis=1), jnp.concatenate([hi, lo], axis=0),
                preferred_element_type=F32)
    o_ref[0] = _rms(x1_ref[0] + gate_ref[0] * y, g_ref[...])


def _combine(pad_start, cnt, base, x1, y_sorted, route, gate, g_final):
    b, s, d = x1.shape
    tm = POST_ROWS
    nt = s // tm
    return pl.pallas_call(
        _combine_kernel,
        grid_spec=pltpu.PrefetchScalarGridSpec(
            num_scalar_prefetch=3, grid=(b * nt,),
            in_specs=[pl.BlockSpec((1, tm, d), lambda t, *_: (t // nt, t % nt, 0)),
                      pl.BlockSpec((1, tm, LANES), lambda t, *_: (t // nt, t % nt, 0)),
                      pl.BlockSpec((1, 1, d), lambda t, *_: (t // nt, 0, 0)),
                      pl.BlockSpec((1, d), lambda t, *_: (0, 0)),
                      pl.BlockSpec(memory_space=pl.ANY)],
            out_specs=pl.BlockSpec((1, tm, d), lambda t, *_: (t // nt, t % nt, 0)),
            scratch_shapes=[pltpu.VMEM((2, LOCAL_ROWS, d), F32), pltpu.SemaphoreType.DMA((2,))]),
        out_shape=jax.ShapeDtypeStruct((b, s, d), F32),
        compiler_params=_cparams(("arbitrary",)),
        name="combine",
    )(pad_start, cnt, base, x1, route, gate, g_final.reshape(1, d), y_sorted)


def kernel(x, c, w_ada, b_ada, g_mix, w_in, g_dil_out, g_sb_out, w_out, g_ffn,
           w_group, w_expert, w_gate, w_up, w_down, g_final):
    b, s, d = x.shape
    depth = w_ada.shape[0]
    assert s % DIL_UNIT == 0 and d == D_DIL + D_SB
    assert depth == 1, "the final rmsnorm is fused into the last layer's combine step"
    n = b * s
    ntiles = n // POST_ROWS
    bias = jnp.asarray(_dilated_bias())
    for layer in range(depth):
        mod = _ada(c, w_ada[layer], b_ada[layer])
        shift_mix, scale_mix, gate_mix, shift_ffn, scale_ffn, gate_ffn = (
            m.reshape(b, 1, d) for m in jnp.split(mod, 6, axis=-1))

        qkv_d, q_s, k_s, v_s = _premix(x, shift_mix, scale_mix, g_mix[layer], w_in[layer].astype(BF16))
        o_dil = _dilated(qkv_d, bias)
        o_sb = _stick(q_s, k_s, v_s)

        w_router = jnp.concatenate(
            [w_group[layer], w_expert[layer],
             jnp.zeros((d, LANES - N_GROUPS - N_EXPERTS), F32)], axis=1)
        x1, h2, route, cnt, base = _postmix(
            x, o_dil, o_sb, g_dil_out[layer], g_sb_out[layer], w_out[layer].astype(BF16),
            gate_mix, shift_ffn, scale_ffn, g_ffn[layer], w_router)

        bm = EXPERT_ROWS
        cnt = cnt[:, 0, ROUTE_LANE0:ROUTE_LANE0 + N_EXPERTS].astype(jnp.int32)
        base = base[:, 0, ROUTE_LANE0:ROUTE_LANE0 + N_EXPERTS].astype(jnp.int32)
        total = base[-1] + cnt[-1]
        cnt = cnt.reshape(-1)
        base = base.reshape(-1)
        padded = (total + bm - 1) // bm * bm
        pad_end = jnp.cumsum(padded)
        pad_start = (pad_end - padded).astype(jnp.int32)
        cap = -(-(2 * n + (SUBLANES - 1) * N_EXPERTS * ntiles) // bm) * bm + N_EXPERTS * bm
        n_blocks = cap // bm
        block_expert = jnp.minimum(
            jnp.sum(pad_end[None, :] <= (jnp.arange(n_blocks) * bm)[:, None], axis=1),
            N_EXPERTS - 1).astype(jnp.int32)
        live_rows = jnp.clip((pad_start + total)[block_expert] - jnp.arange(n_blocks) * bm, 0, bm).astype(jnp.int32)

        buf = _dispatch(pad_start, cnt, base, h2, route, cap)
        y_sorted = _experts(block_expert, live_rows, buf, w_gate[layer].astype(BF16),
                            w_up[layer].astype(BF16), w_down[layer].astype(BF16))
        x = _combine(pad_start, cnt, base, x1, y_sorted, route, gate_ffn, g_final)
    return x
```

```python
import numpy as np
import jax
import jax.numpy as jnp
from jax import lax
from jax.experimental import pallas as pl
from jax.experimental.pallas import tpu as pltpu

HEAD_DIM = 64
N_HEADS_DIL = 8
N_HEADS_SB = 8
D_DIL = N_HEADS_DIL * HEAD_DIM
D_SB = N_HEADS_SB * HEAD_DIM
DILATION_PATTERNS = ((128, 1), (512, 4), (2048, 16))
N_GROUPS = 4
EXPERTS_PER_GROUP = 8
N_EXPERTS = N_GROUPS * EXPERTS_PER_GROUP
NORM_EPS = 1e-6

LANES = 128
SUBLANES = 8
DIL_STEPS = 128
DIL_UNIT = 2048
DIL_TILES_PER_TRIP = 4
SB_BLOCK = 256
SB_QUERY_ROWS = 1024
SB_BLOCKS_PER_TRIP = 2
PRE_ROWS = 512
POST_ROWS = 512
LOCAL_ROWS = 2 * POST_ROWS + 256
EXPERT_ROWS = 512
ROUTE_LANE0 = N_GROUPS
VMEM_LIMIT = 56 * 1024 * 1024

F32 = jnp.float32
BF16 = jnp.bfloat16
NEG_INF = float("-inf")
LOG2E = 1.4426950408889634


def _cparams(sem):
    return pltpu.CompilerParams(dimension_semantics=sem, vmem_limit_bytes=VMEM_LIMIT)


def _rms(v, g):
    return v * lax.rsqrt(jnp.mean(v * v, axis=-1, keepdims=True) + NORM_EPS) * g


def _split3(v):
    hi = v.astype(BF16)
    r = v - hi.astype(F32)
    mid = r.astype(BF16)
    lo = (r - mid.astype(F32)).astype(BF16)
    return hi, mid, lo


def _ada_kernel(c_ref, w_ref, b_ref, o_ref):
    c = c_ref[...]
    cond = c / (1.0 + jnp.exp(-c))
    o_ref[...] = jnp.dot(cond, w_ref[...], precision=lax.Precision.HIGHEST,
                         preferred_element_type=F32) + b_ref[...]


def _ada(c, w_ada, b_ada):
    b, d = c.shape
    n = w_ada.shape[1]
    return pl.pallas_call(
        _ada_kernel,
        grid=(n // d,),
        in_specs=[pl.BlockSpec((b, d), lambda j: (0, 0)),
                  pl.BlockSpec((d, d), lambda j: (0, j)),
                  pl.BlockSpec((1, d), lambda j: (0, j))],
        out_specs=pl.BlockSpec((b, d), lambda j: (0, j)),
        out_shape=jax.ShapeDtypeStruct((b, n), F32),
        compiler_params=_cparams(("arbitrary",)),
        name="ada",
    )(c, w_ada, b_ada.reshape(1, n))


def _premix_kernel(x_ref, shift_ref, scale_ref, g_ref, w_ref, qkvd_ref, qs_ref, ks_ref, vs_ref):
    h = _rms(x_ref[0], g_ref[...]) * (1.0 + scale_ref[0]) + shift_ref[0]
    hb = h.astype(BF16)
    scale = HEAD_DIM ** -0.5 * LOG2E
    for j in range(6):
        r = jnp.dot(hb, w_ref[:, j * 512:(j + 1) * 512], preferred_element_type=F32)
        if j == 0:
            qkvd_ref[0, :, 0:512] = r * scale
        elif j < 3:
            qkvd_ref[0, :, j * 512:(j + 1) * 512] = r
        elif j == 3:
            qs_ref[0] = (r * scale).astype(BF16)
        elif j == 4:
            ks_ref[0] = r.astype(BF16)
        else:
            vs_ref[0] = r.astype(BF16)


def _premix(x, shift, scale, g_mix, w_in_bf16):
    b, s, d = x.shape
    tm = PRE_ROWS
    mod_spec = pl.BlockSpec((1, 1, d), lambda bi, i: (bi, 0, 0))
    sb_spec = pl.BlockSpec((1, tm, D_SB), lambda bi, i: (bi, i, 0))
    return pl.pallas_call(
        _premix_kernel,
        grid=(b, s // tm),
        in_specs=[pl.BlockSpec((1, tm, d), lambda bi, i: (bi, i, 0)),
                  mod_spec, mod_spec,
                  pl.BlockSpec((1, d), lambda bi, i: (0, 0)),
                  pl.BlockSpec((d, 3 * (D_DIL + D_SB)), lambda bi, i: (0, 0))],
        out_specs=[pl.BlockSpec((1, tm, 3 * D_DIL), lambda bi, i: (bi, i, 0)),
                   sb_spec, sb_spec, sb_spec],
        out_shape=[jax.ShapeDtypeStruct((b, s, 3 * D_DIL), F32),
                   jax.ShapeDtypeStruct((b, s, D_SB), BF16),
                   jax.ShapeDtypeStruct((b, s, D_SB), BF16),
                   jax.ShapeDtypeStruct((b, s, D_SB), BF16)],
        compiler_params=_cparams(("arbitrary", "arbitrary")),
        name="premix",
    )(x, shift, scale, g_mix.reshape(1, d), w_in_bf16)


def _dilated_bias():
    n = DIL_STEPS
    slopes = np.array([2.0 ** (-8.0 * (i + 1) / N_HEADS_DIL) for i in range(N_HEADS_DIL)], dtype=np.float32)
    steps = np.arange(n)[:, None] + n - np.arange(2 * n)[None, :]
    valid = (steps >= 0) & (steps <= n)
    out = []
    for _, dilation in DILATION_PATTERNS:
        bias = -slopes[:, None, None] * (steps * dilation).astype(np.float32)[None]
        out.append(np.where(valid[None], bias.astype(np.float64) * LOG2E, -np.inf).astype(np.float32))
    return np.stack(out)


def _dil_kernel(q_ref, kc_ref, kp_ref, vc_ref, vp_ref, bias_ref, o_ref,
                kext, vext, u_scr, m_scr, l_scr):
    n = DIL_STEPS
    g = pl.program_id(1)
    kext[0:DIL_UNIT, :] = kp_ref[0]
    kext[DIL_UNIT:2 * DIL_UNIT, :] = kc_ref[0]
    vext[0:DIL_UNIT, :] = vp_ref[0]
    vext[DIL_UNIT:2 * DIL_UNIT, :] = vc_ref[0]
    lane = lax.broadcasted_iota(jnp.int32, (n, LANES), 1)
    head0 = lane < HEAD_DIM
    col = lax.broadcasted_iota(jnp.int32, (n, 2 * n), 1)

    for p, (_, dil) in enumerate(DILATION_PATTERNS):
        unit = n * dil

        def tiles(it, carry, p=p, dil=dil, unit=unit):
            rows_of, vvs, deads, ss = [], [], [], []
            for t in range(DIL_TILES_PER_TRIP):
                ti = it * DIL_TILES_PER_TRIP + t
                j = ti // dil
                r = ti % dil
                qstart = j * unit + r
                kstart = DIL_UNIT + qstart - unit
                if dil == 1:
                    rows_of.append(pl.ds(qstart, n))
                    krows = pl.ds(kstart, 2 * n)
                else:
                    rows_of.append(pl.ds(qstart, n, stride=dil))
                    krows = pl.ds(kstart, 2 * n, stride=dil)
                q = q_ref[0, rows_of[t], :]
                kk = kext[krows, :].astype(BF16)
                vvs.append(vext[krows, :].astype(BF16))
                deads.append(jnp.where(jnp.logical_and(g == 0, j == 0), n, 0))
                for h in range(2):
                    qh = jnp.where(head0 if h == 0 else jnp.logical_not(head0), q, 0.0).astype(BF16)
                    ss.append(lax.dot_general(qh, kk, (((1,), (1,)), ((), ())), preferred_element_type=F32))
            ms, ls, pes = [], [], []
            for t in range(DIL_TILES_PER_TRIP):
                for h in range(2):
                    logits = jnp.where(col < deads[t], NEG_INF, ss[2 * t + h] + bias_ref[p, h])
                    m = jnp.max(logits, axis=-1, keepdims=True)
                    pe = jnp.exp2(logits - m)
                    ls.append(jnp.sum(pe, axis=-1, keepdims=True))
                    ms.append(m)
                    pes.append(pe.astype(BF16))
            us = [jnp.dot(pes[2 * t + h], vvs[t], preferred_element_type=F32)
                  for t in range(DIL_TILES_PER_TRIP) for h in range(2)]
            for t in range(DIL_TILES_PER_TRIP):
                u_scr[p, rows_of[t], :] = jnp.where(head0, us[2 * t], us[2 * t + 1])
                m_scr[p, rows_of[t], :] = jnp.where(head0, ms[2 * t], ms[2 * t + 1])
                l_scr[p, rows_of[t], :] = jnp.where(head0, ls[2 * t], ls[2 * t + 1])
            return carry

        lax.fori_loop(0, DIL_UNIT // n // DIL_TILES_PER_TRIP, tiles, 0)

    def merge(i, carry):
        rows = pl.ds(pl.multiple_of(i * n, n), n)
        m0, m1, m2 = m_scr[0, rows, :], m_scr[1, rows, :], m_scr[2, rows, :]
        mx = jnp.maximum(jnp.maximum(m0, m1), m2)
        w0, w1, w2 = jnp.exp2(m0 - mx), jnp.exp2(m1 - mx), jnp.exp2(m2 - mx)
        num = w0 * u_scr[0, rows, :] + w1 * u_scr[1, rows, :] + w2 * u_scr[2, rows, :]
        den = w0 * l_scr[0, rows, :] + w1 * l_scr[1, rows, :] + w2 * l_scr[2, rows, :]
        o_ref[0, rows, :] = num / den
        return carry

    lax.fori_loop(0, DIL_UNIT // n, merge, 0)


def _dilated(qkv_d, bias):
    b, s, _ = qkv_d.shape
    u = DIL_UNIT
    npair = D_DIL // LANES
    cur = lambda off: pl.BlockSpec((1, u, LANES), lambda bi, g, p: (bi, g, off + p))
    prev = lambda off: pl.BlockSpec((1, u, LANES), lambda bi, g, p: (bi, jnp.maximum(g - 1, 0), off + p))
    return pl.pallas_call(
        _dil_kernel,
        grid=(b, s // u, npair),
        in_specs=[cur(0), cur(npair), prev(npair), cur(2 * npair), prev(2 * npair),
                  pl.BlockSpec((3, 2, DIL_STEPS, 2 * DIL_STEPS), lambda bi, g, p: (0, p, 0, 0))],
        out_specs=pl.BlockSpec((1, u, LANES), lambda bi, g, p: (bi, g, p)),
        out_shape=jax.ShapeDtypeStruct((b, s, D_DIL), F32),
        scratch_shapes=[pltpu.VMEM((2 * u, LANES), F32), pltpu.VMEM((2 * u, LANES), F32),
                        pltpu.VMEM((3, u, LANES), F32), pltpu.VMEM((3, u, LANES), F32),
                        pltpu.VMEM((3, u, LANES), F32)],
        compiler_params=_cparams(("arbitrary", "arbitrary", "arbitrary")),
        name="dilated",
    )(qkv_d, qkv_d, qkv_d, qkv_d, qkv_d, bias)


def _stick_kernel(q_ref, k_ref, v_ref, tri_ref, o_ref,
                  qh_scr, z_scr, w_scr, acc_scr, carry_scr):
    blk = SB_BLOCK
    nsub = SB_QUERY_ROWS // blk
    assert nsub % 2 == 0
    nchain = 2 * nsub
    qi = pl.program_id(2)
    lane = lax.broadcasted_iota(jnp.int32, (blk, LANES), 1)
    head0 = lane < HEAD_DIM
    for sub in range(nsub):
        q = q_ref[0, sub * blk:(sub + 1) * blk, :]
        zero = jnp.zeros_like(q)
        qh_scr[2 * sub] = jnp.where(head0, q, zero)
        qh_scr[2 * sub + 1] = jnp.where(head0, zero, q)
    acc_scr[...] = jnp.zeros_like(acc_scr)
    carry_scr[...] = jnp.zeros_like(carry_scr)
    sign = jnp.int32(-2 ** 31)

    def rows(kb):
        return pl.ds(pl.multiple_of(kb * blk, blk), blk)

    def scores(kb, which, slot):
        kblk = k_ref[0, rows(kb), :]
        for c in which:
            z_scr[slot * nchain + c] = lax.dot_general(
                qh_scr[c], kblk, (((1,), (1,)), ((), ())), preferred_element_type=F32)

    def weights(which, slot, diag_sub, beside=None):
        causal = (lax.broadcasted_iota(jnp.int32, (blk, blk), 1)
                  < lax.broadcasted_iota(jnp.int32, (blk, blk), 0))
        splits = {}
        for c in which:
            z = z_scr[slot * nchain + c]
            neg_abs = lax.bitcast_convert_type(lax.bitcast_convert_type(z, jnp.int32) | sign, F32)
            softplus = jnp.maximum(z, 0.0) + jnp.log(1.0 + jnp.exp2(neg_abs)) * LOG2E
            if c // 2 == diag_sub:
                softplus = jnp.where(causal, softplus, 0.0)
            hi = softplus.astype(BF16)
            lo = (softplus - hi.astype(F32)).astype(BF16)
            splits[c] = jnp.concatenate([hi, lo], axis=1)
        sums = {}
        for c in which:
            if beside is not None:
                beside(c)
            sums[c] = jnp.dot(splits[c], tri_ref[...], preferred_element_type=F32)
        for c in which:
            carry = carry_scr[c]
            w = jnp.exp2((z_scr[slot * nchain + c] - sums[c]) + carry[:, 0:1])
            if c // 2 == diag_sub:
                w = jnp.where(causal, w, 0.0)
            w_scr[c] = w.astype(BF16)
            carry_scr[c] = carry - sums[c][:, 0:LANES]

    def accumulate(which, kb):
        vblk = v_ref[0, rows(kb), :]
        for c in which:
            acc_scr[c] = acc_scr[c] + jnp.dot(w_scr[c], vblk, preferred_element_type=F32)

    everyone = list(range(nchain))
    top = nsub * qi + nsub - 1
    first = nsub * qi - 1
    diag = [[c for c in everyone if c // 2 >= nsub - 1 - i] for i in range(nsub)]
    for i in range(nsub):
        scores(top - i, diag[i], 2 + i)
    scores(jnp.maximum(first, 0), everyone, 0)
    for i in range(nsub):
        def previous(c, i=i):
            if i > 0 and c in diag[i - 1]:
                accumulate([c], top - (i - 1))

        weights(diag[i], 2 + i, nsub - 1 - i, beside=previous)

    def blocks(kb0, count):
        for j in range(count):
            kb = kb0 - j

            def neighbours(c, kb=kb, j=j):
                accumulate([c], kb + 1)
                scores(jnp.maximum(kb - 1, 0), [c], 1 - j % 2)

            weights(everyone, j % 2, -1, beside=neighbours)

    per = SB_BLOCKS_PER_TRIP
    trips = nsub * qi // per

    def step(i, carry):
        blocks(first - per * i, per)
        return carry

    lax.fori_loop(0, trips, step, 0)
    for rest in range(2, per, 2):
        @pl.when(nsub * qi - trips * per == rest)
        def _(rest=rest):
            blocks(first - per * trips, rest)
    accumulate(everyone, 0)
    for sub in range(nsub):
        o_ref[0, sub * blk:(sub + 1) * blk, :] = jnp.where(head0, acc_scr[2 * sub], acc_scr[2 * sub + 1])


def _stick(q_s, k_s, v_s):
    b, s, _ = q_s.shape
    blk = SB_BLOCK
    qrows = SB_QUERY_ROWS
    nchain = 2 * qrows // blk
    tri = np.tril(np.ones((blk, blk), np.float32))
    tri2 = jnp.asarray(np.concatenate([tri, tri], axis=0), BF16)
    full = pl.BlockSpec((1, s, LANES), lambda bi, p, i: (bi, 0, p))
    return pl.pallas_call(
        _stick_kernel,
        grid=(b, D_SB // LANES, s // qrows),
        in_specs=[pl.BlockSpec((1, qrows, LANES), lambda bi, p, i: (bi, i, p)), full, full,
                  pl.BlockSpec((2 * blk, blk), lambda bi, p, i: (0, 0))],
        out_specs=pl.BlockSpec((1, qrows, LANES), lambda bi, p, i: (bi, i, p)),
        out_shape=jax.ShapeDtypeStruct((b, s, D_SB), F32),
        scratch_shapes=[pltpu.VMEM((nchain, blk, LANES), BF16),
                        pltpu.VMEM(((2 + qrows // blk) * nchain, blk, blk), F32),
                        pltpu.VMEM((nchain, blk, blk), BF16),
                        pltpu.VMEM((nchain, blk, LANES), F32),
                        pltpu.VMEM((nchain, blk, LANES), F32)],
        compiler_params=_cparams(("arbitrary", "arbitrary", "arbitrary")),
        name="stick",
    )(q_s, k_s, v_s, tri2)


def _postmix_kernel(x_ref, od_ref, os_ref, gd_ref, gs_ref, wout_ref, gate_ref, shift_ref, scale_ref,
                    gffn_ref, wr_ref, tril_ref, triu_ref,
                    x1_ref, h2_ref, route_ref, cnt_ref, base_ref, carry_scr):
    tm = POST_ROWS

    @pl.when(jnp.logical_and(pl.program_id(0) == 0, pl.program_id(1) == 0))
    def _():
        carry_scr[...] = jnp.zeros_like(carry_scr)

    mixed = jnp.concatenate([_rms(od_ref[0], gd_ref[...]), _rms(os_ref[0], gs_ref[...])], axis=-1)
    proj = jnp.dot(mixed.astype(BF16), wout_ref[...], preferred_element_type=F32)
    x1 = x_ref[0] + gate_ref[0] * proj
    x1_ref[0] = x1
    h2 = _rms(x1, gffn_ref[...]) * (1.0 + scale_ref[0]) + shift_ref[0]
    h_hi = h2.astype(BF16)
    h_lo = (h2 - h_hi.astype(F32)).astype(BF16)
    wr = wr_ref[...]
    w_hi = wr.astype(BF16)
    w_lo = (wr - w_hi.astype(F32)).astype(BF16)
    logits = jnp.dot(jnp.concatenate([h_hi, h_lo, h_hi], axis=1),
                     jnp.concatenate([w_hi, w_hi, w_lo], axis=0), preferred_element_type=F32)

    lane = lax.broadcasted_iota(jnp.int32, (tm, LANES), 1)
    big = jnp.int32(LANES)
    lmax = lambda v: jnp.max(v, axis=-1, keepdims=True)
    lmin = lambda v: jnp.min(v, axis=-1, keepdims=True)
    lsum = lambda v: jnp.sum(v, axis=-1, keepdims=True)

    gmask = lane < N_GROUPS
    gl = jnp.where(gmask, logits, NEG_INF)
    gmx = lmax(gl)
    group = lmin(jnp.where(jnp.logical_and(gmask, gl == gmx), lane, big))
    group_gate = 1.0 / lsum(jnp.exp(gl - gmx))
    lo = ROUTE_LANE0 + group * EXPERTS_PER_GROUP
    emask = jnp.logical_and(lane >= lo, lane < lo + EXPERTS_PER_GROUP)
    el = jnp.where(emask, logits, NEG_INF)
    l1 = lmax(el)
    i1 = lmin(jnp.where(el == l1, lane, big))
    el2 = jnp.where(lane == i1, NEG_INF, el)
    l2 = lmax(el2)
    i2 = lmin(jnp.where(el2 == l2, lane, big))
    r = jnp.exp(l2 - l1)
    w1 = group_gate / (1.0 + r)
    w2 = group_gate * r / (1.0 + r)

    is1 = lane == i1
    is2 = lane == i2
    oh = jnp.where(is1, 1.0, jnp.where(is2, 1.0, 0.0))
    earlier = jnp.dot(tril_ref[...], oh.astype(BF16), preferred_element_type=F32)
    runs = jnp.floor((jnp.sum(oh, axis=0, keepdims=True) + (SUBLANES - 1.0)) * (1.0 / SUBLANES))
    run_off = jnp.dot(jnp.broadcast_to(runs, (SUBLANES, LANES)).astype(BF16), triu_ref[...],
                      preferred_element_type=F32)[0:1]
    pos = earlier + run_off * SUBLANES
    slot1 = lsum(jnp.where(is1, pos, 0.0))
    slot2 = lsum(jnp.where(is2, pos, 0.0))
    cnt = runs * SUBLANES
    cnt_ref[0] = cnt
    base_ref[0] = carry_scr[...]
    carry_scr[...] = carry_scr[...] + cnt

    h2_ref[0] = h2.astype(BF16)
    route_ref[0] = jnp.where(lane == 0, slot1, jnp.where(lane == 1, slot2,
                                                         jnp.where(lane == 2, w1, jnp.where(lane == 3, w2, 0.0))))


def _postmix(x, o_dil, o_sb, g_dil, g_sb, w_out_bf16, gate, shift, scale, g_ffn, w_router):
    b, s, d = x.shape
    tm = POST_ROWS
    nt = s // tm
    tril = jnp.asarray(np.tril(np.ones((tm, tm), np.float32), -1), BF16)
    triu = jnp.asarray(np.triu(np.ones((LANES, LANES), np.float32), 1), BF16)
    row = lambda w: pl.BlockSpec((1, tm, w), lambda bi, i: (bi, i, 0))
    vec = lambda w: pl.BlockSpec((1, w), lambda bi, i: (0, 0))
    mod_spec = pl.BlockSpec((1, 1, d), lambda bi, i: (bi, 0, 0))
    tile_vec = pl.BlockSpec((1, 1, LANES), lambda bi, i: (bi * nt + i, 0, 0))
    return pl.pallas_call(
        _postmix_kernel,
        grid=(b, nt),
        in_specs=[row(d), row(D_DIL), row(D_SB), vec(D_DIL), vec(D_SB),
                  pl.BlockSpec((d, d), lambda bi, i: (0, 0)),
                  mod_spec, mod_spec, mod_spec, vec(d),
                  pl.BlockSpec((d, LANES), lambda bi, i: (0, 0)),
                  pl.BlockSpec((tm, tm), lambda bi, i: (0, 0)),
                  pl.BlockSpec((LANES, LANES), lambda bi, i: (0, 0))],
        out_specs=[row(d), row(d), row(LANES), tile_vec, tile_vec],
        out_shape=[jax.ShapeDtypeStruct((b, s, d), F32),
                   jax.ShapeDtypeStruct((b, s, d), BF16),
                   jax.ShapeDtypeStruct((b, s, LANES), F32),
                   jax.ShapeDtypeStruct((b * nt, 1, LANES), F32),
                   jax.ShapeDtypeStruct((b * nt, 1, LANES), F32)],
        scratch_shapes=[pltpu.VMEM((1, LANES), F32)],
        compiler_params=_cparams(("arbitrary", "arbitrary")),
        name="postmix",
    )(x, o_dil, o_sb, g_dil.reshape(1, -1), g_sb.reshape(1, -1), w_out_bf16, gate, shift, scale,
      g_ffn.reshape(1, d), w_router, tril, triu)


def _for_each_run_piece(tile, start_ref, cnt_ref, base_ref, fn):
    def body(e, off):
        c = cnt_ref[tile * N_EXPERTS + e]
        sorted0 = start_ref[e] + base_ref[tile * N_EXPERTS + e]
        for k in range(3, 10):
            p = 1 << k

            @pl.when((c & p) != 0)
            def _(p=p):
                done = c - (c & (2 * p - 1))
                fn(pl.multiple_of(off + done, SUBLANES), pl.multiple_of(sorted0 + done, SUBLANES), p)
        return off + c

    return lax.fori_loop(0, N_EXPERTS, body, 0)


def _sort_kernel(start_ref, cnt_ref, base_ref, h2_ref, route_ref, buf_ref, xs_scr, sem):
    tm = POST_ROWS
    lt = LOCAL_ROWS
    d = h2_ref.shape[2]
    tile = pl.program_id(0)
    slot = tile % 2

    def piece(slot, lrow, srow, rows):
        return pltpu.make_async_copy(xs_scr.at[slot, pl.ds(lrow, rows)], buf_ref.at[pl.ds(srow, rows)],
                                     sem.at[slot])

    def drain(tile, slot):
        _for_each_run_piece(tile, start_ref, cnt_ref, base_ref, lambda *a: piece(slot, *a).wait())

    @pl.when(tile >= 2)
    def _():
        drain(tile - 2, slot)

    lane = lax.broadcasted_iota(jnp.int32, (tm, LANES), 1)
    route = route_ref[0]
    w1 = jnp.sum(jnp.where(lane == 2, route, 0.0), axis=-1, keepdims=True)
    w2 = jnp.sum(jnp.where(lane == 3, route, 0.0), axis=-1, keepdims=True)

    def pieces(w):
        hi, mid, lw = _split3(w)
        return jnp.where(lane == 0, hi.astype(F32),
                         jnp.where(lane == 1, mid.astype(F32),
                                   jnp.where(lane == 2, lw.astype(F32), 0.0))).astype(BF16)

    route_t = route.T
    s1 = route_t[0:1, :].astype(jnp.int32)
    s2 = route_t[1:2, :].astype(jnp.int32)
    row = lax.broadcasted_iota(jnp.int32, (lt, tm), 0)
    p1 = jnp.where(row == s1, 1.0, 0.0)
    p2 = jnp.where(row == s2, 1.0, 0.0)
    xs_scr[slot, :, 0:d] = jnp.dot((p1 + p2).astype(BF16), h2_ref[0], preferred_element_type=F32)
    xs_scr[slot, :, d:] = (jnp.dot(p1.astype(BF16), pieces(w1), preferred_element_type=F32)
                           + jnp.dot(p2.astype(BF16), pieces(w2), preferred_element_type=F32))
    _for_each_run_piece(tile, start_ref, cnt_ref, base_ref, lambda *a: piece(slot, *a).start())

    last = pl.num_programs(0) - 1

    @pl.when(jnp.logical_and(tile == last, tile >= 1))
    def _():
        drain(tile - 1, 1 - slot)

    @pl.when(tile == last)
    def _():
        drain(tile, slot)


def _dispatch(pad_start, cnt, base, h2, route, cap):
    b, s, d = h2.shape
    tm = POST_ROWS
    nt = s // tm
    return pl.pallas_call(
        _sort_kernel,
        grid_spec=pltpu.PrefetchScalarGridSpec(
            num_scalar_prefetch=3, grid=(b * nt,),
            in_specs=[pl.BlockSpec((1, tm, d), lambda t, *_: (t // nt, t % nt, 0)),
                      pl.BlockSpec((1, tm, LANES), lambda t, *_: (t // nt, t % nt, 0))],
            out_specs=pl.BlockSpec(memory_space=pl.ANY),
            scratch_shapes=[pltpu.VMEM((2, LOCAL_ROWS, d + LANES), F32), pltpu.SemaphoreType.DMA((2,))]),
        out_shape=jax.ShapeDtypeStruct((cap, d + LANES), F32),
        compiler_params=_cparams(("arbitrary",)),
        name="dispatch",
    )(pad_start, cnt, base, h2, route)


def _expert_kernel(be_ref, live_ref, x_ref, wg_ref, wu_ref, wd_ref, y_ref):
    del be_ref
    d = y_ref.shape[1]
    live = live_ref[pl.program_id(0)]

    @pl.when(live > 0)
    def _():
        keep = lax.broadcasted_iota(jnp.int32, (x_ref.shape[0], 1), 0) < live
        x = jnp.where(keep, x_ref[...], 0.0)
        xb = x[:, 0:d].astype(BF16)
        weight = jnp.sum(x[:, d:], axis=-1, keepdims=True)
        gate = jnp.dot(xb, wg_ref[0], preferred_element_type=F32)
        up = jnp.dot(xb, wu_ref[0], preferred_element_type=F32)
        act = gate / (1.0 + jnp.exp(-gate)) * up
        y_ref[...] = jnp.dot(act.astype(BF16), wd_ref[0], preferred_element_type=F32) * weight

    @pl.when(live == 0)
    def _():
        y_ref[...] = jnp.zeros_like(y_ref)


def _experts(block_expert, live_rows, buf, wg, wu, wd):
    cap, dw = buf.shape
    d, f = wg.shape[1], wg.shape[2]
    bm = EXPERT_ROWS
    return pl.pallas_call(
        _expert_kernel,
        grid_spec=pltpu.PrefetchScalarGridSpec(
            num_scalar_prefetch=2, grid=(cap // bm,),
            in_specs=[pl.BlockSpec((bm, dw), lambda i, be, nu: (i, 0)),
                      pl.BlockSpec((1, d, f), lambda i, be, nu: (be[i], 0, 0)),
                      pl.BlockSpec((1, d, f), lambda i, be, nu: (be[i], 0, 0)),
                      pl.BlockSpec((1, f, d), lambda i, be, nu: (be[i], 0, 0))],
            out_specs=pl.BlockSpec((bm, d), lambda i, be, nu: (i, 0))),
        out_shape=jax.ShapeDtypeStruct((cap, d), F32),
        compiler_params=_cparams(("arbitrary",)),
        name="experts",
    )(block_expert, live_rows, buf, wg, wu, wd)


def _combine_kernel(start_ref, cnt_ref, base_ref, x1_ref, route_ref, gate_ref, g_ref, y_hbm_ref, o_ref,
                    y_scr, sem):
    tm = POST_ROWS
    lt = LOCAL_ROWS
    tile = pl.program_id(0)
    slot = tile % 2

    def piece(slot, lrow, srow, rows):
        return pltpu.make_async_copy(y_hbm_ref.at[pl.ds(srow, rows)], y_scr.at[slot, pl.ds(lrow, rows)],
                                     sem.at[slot])

    def fetch(tile, slot):
        _for_each_run_piece(tile, start_ref, cnt_ref, base_ref, lambda *a: piece(slot, *a).start())

    @pl.when(tile == 0)
    def _():
        fetch(tile, slot)

    @pl.when(tile + 1 < pl.num_programs(0))
    def _():
        fetch(tile + 1, 1 - slot)

    lane = lax.broadcasted_iota(jnp.int32, (tm, LANES), 1)
    route = route_ref[0]
    s1 = jnp.sum(jnp.where(lane == 0, route, 0.0), axis=-1, keepdims=True).astype(jnp.int32)
    s2 = jnp.sum(jnp.where(lane == 1, route, 0.0), axis=-1, keepdims=True).astype(jnp.int32)
    col = lax.broadcasted_iota(jnp.int32, (tm, lt), 1)
    pick = jnp.where(col == s1, 1.0, jnp.where(col == s2, 1.0, 0.0)).astype(BF16)
    used = _for_each_run_piece(tile, start_ref, cnt_ref, base_ref, lambda *a: piece(slot, *a).wait())
    live = lax.broadcasted_iota(jnp.int32, (lt, 1), 0) < used
    yv = jnp.where(live, y_scr[slot], 0.0)
    hi = yv.astype(BF16)
    lo = (yv - hi.astype(F32)).astype(BF16)
    y = jnp.dot(jnp.concatenate([pick, pick], axis=1), jnp.concatenate([hi, lo], axis=0),
                preferred_element_type=F32)
    o_ref[0] = _rms(x1_ref[0] + gate_ref[0] * y, g_ref[...])


def _combine(pad_start, cnt, base, x1, y_sorted, route, gate, g_final):
    b, s, d = x1.shape
    tm = POST_ROWS
    nt = s // tm
    return pl.pallas_call(
        _combine_kernel,
        grid_spec=pltpu.PrefetchScalarGridSpec(
            num_scalar_prefetch=3, grid=(b * nt,),
            in_specs=[pl.BlockSpec((1, tm, d), lambda t, *_: (t // nt, t % nt, 0)),
                      pl.BlockSpec((1, tm, LANES), lambda t, *_: (t // nt, t % nt, 0)),
                      pl.BlockSpec((1, 1, d), lambda t, *_: (t // nt, 0, 0)),
                      pl.BlockSpec((1, d), lambda t, *_: (0, 0)),
                      pl.BlockSpec(memory_space=pl.ANY)],
            out_specs=pl.BlockSpec((1, tm, d), lambda t, *_: (t // nt, t % nt, 0)),
            scratch_shapes=[pltpu.VMEM((2, LOCAL_ROWS, d), F32), pltpu.SemaphoreType.DMA((2,))]),
        out_shape=jax.ShapeDtypeStruct((b, s, d), F32),
        compiler_params=_cparams(("arbitrary",)),
        name="combine",
    )(pad_start, cnt, base, x1, route, gate, g_final.reshape(1, d), y_sorted)


def kernel(x, c, w_ada, b_ada, g_mix, w_in, g_dil_out, g_sb_out, w_out, g_ffn,
           w_group, w_expert, w_gate, w_up, w_down, g_final):
    b, s, d = x.shape
    depth = w_ada.shape[0]
    assert s % DIL_UNIT == 0 and d == D_DIL + D_SB
    assert depth == 1, "the final rmsnorm is fused into the last layer's combine step"
    n = b * s
    ntiles = n // POST_ROWS
    bias = jnp.asarray(_dilated_bias())
    for layer in range(depth):
        mod = _ada(c, w_ada[layer], b_ada[layer])
        shift_mix, scale_mix, gate_mix, shift_ffn, scale_ffn, gate_ffn = (
            m.reshape(b, 1, d) for m in jnp.split(mod, 6, axis=-1))

        qkv_d, q_s, k_s, v_s = _premix(x, shift_mix, scale_mix, g_mix[layer], w_in[layer].astype(BF16))
        o_dil = _dilated(qkv_d, bias)
        o_sb = _stick(q_s, k_s, v_s)

        w_router = jnp.concatenate(
            [w_group[layer], w_expert[layer],
             jnp.zeros((d, LANES - N_GROUPS - N_EXPERTS), F32)], axis=1)
        x1, h2, route, cnt, base = _postmix(
            x, o_dil, o_sb, g_dil_out[layer], g_sb_out[layer], w_out[layer].astype(BF16),
            gate_mix, shift_ffn, scale_ffn, g_ffn[layer], w_router)

        bm = EXPERT_ROWS
        cnt = cnt[:, 0, ROUTE_LANE0:ROUTE_LANE0 + N_EXPERTS].astype(jnp.int32)
        base = base[:, 0, ROUTE_LANE0:ROUTE_LANE0 + N_EXPERTS].astype(jnp.int32)
        total = base[-1] + cnt[-1]
        cnt = cnt.reshape(-1)
        base = base.reshape(-1)
        padded = (total + bm - 1) // bm * bm
        pad_end = jnp.cumsum(padded)
        pad_start = (pad_end - padded).astype(jnp.int32)
        cap = -(-(2 * n + (SUBLANES - 1) * N_EXPERTS * ntiles) // bm) * bm + N_EXPERTS * bm
        n_blocks = cap // bm
        block_expert = jnp.minimum(
            jnp.sum(pad_end[None, :] <= (jnp.arange(n_blocks) * bm)[:, None], axis=1),
            N_EXPERTS - 1).astype(jnp.int32)
        live_rows = jnp.clip((pad_start + total)[block_expert] - jnp.arange(n_blocks) * bm, 0, bm).astype(jnp.int32)

        buf = _dispatch(pad_start, cnt, base, h2, route, cap)
        y_sorted = _experts(block_expert, live_rows, buf, w_gate[layer].astype(BF16),
                            w_up[layer].astype(BF16), w_down[layer].astype(BF16))
        x = _combine(pad_start, cnt, base, x1, y_sorted, route, gate_ffn, g_final)
    return x
```

```python
import numpy as np
import jax
import jax.numpy as jnp
from jax import lax
from jax.experimental import pallas as pl
from jax.experimental.pallas import tpu as pltpu

HEAD_DIM = 64
N_HEADS_DIL = 8
N_HEADS_SB = 8
D_DIL = N_HEADS_DIL * HEAD_DIM
D_SB = N_HEADS_SB * HEAD_DIM
DILATION_PATTERNS = ((128, 1), (512, 4), (2048, 16))
N_GROUPS = 4
EXPERTS_PER_GROUP = 8
N_EXPERTS = N_GROUPS * EXPERTS_PER_GROUP
NORM_EPS = 1e-6

LANES = 128
SUBLANES = 8
DIL_STEPS = 128
DIL_UNIT = 2048
DIL_TILES_PER_TRIP = 8
SB_BLOCK = 256
SB_QUERY_ROWS = 1024
SB_BLOCKS_PER_TRIP = 2
PRE_ROWS = 512
POST_ROWS = 512
POST_ROW_GROUPS = 2
LOCAL_ROWS = 2 * POST_ROWS + 256
EXPERT_ROWS = 512
ROUTE_LANE0 = N_GROUPS
VMEM_LIMIT = 56 * 1024 * 1024

F32 = jnp.float32
BF16 = jnp.bfloat16
NEG_INF = float("-inf")
LOG2E = 1.4426950408889634


def _cparams(sem):
    return pltpu.CompilerParams(dimension_semantics=sem, vmem_limit_bytes=VMEM_LIMIT)


def _rms(v, g):
    return v * lax.rsqrt(jnp.mean(v * v, axis=-1, keepdims=True) + NORM_EPS) * g


def _split3(v):
    hi = v.astype(BF16)
    r = v - hi.astype(F32)
    mid = r.astype(BF16)
    lo = (r - mid.astype(F32)).astype(BF16)
    return hi, mid, lo


def _ada_kernel(c_ref, w_ref, b_ref, o_ref):
    c = c_ref[...]
    cond = c / (1.0 + jnp.exp(-c))
    o_ref[...] = jnp.dot(cond, w_ref[...], precision=lax.Precision.HIGHEST,
                         preferred_element_type=F32) + b_ref[...]


def _ada(c, w_ada, b_ada):
    b, d = c.shape
    n = w_ada.shape[1]
    return pl.pallas_call(
        _ada_kernel,
        grid=(n // d,),
        in_specs=[pl.BlockSpec((b, d), lambda j: (0, 0)),
                  pl.BlockSpec((d, d), lambda j: (0, j)),
                  pl.BlockSpec((1, d), lambda j: (0, j))],
        out_specs=pl.BlockSpec((b, d), lambda j: (0, j)),
        out_shape=jax.ShapeDtypeStruct((b, n), F32),
        compiler_params=_cparams(("arbitrary",)),
        name="ada",
    )(c, w_ada, b_ada.reshape(1, n))


def _premix_kernel(x_ref, shift_ref, scale_ref, g_ref, w_ref, qkvd_ref, qs_ref, ks_ref, vs_ref):
    h = _rms(x_ref[0], g_ref[...]) * (1.0 + scale_ref[0]) + shift_ref[0]
    hb = h.astype(BF16)
    scale = HEAD_DIM ** -0.5 * LOG2E
    for j in range(6):
        r = jnp.dot(hb, w_ref[:, j * 512:(j + 1) * 512], preferred_element_type=F32)
        if j == 0:
            qkvd_ref[0, :, 0:512] = r * scale
        elif j < 3:
            qkvd_ref[0, :, j * 512:(j + 1) * 512] = r
        elif j == 3:
            qs_ref[0] = (r * scale).astype(BF16)
        elif j == 4:
            ks_ref[0] = r.astype(BF16)
        else:
            vs_ref[0] = r.astype(BF16)


def _premix(x, shift, scale, g_mix, w_in_bf16):
    b, s, d = x.shape
    tm = PRE_ROWS
    mod_spec = pl.BlockSpec((1, 1, d), lambda bi, i: (bi, 0, 0))
    sb_spec = pl.BlockSpec((1, tm, D_SB), lambda bi, i: (bi, i, 0))
    return pl.pallas_call(
        _premix_kernel,
        grid=(b, s // tm),
        in_specs=[pl.BlockSpec((1, tm, d), lambda bi, i: (bi, i, 0)),
                  mod_spec, mod_spec,
                  pl.BlockSpec((1, d), lambda bi, i: (0, 0)),
                  pl.BlockSpec((d, 3 * (D_DIL + D_SB)), lambda bi, i: (0, 0))],
        out_specs=[pl.BlockSpec((1, tm, 3 * D_DIL), lambda bi, i: (bi, i, 0)),
                   sb_spec, sb_spec, sb_spec],
        out_shape=[jax.ShapeDtypeStruct((b, s, 3 * D_DIL), F32),
                   jax.ShapeDtypeStruct((b, s, D_SB), BF16),
                   jax.ShapeDtypeStruct((b, s, D_SB), BF16),
                   jax.ShapeDtypeStruct((b, s, D_SB), BF16)],
        compiler_params=_cparams(("arbitrary", "arbitrary")),
        name="premix",
    )(x, shift, scale, g_mix.reshape(1, d), w_in_bf16)


def _dilated_bias():
    n = DIL_STEPS
    slopes = np.array([2.0 ** (-8.0 * (i + 1) / N_HEADS_DIL) for i in range(N_HEADS_DIL)], dtype=np.float32)
    steps = np.arange(n)[:, None] + n - np.arange(2 * n)[None, :]
    valid = (steps >= 0) & (steps <= n)
    out = []
    for _, dilation in DILATION_PATTERNS:
        bias = -slopes[:, None, None] * (steps * dilation).astype(np.float32)[None]
        out.append(np.where(valid[None], bias.astype(np.float64) * LOG2E, -np.inf).astype(np.float32))
    return np.stack(out)


def _dil_kernel(q_ref, kc_ref, kp_ref, vc_ref, vp_ref, bias_ref, o_ref,
                kext, vext, u_scr, m_scr, l_scr):
    n = DIL_STEPS
    g = pl.program_id(1)
    kext[0:DIL_UNIT, :] = kp_ref[0]
    kext[DIL_UNIT:2 * DIL_UNIT, :] = kc_ref[0]
    vext[0:DIL_UNIT, :] = vp_ref[0]
    vext[DIL_UNIT:2 * DIL_UNIT, :] = vc_ref[0]
    lane = lax.broadcasted_iota(jnp.int32, (n, LANES), 1)
    head0 = lane < HEAD_DIM
    col = lax.broadcasted_iota(jnp.int32, (n, 2 * n), 1)

    for p, (_, dil) in enumerate(DILATION_PATTERNS):
        unit = n * dil

        def tiles(it, carry, p=p, dil=dil, unit=unit):
            rows_of, vvs, deads, ss = [], [], [], []
            for t in range(DIL_TILES_PER_TRIP):
                ti = it * DIL_TILES_PER_TRIP + t
                j = ti // dil
                r = ti % dil
                qstart = j * unit + r
                kstart = DIL_UNIT + qstart - unit
                if dil == 1:
                    rows_of.append(pl.ds(qstart, n))
                    krows = pl.ds(kstart, 2 * n)
                else:
                    rows_of.append(pl.ds(qstart, n, stride=dil))
                    krows = pl.ds(kstart, 2 * n, stride=dil)
                q = q_ref[0, rows_of[t], :]
                kk = kext[krows, :].astype(BF16)
                vvs.append(vext[krows, :].astype(BF16))
                deads.append(jnp.where(jnp.logical_and(g == 0, j == 0), n, 0))
                for h in range(2):
                    qh = jnp.where(head0 if h == 0 else jnp.logical_not(head0), q, 0.0).astype(BF16)
                    ss.append(lax.dot_general(qh, kk, (((1,), (1,)), ((), ())), preferred_element_type=F32))
            ms, ls, pes = [], [], []
            for t in range(DIL_TILES_PER_TRIP):
                for h in range(2):
                    logits = jnp.where(col < deads[t], NEG_INF, ss[2 * t + h] + bias_ref[p, h])
                    m = jnp.max(logits, axis=-1, keepdims=True)
                    pe = jnp.exp2(logits - m)
                    ls.append(jnp.sum(pe, axis=-1, keepdims=True))
                    ms.append(m)
                    pes.append(pe.astype(BF16))
            us = [jnp.dot(pes[2 * t + h], vvs[t], preferred_element_type=F32)
                  for t in range(DIL_TILES_PER_TRIP) for h in range(2)]
            for t in range(DIL_TILES_PER_TRIP):
                u_scr[p, rows_of[t], :] = jnp.where(head0, us[2 * t], us[2 * t + 1])
                m_scr[p, rows_of[t], :] = jnp.where(head0, ms[2 * t], ms[2 * t + 1])
                l_scr[p, rows_of[t], :] = jnp.where(head0, ls[2 * t], ls[2 * t + 1])
            return carry

        lax.fori_loop(0, DIL_UNIT // n // DIL_TILES_PER_TRIP, tiles, 0)

    def merge(i, carry):
        rows = pl.ds(pl.multiple_of(i * n, n), n)
        m0, m1, m2 = m_scr[0, rows, :], m_scr[1, rows, :], m_scr[2, rows, :]
        mx = jnp.maximum(jnp.maximum(m0, m1), m2)
        w0, w1, w2 = jnp.exp2(m0 - mx), jnp.exp2(m1 - mx), jnp.exp2(m2 - mx)
        num = w0 * u_scr[0, rows, :] + w1 * u_scr[1, rows, :] + w2 * u_scr[2, rows, :]
        den = w0 * l_scr[0, rows, :] + w1 * l_scr[1, rows, :] + w2 * l_scr[2, rows, :]
        o_ref[0, rows, :] = num / den
        return carry

    lax.fori_loop(0, DIL_UNIT // n, merge, 0)


def _dilated(qkv_d, bias):
    b, s, _ = qkv_d.shape
    u = DIL_UNIT
    npair = D_DIL // LANES
    cur = lambda off: pl.BlockSpec((1, u, LANES), lambda bi, g, p: (bi, g, off + p))
    prev = lambda off: pl.BlockSpec((1, u, LANES), lambda bi, g, p: (bi, jnp.maximum(g - 1, 0), off + p))
    return pl.pallas_call(
        _dil_kernel,
        grid=(b, s // u, npair),
        in_specs=[cur(0), cur(npair), prev(npair), cur(2 * npair), prev(2 * npair),
                  pl.BlockSpec((3, 2, DIL_STEPS, 2 * DIL_STEPS), lambda bi, g, p: (0, p, 0, 0))],
        out_specs=pl.BlockSpec((1, u, LANES), lambda bi, g, p: (bi, g, p)),
        out_shape=jax.ShapeDtypeStruct((b, s, D_DIL), F32),
        scratch_shapes=[pltpu.VMEM((2 * u, LANES), F32), pltpu.VMEM((2 * u, LANES), F32),
                        pltpu.VMEM((3, u, LANES), F32), pltpu.VMEM((3, u, LANES), F32),
                        pltpu.VMEM((3, u, LANES), F32)],
        compiler_params=_cparams(("arbitrary", "arbitrary", "arbitrary")),
        name="dilated",
    )(qkv_d, qkv_d, qkv_d, qkv_d, qkv_d, bias)


def _stick_kernel(q_ref, k_ref, v_ref, tri_ref, o_ref,
                  qh_scr, z_scr, w_scr, acc_scr, carry_scr):
    blk = SB_BLOCK
    nsub = SB_QUERY_ROWS // blk
    assert nsub % 2 == 0
    nchain = 2 * nsub
    qi = pl.program_id(2)
    lane = lax.broadcasted_iota(jnp.int32, (blk, LANES), 1)
    head0 = lane < HEAD_DIM
    for sub in range(nsub):
        q = q_ref[0, sub * blk:(sub + 1) * blk, :]
        zero = jnp.zeros_like(q)
        qh_scr[2 * sub] = jnp.where(head0, q, zero)
        qh_scr[2 * sub + 1] = jnp.where(head0, zero, q)
    acc_scr[...] = jnp.zeros_like(acc_scr)
    carry_scr[...] = jnp.zeros_like(carry_scr)
    sign = jnp.int32(-2 ** 31)

    def rows(kb):
        return pl.ds(pl.multiple_of(kb * blk, blk), blk)

    def scores(kb, which, slot):
        kblk = k_ref[0, rows(kb), :]
        for c in which:
            z_scr[slot * nchain + c] = lax.dot_general(
                qh_scr[c], kblk, (((1,), (1,)), ((), ())), preferred_element_type=F32)

    def weights(which, slot, diag_sub, beside=None):
        causal = (lax.broadcasted_iota(jnp.int32, (blk, blk), 1)
                  < lax.broadcasted_iota(jnp.int32, (blk, blk), 0))
        splits = {}
        for c in which:
            z = z_scr[slot * nchain + c]
            neg_abs = lax.bitcast_convert_type(lax.bitcast_convert_type(z, jnp.int32) | sign, F32)
            softplus = jnp.maximum(z, 0.0) + jnp.log(1.0 + jnp.exp2(neg_abs)) * LOG2E
            if c // 2 == diag_sub:
                softplus = jnp.where(causal, softplus, 0.0)
            hi = softplus.astype(BF16)
            lo = (softplus - hi.astype(F32)).astype(BF16)
            splits[c] = jnp.concatenate([hi, lo], axis=1)
        sums = {}
        for c in which:
            if beside is not None:
                beside(c)
            sums[c] = jnp.dot(splits[c], tri_ref[...], preferred_element_type=F32)
        for c in which:
            carry = carry_scr[c]
            w = jnp.exp2((z_scr[slot * nchain + c] - sums[c]) + carry[:, 0:1])
            if c // 2 == diag_sub:
                w = jnp.where(causal, w, 0.0)
            w_scr[c] = w.astype(BF16)
            carry_scr[c] = carry - sums[c][:, 0:LANES]

    def accumulate(which, kb):
        vblk = v_ref[0, rows(kb), :]
        for c in which:
            acc_scr[c] = acc_scr[c] + jnp.dot(w_scr[c], vblk, preferred_element_type=F32)

    everyone = list(range(nchain))
    top = nsub * qi + nsub - 1
    first = nsub * qi - 1
    diag = [[c for c in everyone if c // 2 >= nsub - 1 - i] for i in range(nsub)]
    for i in range(nsub):
        scores(top - i, diag[i], 2 + i)
    scores(jnp.maximum(first, 0), everyone, 0)
    for i in range(nsub):
        def previous(c, i=i):
            if i > 0 and c in diag[i - 1]:
                accumulate([c], top - (i - 1))

        weights(diag[i], 2 + i, nsub - 1 - i, beside=previous)

    def blocks(kb0, count):
        for j in range(count):
            kb = kb0 - j

            def neighbours(c, kb=kb, j=j):
                accumulate([c], kb + 1)
                scores(jnp.maximum(kb - 1, 0), [c], 1 - j % 2)

            weights(everyone, j % 2, -1, beside=neighbours)

    per = SB_BLOCKS_PER_TRIP
    trips = nsub * qi // per

    def step(i, carry):
        blocks(first - per * i, per)
        return carry

    lax.fori_loop(0, trips, step, 0)
    for rest in range(2, per, 2):
        @pl.when(nsub * qi - trips * per == rest)
        def _(rest=rest):
            blocks(first - per * trips, rest)
    accumulate(everyone, 0)
    for sub in range(nsub):
        o_ref[0, sub * blk:(sub + 1) * blk, :] = jnp.where(head0, acc_scr[2 * sub], acc_scr[2 * sub + 1])


def _stick(q_s, k_s, v_s):
    b, s, _ = q_s.shape
    blk = SB_BLOCK
    qrows = SB_QUERY_ROWS
    nchain = 2 * qrows // blk
    tri = np.tril(np.ones((blk, blk), np.float32))
    tri2 = jnp.asarray(np.concatenate([tri, tri], axis=0), BF16)
    full = pl.BlockSpec((1, s, LANES), lambda bi, p, i: (bi, 0, p))
    return pl.pallas_call(
        _stick_kernel,
        grid=(b, D_SB // LANES, s // qrows),
        in_specs=[pl.BlockSpec((1, qrows, LANES), lambda bi, p, i: (bi, i, p)), full, full,
                  pl.BlockSpec((2 * blk, blk), lambda bi, p, i: (0, 0))],
        out_specs=pl.BlockSpec((1, qrows, LANES), lambda bi, p, i: (bi, i, p)),
        out_shape=jax.ShapeDtypeStruct((b, s, D_SB), F32),
        scratch_shapes=[pltpu.VMEM((nchain, blk, LANES), BF16),
                        pltpu.VMEM(((2 + qrows // blk) * nchain, blk, blk), F32),
                        pltpu.VMEM((nchain, blk, blk), BF16),
                        pltpu.VMEM((nchain, blk, LANES), F32),
                        pltpu.VMEM((nchain, blk, LANES), F32)],
        compiler_params=_cparams(("arbitrary", "arbitrary", "arbitrary")),
        name="stick",
    )(q_s, k_s, v_s, tri2)


def _postmix_kernel(x_ref, od_ref, os_ref, gd_ref, gs_ref, wout_ref, gate_ref, shift_ref, scale_ref,
                    gffn_ref, wr_ref, tril_ref, triu_ref,
                    x1_ref, h2_ref, route_ref, cnt_ref, base_ref, carry_scr):
    tm = POST_ROWS

    @pl.when(jnp.logical_and(pl.program_id(0) == 0, pl.program_id(1) == 0))
    def _():
        carry_scr[...] = jnp.zeros_like(carry_scr)

    big = jnp.int32(LANES)
    lmax = lambda v: jnp.max(v, axis=-1, keepdims=True)
    lmin = lambda v: jnp.min(v, axis=-1, keepdims=True)
    lsum = lambda v: jnp.sum(v, axis=-1, keepdims=True)
    wr = wr_ref[...]
    w_hi = wr.astype(BF16)
    w_lo = (wr - w_hi.astype(F32)).astype(BF16)
    w3 = jnp.concatenate([w_hi, w_hi, w_lo], axis=0)

    def route_rows(rows):
        n = rows.stop - rows.start
        mixed = jnp.concatenate([_rms(od_ref[0, rows, :], gd_ref[...]), _rms(os_ref[0, rows, :], gs_ref[...])],
                                axis=-1)
        proj = jnp.dot(mixed.astype(BF16), wout_ref[...], preferred_element_type=F32)
        x1 = x_ref[0, rows, :] + gate_ref[0] * proj
        x1_ref[0, rows, :] = x1
        h2 = _rms(x1, gffn_ref[...]) * (1.0 + scale_ref[0]) + shift_ref[0]
        h_hi = h2.astype(BF16)
        h_lo = (h2 - h_hi.astype(F32)).astype(BF16)
        h2_ref[0, rows, :] = h_hi
        logits = jnp.dot(jnp.concatenate([h_hi, h_lo, h_hi], axis=1), w3, preferred_element_type=F32)

        lane = lax.broadcasted_iota(jnp.int32, (n, LANES), 1)
        gmask = lane < N_GROUPS
        gl = jnp.where(gmask, logits, NEG_INF)
        gmx = lmax(gl)
        group = lmin(jnp.where(jnp.logical_and(gmask, gl == gmx), lane, big))
        group_gate = 1.0 / lsum(jnp.exp(gl - gmx))
        lo = ROUTE_LANE0 + group * EXPERTS_PER_GROUP
        emask = jnp.logical_and(lane >= lo, lane < lo + EXPERTS_PER_GROUP)
        el = jnp.where(emask, logits, NEG_INF)
        l1 = lmax(el)
        i1 = lmin(jnp.where(el == l1, lane, big))
        el2 = jnp.where(lane == i1, NEG_INF, el)
        l2 = lmax(el2)
        i2 = lmin(jnp.where(el2 == l2, lane, big))
        r = jnp.exp(l2 - l1)
        return i1, i2, group_gate / (1.0 + r), group_gate * r / (1.0 + r)

    hm = tm // POST_ROW_GROUPS
    parts = [route_rows(slice(h * hm, (h + 1) * hm)) for h in range(POST_ROW_GROUPS)]
    i1, i2, w1, w2 = (jnp.concatenate([p[k] for p in parts], axis=0) for k in range(4))
    lane = lax.broadcasted_iota(jnp.int32, (tm, LANES), 1)

    is1 = lane == i1
    is2 = lane == i2
    oh = jnp.where(is1, 1.0, jnp.where(is2, 1.0, 0.0))
    earlier = jnp.dot(tril_ref[...], oh.astype(BF16), preferred_element_type=F32)
    runs = jnp.floor((jnp.sum(oh, axis=0, keepdims=True) + (SUBLANES - 1.0)) * (1.0 / SUBLANES))
    run_off = jnp.dot(jnp.broadcast_to(runs, (SUBLANES, LANES)).astype(BF16), triu_ref[...],
                      preferred_element_type=F32)[0:1]
    pos = earlier + run_off * SUBLANES
    slot1 = lsum(jnp.where(is1, pos, 0.0))
    slot2 = lsum(jnp.where(is2, pos, 0.0))
    cnt = runs * SUBLANES
    cnt_ref[0] = cnt
    base_ref[0] = carry_scr[...]
    carry_scr[...] = carry_scr[...] + cnt

    route_ref[0] = jnp.where(lane == 0, slot1, jnp.where(lane == 1, slot2,
                                                         jnp.where(lane == 2, w1, jnp.where(lane == 3, w2, 0.0))))


def _postmix(x, o_dil, o_sb, g_dil, g_sb, w_out_bf16, gate, shift, scale, g_ffn, w_router):
    b, s, d = x.shape
    tm = POST_ROWS
    nt = s // tm
    tril = jnp.asarray(np.tril(np.ones((tm, tm), np.float32), -1), BF16)
    triu = jnp.asarray(np.triu(np.ones((LANES, LANES), np.float32), 1), BF16)
    row = lambda w: pl.BlockSpec((1, tm, w), lambda bi, i: (bi, i, 0))
    vec = lambda w: pl.BlockSpec((1, w), lambda bi, i: (0, 0))
    mod_spec = pl.BlockSpec((1, 1, d), lambda bi, i: (bi, 0, 0))
    tile_vec = pl.BlockSpec((1, 1, LANES), lambda bi, i: (bi * nt + i, 0, 0))
    return pl.pallas_call(
        _postmix_kernel,
        grid=(b, nt),
        in_specs=[row(d), row(D_DIL), row(D_SB), vec(D_DIL), vec(D_SB),
                  pl.BlockSpec((d, d), lambda bi, i: (0, 0)),
                  mod_spec, mod_spec, mod_spec, vec(d),
                  pl.BlockSpec((d, LANES), lambda bi, i: (0, 0)),
                  pl.BlockSpec((tm, tm), lambda bi, i: (0, 0)),
                  pl.BlockSpec((LANES, LANES), lambda bi, i: (0, 0))],
        out_specs=[row(d), row(d), row(LANES), tile_vec, tile_vec],
        out_shape=[jax.ShapeDtypeStruct((b, s, d), F32),
                   jax.ShapeDtypeStruct((b, s, d), BF16),
                   jax.ShapeDtypeStruct((b, s, LANES), F32),
                   jax.ShapeDtypeStruct((b * nt, 1, LANES), F32),
                   jax.ShapeDtypeStruct((b * nt, 1, LANES), F32)],
        scratch_shapes=[pltpu.VMEM((1, LANES), F32)],
        compiler_params=_cparams(("arbitrary", "arbitrary")),
        name="postmix",
    )(x, o_dil, o_sb, g_dil.reshape(1, -1), g_sb.reshape(1, -1), w_out_bf16, gate, shift, scale,
      g_ffn.reshape(1, d), w_router, tril, triu)


def _for_each_run_piece(tile, start_ref, cnt_ref, base_ref, fn):
    def body(e, off):
        c = cnt_ref[tile * N_EXPERTS + e]
        sorted0 = start_ref[e] + base_ref[tile * N_EXPERTS + e]
        for k in range(3, 10):
            p = 1 << k

            @pl.when((c & p) != 0)
            def _(p=p):
                done = c - (c & (2 * p - 1))
                fn(pl.multiple_of(off + done, SUBLANES), pl.multiple_of(sorted0 + done, SUBLANES), p)
        return off + c

    return lax.fori_loop(0, N_EXPERTS, body, 0)


def _sort_kernel(start_ref, cnt_ref, base_ref, h2_ref, route_ref, buf_ref, xs_scr, sem):
    tm = POST_ROWS
    lt = LOCAL_ROWS
    d = h2_ref.shape[2]
    tile = pl.program_id(0)
    slot = tile % 2

    def piece(slot, lrow, srow, rows):
        return pltpu.make_async_copy(xs_scr.at[slot, pl.ds(lrow, rows)], buf_ref.at[pl.ds(srow, rows)],
                                     sem.at[slot])

    def drain(tile, slot):
        _for_each_run_piece(tile, start_ref, cnt_ref, base_ref, lambda *a: piece(slot, *a).wait())

    @pl.when(tile >= 2)
    def _():
        drain(tile - 2, slot)

    lane = lax.broadcasted_iota(jnp.int32, (tm, LANES), 1)
    route = route_ref[0]
    w1 = jnp.sum(jnp.where(lane == 2, route, 0.0), axis=-1, keepdims=True)
    w2 = jnp.sum(jnp.where(lane == 3, route, 0.0), axis=-1, keepdims=True)

    def pieces(w):
        hi, mid, lw = _split3(w)
        return jnp.where(lane == 0, hi.astype(F32),
                         jnp.where(lane == 1, mid.astype(F32),
                                   jnp.where(lane == 2, lw.astype(F32), 0.0))).astype(BF16)

    route_t = route.T
    s1 = route_t[0:1, :].astype(jnp.int32)
    s2 = route_t[1:2, :].astype(jnp.int32)
    row = lax.broadcasted_iota(jnp.int32, (lt, tm), 0)
    p1 = jnp.where(row == s1, 1.0, 0.0)
    p2 = jnp.where(row == s2, 1.0, 0.0)
    xs_scr[slot, :, 0:d] = jnp.dot((p1 + p2).astype(BF16), h2_ref[0], preferred_element_type=F32)
    xs_scr[slot, :, d:] = (jnp.dot(p1.astype(BF16), pieces(w1), preferred_element_type=F32)
                           + jnp.dot(p2.astype(BF16), pieces(w2), preferred_element_type=F32))
    _for_each_run_piece(tile, start_ref, cnt_ref, base_ref, lambda *a: piece(slot, *a).start())

    last = pl.num_programs(0) - 1

    @pl.when(jnp.logical_and(tile == last, tile >= 1))
    def _():
        drain(tile - 1, 1 - slot)

    @pl.when(tile == last)
    def _():
        drain(tile, slot)


def _dispatch(pad_start, cnt, base, h2, route, cap):
    b, s, d = h2.shape
    tm = POST_ROWS
    nt = s // tm
    return pl.pallas_call(
        _sort_kernel,
        grid_spec=pltpu.PrefetchScalarGridSpec(
            num_scalar_prefetch=3, grid=(b * nt,),
            in_specs=[pl.BlockSpec((1, tm, d), lambda t, *_: (t // nt, t % nt, 0)),
                      pl.BlockSpec((1, tm, LANES), lambda t, *_: (t // nt, t % nt, 0))],
            out_specs=pl.BlockSpec(memory_space=pl.ANY),
            scratch_shapes=[pltpu.VMEM((2, LOCAL_ROWS, d + LANES), F32), pltpu.SemaphoreType.DMA((2,))]),
        out_shape=jax.ShapeDtypeStruct((cap, d + LANES), F32),
        compiler_params=_cparams(("arbitrary",)),
        name="dispatch",
    )(pad_start, cnt, base, h2, route)


def _expert_kernel(be_ref, live_ref, x_ref, wg_ref, wu_ref, wd_ref, y_ref):
    del be_ref
    d = y_ref.shape[1]
    live = live_ref[pl.program_id(0)]

    @pl.when(live > 0)
    def _():
        keep = lax.broadcasted_iota(jnp.int32, (x_ref.shape[0], 1), 0) < live
        x = jnp.where(keep, x_ref[...], 0.0)
        xb = x[:, 0:d].astype(BF16)
        weight = jnp.sum(x[:, d:], axis=-1, keepdims=True)
        gate = jnp.dot(xb, wg_ref[0], preferred_element_type=F32)
        up = jnp.dot(xb, wu_ref[0], preferred_element_type=F32)
        act = gate / (1.0 + jnp.exp(-gate)) * up
        y_ref[...] = jnp.dot(act.astype(BF16), wd_ref[0], preferred_element_type=F32) * weight

    @pl.when(live == 0)
    def _():
        y_ref[...] = jnp.zeros_like(y_ref)


def _experts(block_expert, live_rows, buf, wg, wu, wd):
    cap, dw = buf.shape
    d, f = wg.shape[1], wg.shape[2]
    bm = EXPERT_ROWS
    return pl.pallas_call(
        _expert_kernel,
        grid_spec=pltpu.PrefetchScalarGridSpec(
            num_scalar_prefetch=2, grid=(cap // bm,),
            in_specs=[pl.BlockSpec((bm, dw), lambda i, be, nu: (i, 0)),
                      pl.BlockSpec((1, d, f), lambda i, be, nu: (be[i], 0, 0)),
                      pl.BlockSpec((1, d, f), lambda i, be, nu: (be[i], 0, 0)),
                      pl.BlockSpec((1, f, d), lambda i, be, nu: (be[i], 0, 0))],
            out_specs=pl.BlockSpec((bm, d), lambda i, be, nu: (i, 0))),
        out_shape=jax.ShapeDtypeStruct((cap, d), F32),
        compiler_params=_cparams(("arbitrary",)),
        name="experts",
    )(block_expert, live_rows, buf, wg, wu, wd)


def _combine_kernel(start_ref, cnt_ref, base_ref, x1_ref, route_ref, gate_ref, g_ref, y_hbm_ref, o_ref,
                    y_scr, sem):
    tm = POST_ROWS
    lt = LOCAL_ROWS
    tile = pl.program_id(0)
    slot = tile % 2

    def piece(slot, lrow, srow, rows):
        return pltpu.make_async_copy(y_hbm_ref.at[pl.ds(srow, rows)], y_scr.at[slot, pl.ds(lrow, rows)],
                                     sem.at[slot])

    def fetch(tile, slot):
        _for_each_run_piece(tile, start_ref, cnt_ref, base_ref, lambda *a: piece(slot, *a).start())

    @pl.when(tile == 0)
    def _():
        fetch(tile, slot)

    @pl.when(tile + 1 < pl.num_programs(0))
    def _():
        fetch(tile + 1, 1 - slot)

    lane = lax.broadcasted_iota(jnp.int32, (tm, LANES), 1)
    route = route_ref[0]
    s1 = jnp.sum(jnp.where(lane == 0, route, 0.0), axis=-1, keepdims=True).astype(jnp.int32)
    s2 = jnp.sum(jnp.where(lane == 1, route, 0.0), axis=-1, keepdims=True).astype(jnp.int32)
    col = lax.broadcasted_iota(jnp.int32, (tm, lt), 1)
    pick = jnp.where(col == s1, 1.0, jnp.where(col == s2, 1.0, 0.0)).astype(BF16)
    used = _for_each_run_piece(tile, start_ref, cnt_ref, base_ref, lambda *a: piece(slot, *a).wait())
    live = lax.broadcasted_iota(jnp.int32, (lt, 1), 0) < used
    yv = jnp.where(live, y_scr[slot], 0.0)
    hi = yv.astype(BF16)
    lo = (yv - hi.astype(F32)).astype(BF16)
    y = jnp.dot(jnp.concatenate([pick, pick], axis=1), jnp.concatenate([hi, lo], axis=0),
                preferred_element_type=F32)
    o_ref[0] = _rms(x1_ref[0] + gate_ref[0] * y, g_ref[...])


def _combine(pad_start, cnt, base, x1, y_sorted, route, gate, g_final):
    b, s, d = x1.shape
    tm = POST_ROWS
    nt = s // tm
    return pl.pallas_call(
        _combine_kernel,
        grid_spec=pltpu.PrefetchScalarGridSpec(
            num_scalar_prefetch=3, grid=(b * nt,),
            in_specs=[pl.BlockSpec((1, tm, d), lambda t, *_: (t // nt, t % nt, 0)),
                      pl.BlockSpec((1, tm, LANES), lambda t, *_: (t // nt, t % nt, 0)),
                      pl.BlockSpec((1, 1, d), lambda t, *_: (t // nt, 0, 0)),
                      pl.BlockSpec((1, d), lambda t, *_: (0, 0)),
                      pl.BlockSpec(memory_space=pl.ANY)],
            out_specs=pl.BlockSpec((1, tm, d), lambda t, *_: (t // nt, t % nt, 0)),
            scratch_shapes=[pltpu.VMEM((2, LOCAL_ROWS, d), F32), pltpu.SemaphoreType.DMA((2,))]),
        out_shape=jax.ShapeDtypeStruct((b, s, d), F32),
        compiler_params=_cparams(("arbitrary",)),
        name="combine",
    )(pad_start, cnt, base, x1, route, gate, g_final.reshape(1, d), y_sorted)


def kernel(x, c, w_ada, b_ada, g_mix, w_in, g_dil_out, g_sb_out, w_out, g_ffn,
           w_group, w_expert, w_gate, w_up, w_down, g_final):
    b, s, d = x.shape
    depth = w_ada.shape[0]
    assert s % DIL_UNIT == 0 and d == D_DIL + D_SB
    assert depth == 1, "the final rmsnorm is fused into the last layer's combine step"
    n = b * s
    ntiles = n // POST_ROWS
    bias = jnp.asarray(_dilated_bias())
    for layer in range(depth):
        mod = _ada(c, w_ada[layer], b_ada[layer])
        shift_mix, scale_mix, gate_mix, shift_ffn, scale_ffn, gate_ffn = (
            m.reshape(b, 1, d) for m in jnp.split(mod, 6, axis=-1))

        qkv_d, q_s, k_s, v_s = _premix(x, shift_mix, scale_mix, g_mix[layer], w_in[layer].astype(BF16))
        o_dil = _dilated(qkv_d, bias)
        o_sb = _stick(q_s, k_s, v_s)

        w_router = jnp.concatenate(
            [w_group[layer], w_expert[layer],
             jnp.zeros((d, LANES - N_GROUPS - N_EXPERTS), F32)], axis=1)
        x1, h2, route, cnt, base = _postmix(
            x, o_dil, o_sb, g_dil_out[layer], g_sb_out[layer], w_out[layer].astype(BF16),
            gate_mix, shift_ffn, scale_ffn, g_ffn[layer], w_router)

        bm = EXPERT_ROWS
        cnt = cnt[:, 0, ROUTE_LANE0:ROUTE_LANE0 + N_EXPERTS].astype(jnp.int32)
        base = base[:, 0, ROUTE_LANE0:ROUTE_LANE0 + N_EXPERTS].astype(jnp.int32)
        total = base[-1] + cnt[-1]
        cnt = cnt.reshape(-1)
        base = base.reshape(-1)
        padded = (total + bm - 1) // bm * bm
        pad_end = jnp.cumsum(padded)
        pad_start = (pad_end - padded).astype(jnp.int32)
        cap = -(-(2 * n + (SUBLANES - 1) * N_EXPERTS * ntiles) // bm) * bm + N_EXPERTS * bm
        n_blocks = cap // bm
        block_expert = jnp.minimum(
            jnp.sum(pad_end[None, :] <= (jnp.arange(n_blocks) * bm)[:, None], axis=1),
            N_EXPERTS - 1).astype(jnp.int32)
        live_rows = jnp.clip((pad_start + total)[block_expert] - jnp.arange(n_blocks) * bm, 0, bm).astype(jnp.int32)

        buf = _dispatch(pad_start, cnt, base, h2, route, cap)
        y_sorted = _experts(block_expert, live_rows, buf, w_gate[layer].astype(BF16),
                            w_up[layer].astype(BF16), w_down[layer].astype(BF16))
        x = _combine(pad_start, cnt, base, x1, y_sorted, route, gate_ffn, g_final)
    return x
```

```python
import functools

import numpy as np
import jax
import jax.numpy as jnp
from jax import lax
from jax.experimental import pallas as pl
from jax.experimental.pallas import tpu as pltpu

HEAD_DIM = 64
N_HEADS_DIL = 8
N_HEADS_SB = 8
D_DIL = N_HEADS_DIL * HEAD_DIM
D_SB = N_HEADS_SB * HEAD_DIM
DILATION_PATTERNS = ((128, 1), (512, 4), (2048, 16))
N_GROUPS = 4
EXPERTS_PER_GROUP = 8
N_EXPERTS = N_GROUPS * EXPERTS_PER_GROUP
NORM_EPS = 1e-6

LANES = 128
SUBLANES = 8
DIL_STEPS = 128
DIL_UNIT = 2048
DIL_TILES_PER_TRIP = 8
SB_BLOCK = 256
SB_QUERY_ROWS = 1024
SB_BLOCKS_PER_TRIP = 2
PRE_ROWS = 512
POST_ROWS = 512
POST_ROW_GROUPS = 2
LOCAL_ROWS = 2 * POST_ROWS + 256
EXPERT_ROWS = 512
ROUTE_LANE0 = N_GROUPS
VMEM_LIMIT = 56 * 1024 * 1024

F32 = jnp.float32
BF16 = jnp.bfloat16
NEG_INF = float("-inf")
LOG2E = 1.4426950408889634


def _cparams(sem):
    return pltpu.CompilerParams(dimension_semantics=sem, vmem_limit_bytes=VMEM_LIMIT)


def _rms(v, g):
    return v * lax.rsqrt(jnp.mean(v * v, axis=-1, keepdims=True) + NORM_EPS) * g


def _split3(v):
    hi = v.astype(BF16)
    r = v - hi.astype(F32)
    mid = r.astype(BF16)
    lo = (r - mid.astype(F32)).astype(BF16)
    return hi, mid, lo


def _ada_kernel(c_ref, w_ref, b_ref, o_ref):
    c = c_ref[...]
    cond = c / (1.0 + jnp.exp(-c))
    o_ref[...] = jnp.dot(cond, w_ref[...], precision=lax.Precision.HIGHEST,
                         preferred_element_type=F32) + b_ref[...]


def _ada(c, w_ada, b_ada):
    b, d = c.shape
    n = w_ada.shape[1]
    return pl.pallas_call(
        _ada_kernel,
        grid=(n // d,),
        in_specs=[pl.BlockSpec((b, d), lambda j: (0, 0)),
                  pl.BlockSpec((d, d), lambda j: (0, j)),
                  pl.BlockSpec((1, d), lambda j: (0, j))],
        out_specs=pl.BlockSpec((b, d), lambda j: (0, j)),
        out_shape=jax.ShapeDtypeStruct((b, n), F32),
        compiler_params=_cparams(("arbitrary",)),
        name="ada",
    )(c, w_ada, b_ada.reshape(1, n))


def _premix_kernel(x_ref, shift_ref, scale_ref, g_ref, w_ref, qkvd_ref, qs_ref, ks_ref, vs_ref):
    h = _rms(x_ref[0], g_ref[...]) * (1.0 + scale_ref[0]) + shift_ref[0]
    hb = h.astype(BF16)
    scale = HEAD_DIM ** -0.5 * LOG2E
    for j in range(6):
        r = jnp.dot(hb, w_ref[:, j * 512:(j + 1) * 512], preferred_element_type=F32)
        if j == 0:
            qkvd_ref[0, :, 0:512] = r * scale
        elif j < 3:
            qkvd_ref[0, :, j * 512:(j + 1) * 512] = r
        elif j == 3:
            qs_ref[0] = (r * scale).astype(BF16)
        elif j == 4:
            ks_ref[0] = r.astype(BF16)
        else:
            vs_ref[0] = r.astype(BF16)


def _premix(x, shift, scale, g_mix, w_in_bf16):
    b, s, d = x.shape
    tm = PRE_ROWS
    mod_spec = pl.BlockSpec((1, 1, d), lambda bi, i: (bi, 0, 0))
    sb_spec = pl.BlockSpec((1, tm, D_SB), lambda bi, i: (bi, i, 0))
    return pl.pallas_call(
        _premix_kernel,
        grid=(b, s // tm),
        in_specs=[pl.BlockSpec((1, tm, d), lambda bi, i: (bi, i, 0)),
                  mod_spec, mod_spec,
                  pl.BlockSpec((1, d), lambda bi, i: (0, 0)),
                  pl.BlockSpec((d, 3 * (D_DIL + D_SB)), lambda bi, i: (0, 0))],
        out_specs=[pl.BlockSpec((1, tm, 3 * D_DIL), lambda bi, i: (bi, i, 0)),
                   sb_spec, sb_spec, sb_spec],
        out_shape=[jax.ShapeDtypeStruct((b, s, 3 * D_DIL), F32),
                   jax.ShapeDtypeStruct((b, s, D_SB), BF16),
                   jax.ShapeDtypeStruct((b, s, D_SB), BF16),
                   jax.ShapeDtypeStruct((b, s, D_SB), BF16)],
        compiler_params=_cparams(("arbitrary", "arbitrary")),
        name="premix",
    )(x, shift, scale, g_mix.reshape(1, d), w_in_bf16)


def _dilated_bias():
    n = DIL_STEPS
    slopes = np.array([2.0 ** (-8.0 * (i + 1) / N_HEADS_DIL) for i in range(N_HEADS_DIL)], dtype=np.float32)
    steps = np.arange(n)[:, None] + n - np.arange(2 * n)[None, :]
    valid = (steps >= 0) & (steps <= n)
    out = []
    for _, dilation in DILATION_PATTERNS:
        bias = -slopes[:, None, None] * (steps * dilation).astype(np.float32)[None]
        out.append(np.where(valid[None], bias.astype(np.float64) * LOG2E, -np.inf).astype(np.float32))
    return np.stack(out)


def _dil_kernel(q_ref, kc_ref, kp_ref, vc_ref, vp_ref, bias_ref, o_ref,
                kext, vext, u_scr, m_scr, l_scr):
    n = DIL_STEPS
    g = pl.program_id(1)
    kext[0:DIL_UNIT, :] = kp_ref[0]
    kext[DIL_UNIT:2 * DIL_UNIT, :] = kc_ref[0]
    vext[0:DIL_UNIT, :] = vp_ref[0]
    vext[DIL_UNIT:2 * DIL_UNIT, :] = vc_ref[0]
    lane = lax.broadcasted_iota(jnp.int32, (n, LANES), 1)
    head0 = lane < HEAD_DIM
    col = lax.broadcasted_iota(jnp.int32, (n, 2 * n), 1)

    for p, (_, dil) in enumerate(DILATION_PATTERNS):
        unit = n * dil

        def tiles(it, carry, p=p, dil=dil, unit=unit):
            rows_of, vvs, deads, ss = [], [], [], []
            for t in range(DIL_TILES_PER_TRIP):
                ti = it * DIL_TILES_PER_TRIP + t
                j = ti // dil
                r = ti % dil
                qstart = j * unit + r
                kstart = DIL_UNIT + qstart - unit
                if dil == 1:
                    rows_of.append(pl.ds(qstart, n))
                    krows = pl.ds(kstart, 2 * n)
                else:
                    rows_of.append(pl.ds(qstart, n, stride=dil))
                    krows = pl.ds(kstart, 2 * n, stride=dil)
                q = q_ref[0, rows_of[t], :]
                kk = kext[krows, :].astype(BF16)
                vvs.append(vext[krows, :].astype(BF16))
                deads.append(jnp.where(jnp.logical_and(g == 0, j == 0), n, 0))
                for h in range(2):
                    qh = jnp.where(head0 if h == 0 else jnp.logical_not(head0), q, 0.0).astype(BF16)
                    ss.append(lax.dot_general(qh, kk, (((1,), (1,)), ((), ())), preferred_element_type=F32))
            ms, ls, pes = [], [], []
            for t in range(DIL_TILES_PER_TRIP):
                for h in range(2):
                    logits = jnp.where(col < deads[t], NEG_INF, ss[2 * t + h] + bias_ref[p, h])
                    m = jnp.max(logits, axis=-1, keepdims=True)
                    pe = jnp.exp2(logits - m)
                    ls.append(jnp.sum(pe, axis=-1, keepdims=True))
                    ms.append(m)
                    pes.append(pe.astype(BF16))
            us = [jnp.dot(pes[2 * t + h], vvs[t], preferred_element_type=F32)
                  for t in range(DIL_TILES_PER_TRIP) for h in range(2)]
            for t in range(DIL_TILES_PER_TRIP):
                u_scr[p, rows_of[t], :] = jnp.where(head0, us[2 * t], us[2 * t + 1])
                m_scr[p, rows_of[t], :] = jnp.where(head0, ms[2 * t], ms[2 * t + 1])
                l_scr[p, rows_of[t], :] = jnp.where(head0, ls[2 * t], ls[2 * t + 1])
            return carry

        lax.fori_loop(0, DIL_UNIT // n // DIL_TILES_PER_TRIP, tiles, 0)

    def merge(i, carry):
        rows = pl.ds(pl.multiple_of(i * n, n), n)
        m0, m1, m2 = m_scr[0, rows, :], m_scr[1, rows, :], m_scr[2, rows, :]
        mx = jnp.maximum(jnp.maximum(m0, m1), m2)
        w0, w1, w2 = jnp.exp2(m0 - mx), jnp.exp2(m1 - mx), jnp.exp2(m2 - mx)
        num = w0 * u_scr[0, rows, :] + w1 * u_scr[1, rows, :] + w2 * u_scr[2, rows, :]
        den = w0 * l_scr[0, rows, :] + w1 * l_scr[1, rows, :] + w2 * l_scr[2, rows, :]
        o_ref[0, rows, :] = num / den
        return carry

    lax.fori_loop(0, DIL_UNIT // n, merge, 0)


def _dilated(qkv_d, bias):
    b, s, _ = qkv_d.shape
    u = DIL_UNIT
    npair = D_DIL // LANES
    cur = lambda off: pl.BlockSpec((1, u, LANES), lambda bi, g, p: (bi, g, off + p))
    prev = lambda off: pl.BlockSpec((1, u, LANES), lambda bi, g, p: (bi, jnp.maximum(g - 1, 0), off + p))
    return pl.pallas_call(
        _dil_kernel,
        grid=(b, s // u, npair),
        in_specs=[cur(0), cur(npair), prev(npair), cur(2 * npair), prev(2 * npair),
                  pl.BlockSpec((3, 2, DIL_STEPS, 2 * DIL_STEPS), lambda bi, g, p: (0, p, 0, 0))],
        out_specs=pl.BlockSpec((1, u, LANES), lambda bi, g, p: (bi, g, p)),
        out_shape=jax.ShapeDtypeStruct((b, s, D_DIL), F32),
        scratch_shapes=[pltpu.VMEM((2 * u, LANES), F32), pltpu.VMEM((2 * u, LANES), F32),
                        pltpu.VMEM((3, u, LANES), F32), pltpu.VMEM((3, u, LANES), F32),
                        pltpu.VMEM((3, u, LANES), F32)],
        compiler_params=_cparams(("arbitrary", "arbitrary", "arbitrary")),
        name="dilated",
    )(qkv_d, qkv_d, qkv_d, qkv_d, qkv_d, bias)


def _stick_kernel(q_ref, k_ref, v_ref, tri_ref, o_ref,
                  qh_scr, z_scr, w_scr, acc_scr, carry_scr):
    blk = SB_BLOCK
    nsub = SB_QUERY_ROWS // blk
    assert nsub % 2 == 0
    nchain = 2 * nsub
    qi = pl.program_id(2)
    lane = lax.broadcasted_iota(jnp.int32, (blk, LANES), 1)
    head0 = lane < HEAD_DIM
    for sub in range(nsub):
        q = q_ref[0, sub * blk:(sub + 1) * blk, :]
        zero = jnp.zeros_like(q)
        qh_scr[2 * sub] = jnp.where(head0, q, zero)
        qh_scr[2 * sub + 1] = jnp.where(head0, zero, q)
    acc_scr[...] = jnp.zeros_like(acc_scr)
    carry_scr[...] = jnp.zeros_like(carry_scr)
    sign = jnp.int32(-2 ** 31)

    def rows(kb):
        return pl.ds(pl.multiple_of(kb * blk, blk), blk)

    def scores(kb, which, slot):
        kblk = k_ref[0, rows(kb), :]
        for c in which:
            z_scr[slot * nchain + c] = lax.dot_general(
                qh_scr[c], kblk, (((1,), (1,)), ((), ())), preferred_element_type=F32)

    def weights(which, slot, diag_sub, beside=None):
        causal = (lax.broadcasted_iota(jnp.int32, (blk, blk), 1)
                  < lax.broadcasted_iota(jnp.int32, (blk, blk), 0))
        splits = {}
        for c in which:
            z = z_scr[slot * nchain + c]
            neg_abs = lax.bitcast_convert_type(lax.bitcast_convert_type(z, jnp.int32) | sign, F32)
            softplus = jnp.maximum(z, 0.0) + jnp.log(1.0 + jnp.exp2(neg_abs)) * LOG2E
            if c // 2 == diag_sub:
                softplus = jnp.where(causal, softplus, 0.0)
            hi = softplus.astype(BF16)
            lo = (softplus - hi.astype(F32)).astype(BF16)
            splits[c] = jnp.concatenate([hi, lo], axis=1)
        sums = {}
        for c in which:
            if beside is not None:
                beside(c)
            sums[c] = jnp.dot(splits[c], tri_ref[...], preferred_element_type=F32)
        for c in which:
            carry = carry_scr[c]
            w = jnp.exp2((z_scr[slot * nchain + c] - sums[c]) + carry[:, 0:1])
            if c // 2 == diag_sub:
                w = jnp.where(causal, w, 0.0)
            w_scr[c] = w.astype(BF16)
            carry_scr[c] = carry - sums[c][:, 0:LANES]

    def accumulate(which, kb):
        vblk = v_ref[0, rows(kb), :]
        for c in which:
            acc_scr[c] = acc_scr[c] + jnp.dot(w_scr[c], vblk, preferred_element_type=F32)

    everyone = list(range(nchain))
    top = nsub * qi + nsub - 1
    first = nsub * qi - 1
    diag = [[c for c in everyone if c // 2 >= nsub - 1 - i] for i in range(nsub)]
    for i in range(nsub):
        scores(top - i, diag[i], 2 + i)
    scores(jnp.maximum(first, 0), everyone, 0)
    for i in range(nsub):
        def previous(c, i=i):
            if i > 0 and c in diag[i - 1]:
                accumulate([c], top - (i - 1))

        weights(diag[i], 2 + i, nsub - 1 - i, beside=previous)

    def blocks(kb0, count):
        for j in range(count):
            kb = kb0 - j

            def neighbours(c, kb=kb, j=j):
                accumulate([c], kb + 1)
                scores(jnp.maximum(kb - 1, 0), [c], 1 - j % 2)

            weights(everyone, j % 2, -1, beside=neighbours)

    per = SB_BLOCKS_PER_TRIP
    trips = nsub * qi // per

    def step(i, carry):
        blocks(first - per * i, per)
        return carry

    lax.fori_loop(0, trips, step, 0)
    for rest in range(2, per, 2):
        @pl.when(nsub * qi - trips * per == rest)
        def _(rest=rest):
            blocks(first - per * trips, rest)
    accumulate(everyone, 0)
    for sub in range(nsub):
        o_ref[0, sub * blk:(sub + 1) * blk, :] = jnp.where(head0, acc_scr[2 * sub], acc_scr[2 * sub + 1])


def _stick(q_s, k_s, v_s):
    b, s, _ = q_s.shape
    blk = SB_BLOCK
    qrows = SB_QUERY_ROWS
    nchain = 2 * qrows // blk
    tri = np.tril(np.ones((blk, blk), np.float32))
    tri2 = jnp.asarray(np.concatenate([tri, tri], axis=0), BF16)
    full = pl.BlockSpec((1, s, LANES), lambda bi, p, i: (bi, 0, p))
    return pl.pallas_call(
        _stick_kernel,
        grid=(b, D_SB // LANES, s // qrows),
        in_specs=[pl.BlockSpec((1, qrows, LANES), lambda bi, p, i: (bi, i, p)), full, full,
                  pl.BlockSpec((2 * blk, blk), lambda bi, p, i: (0, 0))],
        out_specs=pl.BlockSpec((1, qrows, LANES), lambda bi, p, i: (bi, i, p)),
        out_shape=jax.ShapeDtypeStruct((b, s, D_SB), F32),
        scratch_shapes=[pltpu.VMEM((nchain, blk, LANES), BF16),
                        pltpu.VMEM(((2 + qrows // blk) * nchain, blk, blk), F32),
                        pltpu.VMEM((nchain, blk, blk), BF16),
                        pltpu.VMEM((nchain, blk, LANES), F32),
                        pltpu.VMEM((nchain, blk, LANES), F32)],
        compiler_params=_cparams(("arbitrary", "arbitrary", "arbitrary")),
        name="stick",
    )(q_s, k_s, v_s, tri2)


def _postmix_kernel(x_ref, od_ref, os_ref, gd_ref, gs_ref, wout_ref, gate_ref, shift_ref, scale_ref,
                    gffn_ref, wr_ref, tril_ref, triu_ref,
                    x1_ref, h2_ref, route_ref, cnt_ref, base_ref, carry_scr):
    tm = POST_ROWS

    @pl.when(jnp.logical_and(pl.program_id(0) == 0, pl.program_id(1) == 0))
    def _():
        carry_scr[...] = jnp.zeros_like(carry_scr)

    big = jnp.int32(LANES)
    lmax = lambda v: jnp.max(v, axis=-1, keepdims=True)
    lmin = lambda v: jnp.min(v, axis=-1, keepdims=True)
    lsum = lambda v: jnp.sum(v, axis=-1, keepdims=True)
    wr = wr_ref[...]
    w_hi = wr.astype(BF16)
    w_lo = (wr - w_hi.astype(F32)).astype(BF16)
    w3 = jnp.concatenate([w_hi, w_hi, w_lo], axis=0)

    def route_rows(rows):
        n = rows.stop - rows.start
        mixed = jnp.concatenate([_rms(od_ref[0, rows, :], gd_ref[...]), _rms(os_ref[0, rows, :], gs_ref[...])],
                                axis=-1)
        proj = jnp.dot(mixed.astype(BF16), wout_ref[...], preferred_element_type=F32)
        x1 = x_ref[0, rows, :] + gate_ref[0] * proj
        x1_ref[0, rows, :] = x1
        h2 = _rms(x1, gffn_ref[...]) * (1.0 + scale_ref[0]) + shift_ref[0]
        h_hi = h2.astype(BF16)
        h_lo = (h2 - h_hi.astype(F32)).astype(BF16)
        h2_ref[0, rows, :] = h_hi
        logits = jnp.dot(jnp.concatenate([h_hi, h_lo, h_hi], axis=1), w3, preferred_element_type=F32)

        lane = lax.broadcasted_iota(jnp.int32, (n, LANES), 1)
        gmask = lane < N_GROUPS
        gl = jnp.where(gmask, logits, NEG_INF)
        gmx = lmax(gl)
        group = lmin(jnp.where(jnp.logical_and(gmask, gl == gmx), lane, big))
        group_gate = 1.0 / lsum(jnp.exp(gl - gmx))
        lo = ROUTE_LANE0 + group * EXPERTS_PER_GROUP
        emask = jnp.logical_and(lane >= lo, lane < lo + EXPERTS_PER_GROUP)
        el = jnp.where(emask, logits, NEG_INF)
        l1 = lmax(el)
        i1 = lmin(jnp.where(el == l1, lane, big))
        el2 = jnp.where(lane == i1, NEG_INF, el)
        l2 = lmax(el2)
        i2 = lmin(jnp.where(el2 == l2, lane, big))
        r = jnp.exp(l2 - l1)
        return i1, i2, group_gate / (1.0 + r), group_gate * r / (1.0 + r)

    hm = tm // POST_ROW_GROUPS
    parts = [route_rows(slice(h * hm, (h + 1) * hm)) for h in range(POST_ROW_GROUPS)]
    i1, i2, w1, w2 = (jnp.concatenate([p[k] for p in parts], axis=0) for k in range(4))
    lane = lax.broadcasted_iota(jnp.int32, (tm, LANES), 1)

    is1 = lane == i1
    is2 = lane == i2
    oh = jnp.where(is1, 1.0, jnp.where(is2, 1.0, 0.0))
    earlier = jnp.dot(tril_ref[...], oh.astype(BF16), preferred_element_type=F32)
    runs = jnp.floor((jnp.sum(oh, axis=0, keepdims=True) + (SUBLANES - 1.0)) * (1.0 / SUBLANES))
    run_off = jnp.dot(jnp.broadcast_to(runs, (SUBLANES, LANES)).astype(BF16), triu_ref[...],
                      preferred_element_type=F32)[0:1]
    pos = earlier + run_off * SUBLANES
    slot1 = lsum(jnp.where(is1, pos, 0.0))
    slot2 = lsum(jnp.where(is2, pos, 0.0))
    cnt = runs * SUBLANES
    cnt_ref[0] = cnt
    base_ref[0] = carry_scr[...]
    carry_scr[...] = carry_scr[...] + cnt

    route_ref[0] = jnp.where(lane == 0, slot1, jnp.where(lane == 1, slot2,
                                                         jnp.where(lane == 2, w1, jnp.where(lane == 3, w2, 0.0))))


def _postmix(x, o_dil, o_sb, g_dil, g_sb, w_out_bf16, gate, shift, scale, g_ffn, w_router):
    b, s, d = x.shape
    tm = POST_ROWS
    nt = s // tm
    tril = jnp.asarray(np.tril(np.ones((tm, tm), np.float32), -1), BF16)
    triu = jnp.asarray(np.triu(np.ones((LANES, LANES), np.float32), 1), BF16)
    row = lambda w: pl.BlockSpec((1, tm, w), lambda bi, i: (bi, i, 0))
    vec = lambda w: pl.BlockSpec((1, w), lambda bi, i: (0, 0))
    mod_spec = pl.BlockSpec((1, 1, d), lambda bi, i: (bi, 0, 0))
    tile_vec = pl.BlockSpec((1, 1, LANES), lambda bi, i: (bi * nt + i, 0, 0))
    return pl.pallas_call(
        _postmix_kernel,
        grid=(b, nt),
        in_specs=[row(d), row(D_DIL), row(D_SB), vec(D_DIL), vec(D_SB),
                  pl.BlockSpec((d, d), lambda bi, i: (0, 0)),
                  mod_spec, mod_spec, mod_spec, vec(d),
                  pl.BlockSpec((d, LANES), lambda bi, i: (0, 0)),
                  pl.BlockSpec((tm, tm), lambda bi, i: (0, 0)),
                  pl.BlockSpec((LANES, LANES), lambda bi, i: (0, 0))],
        out_specs=[row(d), row(d), row(LANES), tile_vec, tile_vec],
        out_shape=[jax.ShapeDtypeStruct((b, s, d), F32),
                   jax.ShapeDtypeStruct((b, s, d), BF16),
                   jax.ShapeDtypeStruct((b, s, LANES), F32),
                   jax.ShapeDtypeStruct((b * nt, 1, LANES), F32),
                   jax.ShapeDtypeStruct((b * nt, 1, LANES), F32)],
        scratch_shapes=[pltpu.VMEM((1, LANES), F32)],
        compiler_params=_cparams(("arbitrary", "arbitrary")),
        name="postmix",
    )(x, o_dil, o_sb, g_dil.reshape(1, -1), g_sb.reshape(1, -1), w_out_bf16, gate, shift, scale,
      g_ffn.reshape(1, d), w_router, tril, triu)


def _for_each_run_piece(tile, start_ref, cnt_ref, base_ref, fn):
    def body(e, off):
        c = cnt_ref[tile * N_EXPERTS + e]
        sorted0 = start_ref[e] + base_ref[tile * N_EXPERTS + e]
        for k in range(3, 10):
            p = 1 << k

            @pl.when((c & p) != 0)
            def _(p=p):
                done = c - (c & (2 * p - 1))
                fn(pl.multiple_of(off + done, SUBLANES), pl.multiple_of(sorted0 + done, SUBLANES), p)
        return off + c

    return lax.fori_loop(0, N_EXPERTS, body, 0)


def _wait_rows(total, piece):
    for k in range(3, LOCAL_ROWS.bit_length()):
        p = 1 << k

        @pl.when((total & p) != 0)
        def _(p=p):
            piece(0, 0, p).wait()


def _sort_kernel(start_ref, cnt_ref, base_ref, rows_ref, h2_ref, route_ref, buf_ref, xs_scr, sem):
    tm = POST_ROWS
    lt = LOCAL_ROWS
    d = h2_ref.shape[2]
    tile = pl.program_id(0)
    slot = tile % 2

    def piece(slot, lrow, srow, rows):
        return pltpu.make_async_copy(xs_scr.at[slot, pl.ds(lrow, rows)], buf_ref.at[pl.ds(srow, rows)],
                                     sem.at[slot])

    def drain(tile, slot):
        _wait_rows(rows_ref[tile], functools.partial(piece, slot))

    @pl.when(tile >= 2)
    def _():
        drain(tile - 2, slot)

    lane = lax.broadcasted_iota(jnp.int32, (tm, LANES), 1)
    route = route_ref[0]
    w1 = jnp.sum(jnp.where(lane == 2, route, 0.0), axis=-1, keepdims=True)
    w2 = jnp.sum(jnp.where(lane == 3, route, 0.0), axis=-1, keepdims=True)

    def pieces(w):
        hi, mid, lw = _split3(w)
        return jnp.where(lane == 0, hi.astype(F32),
                         jnp.where(lane == 1, mid.astype(F32),
                                   jnp.where(lane == 2, lw.astype(F32), 0.0))).astype(BF16)

    route_t = route.T
    s1 = route_t[0:1, :].astype(jnp.int32)
    s2 = route_t[1:2, :].astype(jnp.int32)
    row = lax.broadcasted_iota(jnp.int32, (lt, tm), 0)
    p1 = jnp.where(row == s1, 1.0, 0.0)
    p2 = jnp.where(row == s2, 1.0, 0.0)
    xs_scr[slot, :, 0:d] = jnp.dot((p1 + p2).astype(BF16), h2_ref[0], preferred_element_type=F32)
    xs_scr[slot, :, d:] = (jnp.dot(p1.astype(BF16), pieces(w1), preferred_element_type=F32)
                           + jnp.dot(p2.astype(BF16), pieces(w2), preferred_element_type=F32))
    _for_each_run_piece(tile, start_ref, cnt_ref, base_ref, lambda *a: piece(slot, *a).start())

    last = pl.num_programs(0) - 1

    @pl.when(jnp.logical_and(tile == last, tile >= 1))
    def _():
        drain(tile - 1, 1 - slot)

    @pl.when(tile == last)
    def _():
        drain(tile, slot)


def _dispatch(pad_start, cnt, base, tile_rows, h2, route, cap):
    b, s, d = h2.shape
    tm = POST_ROWS
    nt = s // tm
    return pl.pallas_call(
        _sort_kernel,
        grid_spec=pltpu.PrefetchScalarGridSpec(
            num_scalar_prefetch=4, grid=(b * nt,),
            in_specs=[pl.BlockSpec((1, tm, d), lambda t, *_: (t // nt, t % nt, 0)),
                      pl.BlockSpec((1, tm, LANES), lambda t, *_: (t // nt, t % nt, 0))],
            out_specs=pl.BlockSpec(memory_space=pl.ANY),
            scratch_shapes=[pltpu.VMEM((2, LOCAL_ROWS, d + LANES), F32), pltpu.SemaphoreType.DMA((2,))]),
        out_shape=jax.ShapeDtypeStruct((cap, d + LANES), F32),
        compiler_params=_cparams(("arbitrary",)),
        name="dispatch",
    )(pad_start, cnt, base, tile_rows, h2, route)


def _expert_kernel(be_ref, live_ref, x_ref, wg_ref, wu_ref, wd_ref, y_ref):
    del be_ref
    d = y_ref.shape[1]
    live = live_ref[pl.program_id(0)]

    @pl.when(live > 0)
    def _():
        keep = lax.broadcasted_iota(jnp.int32, (x_ref.shape[0], 1), 0) < live
        x = jnp.where(keep, x_ref[...], 0.0)
        xb = x[:, 0:d].astype(BF16)
        weight = jnp.sum(x[:, d:], axis=-1, keepdims=True)
        gate = jnp.dot(xb, wg_ref[0], preferred_element_type=F32)
        up = jnp.dot(xb, wu_ref[0], preferred_element_type=F32)
        act = gate / (1.0 + jnp.exp(-gate)) * up
        y_ref[...] = jnp.dot(act.astype(BF16), wd_ref[0], preferred_element_type=F32) * weight

    @pl.when(live == 0)
    def _():
        y_ref[...] = jnp.zeros_like(y_ref)


def _experts(block_expert, live_rows, buf, wg, wu, wd):
    cap, dw = buf.shape
    d, f = wg.shape[1], wg.shape[2]
    bm = EXPERT_ROWS
    return pl.pallas_call(
        _expert_kernel,
        grid_spec=pltpu.PrefetchScalarGridSpec(
            num_scalar_prefetch=2, grid=(cap // bm,),
            in_specs=[pl.BlockSpec((bm, dw), lambda i, be, nu: (i, 0)),
                      pl.BlockSpec((1, d, f), lambda i, be, nu: (be[i], 0, 0)),
                      pl.BlockSpec((1, d, f), lambda i, be, nu: (be[i], 0, 0)),
                      pl.BlockSpec((1, f, d), lambda i, be, nu: (be[i], 0, 0))],
            out_specs=pl.BlockSpec((bm, d), lambda i, be, nu: (i, 0))),
        out_shape=jax.ShapeDtypeStruct((cap, d), F32),
        compiler_params=_cparams(("arbitrary",)),
        name="experts",
    )(block_expert, live_rows, buf, wg, wu, wd)


def _combine_kernel(start_ref, cnt_ref, base_ref, rows_ref, x1_ref, route_ref, gate_ref, g_ref, y_hbm_ref,
                    o_ref, y_scr, sem):
    tm = POST_ROWS
    lt = LOCAL_ROWS
    tile = pl.program_id(0)
    slot = tile % 2

    def piece(slot, lrow, srow, rows):
        return pltpu.make_async_copy(y_hbm_ref.at[pl.ds(srow, rows)], y_scr.at[slot, pl.ds(lrow, rows)],
                                     sem.at[slot])

    def fetch(tile, slot):
        _for_each_run_piece(tile, start_ref, cnt_ref, base_ref, lambda *a: piece(slot, *a).start())

    @pl.when(tile == 0)
    def _():
        fetch(tile, slot)

    @pl.when(tile + 1 < pl.num_programs(0))
    def _():
        fetch(tile + 1, 1 - slot)

    lane = lax.broadcasted_iota(jnp.int32, (tm, LANES), 1)
    route = route_ref[0]
    s1 = jnp.sum(jnp.where(lane == 0, route, 0.0), axis=-1, keepdims=True).astype(jnp.int32)
    s2 = jnp.sum(jnp.where(lane == 1, route, 0.0), axis=-1, keepdims=True).astype(jnp.int32)
    col = lax.broadcasted_iota(jnp.int32, (tm, lt), 1)
    pick = jnp.where(col == s1, 1.0, jnp.where(col == s2, 1.0, 0.0)).astype(BF16)
    used = rows_ref[tile]
    _wait_rows(used, functools.partial(piece, slot))
    live = lax.broadcasted_iota(jnp.int32, (lt, 1), 0) < used
    yv = jnp.where(live, y_scr[slot], 0.0)
    hi = yv.astype(BF16)
    lo = (yv - hi.astype(F32)).astype(BF16)
    y = jnp.dot(jnp.concatenate([pick, pick], axis=1), jnp.concatenate([hi, lo], axis=0),
                preferred_element_type=F32)
    o_ref[0] = _rms(x1_ref[0] + gate_ref[0] * y, g_ref[...])


def _combine(pad_start, cnt, base, tile_rows, x1, y_sorted, route, gate, g_final):
    b, s, d = x1.shape
    tm = POST_ROWS
    nt = s // tm
    return pl.pallas_call(
        _combine_kernel,
        grid_spec=pltpu.PrefetchScalarGridSpec(
            num_scalar_prefetch=4, grid=(b * nt,),
            in_specs=[pl.BlockSpec((1, tm, d), lambda t, *_: (t // nt, t % nt, 0)),
                      pl.BlockSpec((1, tm, LANES), lambda t, *_: (t // nt, t % nt, 0)),
                      pl.BlockSpec((1, 1, d), lambda t, *_: (t // nt, 0, 0)),
                      pl.BlockSpec((1, d), lambda t, *_: (0, 0)),
                      pl.BlockSpec(memory_space=pl.ANY)],
            out_specs=pl.BlockSpec((1, tm, d), lambda t, *_: (t // nt, t % nt, 0)),
            scratch_shapes=[pltpu.VMEM((2, LOCAL_ROWS, d), F32), pltpu.SemaphoreType.DMA((2,))]),
        out_shape=jax.ShapeDtypeStruct((b, s, d), F32),
        compiler_params=_cparams(("arbitrary",)),
        name="combine",
    )(pad_start, cnt, base, tile_rows, x1, route, gate, g_final.reshape(1, d), y_sorted)


def kernel(x, c, w_ada, b_ada, g_mix, w_in, g_dil_out, g_sb_out, w_out, g_ffn,
           w_group, w_expert, w_gate, w_up, w_down, g_final):
    b, s, d = x.shape
    depth = w_ada.shape[0]
    assert s % DIL_UNIT == 0 and d == D_DIL + D_SB
    assert depth == 1, "the final rmsnorm is fused into the last layer's combine step"
    n = b * s
    ntiles = n // POST_ROWS
    bias = jnp.asarray(_dilated_bias())
    for layer in range(depth):
        mod = _ada(c, w_ada[layer], b_ada[layer])
        shift_mix, scale_mix, gate_mix, shift_ffn, scale_ffn, gate_ffn = (
            m.reshape(b, 1, d) for m in jnp.split(mod, 6, axis=-1))

        qkv_d, q_s, k_s, v_s = _premix(x, shift_mix, scale_mix, g_mix[layer], w_in[layer].astype(BF16))
        o_dil = _dilated(qkv_d, bias)
        o_sb = _stick(q_s, k_s, v_s)

        w_router = jnp.concatenate(
            [w_group[layer], w_expert[layer],
             jnp.zeros((d, LANES - N_GROUPS - N_EXPERTS), F32)], axis=1)
        x1, h2, route, cnt, base = _postmix(
            x, o_dil, o_sb, g_dil_out[layer], g_sb_out[layer], w_out[layer].astype(BF16),
            gate_mix, shift_ffn, scale_ffn, g_ffn[layer], w_router)

        bm = EXPERT_ROWS
        cnt = cnt[:, 0, ROUTE_LANE0:ROUTE_LANE0 + N_EXPERTS].astype(jnp.int32)
        base = base[:, 0, ROUTE_LANE0:ROUTE_LANE0 + N_EXPERTS].astype(jnp.int32)
        total = base[-1] + cnt[-1]
        tile_rows = jnp.sum(cnt, axis=1)
        cnt = cnt.reshape(-1)
        base = base.reshape(-1)
        padded = (total + bm - 1) // bm * bm
        pad_end = jnp.cumsum(padded)
        pad_start = (pad_end - padded).astype(jnp.int32)
        cap = -(-(2 * n + (SUBLANES - 1) * N_EXPERTS * ntiles) // bm) * bm + N_EXPERTS * bm
        n_blocks = cap // bm
        block_expert = jnp.minimum(
            jnp.sum(pad_end[None, :] <= (jnp.arange(n_blocks) * bm)[:, None], axis=1),
            N_EXPERTS - 1).astype(jnp.int32)
        live_rows = jnp.clip((pad_start + total)[block_expert] - jnp.arange(n_blocks) * bm, 0, bm).astype(jnp.int32)

        buf = _dispatch(pad_start, cnt, base, tile_rows, h2, route, cap)
        y_sorted = _experts(block_expert, live_rows, buf, w_gate[layer].astype(BF16),
                            w_up[layer].astype(BF16), w_down[layer].astype(BF16))
        x = _combine(pad_start, cnt, base, tile_rows, x1, y_sorted, route, gate_ffn, g_final)
    return x
```

```python
import functools

import numpy as np
import jax
import jax.numpy as jnp
from jax import lax
from jax.experimental import pallas as pl
from jax.experimental.pallas import tpu as pltpu

HEAD_DIM = 64
N_HEADS_DIL = 8
N_HEADS_SB = 8
D_DIL = N_HEADS_DIL * HEAD_DIM
D_SB = N_HEADS_SB * HEAD_DIM
DILATION_PATTERNS = ((128, 1), (512, 4), (2048, 16))
N_GROUPS = 4
EXPERTS_PER_GROUP = 8
N_EXPERTS = N_GROUPS * EXPERTS_PER_GROUP
NORM_EPS = 1e-6

LANES = 128
SUBLANES = 8
DIL_STEPS = 128
DIL_UNIT = 2048
DIL_TILES_PER_TRIP = 8
SB_BLOCK = 256
SB_QUERY_ROWS = 1024
SB_BLOCKS_PER_TRIP = 2
PRE_ROWS = 512
POST_ROWS = 512
POST_ROW_GROUPS = 2
LOCAL_ROWS = -(-(2 * POST_ROWS + (SUBLANES - 1) * N_EXPERTS) // LANES) * LANES
EXPERT_ROWS = 512
ROUTE_LANE0 = N_GROUPS
VMEM_LIMIT = 56 * 1024 * 1024

F32 = jnp.float32
BF16 = jnp.bfloat16
NEG_INF = float("-inf")
LOG2E = 1.4426950408889634


def _cparams(sem):
    return pltpu.CompilerParams(dimension_semantics=sem, vmem_limit_bytes=VMEM_LIMIT)


def _rms(v, g):
    return v * lax.rsqrt(jnp.mean(v * v, axis=-1, keepdims=True) + NORM_EPS) * g


def _split3(v):
    hi = v.astype(BF16)
    r = v - hi.astype(F32)
    mid = r.astype(BF16)
    lo = (r - mid.astype(F32)).astype(BF16)
    return hi, mid, lo


def _ada_kernel(c_ref, w_ref, b_ref, o_ref):
    c = c_ref[...]
    cond = c / (1.0 + jnp.exp(-c))
    o_ref[...] = jnp.dot(cond, w_ref[...], precision=lax.Precision.HIGHEST,
                         preferred_element_type=F32) + b_ref[...]


def _ada(c, w_ada, b_ada):
    b, d = c.shape
    n = w_ada.shape[1]
    return pl.pallas_call(
        _ada_kernel,
        grid=(n // d,),
        in_specs=[pl.BlockSpec((b, d), lambda j: (0, 0)),
                  pl.BlockSpec((d, d), lambda j: (0, j)),
                  pl.BlockSpec((1, d), lambda j: (0, j))],
        out_specs=pl.BlockSpec((b, d), lambda j: (0, j)),
        out_shape=jax.ShapeDtypeStruct((b, n), F32),
        compiler_params=_cparams(("arbitrary",)),
        name="ada",
    )(c, w_ada, b_ada.reshape(1, n))


def _premix_kernel(x_ref, shift_ref, scale_ref, g_ref, w_ref, qkvd_ref, qs_ref, ks_ref, vs_ref):
    h = _rms(x_ref[0], g_ref[...]) * (1.0 + scale_ref[0]) + shift_ref[0]
    hb = h.astype(BF16)
    scale = HEAD_DIM ** -0.5 * LOG2E
    assert D_DIL == D_SB
    for j in range(6):
        cols = slice(j * D_DIL, (j + 1) * D_DIL)
        r = jnp.dot(hb, w_ref[:, cols], preferred_element_type=F32)
        if j == 0:
            qkvd_ref[0, :, cols] = r * scale
        elif j < 3:
            qkvd_ref[0, :, cols] = r
        elif j == 3:
            qs_ref[0] = (r * scale).astype(BF16)
        elif j == 4:
            ks_ref[0] = r.astype(BF16)
        else:
            vs_ref[0] = r.astype(BF16)


def _premix(x, shift, scale, g_mix, w_in_bf16):
    b, s, d = x.shape
    tm = PRE_ROWS
    mod_spec = pl.BlockSpec((1, 1, d), lambda bi, i: (bi, 0, 0))
    sb_spec = pl.BlockSpec((1, tm, D_SB), lambda bi, i: (bi, i, 0))
    return pl.pallas_call(
        _premix_kernel,
        grid=(b, s // tm),
        in_specs=[pl.BlockSpec((1, tm, d), lambda bi, i: (bi, i, 0)),
                  mod_spec, mod_spec,
                  pl.BlockSpec((1, d), lambda bi, i: (0, 0)),
                  pl.BlockSpec((d, 3 * (D_DIL + D_SB)), lambda bi, i: (0, 0))],
        out_specs=[pl.BlockSpec((1, tm, 3 * D_DIL), lambda bi, i: (bi, i, 0)),
                   sb_spec, sb_spec, sb_spec],
        out_shape=[jax.ShapeDtypeStruct((b, s, 3 * D_DIL), F32),
                   jax.ShapeDtypeStruct((b, s, D_SB), BF16),
                   jax.ShapeDtypeStruct((b, s, D_SB), BF16),
                   jax.ShapeDtypeStruct((b, s, D_SB), BF16)],
        compiler_params=_cparams(("arbitrary", "arbitrary")),
        name="premix",
    )(x, shift, scale, g_mix.reshape(1, d), w_in_bf16)


def _dilated_bias():
    n = DIL_STEPS
    slopes = np.array([2.0 ** (-8.0 * (i + 1) / N_HEADS_DIL) for i in range(N_HEADS_DIL)], dtype=np.float32)
    steps = np.arange(n)[:, None] + n - np.arange(2 * n)[None, :]
    valid = (steps >= 0) & (steps <= n)
    out = []
    for _, dilation in DILATION_PATTERNS:
        bias = -slopes[:, None, None] * (steps * dilation).astype(np.float32)[None]
        out.append(np.where(valid[None], bias.astype(np.float64) * LOG2E, -np.inf).astype(np.float32))
    return np.stack(out)


def _dil_kernel(q_ref, kc_ref, kp_ref, vc_ref, vp_ref, bias_ref, o_ref,
                kext, vext, u_scr, m_scr, l_scr):
    n = DIL_STEPS
    g = pl.program_id(1)
    kext[0:DIL_UNIT, :] = kp_ref[0]
    kext[DIL_UNIT:2 * DIL_UNIT, :] = kc_ref[0]
    vext[0:DIL_UNIT, :] = vp_ref[0]
    vext[DIL_UNIT:2 * DIL_UNIT, :] = vc_ref[0]
    lane = lax.broadcasted_iota(jnp.int32, (n, LANES), 1)
    head0 = lane < HEAD_DIM
    col = lax.broadcasted_iota(jnp.int32, (n, 2 * n), 1)

    for p, (_, dil) in enumerate(DILATION_PATTERNS):
        unit = n * dil

        def tiles(it, carry, p=p, dil=dil, unit=unit):
            rows_of, vvs, deads, ss = [], [], [], []
            for t in range(DIL_TILES_PER_TRIP):
                ti = it * DIL_TILES_PER_TRIP + t
                j = ti // dil
                r = ti % dil
                qstart = j * unit + r
                kstart = DIL_UNIT + qstart - unit
                if dil == 1:
                    rows_of.append(pl.ds(qstart, n))
                    krows = pl.ds(kstart, 2 * n)
                else:
                    rows_of.append(pl.ds(qstart, n, stride=dil))
                    krows = pl.ds(kstart, 2 * n, stride=dil)
                q = q_ref[0, rows_of[t], :]
                kk = kext[krows, :].astype(BF16)
                vvs.append(vext[krows, :].astype(BF16))
                deads.append(jnp.where(jnp.logical_and(g == 0, j == 0), n, 0))
                for h in range(2):
                    qh = jnp.where(head0 if h == 0 else jnp.logical_not(head0), q, 0.0).astype(BF16)
                    ss.append(lax.dot_general(qh, kk, (((1,), (1,)), ((), ())), preferred_element_type=F32))
            ms, ls, pes = [], [], []
            for t in range(DIL_TILES_PER_TRIP):
                for h in range(2):
                    logits = jnp.where(col < deads[t], NEG_INF, ss[2 * t + h] + bias_ref[p, h])
                    m = jnp.max(logits, axis=-1, keepdims=True)
                    pe = jnp.exp2(logits - m)
                    ls.append(jnp.sum(pe, axis=-1, keepdims=True))
                    ms.append(m)
                    pes.append(pe.astype(BF16))
            us = [jnp.dot(pes[2 * t + h], vvs[t], preferred_element_type=F32)
                  for t in range(DIL_TILES_PER_TRIP) for h in range(2)]
            for t in range(DIL_TILES_PER_TRIP):
                u_scr[p, rows_of[t], :] = jnp.where(head0, us[2 * t], us[2 * t + 1])
                m_scr[p, rows_of[t], :] = jnp.where(head0, ms[2 * t], ms[2 * t + 1])
                l_scr[p, rows_of[t], :] = jnp.where(head0, ls[2 * t], ls[2 * t + 1])
            return carry

        lax.fori_loop(0, DIL_UNIT // n // DIL_TILES_PER_TRIP, tiles, 0)

    def merge(i, carry):
        rows = pl.ds(pl.multiple_of(i * n, n), n)
        m0, m1, m2 = m_scr[0, rows, :], m_scr[1, rows, :], m_scr[2, rows, :]
        mx = jnp.maximum(jnp.maximum(m0, m1), m2)
        w0, w1, w2 = jnp.exp2(m0 - mx), jnp.exp2(m1 - mx), jnp.exp2(m2 - mx)
        num = w0 * u_scr[0, rows, :] + w1 * u_scr[1, rows, :] + w2 * u_scr[2, rows, :]
        den = w0 * l_scr[0, rows, :] + w1 * l_scr[1, rows, :] + w2 * l_scr[2, rows, :]
        o_ref[0, rows, :] = num / den
        return carry

    lax.fori_loop(0, DIL_UNIT // n, merge, 0)


def _dilated(qkv_d, bias):
    b, s, _ = qkv_d.shape
    u = DIL_UNIT
    npair = D_DIL // LANES
    cur = lambda off: pl.BlockSpec((1, u, LANES), lambda bi, g, p: (bi, g, off + p))
    prev = lambda off: pl.BlockSpec((1, u, LANES), lambda bi, g, p: (bi, jnp.maximum(g - 1, 0), off + p))
    return pl.pallas_call(
        _dil_kernel,
        grid=(b, s // u, npair),
        in_specs=[cur(0), cur(npair), prev(npair), cur(2 * npair), prev(2 * npair),
                  pl.BlockSpec((3, 2, DIL_STEPS, 2 * DIL_STEPS), lambda bi, g, p: (0, p, 0, 0))],
        out_specs=pl.BlockSpec((1, u, LANES), lambda bi, g, p: (bi, g, p)),
        out_shape=jax.ShapeDtypeStruct((b, s, D_DIL), F32),
        scratch_shapes=[pltpu.VMEM((2 * u, LANES), F32), pltpu.VMEM((2 * u, LANES), F32),
                        pltpu.VMEM((3, u, LANES), F32), pltpu.VMEM((3, u, LANES), F32),
                        pltpu.VMEM((3, u, LANES), F32)],
        compiler_params=_cparams(("arbitrary", "arbitrary", "arbitrary")),
        name="dilated",
    )(qkv_d, qkv_d, qkv_d, qkv_d, qkv_d, bias)


def _stick_kernel(q_ref, k_ref, v_ref, tri_ref, o_ref,
                  qh_scr, z_scr, w_scr, acc_scr, carry_scr):
    blk = SB_BLOCK
    nsub = SB_QUERY_ROWS // blk
    assert nsub % 2 == 0
    nchain = 2 * nsub
    qi = pl.program_id(2)
    lane = lax.broadcasted_iota(jnp.int32, (blk, LANES), 1)
    head0 = lane < HEAD_DIM
    for sub in range(nsub):
        q = q_ref[0, sub * blk:(sub + 1) * blk, :]
        zero = jnp.zeros_like(q)
        qh_scr[2 * sub] = jnp.where(head0, q, zero)
        qh_scr[2 * sub + 1] = jnp.where(head0, zero, q)
    acc_scr[...] = jnp.zeros_like(acc_scr)
    carry_scr[...] = jnp.zeros_like(carry_scr)
    sign = jnp.int32(-2 ** 31)

    def rows(kb):
        return pl.ds(pl.multiple_of(kb * blk, blk), blk)

    def scores(kb, which, slot):
        kblk = k_ref[0, rows(kb), :]
        for c in which:
            z_scr[slot * nchain + c] = lax.dot_general(
                qh_scr[c], kblk, (((1,), (1,)), ((), ())), preferred_element_type=F32)

    def weights(which, slot, diag_sub, beside=None):
        causal = (lax.broadcasted_iota(jnp.int32, (blk, blk), 1)
                  < lax.broadcasted_iota(jnp.int32, (blk, blk), 0))
        splits = {}
        for c in which:
            z = z_scr[slot * nchain + c]
            neg_abs = lax.bitcast_convert_type(lax.bitcast_convert_type(z, jnp.int32) | sign, F32)
            softplus = jnp.maximum(z, 0.0) + jnp.log(1.0 + jnp.exp2(neg_abs)) * LOG2E
            if c // 2 == diag_sub:
                softplus = jnp.where(causal, softplus, 0.0)
            hi = softplus.astype(BF16)
            lo = (softplus - hi.astype(F32)).astype(BF16)
            splits[c] = jnp.concatenate([hi, lo], axis=1)
        sums = {}
        for c in which:
            if beside is not None:
                beside(c)
            sums[c] = jnp.dot(splits[c], tri_ref[...], preferred_element_type=F32)
        for c in which:
            carry = carry_scr[c]
            w = jnp.exp2((z_scr[slot * nchain + c] - sums[c]) + carry[:, 0:1])
            if c // 2 == diag_sub:
                w = jnp.where(causal, w, 0.0)
            w_scr[c] = w.astype(BF16)
            carry_scr[c] = carry - sums[c][:, 0:LANES]

    def accumulate(which, kb):
        vblk = v_ref[0, rows(kb), :]
        for c in which:
            acc_scr[c] = acc_scr[c] + jnp.dot(w_scr[c], vblk, preferred_element_type=F32)

    everyone = list(range(nchain))
    top = nsub * qi + nsub - 1
    first = nsub * qi - 1
    diag = [[c for c in everyone if c // 2 >= nsub - 1 - i] for i in range(nsub)]
    for i in range(nsub):
        scores(top - i, diag[i], 2 + i)
    scores(jnp.maximum(first, 0), everyone, 0)
    for i in range(nsub):
        def previous(c, i=i):
            if i > 0 and c in diag[i - 1]:
                accumulate([c], top - (i - 1))

        weights(diag[i], 2 + i, nsub - 1 - i, beside=previous)

    def blocks(kb0, count):
        for j in range(count):
            kb = kb0 - j

            def neighbours(c, kb=kb, j=j):
                accumulate([c], kb + 1)
                scores(jnp.maximum(kb - 1, 0), [c], 1 - j % 2)

            weights(everyone, j % 2, -1, beside=neighbours)

    per = SB_BLOCKS_PER_TRIP
    trips = nsub * qi // per

    def step(i, carry):
        blocks(first - per * i, per)
        return carry

    lax.fori_loop(0, trips, step, 0)
    for rest in range(2, per, 2):
        @pl.when(nsub * qi - trips * per == rest)
        def _(rest=rest):
            blocks(first - per * trips, rest)
    accumulate(everyone, 0)
    for sub in range(nsub):
        o_ref[0, sub * blk:(sub + 1) * blk, :] = jnp.where(head0, acc_scr[2 * sub], acc_scr[2 * sub + 1])


def _stick(q_s, k_s, v_s):
    b, s, _ = q_s.shape
    blk = SB_BLOCK
    qrows = SB_QUERY_ROWS
    nchain = 2 * qrows // blk
    tri = np.tril(np.ones((blk, blk), np.float32))
    tri2 = jnp.asarray(np.concatenate([tri, tri], axis=0), BF16)
    full = pl.BlockSpec((1, s, LANES), lambda bi, p, i: (bi, 0, p))
    return pl.pallas_call(
        _stick_kernel,
        grid=(b, D_SB // LANES, s // qrows),
        in_specs=[pl.BlockSpec((1, qrows, LANES), lambda bi, p, i: (bi, i, p)), full, full,
                  pl.BlockSpec((2 * blk, blk), lambda bi, p, i: (0, 0))],
        out_specs=pl.BlockSpec((1, qrows, LANES), lambda bi, p, i: (bi, i, p)),
        out_shape=jax.ShapeDtypeStruct((b, s, D_SB), F32),
        scratch_shapes=[pltpu.VMEM((nchain, blk, LANES), BF16),
                        pltpu.VMEM(((2 + qrows // blk) * nchain, blk, blk), F32),
                        pltpu.VMEM((nchain, blk, blk), BF16),
                        pltpu.VMEM((nchain, blk, LANES), F32),
                        pltpu.VMEM((nchain, blk, LANES), F32)],
        compiler_params=_cparams(("arbitrary", "arbitrary", "arbitrary")),
        name="stick",
    )(q_s, k_s, v_s, tri2)


def _postmix_kernel(x_ref, od_ref, os_ref, gd_ref, gs_ref, wout_ref, gate_ref, shift_ref, scale_ref,
                    gffn_ref, wr_ref, tril_ref, triu_ref,
                    x1_ref, h2_ref, route_ref, cnt_ref, base_ref, carry_scr):
    tm = POST_ROWS

    @pl.when(jnp.logical_and(pl.program_id(0) == 0, pl.program_id(1) == 0))
    def _():
        carry_scr[...] = jnp.zeros_like(carry_scr)

    big = jnp.int32(LANES)
    lmax = lambda v: jnp.max(v, axis=-1, keepdims=True)
    lmin = lambda v: jnp.min(v, axis=-1, keepdims=True)
    lsum = lambda v: jnp.sum(v, axis=-1, keepdims=True)
    wr = wr_ref[...]
    w_hi = wr.astype(BF16)
    w_lo = (wr - w_hi.astype(F32)).astype(BF16)
    w3 = jnp.concatenate([w_hi, w_hi, w_lo], axis=0)

    def route_rows(rows):
        n = rows.stop - rows.start
        mixed = jnp.concatenate([_rms(od_ref[0, rows, :], gd_ref[...]), _rms(os_ref[0, rows, :], gs_ref[...])],
                                axis=-1)
        proj = jnp.dot(mixed.astype(BF16), wout_ref[...], preferred_element_type=F32)
        x1 = x_ref[0, rows, :] + gate_ref[0] * proj
        x1_ref[0, rows, :] = x1
        h2 = _rms(x1, gffn_ref[...]) * (1.0 + scale_ref[0]) + shift_ref[0]
        h_hi = h2.astype(BF16)
        h_lo = (h2 - h_hi.astype(F32)).astype(BF16)
        h2_ref[0, rows, :] = h_hi
        logits = jnp.dot(jnp.concatenate([h_hi, h_lo, h_hi], axis=1), w3, preferred_element_type=F32)

        lane = lax.broadcasted_iota(jnp.int32, (n, LANES), 1)
        gmask = lane < N_GROUPS
        gl = jnp.where(gmask, logits, NEG_INF)
        gmx = lmax(gl)
        group = lmin(jnp.where(jnp.logical_and(gmask, gl == gmx), lane, big))
        group_gate = 1.0 / lsum(jnp.exp(gl - gmx))
        lo = ROUTE_LANE0 + group * EXPERTS_PER_GROUP
        emask = jnp.logical_and(lane >= lo, lane < lo + EXPERTS_PER_GROUP)
        el = jnp.where(emask, logits, NEG_INF)
        l1 = lmax(el)
        i1 = lmin(jnp.where(el == l1, lane, big))
        el2 = jnp.where(lane == i1, NEG_INF, el)
        l2 = lmax(el2)
        i2 = lmin(jnp.where(el2 == l2, lane, big))
        r = jnp.exp(l2 - l1)
        return i1, i2, group_gate / (1.0 + r), group_gate * r / (1.0 + r)

    hm = tm // POST_ROW_GROUPS
    parts = [route_rows(slice(h * hm, (h + 1) * hm)) for h in range(POST_ROW_GROUPS)]
    i1, i2, w1, w2 = (jnp.concatenate([p[k] for p in parts], axis=0) for k in range(4))
    lane = lax.broadcasted_iota(jnp.int32, (tm, LANES), 1)

    is1 = lane == i1
    is2 = lane == i2
    oh = jnp.where(is1, 1.0, jnp.where(is2, 1.0, 0.0))
    earlier = jnp.dot(tril_ref[...], oh.astype(BF16), preferred_element_type=F32)
    runs = jnp.floor((jnp.sum(oh, axis=0, keepdims=True) + (SUBLANES - 1.0)) * (1.0 / SUBLANES))
    run_off = jnp.dot(jnp.broadcast_to(runs, (SUBLANES, LANES)).astype(BF16), triu_ref[...],
                      preferred_element_type=F32)[0:1]
    pos = earlier + run_off * SUBLANES
    slot1 = lsum(jnp.where(is1, pos, 0.0))
    slot2 = lsum(jnp.where(is2, pos, 0.0))
    cnt = runs * SUBLANES
    cnt_ref[0] = cnt
    base_ref[0] = carry_scr[...]
    carry_scr[...] = carry_scr[...] + cnt

    route_ref[0] = jnp.where(lane == 0, slot1, jnp.where(lane == 1, slot2,
                                                         jnp.where(lane == 2, w1, jnp.where(lane == 3, w2, 0.0))))


def _postmix(x, o_dil, o_sb, g_dil, g_sb, w_out_bf16, gate, shift, scale, g_ffn, w_router):
    b, s, d = x.shape
    tm = POST_ROWS
    nt = s // tm
    tril = jnp.asarray(np.tril(np.ones((tm, tm), np.float32), -1), BF16)
    triu = jnp.asarray(np.triu(np.ones((LANES, LANES), np.float32), 1), BF16)
    row = lambda w: pl.BlockSpec((1, tm, w), lambda bi, i: (bi, i, 0))
    vec = lambda w: pl.BlockSpec((1, w), lambda bi, i: (0, 0))
    mod_spec = pl.BlockSpec((1, 1, d), lambda bi, i: (bi, 0, 0))
    tile_vec = pl.BlockSpec((1, 1, LANES), lambda bi, i: (bi * nt + i, 0, 0))
    return pl.pallas_call(
        _postmix_kernel,
        grid=(b, nt),
        in_specs=[row(d), row(D_DIL), row(D_SB), vec(D_DIL), vec(D_SB),
                  pl.BlockSpec((d, d), lambda bi, i: (0, 0)),
                  mod_spec, mod_spec, mod_spec, vec(d),
                  pl.BlockSpec((d, LANES), lambda bi, i: (0, 0)),
                  pl.BlockSpec((tm, tm), lambda bi, i: (0, 0)),
                  pl.BlockSpec((LANES, LANES), lambda bi, i: (0, 0))],
        out_specs=[row(d), row(d), row(LANES), tile_vec, tile_vec],
        out_shape=[jax.ShapeDtypeStruct((b, s, d), F32),
                   jax.ShapeDtypeStruct((b, s, d), BF16),
                   jax.ShapeDtypeStruct((b, s, LANES), F32),
                   jax.ShapeDtypeStruct((b * nt, 1, LANES), F32),
                   jax.ShapeDtypeStruct((b * nt, 1, LANES), F32)],
        scratch_shapes=[pltpu.VMEM((1, LANES), F32)],
        compiler_params=_cparams(("arbitrary", "arbitrary")),
        name="postmix",
    )(x, o_dil, o_sb, g_dil.reshape(1, -1), g_sb.reshape(1, -1), w_out_bf16, gate, shift, scale,
      g_ffn.reshape(1, d), w_router, tril, triu)


def _for_each_run_piece(tile, start_ref, cnt_ref, base_ref, fn):
    def body(e, off):
        c = cnt_ref[tile * N_EXPERTS + e]
        sorted0 = start_ref[e] + base_ref[tile * N_EXPERTS + e]
        for k in range(SUBLANES.bit_length() - 1, POST_ROWS.bit_length()):
            p = 1 << k

            @pl.when((c & p) != 0)
            def _(p=p):
                done = c - (c & (2 * p - 1))
                fn(pl.multiple_of(off + done, SUBLANES), pl.multiple_of(sorted0 + done, SUBLANES), p)
        return off + c

    return lax.fori_loop(0, N_EXPERTS, body, 0)


def _wait_rows(total, piece):
    for k in range(SUBLANES.bit_length() - 1, LOCAL_ROWS.bit_length()):
        p = 1 << k

        @pl.when((total & p) != 0)
        def _(p=p):
            piece(0, 0, p).wait()


def _sort_kernel(start_ref, cnt_ref, base_ref, rows_ref, h2_ref, route_ref, buf_ref, xs_scr, sem):
    tm = POST_ROWS
    lt = LOCAL_ROWS
    d = h2_ref.shape[2]
    tile = pl.program_id(0)
    slot = tile % 2

    def piece(slot, lrow, srow, rows):
        return pltpu.make_async_copy(xs_scr.at[slot, pl.ds(lrow, rows)], buf_ref.at[pl.ds(srow, rows)],
                                     sem.at[slot])

    def drain(tile, slot):
        _wait_rows(rows_ref[tile], functools.partial(piece, slot))

    @pl.when(tile >= 2)
    def _():
        drain(tile - 2, slot)

    lane = lax.broadcasted_iota(jnp.int32, (tm, LANES), 1)
    route = route_ref[0]
    w1 = jnp.sum(jnp.where(lane == 2, route, 0.0), axis=-1, keepdims=True)
    w2 = jnp.sum(jnp.where(lane == 3, route, 0.0), axis=-1, keepdims=True)

    def pieces(w):
        hi, mid, lw = _split3(w)
        return jnp.where(lane == 0, hi.astype(F32),
                         jnp.where(lane == 1, mid.astype(F32),
                                   jnp.where(lane == 2, lw.astype(F32), 0.0))).astype(BF16)

    route_t = route.T
    s1 = route_t[0:1, :].astype(jnp.int32)
    s2 = route_t[1:2, :].astype(jnp.int32)
    row = lax.broadcasted_iota(jnp.int32, (lt, tm), 0)
    p1 = jnp.where(row == s1, 1.0, 0.0)
    p2 = jnp.where(row == s2, 1.0, 0.0)
    xs_scr[slot, :, 0:d] = jnp.dot((p1 + p2).astype(BF16), h2_ref[0], preferred_element_type=F32)
    xs_scr[slot, :, d:] = (jnp.dot(p1.astype(BF16), pieces(w1), preferred_element_type=F32)
                           + jnp.dot(p2.astype(BF16), pieces(w2), preferred_element_type=F32))
    _for_each_run_piece(tile, start_ref, cnt_ref, base_ref, lambda *a: piece(slot, *a).start())

    last = pl.num_programs(0) - 1

    @pl.when(jnp.logical_and(tile == last, tile >= 1))
    def _():
        drain(tile - 1, 1 - slot)

    @pl.when(tile == last)
    def _():
        drain(tile, slot)


def _dispatch(pad_start, cnt, base, tile_rows, h2, route, cap):
    b, s, d = h2.shape
    tm = POST_ROWS
    nt = s // tm
    return pl.pallas_call(
        _sort_kernel,
        grid_spec=pltpu.PrefetchScalarGridSpec(
            num_scalar_prefetch=4, grid=(b * nt,),
            in_specs=[pl.BlockSpec((1, tm, d), lambda t, *_: (t // nt, t % nt, 0)),
                      pl.BlockSpec((1, tm, LANES), lambda t, *_: (t // nt, t % nt, 0))],
            out_specs=pl.BlockSpec(memory_space=pl.ANY),
            scratch_shapes=[pltpu.VMEM((2, LOCAL_ROWS, d + LANES), F32), pltpu.SemaphoreType.DMA((2,))]),
        out_shape=jax.ShapeDtypeStruct((cap, d + LANES), F32),
        compiler_params=_cparams(("arbitrary",)),
        name="dispatch",
    )(pad_start, cnt, base, tile_rows, h2, route)


def _expert_kernel(be_ref, live_ref, x_ref, wg_ref, wu_ref, wd_ref, y_ref):
    del be_ref
    d = y_ref.shape[1]
    live = live_ref[pl.program_id(0)]

    @pl.when(live > 0)
    def _():
        keep = lax.broadcasted_iota(jnp.int32, (x_ref.shape[0], 1), 0) < live
        x = jnp.where(keep, x_ref[...], 0.0)
        xb = x[:, 0:d].astype(BF16)
        weight = jnp.sum(x[:, d:], axis=-1, keepdims=True)
        gate = jnp.dot(xb, wg_ref[0], preferred_element_type=F32)
        up = jnp.dot(xb, wu_ref[0], preferred_element_type=F32)
        act = gate / (1.0 + jnp.exp(-gate)) * up
        y_ref[...] = jnp.dot(act.astype(BF16), wd_ref[0], preferred_element_type=F32) * weight

    @pl.when(live == 0)
    def _():
        y_ref[...] = jnp.zeros_like(y_ref)


def _experts(block_expert, live_rows, buf, wg, wu, wd):
    cap, dw = buf.shape
    d, f = wg.shape[1], wg.shape[2]
    bm = EXPERT_ROWS
    return pl.pallas_call(
        _expert_kernel,
        grid_spec=pltpu.PrefetchScalarGridSpec(
            num_scalar_prefetch=2, grid=(cap // bm,),
            in_specs=[pl.BlockSpec((bm, dw), lambda i, be, nu: (i, 0)),
                      pl.BlockSpec((1, d, f), lambda i, be, nu: (be[i], 0, 0)),
                      pl.BlockSpec((1, d, f), lambda i, be, nu: (be[i], 0, 0)),
                      pl.BlockSpec((1, f, d), lambda i, be, nu: (be[i], 0, 0))],
            out_specs=pl.BlockSpec((bm, d), lambda i, be, nu: (i, 0))),
        out_shape=jax.ShapeDtypeStruct((cap, d), F32),
        compiler_params=_cparams(("arbitrary",)),
        name="experts",
    )(block_expert, live_rows, buf, wg, wu, wd)


def _combine_kernel(start_ref, cnt_ref, base_ref, rows_ref, x1_ref, route_ref, gate_ref, g_ref, y_hbm_ref,
                    o_ref, y_scr, sem):
    tm = POST_ROWS
    lt = LOCAL_ROWS
    tile = pl.program_id(0)
    slot = tile % 2

    def piece(slot, lrow, srow, rows):
        return pltpu.make_async_copy(y_hbm_ref.at[pl.ds(srow, rows)], y_scr.at[slot, pl.ds(lrow, rows)],
                                     sem.at[slot])

    def fetch(tile, slot):
        _for_each_run_piece(tile, start_ref, cnt_ref, base_ref, lambda *a: piece(slot, *a).start())

    @pl.when(tile == 0)
    def _():
        fetch(tile, slot)

    @pl.when(tile + 1 < pl.num_programs(0))
    def _():
        fetch(tile + 1, 1 - slot)

    lane = lax.broadcasted_iota(jnp.int32, (tm, LANES), 1)
    route = route_ref[0]
    s1 = jnp.sum(jnp.where(lane == 0, route, 0.0), axis=-1, keepdims=True).astype(jnp.int32)
    s2 = jnp.sum(jnp.where(lane == 1, route, 0.0), axis=-1, keepdims=True).astype(jnp.int32)
    col = lax.broadcasted_iota(jnp.int32, (tm, lt), 1)
    pick = jnp.where(col == s1, 1.0, jnp.where(col == s2, 1.0, 0.0)).astype(BF16)
    used = rows_ref[tile]
    _wait_rows(used, functools.partial(piece, slot))
    live = lax.broadcasted_iota(jnp.int32, (lt, 1), 0) < used
    yv = jnp.where(live, y_scr[slot], 0.0)
    hi = yv.astype(BF16)
    lo = (yv - hi.astype(F32)).astype(BF16)
    y = jnp.dot(jnp.concatenate([pick, pick], axis=1), jnp.concatenate([hi, lo], axis=0),
                preferred_element_type=F32)
    o_ref[0] = _rms(x1_ref[0] + gate_ref[0] * y, g_ref[...])


def _combine(pad_start, cnt, base, tile_rows, x1, y_sorted, route, gate, g_final):
    b, s, d = x1.shape
    tm = POST_ROWS
    nt = s // tm
    return pl.pallas_call(
        _combine_kernel,
        grid_spec=pltpu.PrefetchScalarGridSpec(
            num_scalar_prefetch=4, grid=(b * nt,),
            in_specs=[pl.BlockSpec((1, tm, d), lambda t, *_: (t // nt, t % nt, 0)),
                      pl.BlockSpec((1, tm, LANES), lambda t, *_: (t // nt, t % nt, 0)),
                      pl.BlockSpec((1, 1, d), lambda t, *_: (t // nt, 0, 0)),
                      pl.BlockSpec((1, d), lambda t, *_: (0, 0)),
                      pl.BlockSpec(memory_space=pl.ANY)],
            out_specs=pl.BlockSpec((1, tm, d), lambda t, *_: (t // nt, t % nt, 0)),
            scratch_shapes=[pltpu.VMEM((2, LOCAL_ROWS, d), F32), pltpu.SemaphoreType.DMA((2,))]),
        out_shape=jax.ShapeDtypeStruct((b, s, d), F32),
        compiler_params=_cparams(("arbitrary",)),
        name="combine",
    )(pad_start, cnt, base, tile_rows, x1, route, gate, g_final.reshape(1, d), y_sorted)


def kernel(x, c, w_ada, b_ada, g_mix, w_in, g_dil_out, g_sb_out, w_out, g_ffn,
           w_group, w_expert, w_gate, w_up, w_down, g_final):
    b, s, d = x.shape
    depth = w_ada.shape[0]
    assert s % DIL_UNIT == 0 and d == D_DIL + D_SB
    assert depth == 1, "the final rmsnorm is fused into the last layer's combine step"
    n = b * s
    ntiles = n // POST_ROWS
    bias = jnp.asarray(_dilated_bias())
    for layer in range(depth):
        mod = _ada(c, w_ada[layer], b_ada[layer])
        shift_mix, scale_mix, gate_mix, shift_ffn, scale_ffn, gate_ffn = (
            m.reshape(b, 1, d) for m in jnp.split(mod, 6, axis=-1))

        qkv_d, q_s, k_s, v_s = _premix(x, shift_mix, scale_mix, g_mix[layer], w_in[layer].astype(BF16))
        o_dil = _dilated(qkv_d, bias)
        o_sb = _stick(q_s, k_s, v_s)

        w_router = jnp.concatenate(
            [w_group[layer], w_expert[layer],
             jnp.zeros((d, LANES - N_GROUPS - N_EXPERTS), F32)], axis=1)
        x1, h2, route, cnt, base = _postmix(
            x, o_dil, o_sb, g_dil_out[layer], g_sb_out[layer], w_out[layer].astype(BF16),
            gate_mix, shift_ffn, scale_ffn, g_ffn[layer], w_router)

        bm = EXPERT_ROWS
        cnt = cnt[:, 0, ROUTE_LANE0:ROUTE_LANE0 + N_EXPERTS].astype(jnp.int32)
        base = base[:, 0, ROUTE_LANE0:ROUTE_LANE0 + N_EXPERTS].astype(jnp.int32)
        total = base[-1] + cnt[-1]
        tile_rows = jnp.sum(cnt, axis=1)
        cnt = cnt.reshape(-1)
        base = base.reshape(-1)
        padded = (total + bm - 1) // bm * bm
        pad_end = jnp.cumsum(padded)
        pad_start = (pad_end - padded).astype(jnp.int32)
        cap = -(-(2 * n + (SUBLANES - 1) * N_EXPERTS * ntiles) // bm) * bm + N_EXPERTS * bm
        n_blocks = cap // bm
        block_expert = jnp.minimum(
            jnp.sum(pad_end[None, :] <= (jnp.arange(n_blocks) * bm)[:, None], axis=1),
            N_EXPERTS - 1).astype(jnp.int32)
        live_rows = jnp.clip((pad_start + total)[block_expert] - jnp.arange(n_blocks) * bm, 0, bm).astype(jnp.int32)

        buf = _dispatch(pad_start, cnt, base, tile_rows, h2, route, cap)
        y_sorted = _experts(block_expert, live_rows, buf, w_gate[layer].astype(BF16),
                            w_up[layer].astype(BF16), w_down[layer].astype(BF16))
        x = _combine(pad_start, cnt, base, tile_rows, x1, y_sorted, route, gate_ffn, g_final)
    return x
```

```python
import functools

import numpy as np
import jax
import jax.numpy as jnp
from jax import lax
from jax.experimental import pallas as pl
from jax.experimental.pallas import tpu as pltpu

HEAD_DIM = 64
N_HEADS_DIL = 8
N_HEADS_SB = 8
D_DIL = N_HEADS_DIL * HEAD_DIM
D_SB = N_HEADS_SB * HEAD_DIM
DILATION_PATTERNS = ((128, 1), (512, 4), (2048, 16))
N_GROUPS = 4
EXPERTS_PER_GROUP = 8
N_EXPERTS = N_GROUPS * EXPERTS_PER_GROUP
NORM_EPS = 1e-6

LANES = 128
SUBLANES = 8
DIL_STEPS = 128
DIL_UNIT = 2048
DIL_TILES_PER_TRIP = 8
SB_BLOCK = 256
SB_QUERY_ROWS = 1024
SB_BLOCKS_PER_TRIP = 2
PRE_ROWS = 512
POST_ROWS = 512
POST_ROW_GROUPS = 2
LOCAL_ROWS = -(-(2 * POST_ROWS + (SUBLANES - 1) * N_EXPERTS) // LANES) * LANES
EXPERT_ROWS = 512
ROUTE_LANE0 = N_GROUPS
VMEM_LIMIT = 56 * 1024 * 1024

F32 = jnp.float32
BF16 = jnp.bfloat16
NEG_INF = float("-inf")
LOG2E = 1.4426950408889634


def _cparams(sem):
    return pltpu.CompilerParams(dimension_semantics=sem, vmem_limit_bytes=VMEM_LIMIT)


def _rms(v, g):
    return v * lax.rsqrt(jnp.mean(v * v, axis=-1, keepdims=True) + NORM_EPS) * g


def _split3(v):
    hi = v.astype(BF16)
    r = v - hi.astype(F32)
    mid = r.astype(BF16)
    lo = (r - mid.astype(F32)).astype(BF16)
    return hi, mid, lo


def _ada_kernel(c_ref, w_ref, b_ref, o_ref):
    c = c_ref[...]
    cond = c / (1.0 + jnp.exp(-c))
    o_ref[...] = jnp.dot(cond, w_ref[...], precision=lax.Precision.HIGHEST,
                         preferred_element_type=F32) + b_ref[...]


def _ada(c, w_ada, b_ada):
    b, d = c.shape
    n = w_ada.shape[1]
    return pl.pallas_call(
        _ada_kernel,
        grid=(n // d,),
        in_specs=[pl.BlockSpec((b, d), lambda j: (0, 0)),
                  pl.BlockSpec((d, d), lambda j: (0, j)),
                  pl.BlockSpec((1, d), lambda j: (0, j))],
        out_specs=pl.BlockSpec((b, d), lambda j: (0, j)),
        out_shape=jax.ShapeDtypeStruct((b, n), F32),
        compiler_params=_cparams(("arbitrary",)),
        name="ada",
    )(c, w_ada, b_ada.reshape(1, n))


def _premix_kernel(x_ref, shift_ref, scale_ref, g_ref, w_ref, qkvd_ref, qs_ref, ks_ref, vs_ref):
    h = _rms(x_ref[0], g_ref[...]) * (1.0 + scale_ref[0]) + shift_ref[0]
    hb = h.astype(BF16)
    scale = HEAD_DIM ** -0.5 * LOG2E
    assert D_DIL == D_SB
    for j in range(6):
        cols = slice(j * D_DIL, (j + 1) * D_DIL)
        r = jnp.dot(hb, w_ref[:, cols], preferred_element_type=F32)
        if j == 0:
            qkvd_ref[0, :, cols] = r * scale
        elif j < 3:
            qkvd_ref[0, :, cols] = r
        elif j == 3:
            qs_ref[0] = (r * scale).astype(BF16)
        elif j == 4:
            ks_ref[0] = r.astype(BF16)
        else:
            vs_ref[0] = r.astype(BF16)


def _premix(x, shift, scale, g_mix, w_in_bf16):
    b, s, d = x.shape
    tm = PRE_ROWS
    mod_spec = pl.BlockSpec((1, 1, d), lambda bi, i: (bi, 0, 0))
    sb_spec = pl.BlockSpec((1, tm, D_SB), lambda bi, i: (bi, i, 0))
    return pl.pallas_call(
        _premix_kernel,
        grid=(b, s // tm),
        in_specs=[pl.BlockSpec((1, tm, d), lambda bi, i: (bi, i, 0)),
                  mod_spec, mod_spec,
                  pl.BlockSpec((1, d), lambda bi, i: (0, 0)),
                  pl.BlockSpec((d, 3 * (D_DIL + D_SB)), lambda bi, i: (0, 0))],
        out_specs=[pl.BlockSpec((1, tm, 3 * D_DIL), lambda bi, i: (bi, i, 0)),
                   sb_spec, sb_spec, sb_spec],
        out_shape=[jax.ShapeDtypeStruct((b, s, 3 * D_DIL), F32),
                   jax.ShapeDtypeStruct((b, s, D_SB), BF16),
                   jax.ShapeDtypeStruct((b, s, D_SB), BF16),
                   jax.ShapeDtypeStruct((b, s, D_SB), BF16)],
        compiler_params=_cparams(("arbitrary", "arbitrary")),
        name="premix",
    )(x, shift, scale, g_mix.reshape(1, d), w_in_bf16)


def _dilated_bias():
    n = DIL_STEPS
    slopes = np.array([2.0 ** (-8.0 * (i + 1) / N_HEADS_DIL) for i in range(N_HEADS_DIL)], dtype=np.float32)
    steps = np.arange(n)[:, None] + n - np.arange(2 * n)[None, :]
    valid = (steps >= 0) & (steps <= n)
    out = []
    for _, dilation in DILATION_PATTERNS:
        bias = -slopes[:, None, None] * (steps * dilation).astype(np.float32)[None]
        out.append(np.where(valid[None], bias.astype(np.float64) * LOG2E, -np.inf).astype(np.float32))
    return np.stack(out)


def _dil_kernel(q_ref, kc_ref, kp_ref, vc_ref, vp_ref, bias_ref, o_ref,
                kext, vext, u_scr, m_scr, l_scr):
    n = DIL_STEPS
    g = pl.program_id(1)
    kext[0:DIL_UNIT, :] = kp_ref[0]
    kext[DIL_UNIT:2 * DIL_UNIT, :] = kc_ref[0]
    vext[0:DIL_UNIT, :] = vp_ref[0]
    vext[DIL_UNIT:2 * DIL_UNIT, :] = vc_ref[0]
    lane = lax.broadcasted_iota(jnp.int32, (n, LANES), 1)
    head0 = lane < HEAD_DIM
    col = lax.broadcasted_iota(jnp.int32, (n, 2 * n), 1)

    for p, (_, dil) in enumerate(DILATION_PATTERNS):
        unit = n * dil

        def tiles(it, carry, p=p, dil=dil, unit=unit):
            rows_of, vvs, deads, ss = [], [], [], []
            for t in range(DIL_TILES_PER_TRIP):
                ti = it * DIL_TILES_PER_TRIP + t
                j = ti // dil
                r = ti % dil
                qstart = j * unit + r
                kstart = DIL_UNIT + qstart - unit
                if dil == 1:
                    rows_of.append(pl.ds(qstart, n))
                    krows = pl.ds(kstart, 2 * n)
                else:
                    rows_of.append(pl.ds(qstart, n, stride=dil))
                    krows = pl.ds(kstart, 2 * n, stride=dil)
                q = q_ref[0, rows_of[t], :]
                kk = kext[krows, :].astype(BF16)
                vvs.append(vext[krows, :].astype(BF16))
                deads.append(jnp.where(jnp.logical_and(g == 0, j == 0), n, 0))
                for h in range(2):
                    qh = jnp.where(head0 if h == 0 else jnp.logical_not(head0), q, 0.0).astype(BF16)
                    ss.append(lax.dot_general(qh, kk, (((1,), (1,)), ((), ())), preferred_element_type=F32))
            ms, ls, pes = [], [], []
            for t in range(DIL_TILES_PER_TRIP):
                for h in range(2):
                    logits = jnp.where(col < deads[t], NEG_INF, ss[2 * t + h] + bias_ref[p, h])
                    m = jnp.max(logits, axis=-1, keepdims=True)
                    pe = jnp.exp2(logits - m)
                    ls.append(jnp.sum(pe, axis=-1, keepdims=True))
                    ms.append(m)
                    pes.append(pe.astype(BF16))
            us = [jnp.dot(pes[2 * t + h], vvs[t], preferred_element_type=F32)
                  for t in range(DIL_TILES_PER_TRIP) for h in range(2)]
            for t in range(DIL_TILES_PER_TRIP):
                u_scr[p, rows_of[t], :] = jnp.where(head0, us[2 * t], us[2 * t + 1])
                m_scr[p, rows_of[t], :] = jnp.where(head0, ms[2 * t], ms[2 * t + 1])
                l_scr[p, rows_of[t], :] = jnp.where(head0, ls[2 * t], ls[2 * t + 1])
            return carry

        lax.fori_loop(0, DIL_UNIT // n // DIL_TILES_PER_TRIP, tiles, 0)

    def merge(i, carry):
        rows = pl.ds(pl.multiple_of(i * n, n), n)
        m0, m1, m2 = m_scr[0, rows, :], m_scr[1, rows, :], m_scr[2, rows, :]
        mx = jnp.maximum(jnp.maximum(m0, m1), m2)
        w0, w1, w2 = jnp.exp2(m0 - mx), jnp.exp2(m1 - mx), jnp.exp2(m2 - mx)
        num = w0 * u_scr[0, rows, :] + w1 * u_scr[1, rows, :] + w2 * u_scr[2, rows, :]
        den = w0 * l_scr[0, rows, :] + w1 * l_scr[1, rows, :] + w2 * l_scr[2, rows, :]
        o_ref[0, rows, :] = num / den
        return carry

    lax.fori_loop(0, DIL_UNIT // n, merge, 0)


def _dilated(qkv_d, bias):
    b, s, _ = qkv_d.shape
    u = DIL_UNIT
    npair = D_DIL // LANES
    cur = lambda off: pl.BlockSpec((1, u, LANES), lambda bi, g, p: (bi, g, off + p))
    prev = lambda off: pl.BlockSpec((1, u, LANES), lambda bi, g, p: (bi, jnp.maximum(g - 1, 0), off + p))
    return pl.pallas_call(
        _dil_kernel,
        grid=(b, s // u, npair),
        in_specs=[cur(0), cur(npair), prev(npair), cur(2 * npair), prev(2 * npair),
                  pl.BlockSpec((3, 2, DIL_STEPS, 2 * DIL_STEPS), lambda bi, g, p: (0, p, 0, 0))],
        out_specs=pl.BlockSpec((1, u, LANES), lambda bi, g, p: (bi, g, p)),
        out_shape=jax.ShapeDtypeStruct((b, s, D_DIL), F32),
        scratch_shapes=[pltpu.VMEM((2 * u, LANES), F32), pltpu.VMEM((2 * u, LANES), F32),
                        pltpu.VMEM((3, u, LANES), F32), pltpu.VMEM((3, u, LANES), F32),
                        pltpu.VMEM((3, u, LANES), F32)],
        compiler_params=_cparams(("arbitrary", "arbitrary", "arbitrary")),
        name="dilated",
    )(qkv_d, qkv_d, qkv_d, qkv_d, qkv_d, bias)


def _stick_kernel(q_ref, k_ref, v_ref, tri_ref, o_ref,
                  qh_scr, z_scr, w_scr, acc_scr, carry_scr):
    blk = SB_BLOCK
    nsub = SB_QUERY_ROWS // blk
    assert nsub % 2 == 0
    nchain = 2 * nsub
    qi = pl.program_id(2)
    lane = lax.broadcasted_iota(jnp.int32, (blk, LANES), 1)
    head0 = lane < HEAD_DIM
    for sub in range(nsub):
        q = q_ref[0, sub * blk:(sub + 1) * blk, :]
        zero = jnp.zeros_like(q)
        qh_scr[2 * sub] = jnp.where(head0, q, zero)
        qh_scr[2 * sub + 1] = jnp.where(head0, zero, q)
    acc_scr[...] = jnp.zeros_like(acc_scr)
    carry_scr[...] = jnp.zeros_like(carry_scr)
    sign = jnp.int32(-2 ** 31)

    def rows(kb):
        return pl.ds(pl.multiple_of(kb * blk, blk), blk)

    def scores(kb, which, slot):
        kblk = k_ref[0, rows(kb), :]
        for c in which:
            z_scr[slot * nchain + c] = lax.dot_general(
                qh_scr[c], kblk, (((1,), (1,)), ((), ())), preferred_element_type=F32)

    def weights(which, slot, diag_sub, beside=None):
        causal = (lax.broadcasted_iota(jnp.int32, (blk, blk), 1)
                  < lax.broadcasted_iota(jnp.int32, (blk, blk), 0))
        splits = {}
        for c in which:
            z = z_scr[slot * nchain + c]
            neg_abs = lax.bitcast_convert_type(lax.bitcast_convert_type(z, jnp.int32) | sign, F32)
            softplus = jnp.maximum(z, 0.0) + jnp.log(1.0 + jnp.exp2(neg_abs)) * LOG2E
            if c // 2 == diag_sub:
                softplus = jnp.where(causal, softplus, 0.0)
            hi = softplus.astype(BF16)
            lo = (softplus - hi.astype(F32)).astype(BF16)
            splits[c] = jnp.concatenate([hi, lo], axis=1)
        sums = {}
        for c in which:
            if beside is not None:
                beside(c)
            sums[c] = jnp.dot(splits[c], tri_ref[...], preferred_element_type=F32)
        for c in which:
            carry = carry_scr[c]
            w = jnp.exp2((z_scr[slot * nchain + c] - sums[c]) + carry[:, 0:1])
            if c // 2 == diag_sub:
                w = jnp.where(causal, w, 0.0)
            w_scr[c] = w.astype(BF16)
            carry_scr[c] = carry - sums[c][:, 0:LANES]

    def accumulate(which, kb):
        vblk = v_ref[0, rows(kb), :]
        for c in which:
            acc_scr[c] = acc_scr[c] + jnp.dot(w_scr[c], vblk, preferred_element_type=F32)

    everyone = list(range(nchain))
    top = nsub * qi + nsub - 1
    first = nsub * qi - 1
    diag = [[c for c in everyone if c // 2 >= nsub - 1 - i] for i in range(nsub)]
    for i in range(nsub):
        scores(top - i, diag[i], 2 + i)
    scores(jnp.maximum(first, 0), everyone, 0)
    for i in range(nsub):
        def previous(c, i=i):
            if i > 0 and c in diag[i - 1]:
                accumulate([c], top - (i - 1))

        weights(diag[i], 2 + i, nsub - 1 - i, beside=previous)

    def blocks(kb0, count):
        for j in range(count):
            kb = kb0 - j

            def neighbours(c, kb=kb, j=j):
                accumulate([c], kb + 1)
                scores(jnp.maximum(kb - 1, 0), [c], 1 - j % 2)

            weights(everyone, j % 2, -1, beside=neighbours)

    per = SB_BLOCKS_PER_TRIP
    trips = nsub * qi // per

    def step(i, carry):
        blocks(first - per * i, per)
        return carry

    lax.fori_loop(0, trips, step, 0)
    for rest in range(2, per, 2):
        @pl.when(nsub * qi - trips * per == rest)
        def _(rest=rest):
            blocks(first - per * trips, rest)
    accumulate(everyone, 0)
    for sub in range(nsub):
        o_ref[0, sub * blk:(sub + 1) * blk, :] = jnp.where(head0, acc_scr[2 * sub], acc_scr[2 * sub + 1])


def _stick(q_s, k_s, v_s):
    b, s, _ = q_s.shape
    blk = SB_BLOCK
    qrows = SB_QUERY_ROWS
    nchain = 2 * qrows // blk
    tri = np.tril(np.ones((blk, blk), np.float32))
    tri2 = jnp.asarray(np.concatenate([tri, tri], axis=0), BF16)
    full = pl.BlockSpec((1, s, LANES), lambda bi, p, i: (bi, 0, p))
    return pl.pallas_call(
        _stick_kernel,
        grid=(b, D_SB // LANES, s // qrows),
        in_specs=[pl.BlockSpec((1, qrows, LANES), lambda bi, p, i: (bi, i, p)), full, full,
                  pl.BlockSpec((2 * blk, blk), lambda bi, p, i: (0, 0))],
        out_specs=pl.BlockSpec((1, qrows, LANES), lambda bi, p, i: (bi, i, p)),
        out_shape=jax.ShapeDtypeStruct((b, s, D_SB), F32),
        scratch_shapes=[pltpu.VMEM((nchain, blk, LANES), BF16),
                        pltpu.VMEM(((2 + qrows // blk) * nchain, blk, blk), F32),
                        pltpu.VMEM((nchain, blk, blk), BF16),
                        pltpu.VMEM((nchain, blk, LANES), F32),
                        pltpu.VMEM((nchain, blk, LANES), F32)],
        compiler_params=_cparams(("arbitrary", "arbitrary", "arbitrary")),
        name="stick",
    )(q_s, k_s, v_s, tri2)


def _postmix_kernel(x_ref, od_ref, os_ref, gd_ref, gs_ref, wout_ref, gate_ref, shift_ref, scale_ref,
                    gffn_ref, wr_ref, tril_ref, triu_ref,
                    x1_ref, h2_ref, route_ref, cnt_ref, base_ref, carry_scr):
    tm = POST_ROWS

    @pl.when(jnp.logical_and(pl.program_id(0) == 0, pl.program_id(1) == 0))
    def _():
        carry_scr[...] = jnp.zeros_like(carry_scr)

    big = jnp.int32(LANES)
    lmax = lambda v: jnp.max(v, axis=-1, keepdims=True)
    lmin = lambda v: jnp.min(v, axis=-1, keepdims=True)
    lsum = lambda v: jnp.sum(v, axis=-1, keepdims=True)
    wr = wr_ref[...]
    w_hi = wr.astype(BF16)
    w_lo = (wr - w_hi.astype(F32)).astype(BF16)
    w3 = jnp.concatenate([w_hi, w_hi, w_lo], axis=0)

    def route_rows(rows):
        n = rows.stop - rows.start
        mixed = jnp.concatenate([_rms(od_ref[0, rows, :], gd_ref[...]), _rms(os_ref[0, rows, :], gs_ref[...])],
                                axis=-1)
        proj = jnp.dot(mixed.astype(BF16), wout_ref[...], preferred_element_type=F32)
        x1 = x_ref[0, rows, :] + gate_ref[0] * proj
        x1_ref[0, rows, :] = x1
        h2 = _rms(x1, gffn_ref[...]) * (1.0 + scale_ref[0]) + shift_ref[0]
        h_hi = h2.astype(BF16)
        h_lo = (h2 - h_hi.astype(F32)).astype(BF16)
        h2_ref[0, rows, :] = h_hi
        logits = jnp.dot(jnp.concatenate([h_hi, h_lo, h_hi], axis=1), w3, preferred_element_type=F32)

        lane = lax.broadcasted_iota(jnp.int32, (n, LANES), 1)
        gmask = lane < N_GROUPS
        gl = jnp.where(gmask, logits, NEG_INF)
        gmx = lmax(gl)
        group = lmin(jnp.where(jnp.logical_and(gmask, gl == gmx), lane, big))
        group_gate = 1.0 / lsum(jnp.exp(gl - gmx))
        lo = ROUTE_LANE0 + group * EXPERTS_PER_GROUP
        emask = jnp.logical_and(lane >= lo, lane < lo + EXPERTS_PER_GROUP)
        el = jnp.where(emask, logits, NEG_INF)
        l1 = lmax(el)
        i1 = lmin(jnp.where(el == l1, lane, big))
        el2 = jnp.where(lane == i1, NEG_INF, el)
        l2 = lmax(el2)
        i2 = lmin(jnp.where(el2 == l2, lane, big))
        r = jnp.exp(l2 - l1)
        return i1, i2, group_gate / (1.0 + r), group_gate * r / (1.0 + r)

    hm = tm // POST_ROW_GROUPS
    parts = [route_rows(slice(h * hm, (h + 1) * hm)) for h in range(POST_ROW_GROUPS)]
    i1, i2, w1, w2 = (jnp.concatenate([p[k] for p in parts], axis=0) for k in range(4))
    lane = lax.broadcasted_iota(jnp.int32, (tm, LANES), 1)

    is1 = lane == i1
    is2 = lane == i2
    oh = jnp.where(is1, 1.0, jnp.where(is2, 1.0, 0.0))
    earlier = jnp.dot(tril_ref[...], oh.astype(BF16), preferred_element_type=F32)
    runs = jnp.floor((jnp.sum(oh, axis=0, keepdims=True) + (SUBLANES - 1.0)) * (1.0 / SUBLANES))
    run_off = jnp.dot(jnp.broadcast_to(runs, (SUBLANES, LANES)).astype(BF16), triu_ref[...],
                      preferred_element_type=F32)[0:1]
    pos = earlier + run_off * SUBLANES
    slot1 = lsum(jnp.where(is1, pos, 0.0))
    slot2 = lsum(jnp.where(is2, pos, 0.0))
    cnt = runs * SUBLANES
    cnt_ref[0] = cnt
    base_ref[0] = carry_scr[...]
    carry_scr[...] = carry_scr[...] + cnt

    route_ref[0] = jnp.where(lane == 0, slot1, jnp.where(lane == 1, slot2,
                                                         jnp.where(lane == 2, w1, jnp.where(lane == 3, w2, 0.0))))


def _postmix(x, o_dil, o_sb, g_dil, g_sb, w_out_bf16, gate, shift, scale, g_ffn, w_router):
    b, s, d = x.shape
    tm = POST_ROWS
    nt = s // tm
    tril = jnp.asarray(np.tril(np.ones((tm, tm), np.float32), -1), BF16)
    triu = jnp.asarray(np.triu(np.ones((LANES, LANES), np.float32), 1), BF16)
    row = lambda w: pl.BlockSpec((1, tm, w), lambda bi, i: (bi, i, 0))
    vec = lambda w: pl.BlockSpec((1, w), lambda bi, i: (0, 0))
    mod_spec = pl.BlockSpec((1, 1, d), lambda bi, i: (bi, 0, 0))
    tile_vec = pl.BlockSpec((1, 1, LANES), lambda bi, i: (bi * nt + i, 0, 0))
    return pl.pallas_call(
        _postmix_kernel,
        grid=(b, nt),
        in_specs=[row(d), row(D_DIL), row(D_SB), vec(D_DIL), vec(D_SB),
                  pl.BlockSpec((d, d), lambda bi, i: (0, 0)),
                  mod_spec, mod_spec, mod_spec, vec(d),
                  pl.BlockSpec((d, LANES), lambda bi, i: (0, 0)),
                  pl.BlockSpec((tm, tm), lambda bi, i: (0, 0)),
                  pl.BlockSpec((LANES, LANES), lambda bi, i: (0, 0))],
        out_specs=[row(d), row(d), row(LANES), tile_vec, tile_vec],
        out_shape=[jax.ShapeDtypeStruct((b, s, d), F32),
                   jax.ShapeDtypeStruct((b, s, d), BF16),
                   jax.ShapeDtypeStruct((b, s, LANES), F32),
                   jax.ShapeDtypeStruct((b * nt, 1, LANES), F32),
                   jax.ShapeDtypeStruct((b * nt, 1, LANES), F32)],
        scratch_shapes=[pltpu.VMEM((1, LANES), F32)],
        compiler_params=_cparams(("arbitrary", "arbitrary")),
        name="postmix",
    )(x, o_dil, o_sb, g_dil.reshape(1, -1), g_sb.reshape(1, -1), w_out_bf16, gate, shift, scale,
      g_ffn.reshape(1, d), w_router, tril, triu)


def _for_each_run_piece(tile, start_ref, cnt_ref, base_ref, fn):
    def body(e, off):
        c = cnt_ref[tile * N_EXPERTS + e]
        sorted0 = start_ref[e] + base_ref[tile * N_EXPERTS + e]
        for k in range(SUBLANES.bit_length() - 1, POST_ROWS.bit_length()):
            p = 1 << k

            @pl.when((c & p) != 0)
            def _(p=p):
                done = c - (c & (2 * p - 1))
                fn(pl.multiple_of(off + done, SUBLANES), pl.multiple_of(sorted0 + done, SUBLANES), p)
        return off + c

    return lax.fori_loop(0, N_EXPERTS, body, 0)


def _wait_rows(total, piece):
    for k in range(SUBLANES.bit_length() - 1, LOCAL_ROWS.bit_length()):
        p = 1 << k

        @pl.when((total & p) != 0)
        def _(p=p):
            piece(0, 0, p).wait()


def _sort_kernel(start_ref, cnt_ref, base_ref, rows_ref, h2_ref, route_ref, buf_ref, xs_scr, sem):
    tm = POST_ROWS
    lt = LOCAL_ROWS
    d = h2_ref.shape[2]
    tile = pl.program_id(0)
    slot = tile % 2

    def piece(slot, lrow, srow, rows):
        return pltpu.make_async_copy(xs_scr.at[slot, pl.ds(lrow, rows)], buf_ref.at[pl.ds(srow, rows)],
                                     sem.at[slot])

    def drain(tile, slot):
        _wait_rows(rows_ref[tile], functools.partial(piece, slot))

    @pl.when(tile >= 2)
    def _():
        drain(tile - 2, slot)

    lane = lax.broadcasted_iota(jnp.int32, (tm, LANES), 1)
    route = route_ref[0]
    w1 = jnp.sum(jnp.where(lane == 2, route, 0.0), axis=-1, keepdims=True)
    w2 = jnp.sum(jnp.where(lane == 3, route, 0.0), axis=-1, keepdims=True)

    def pieces(w):
        hi, mid, lw = _split3(w)
        return jnp.where(lane == 0, hi.astype(F32),
                         jnp.where(lane == 1, mid.astype(F32),
                                   jnp.where(lane == 2, lw.astype(F32), 0.0))).astype(BF16)

    route_t = route.T
    s1 = route_t[0:1, :].astype(jnp.int32)
    s2 = route_t[1:2, :].astype(jnp.int32)
    row = lax.broadcasted_iota(jnp.int32, (lt, tm), 0)
    p1 = jnp.where(row == s1, 1.0, 0.0)
    p2 = jnp.where(row == s2, 1.0, 0.0)
    xs_scr[slot, :, 0:d] = jnp.dot((p1 + p2).astype(BF16), h2_ref[0], preferred_element_type=F32)
    xs_scr[slot, :, d:] = (jnp.dot(p1.astype(BF16), pieces(w1), preferred_element_type=F32)
                           + jnp.dot(p2.astype(BF16), pieces(w2), preferred_element_type=F32))
    _for_each_run_piece(tile, start_ref, cnt_ref, base_ref, lambda *a: piece(slot, *a).start())

    last = pl.num_programs(0) - 1

    @pl.when(jnp.logical_and(tile == last, tile >= 1))
    def _():
        drain(tile - 1, 1 - slot)

    @pl.when(tile == last)
    def _():
        drain(tile, slot)


def _dispatch(pad_start, cnt, base, tile_rows, h2, route, cap):
    b, s, d = h2.shape
    tm = POST_ROWS
    nt = s // tm
    return pl.pallas_call(
        _sort_kernel,
        grid_spec=pltpu.PrefetchScalarGridSpec(
            num_scalar_prefetch=4, grid=(b * nt,),
            in_specs=[pl.BlockSpec((1, tm, d), lambda t, *_: (t // nt, t % nt, 0)),
                      pl.BlockSpec((1, tm, LANES), lambda t, *_: (t // nt, t % nt, 0))],
            out_specs=pl.BlockSpec(memory_space=pl.ANY),
            scratch_shapes=[pltpu.VMEM((2, LOCAL_ROWS, d + LANES), F32), pltpu.SemaphoreType.DMA((2,))]),
        out_shape=jax.ShapeDtypeStruct((cap, d + LANES), F32),
        compiler_params=_cparams(("arbitrary",)),
        name="dispatch",
    )(pad_start, cnt, base, tile_rows, h2, route)


def _expert_kernel(be_ref, live_ref, nlive_ref, x_ref, wg_ref, wu_ref, wd_ref, y_ref):
    del be_ref
    d = y_ref.shape[1]
    live = live_ref[pl.program_id(0)]

    @pl.when(pl.program_id(0) < nlive_ref[0])
    def _():
        keep = lax.broadcasted_iota(jnp.int32, (x_ref.shape[0], 1), 0) < live
        x = jnp.where(keep, x_ref[...], 0.0)
        xb = x[:, 0:d].astype(BF16)
        weight = jnp.sum(x[:, d:], axis=-1, keepdims=True)
        gate = jnp.dot(xb, wg_ref[0].astype(BF16), preferred_element_type=F32)
        up = jnp.dot(xb, wu_ref[0].astype(BF16), preferred_element_type=F32)
        act = gate / (1.0 + jnp.exp(-gate)) * up
        y_ref[...] = jnp.dot(act.astype(BF16), wd_ref[0].astype(BF16), preferred_element_type=F32) * weight


def _experts(block_expert, live_rows, n_live, buf, wg, wu, wd):
    cap, dw = buf.shape
    d, f = wg.shape[1], wg.shape[2]
    bm = EXPERT_ROWS

    def blk(i, nl):
        return jnp.minimum(i, nl[0] - 1)

    return pl.pallas_call(
        _expert_kernel,
        grid_spec=pltpu.PrefetchScalarGridSpec(
            num_scalar_prefetch=3, grid=(cap // bm,),
            in_specs=[pl.BlockSpec((bm, dw), lambda i, be, lv, nl: (blk(i, nl), 0)),
                      pl.BlockSpec((1, d, f), lambda i, be, lv, nl: (be[blk(i, nl)], 0, 0)),
                      pl.BlockSpec((1, d, f), lambda i, be, lv, nl: (be[blk(i, nl)], 0, 0)),
                      pl.BlockSpec((1, f, d), lambda i, be, lv, nl: (be[blk(i, nl)], 0, 0))],
            out_specs=pl.BlockSpec((bm, d), lambda i, be, lv, nl: (blk(i, nl), 0))),
        out_shape=jax.ShapeDtypeStruct((cap, d), F32),
        compiler_params=_cparams(("arbitrary",)),
        name="experts",
    )(block_expert, live_rows, n_live, buf, wg, wu, wd)


def _combine_kernel(start_ref, cnt_ref, base_ref, rows_ref, x1_ref, route_ref, gate_ref, g_ref, y_hbm_ref,
                    o_ref, y_scr, sem):
    tm = POST_ROWS
    lt = LOCAL_ROWS
    tile = pl.program_id(0)
    slot = tile % 2

    def piece(slot, lrow, srow, rows):
        return pltpu.make_async_copy(y_hbm_ref.at[pl.ds(srow, rows)], y_scr.at[slot, pl.ds(lrow, rows)],
                                     sem.at[slot])

    def fetch(tile, slot):
        _for_each_run_piece(tile, start_ref, cnt_ref, base_ref, lambda *a: piece(slot, *a).start())

    @pl.when(tile == 0)
    def _():
        fetch(tile, slot)

    @pl.when(tile + 1 < pl.num_programs(0))
    def _():
        fetch(tile + 1, 1 - slot)

    lane = lax.broadcasted_iota(jnp.int32, (tm, LANES), 1)
    route = route_ref[0]
    s1 = jnp.sum(jnp.where(lane == 0, route, 0.0), axis=-1, keepdims=True).astype(jnp.int32)
    s2 = jnp.sum(jnp.where(lane == 1, route, 0.0), axis=-1, keepdims=True).astype(jnp.int32)
    col = lax.broadcasted_iota(jnp.int32, (tm, lt), 1)
    pick = jnp.where(col == s1, 1.0, jnp.where(col == s2, 1.0, 0.0)).astype(BF16)
    used = rows_ref[tile]
    _wait_rows(used, functools.partial(piece, slot))
    live = lax.broadcasted_iota(jnp.int32, (lt, 1), 0) < used
    yv = jnp.where(live, y_scr[slot], 0.0)
    hi = yv.astype(BF16)
    lo = (yv - hi.astype(F32)).astype(BF16)
    y = jnp.dot(jnp.concatenate([pick, pick], axis=1), jnp.concatenate([hi, lo], axis=0),
                preferred_element_type=F32)
    o_ref[0] = _rms(x1_ref[0] + gate_ref[0] * y, g_ref[...])


def _combine(pad_start, cnt, base, tile_rows, x1, y_sorted, route, gate, g_final):
    b, s, d = x1.shape
    tm = POST_ROWS
    nt = s // tm
    return pl.pallas_call(
        _combine_kernel,
        grid_spec=pltpu.PrefetchScalarGridSpec(
            num_scalar_prefetch=4, grid=(b * nt,),
            in_specs=[pl.BlockSpec((1, tm, d), lambda t, *_: (t // nt, t % nt, 0)),
                      pl.BlockSpec((1, tm, LANES), lambda t, *_: (t // nt, t % nt, 0)),
                      pl.BlockSpec((1, 1, d), lambda t, *_: (t // nt, 0, 0)),
                      pl.BlockSpec((1, d), lambda t, *_: (0, 0)),
                      pl.BlockSpec(memory_space=pl.ANY)],
            out_specs=pl.BlockSpec((1, tm, d), lambda t, *_: (t // nt, t % nt, 0)),
            scratch_shapes=[pltpu.VMEM((2, LOCAL_ROWS, d), F32), pltpu.SemaphoreType.DMA((2,))]),
        out_shape=jax.ShapeDtypeStruct((b, s, d), F32),
        compiler_params=_cparams(("arbitrary",)),
        name="combine",
    )(pad_start, cnt, base, tile_rows, x1, route, gate, g_final.reshape(1, d), y_sorted)


def kernel(x, c, w_ada, b_ada, g_mix, w_in, g_dil_out, g_sb_out, w_out, g_ffn,
           w_group, w_expert, w_gate, w_up, w_down, g_final):
    b, s, d = x.shape
    depth = w_ada.shape[0]
    assert s % DIL_UNIT == 0 and d == D_DIL + D_SB
    assert depth == 1, "the final rmsnorm is fused into the last layer's combine step"
    n = b * s
    ntiles = n // POST_ROWS
    bias = jnp.asarray(_dilated_bias())
    for layer in range(depth):
        mod = _ada(c, w_ada[layer], b_ada[layer])
        shift_mix, scale_mix, gate_mix, shift_ffn, scale_ffn, gate_ffn = (
            m.reshape(b, 1, d) for m in jnp.split(mod, 6, axis=-1))

        qkv_d, q_s, k_s, v_s = _premix(x, shift_mix, scale_mix, g_mix[layer], w_in[layer].astype(BF16))
        o_dil = _dilated(qkv_d, bias)
        o_sb = _stick(q_s, k_s, v_s)

        w_router = jnp.concatenate(
            [w_group[layer], w_expert[layer],
             jnp.zeros((d, LANES - N_GROUPS - N_EXPERTS), F32)], axis=1)
        x1, h2, route, cnt, base = _postmix(
            x, o_dil, o_sb, g_dil_out[layer], g_sb_out[layer], w_out[layer].astype(BF16),
            gate_mix, shift_ffn, scale_ffn, g_ffn[layer], w_router)

        bm = EXPERT_ROWS
        cnt = cnt[:, 0, ROUTE_LANE0:ROUTE_LANE0 + N_EXPERTS].astype(jnp.int32)
        base = base[:, 0, ROUTE_LANE0:ROUTE_LANE0 + N_EXPERTS].astype(jnp.int32)
        total = base[-1] + cnt[-1]
        tile_rows = jnp.sum(cnt, axis=1)
        cnt = cnt.reshape(-1)
        base = base.reshape(-1)
        padded = (total + bm - 1) // bm * bm
        pad_end = jnp.cumsum(padded)
        pad_start = (pad_end - padded).astype(jnp.int32)
        cap = -(-(2 * n + (SUBLANES - 1) * N_EXPERTS * ntiles) // bm) * bm + N_EXPERTS * bm
        n_blocks = cap // bm
        block_expert = jnp.minimum(
            jnp.sum(pad_end[None, :] <= (jnp.arange(n_blocks) * bm)[:, None], axis=1),
            N_EXPERTS - 1).astype(jnp.int32)
        live_rows = jnp.clip((pad_start + total)[block_expert] - jnp.arange(n_blocks) * bm, 0, bm).astype(jnp.int32)

        buf = _dispatch(pad_start, cnt, base, tile_rows, h2, route, cap)
        n_live = (pad_end[-1:] // bm).astype(jnp.int32)
        y_sorted = _experts(block_expert, live_rows, n_live, buf, w_gate[layer], w_up[layer], w_down[layer])
        x = _combine(pad_start, cnt, base, tile_rows, x1, y_sorted, route, gate_ffn, g_final)
    return x
```

```python
import functools

import numpy as np
import jax
import jax.numpy as jnp
from jax import lax
from jax.experimental import pallas as pl
from jax.experimental.pallas import tpu as pltpu

HEAD_DIM = 64
N_HEADS_DIL = 8
N_HEADS_SB = 8
D_DIL = N_HEADS_DIL * HEAD_DIM
D_SB = N_HEADS_SB * HEAD_DIM
DILATION_PATTERNS = ((128, 1), (512, 4), (2048, 16))
N_GROUPS = 4
EXPERTS_PER_GROUP = 8
N_EXPERTS = N_GROUPS * EXPERTS_PER_GROUP
NORM_EPS = 1e-6

LANES = 128
SUBLANES = 8
DIL_STEPS = 128
DIL_UNIT = 2048
DIL_TILES_PER_TRIP = 8
SB_BLOCK = 256
SB_QUERY_ROWS = 1024
SB_BLOCKS_PER_TRIP = 2
PRE_ROWS = 512
POST_ROWS = 512
POST_ROW_GROUPS = 2
LOCAL_ROWS = -(-(2 * POST_ROWS + (SUBLANES - 1) * N_EXPERTS) // LANES) * LANES
EXPERT_ROWS = 512
ROUTE_LANE0 = N_GROUPS
VMEM_LIMIT = 56 * 1024 * 1024

F32 = jnp.float32
BF16 = jnp.bfloat16
NEG_INF = float("-inf")
LOG2E = 1.4426950408889634


def _cparams(sem):
    return pltpu.CompilerParams(dimension_semantics=sem, vmem_limit_bytes=VMEM_LIMIT)


def _rms(v, g):
    return v * lax.rsqrt(jnp.mean(v * v, axis=-1, keepdims=True) + NORM_EPS) * g


def _split3(v):
    hi = v.astype(BF16)
    r = v - hi.astype(F32)
    mid = r.astype(BF16)
    lo = (r - mid.astype(F32)).astype(BF16)
    return hi, mid, lo


def _ada_kernel(c_ref, w_ref, b_ref, o_ref):
    c = c_ref[...]
    cond = c / (1.0 + jnp.exp(-c))
    o_ref[...] = jnp.dot(cond, w_ref[...], precision=lax.Precision.HIGHEST,
                         preferred_element_type=F32) + b_ref[...]


def _ada(c, w_ada, b_ada):
    b, d = c.shape
    n = w_ada.shape[1]
    return pl.pallas_call(
        _ada_kernel,
        grid=(n // d,),
        in_specs=[pl.BlockSpec((b, d), lambda j: (0, 0)),
                  pl.BlockSpec((d, d), lambda j: (0, j)),
                  pl.BlockSpec((1, d), lambda j: (0, j))],
        out_specs=pl.BlockSpec((b, d), lambda j: (0, j)),
        out_shape=jax.ShapeDtypeStruct((b, n), F32),
        compiler_params=_cparams(("arbitrary",)),
        name="ada",
    )(c, w_ada, b_ada.reshape(1, n))


def _premix_kernel(x_ref, shift_ref, scale_ref, g_ref, w_ref, qkvd_ref, qs_ref, ks_ref, vs_ref):
    h = _rms(x_ref[0], g_ref[...]) * (1.0 + scale_ref[0]) + shift_ref[0]
    hb = h.astype(BF16)
    scale = HEAD_DIM ** -0.5 * LOG2E
    assert D_DIL == D_SB
    for j in range(6):
        cols = slice(j * D_DIL, (j + 1) * D_DIL)
        r = jnp.dot(hb, w_ref[:, cols], preferred_element_type=F32)
        if j == 0:
            qkvd_ref[0, :, cols] = r * scale
        elif j < 3:
            qkvd_ref[0, :, cols] = r
        elif j == 3:
            qs_ref[0] = (r * scale).astype(BF16)
        elif j == 4:
            ks_ref[0] = r.astype(BF16)
        else:
            vs_ref[0] = r.astype(BF16)


def _premix(x, shift, scale, g_mix, w_in_bf16):
    b, s, d = x.shape
    tm = PRE_ROWS
    mod_spec = pl.BlockSpec((1, 1, d), lambda bi, i: (bi, 0, 0))
    sb_spec = pl.BlockSpec((1, tm, D_SB), lambda bi, i: (bi, i, 0))
    return pl.pallas_call(
        _premix_kernel,
        grid=(b, s // tm),
        in_specs=[pl.BlockSpec((1, tm, d), lambda bi, i: (bi, i, 0)),
                  mod_spec, mod_spec,
                  pl.BlockSpec((1, d), lambda bi, i: (0, 0)),
                  pl.BlockSpec((d, 3 * (D_DIL + D_SB)), lambda bi, i: (0, 0))],
        out_specs=[pl.BlockSpec((1, tm, 3 * D_DIL), lambda bi, i: (bi, i, 0)),
                   sb_spec, sb_spec, sb_spec],
        out_shape=[jax.ShapeDtypeStruct((b, s, 3 * D_DIL), F32),
                   jax.ShapeDtypeStruct((b, s, D_SB), BF16),
                   jax.ShapeDtypeStruct((b, s, D_SB), BF16),
                   jax.ShapeDtypeStruct((b, s, D_SB), BF16)],
        compiler_params=_cparams(("arbitrary", "arbitrary")),
        name="premix",
    )(x, shift, scale, g_mix.reshape(1, d), w_in_bf16)


def _dilated_bias():
    n = DIL_STEPS
    slopes = np.array([2.0 ** (-8.0 * (i + 1) / N_HEADS_DIL) for i in range(N_HEADS_DIL)], dtype=np.float32)
    steps = np.arange(n)[:, None] + n - np.arange(2 * n)[None, :]
    valid = (steps >= 0) & (steps <= n)
    out = []
    for _, dilation in DILATION_PATTERNS:
        bias = -slopes[:, None, None] * (steps * dilation).astype(np.float32)[None]
        out.append(np.where(valid[None], bias.astype(np.float64) * LOG2E, -np.inf).astype(np.float32))
    return np.stack(out)


def _dil_kernel(q_ref, kc_ref, kp_ref, vc_ref, vp_ref, bias_ref, o_ref,
                kext, vext, u_scr, m_scr, l_scr):
    n = DIL_STEPS
    g = pl.program_id(1)
    kext[0:DIL_UNIT, :] = kp_ref[0]
    kext[DIL_UNIT:2 * DIL_UNIT, :] = kc_ref[0]
    vext[0:DIL_UNIT, :] = vp_ref[0]
    vext[DIL_UNIT:2 * DIL_UNIT, :] = vc_ref[0]
    lane = lax.broadcasted_iota(jnp.int32, (n, LANES), 1)
    head0 = lane < HEAD_DIM
    col = lax.broadcasted_iota(jnp.int32, (n, 2 * n), 1)

    for p, (_, dil) in enumerate(DILATION_PATTERNS):
        unit = n * dil

        def tiles(it, carry, p=p, dil=dil, unit=unit):
            rows_of, vvs, deads, ss = [], [], [], []
            for t in range(DIL_TILES_PER_TRIP):
                ti = it * DIL_TILES_PER_TRIP + t
                j = ti // dil
                r = ti % dil
                qstart = j * unit + r
                kstart = DIL_UNIT + qstart - unit
                if dil == 1:
                    rows_of.append(pl.ds(qstart, n))
                    krows = pl.ds(kstart, 2 * n)
                else:
                    rows_of.append(pl.ds(qstart, n, stride=dil))
                    krows = pl.ds(kstart, 2 * n, stride=dil)
                q = q_ref[0, rows_of[t], :]
                kk = kext[krows, :].astype(BF16)
                vvs.append(vext[krows, :].astype(BF16))
                deads.append(jnp.where(jnp.logical_and(g == 0, j == 0), n, 0))
                for h in range(2):
                    qh = jnp.where(head0 if h == 0 else jnp.logical_not(head0), q, 0.0).astype(BF16)
                    ss.append(lax.dot_general(qh, kk, (((1,), (1,)), ((), ())), preferred_element_type=F32))
            ms, ls, pes = [], [], []
            for t in range(DIL_TILES_PER_TRIP):
                for h in range(2):
                    logits = jnp.where(col < deads[t], NEG_INF, ss[2 * t + h] + bias_ref[p, h])
                    m = jnp.max(logits, axis=-1, keepdims=True)
                    pe = jnp.exp2(logits - m)
                    ls.append(jnp.sum(pe, axis=-1, keepdims=True))
                    ms.append(m)
                    pes.append(pe.astype(BF16))
            us = [jnp.dot(pes[2 * t + h], vvs[t], preferred_element_type=F32)
                  for t in range(DIL_TILES_PER_TRIP) for h in range(2)]
            for t in range(DIL_TILES_PER_TRIP):
                u_scr[p, rows_of[t], :] = jnp.where(head0, us[2 * t], us[2 * t + 1])
                m_scr[p, rows_of[t], :] = jnp.where(head0, ms[2 * t], ms[2 * t + 1])
                l_scr[p, rows_of[t], :] = jnp.where(head0, ls[2 * t], ls[2 * t + 1])
            return carry

        lax.fori_loop(0, DIL_UNIT // n // DIL_TILES_PER_TRIP, tiles, 0)

    def merge(i, carry):
        rows = pl.ds(pl.multiple_of(i * n, n), n)
        m0, m1, m2 = m_scr[0, rows, :], m_scr[1, rows, :], m_scr[2, rows, :]
        mx = jnp.maximum(jnp.maximum(m0, m1), m2)
        w0, w1, w2 = jnp.exp2(m0 - mx), jnp.exp2(m1 - mx), jnp.exp2(m2 - mx)
        num = w0 * u_scr[0, rows, :] + w1 * u_scr[1, rows, :] + w2 * u_scr[2, rows, :]
        den = w0 * l_scr[0, rows, :] + w1 * l_scr[1, rows, :] + w2 * l_scr[2, rows, :]
        o_ref[0, rows, :] = num / den
        return carry

    lax.fori_loop(0, DIL_UNIT // n, merge, 0)


def _dilated(qkv_d, bias):
    b, s, _ = qkv_d.shape
    u = DIL_UNIT
    npair = D_DIL // LANES
    cur = lambda off: pl.BlockSpec((1, u, LANES), lambda bi, g, p: (bi, g, off + p))
    prev = lambda off: pl.BlockSpec((1, u, LANES), lambda bi, g, p: (bi, jnp.maximum(g - 1, 0), off + p))
    return pl.pallas_call(
        _dil_kernel,
        grid=(b, s // u, npair),
        in_specs=[cur(0), cur(npair), prev(npair), cur(2 * npair), prev(2 * npair),
                  pl.BlockSpec((3, 2, DIL_STEPS, 2 * DIL_STEPS), lambda bi, g, p: (0, p, 0, 0))],
        out_specs=pl.BlockSpec((1, u, LANES), lambda bi, g, p: (bi, g, p)),
        out_shape=jax.ShapeDtypeStruct((b, s, D_DIL), F32),
        scratch_shapes=[pltpu.VMEM((2 * u, LANES), F32), pltpu.VMEM((2 * u, LANES), F32),
                        pltpu.VMEM((3, u, LANES), F32), pltpu.VMEM((3, u, LANES), F32),
                        pltpu.VMEM((3, u, LANES), F32)],
        compiler_params=_cparams(("arbitrary", "arbitrary", "arbitrary")),
        name="dilated",
    )(qkv_d, qkv_d, qkv_d, qkv_d, qkv_d, bias)


def _stick_kernel(q_ref, k_ref, v_ref, tri_ref, o_ref,
                  qh_scr, z_scr, w_scr, acc_scr, carry_scr):
    blk = SB_BLOCK
    nsub = SB_QUERY_ROWS // blk
    assert nsub % 2 == 0
    nchain = 2 * nsub
    qi = pl.program_id(2)
    lane = lax.broadcasted_iota(jnp.int32, (blk, LANES), 1)
    head0 = lane < HEAD_DIM
    for sub in range(nsub):
        q = q_ref[0, sub * blk:(sub + 1) * blk, :]
        zero = jnp.zeros_like(q)
        qh_scr[2 * sub] = jnp.where(head0, q, zero)
        qh_scr[2 * sub + 1] = jnp.where(head0, zero, q)
    acc_scr[...] = jnp.zeros_like(acc_scr)
    carry_scr[...] = jnp.zeros_like(carry_scr)
    sign = jnp.int32(-2 ** 31)

    def rows(kb):
        return pl.ds(pl.multiple_of(kb * blk, blk), blk)

    def scores(kb, which, slot):
        kblk = k_ref[0, rows(kb), :]
        for c in which:
            z_scr[slot * nchain + c] = lax.dot_general(
                qh_scr[c], kblk, (((1,), (1,)), ((), ())), preferred_element_type=F32)

    def weights(which, slot, diag_sub, beside=None):
        causal = (lax.broadcasted_iota(jnp.int32, (blk, blk), 1)
                  < lax.broadcasted_iota(jnp.int32, (blk, blk), 0))
        splits = {}
        for c in which:
            z = z_scr[slot * nchain + c]
            neg_abs = lax.bitcast_convert_type(lax.bitcast_convert_type(z, jnp.int32) | sign, F32)
            softplus = jnp.maximum(z, 0.0) + jnp.log(1.0 + jnp.exp2(neg_abs)) * LOG2E
            if c // 2 == diag_sub:
                softplus = jnp.where(causal, softplus, 0.0)
            hi = softplus.astype(BF16)
            lo = (softplus - hi.astype(F32)).astype(BF16)
            splits[c] = jnp.concatenate([hi, lo], axis=1)
        sums = {}
        for c in which:
            if beside is not None:
                beside(c)
            sums[c] = jnp.dot(splits[c], tri_ref[...], preferred_element_type=F32)
        for c in which:
            carry = carry_scr[c]
            w = jnp.exp2((z_scr[slot * nchain + c] - sums[c]) + carry[:, 0:1])
            if c // 2 == diag_sub:
                w = jnp.where(causal, w, 0.0)
            w_scr[c] = w.astype(BF16)
            carry_scr[c] = carry - sums[c][:, 0:LANES]

    def accumulate(which, kb):
        vblk = v_ref[0, rows(kb), :]
        for c in which:
            acc_scr[c] = acc_scr[c] + jnp.dot(w_scr[c], vblk, preferred_element_type=F32)

    everyone = list(range(nchain))
    top = nsub * qi + nsub - 1
    first = nsub * qi - 1
    diag = [[c for c in everyone if c // 2 >= nsub - 1 - i] for i in range(nsub)]
    for i in range(nsub):
        scores(top - i, diag[i], 2 + i)
    scores(jnp.maximum(first, 0), everyone, 0)
    for i in range(nsub):
        def previous(c, i=i):
            if i > 0 and c in diag[i - 1]:
                accumulate([c], top - (i - 1))

        weights(diag[i], 2 + i, nsub - 1 - i, beside=previous)

    def blocks(kb0, count):
        for j in range(count):
            kb = kb0 - j

            def neighbours(c, kb=kb, j=j):
                accumulate([c], kb + 1)
                scores(jnp.maximum(kb - 1, 0), [c], 1 - j % 2)

            weights(everyone, j % 2, -1, beside=neighbours)

    per = SB_BLOCKS_PER_TRIP
    trips = nsub * qi // per

    def step(i, carry):
        blocks(first - per * i, per)
        return carry

    lax.fori_loop(0, trips, step, 0)
    for rest in range(2, per, 2):
        @pl.when(nsub * qi - trips * per == rest)
        def _(rest=rest):
            blocks(first - per * trips, rest)
    accumulate(everyone, 0)
    for sub in range(nsub):
        o_ref[0, sub * blk:(sub + 1) * blk, :] = jnp.where(head0, acc_scr[2 * sub], acc_scr[2 * sub + 1])


def _stick(q_s, k_s, v_s):
    b, s, _ = q_s.shape
    blk = SB_BLOCK
    qrows = SB_QUERY_ROWS
    nchain = 2 * qrows // blk
    tri = np.tril(np.ones((blk, blk), np.float32))
    tri2 = jnp.asarray(np.concatenate([tri, tri], axis=0), BF16)
    full = pl.BlockSpec((1, s, LANES), lambda bi, p, i: (bi, 0, p))
    return pl.pallas_call(
        _stick_kernel,
        grid=(b, D_SB // LANES, s // qrows),
        in_specs=[pl.BlockSpec((1, qrows, LANES), lambda bi, p, i: (bi, i, p)), full, full,
                  pl.BlockSpec((2 * blk, blk), lambda bi, p, i: (0, 0))],
        out_specs=pl.BlockSpec((1, qrows, LANES), lambda bi, p, i: (bi, i, p)),
        out_shape=jax.ShapeDtypeStruct((b, s, D_SB), F32),
        scratch_shapes=[pltpu.VMEM((nchain, blk, LANES), BF16),
                        pltpu.VMEM(((2 + qrows // blk) * nchain, blk, blk), F32),
                        pltpu.VMEM((nchain, blk, blk), BF16),
                        pltpu.VMEM((nchain, blk, LANES), F32),
                        pltpu.VMEM((nchain, blk, LANES), F32)],
        compiler_params=_cparams(("arbitrary", "arbitrary", "arbitrary")),
        name="stick",
    )(q_s, k_s, v_s, tri2)


def _postmix_kernel(x_ref, od_ref, os_ref, gd_ref, gs_ref, wout_ref, gate_ref, shift_ref, scale_ref,
                    gffn_ref, wr_ref, tril_ref, triu_ref,
                    x1_ref, h2_ref, route_ref, cnt_ref, base_ref, carry_scr):
    tm = POST_ROWS

    @pl.when(jnp.logical_and(pl.program_id(0) == 0, pl.program_id(1) == 0))
    def _():
        carry_scr[...] = jnp.zeros_like(carry_scr)

    big = jnp.int32(LANES)
    lmax = lambda v: jnp.max(v, axis=-1, keepdims=True)
    lmin = lambda v: jnp.min(v, axis=-1, keepdims=True)
    lsum = lambda v: jnp.sum(v, axis=-1, keepdims=True)
    wr = wr_ref[...]
    w_hi = wr.astype(BF16)
    w_lo = (wr - w_hi.astype(F32)).astype(BF16)
    w3 = jnp.concatenate([w_hi, w_hi, w_lo], axis=0)

    def route_rows(rows):
        n = rows.stop - rows.start
        mixed = jnp.concatenate([_rms(od_ref[0, rows, :], gd_ref[...]), _rms(os_ref[0, rows, :], gs_ref[...])],
                                axis=-1)
        proj = jnp.dot(mixed.astype(BF16), wout_ref[...], preferred_element_type=F32)
        x1 = x_ref[0, rows, :] + gate_ref[0] * proj
        x1_ref[0, rows, :] = x1
        h2 = _rms(x1, gffn_ref[...]) * (1.0 + scale_ref[0]) + shift_ref[0]
        h_hi = h2.astype(BF16)
        h_lo = (h2 - h_hi.astype(F32)).astype(BF16)
        h2_ref[0, rows, :] = h_hi
        logits = jnp.dot(jnp.concatenate([h_hi, h_lo, h_hi], axis=1), w3, preferred_element_type=F32)

        lane = lax.broadcasted_iota(jnp.int32, (n, LANES), 1)
        gmask = lane < N_GROUPS
        gl = jnp.where(gmask, logits, NEG_INF)
        gmx = lmax(gl)
        group = lmin(jnp.where(jnp.logical_and(gmask, gl == gmx), lane, big))
        group_gate = 1.0 / lsum(jnp.exp(gl - gmx))
        lo = ROUTE_LANE0 + group * EXPERTS_PER_GROUP
        emask = jnp.logical_and(lane >= lo, lane < lo + EXPERTS_PER_GROUP)
        el = jnp.where(emask, logits, NEG_INF)
        l1 = lmax(el)
        i1 = lmin(jnp.where(el == l1, lane, big))
        el2 = jnp.where(lane == i1, NEG_INF, el)
        l2 = lmax(el2)
        i2 = lmin(jnp.where(el2 == l2, lane, big))
        r = jnp.exp(l2 - l1)
        return i1, i2, group_gate / (1.0 + r), group_gate * r / (1.0 + r)

    hm = tm // POST_ROW_GROUPS
    parts = [route_rows(slice(h * hm, (h + 1) * hm)) for h in range(POST_ROW_GROUPS)]
    i1, i2, w1, w2 = (jnp.concatenate([p[k] for p in parts], axis=0) for k in range(4))
    lane = lax.broadcasted_iota(jnp.int32, (tm, LANES), 1)

    is1 = lane == i1
    is2 = lane == i2
    oh = jnp.where(is1, 1.0, jnp.where(is2, 1.0, 0.0))
    earlier = jnp.dot(tril_ref[...], oh.astype(BF16), preferred_element_type=F32)
    runs = jnp.floor((jnp.sum(oh, axis=0, keepdims=True) + (SUBLANES - 1.0)) * (1.0 / SUBLANES))
    run_off = jnp.dot(jnp.broadcast_to(runs, (SUBLANES, LANES)).astype(BF16), triu_ref[...],
                      preferred_element_type=F32)[0:1]
    pos = earlier + run_off * SUBLANES
    slot1 = lsum(jnp.where(is1, pos, 0.0))
    slot2 = lsum(jnp.where(is2, pos, 0.0))
    cnt = runs * SUBLANES
    cnt_ref[0] = cnt
    base_ref[0] = carry_scr[...]
    carry_scr[...] = carry_scr[...] + cnt

    route_ref[0] = jnp.where(lane == 0, slot1, jnp.where(lane == 1, slot2,
                                                         jnp.where(lane == 2, w1, jnp.where(lane == 3, w2, 0.0))))


def _postmix(x, o_dil, o_sb, g_dil, g_sb, w_out_bf16, gate, shift, scale, g_ffn, w_router):
    b, s, d = x.shape
    tm = POST_ROWS
    nt = s // tm
    tril = jnp.asarray(np.tril(np.ones((tm, tm), np.float32), -1), BF16)
    triu = jnp.asarray(np.triu(np.ones((LANES, LANES), np.float32), 1), BF16)
    row = lambda w: pl.BlockSpec((1, tm, w), lambda bi, i: (bi, i, 0))
    vec = lambda w: pl.BlockSpec((1, w), lambda bi, i: (0, 0))
    mod_spec = pl.BlockSpec((1, 1, d), lambda bi, i: (bi, 0, 0))
    tile_vec = pl.BlockSpec((1, 1, LANES), lambda bi, i: (bi * nt + i, 0, 0))
    return pl.pallas_call(
        _postmix_kernel,
        grid=(b, nt),
        in_specs=[row(d), row(D_DIL), row(D_SB), vec(D_DIL), vec(D_SB),
                  pl.BlockSpec((d, d), lambda bi, i: (0, 0)),
                  mod_spec, mod_spec, mod_spec, vec(d),
                  pl.BlockSpec((d, LANES), lambda bi, i: (0, 0)),
                  pl.BlockSpec((tm, tm), lambda bi, i: (0, 0)),
                  pl.BlockSpec((LANES, LANES), lambda bi, i: (0, 0))],
        out_specs=[row(d), row(d), row(LANES), tile_vec, tile_vec],
        out_shape=[jax.ShapeDtypeStruct((b, s, d), F32),
                   jax.ShapeDtypeStruct((b, s, d), BF16),
                   jax.ShapeDtypeStruct((b, s, LANES), F32),
                   jax.ShapeDtypeStruct((b * nt, 1, LANES), F32),
                   jax.ShapeDtypeStruct((b * nt, 1, LANES), F32)],
        scratch_shapes=[pltpu.VMEM((1, LANES), F32)],
        compiler_params=_cparams(("arbitrary", "arbitrary")),
        name="postmix",
    )(x, o_dil, o_sb, g_dil.reshape(1, -1), g_sb.reshape(1, -1), w_out_bf16, gate, shift, scale,
      g_ffn.reshape(1, d), w_router, tril, triu)


def _for_each_run_piece(tile, start_ref, cnt_ref, base_ref, fn):
    def body(e, off):
        c = cnt_ref[tile * N_EXPERTS + e]
        sorted0 = start_ref[e] + base_ref[tile * N_EXPERTS + e]
        for k in range(SUBLANES.bit_length() - 1, POST_ROWS.bit_length()):
            p = 1 << k

            @pl.when((c & p) != 0)
            def _(p=p):
                done = c - (c & (2 * p - 1))
                fn(pl.multiple_of(off + done, SUBLANES), pl.multiple_of(sorted0 + done, SUBLANES), p)
        return off + c

    return lax.fori_loop(0, N_EXPERTS, body, 0)


def _wait_rows(total, piece):
    for k in range(SUBLANES.bit_length() - 1, LOCAL_ROWS.bit_length()):
        p = 1 << k

        @pl.when((total & p) != 0)
        def _(p=p):
            piece(0, 0, p).wait()


def _sort_kernel(start_ref, cnt_ref, base_ref, rows_ref, h2_ref, route_ref, buf_ref, xs_scr, sem):
    tm = POST_ROWS
    lt = LOCAL_ROWS
    d = h2_ref.shape[2]
    tile = pl.program_id(0)
    slot = tile % 2

    def piece(slot, lrow, srow, rows):
        return pltpu.make_async_copy(xs_scr.at[slot, pl.ds(lrow, rows)], buf_ref.at[pl.ds(srow, rows)],
                                     sem.at[slot])

    def drain(tile, slot):
        _wait_rows(rows_ref[tile], functools.partial(piece, slot))

    @pl.when(tile >= 2)
    def _():
        drain(tile - 2, slot)

    lane = lax.broadcasted_iota(jnp.int32, (tm, LANES), 1)
    route = route_ref[0]
    w1 = jnp.sum(jnp.where(lane == 2, route, 0.0), axis=-1, keepdims=True)
    w2 = jnp.sum(jnp.where(lane == 3, route, 0.0), axis=-1, keepdims=True)

    def pieces(w):
        hi, mid, lw = _split3(w)
        return jnp.where(lane == 0, hi.astype(F32),
                         jnp.where(lane == 1, mid.astype(F32),
                                   jnp.where(lane == 2, lw.astype(F32), 0.0))).astype(BF16)

    route_t = route.T
    s1 = route_t[0:1, :].astype(jnp.int32)
    s2 = route_t[1:2, :].astype(jnp.int32)
    row = lax.broadcasted_iota(jnp.int32, (lt, tm), 0)
    p1 = jnp.where(row == s1, 1.0, 0.0)
    p2 = jnp.where(row == s2, 1.0, 0.0)
    xs_scr[slot, :, 0:d] = jnp.dot((p1 + p2).astype(BF16), h2_ref[0], preferred_element_type=F32)
    xs_scr[slot, :, d:] = (jnp.dot(p1.astype(BF16), pieces(w1), preferred_element_type=F32)
                           + jnp.dot(p2.astype(BF16), pieces(w2), preferred_element_type=F32))
    _for_each_run_piece(tile, start_ref, cnt_ref, base_ref, lambda *a: piece(slot, *a).start())

    last = pl.num_programs(0) - 1

    @pl.when(jnp.logical_and(tile == last, tile >= 1))
    def _():
        drain(tile - 1, 1 - slot)

    @pl.when(tile == last)
    def _():
        drain(tile, slot)
        xs_scr[0] = jnp.zeros_like(xs_scr[0])

        def zeros_to(srow, rows):
            return pltpu.make_async_copy(xs_scr.at[0, pl.ds(0, rows)], buf_ref.at[pl.ds(srow, rows)], sem.at[0])

        def gaps(fn):
            def body(e, carry):
                end = start_ref[e] + base_ref[last * N_EXPERTS + e] + cnt_ref[last * N_EXPERTS + e]
                n = start_ref[e + 1] - end
                for k in range(SUBLANES.bit_length() - 1, EXPERT_ROWS.bit_length() - 1):
                    p = 1 << k

                    @pl.when((n & p) != 0)
                    def _(p=p):
                        fn(zeros_to(pl.multiple_of(end + n - (n & (2 * p - 1)), SUBLANES), p))
                return carry

            lax.fori_loop(0, N_EXPERTS, body, 0)

        def tail(fn):
            def body(i, carry):
                fn(zeros_to(pl.multiple_of(start_ref[N_EXPERTS] + i * EXPERT_ROWS, EXPERT_ROWS), EXPERT_ROWS))
                return carry

            lax.fori_loop(0, (buf_ref.shape[0] - start_ref[N_EXPERTS]) // EXPERT_ROWS, body, 0)

        for walk in (gaps, tail):
            walk(lambda cp: cp.start())
        for walk in (gaps, tail):
            walk(lambda cp: cp.wait())


def _dispatch(pad_bounds, cnt, base, tile_rows, h2, route, cap):
    b, s, d = h2.shape
    tm = POST_ROWS
    nt = s // tm
    return pl.pallas_call(
        _sort_kernel,
        grid_spec=pltpu.PrefetchScalarGridSpec(
            num_scalar_prefetch=4, grid=(b * nt,),
            in_specs=[pl.BlockSpec((1, tm, d), lambda t, *_: (t // nt, t % nt, 0)),
                      pl.BlockSpec((1, tm, LANES), lambda t, *_: (t // nt, t % nt, 0))],
            out_specs=pl.BlockSpec(memory_space=pl.ANY),
            scratch_shapes=[pltpu.VMEM((2, LOCAL_ROWS, d + LANES), F32), pltpu.SemaphoreType.DMA((2,))]),
        out_shape=jax.ShapeDtypeStruct((cap, d + LANES), F32),
        compiler_params=_cparams(("arbitrary",)),
        name="dispatch",
    )(pad_bounds, cnt, base, tile_rows, h2, route)


def _expert_kernel(be_ref, nlive_ref, x_ref, wg_ref, wu_ref, wd_ref, y_ref):
    del be_ref
    d = y_ref.shape[1]

    @pl.when(pl.program_id(0) < nlive_ref[0])
    def _():
        xb = x_ref[:, 0:d].astype(BF16)
        weight = jnp.sum(x_ref[:, d:], axis=-1, keepdims=True)
        gate = jnp.dot(xb, wg_ref[0].astype(BF16), preferred_element_type=F32)
        up = jnp.dot(xb, wu_ref[0].astype(BF16), preferred_element_type=F32)
        act = gate / (1.0 + jnp.exp(-gate)) * up
        y_ref[...] = jnp.dot(act.astype(BF16), wd_ref[0].astype(BF16), preferred_element_type=F32) * weight

    @pl.when(pl.program_id(0) >= nlive_ref[0])
    def _():
        y_ref[...] = jnp.zeros_like(y_ref)


def _experts(block_expert, n_live, buf, wg, wu, wd):
    cap, dw = buf.shape
    d, f = wg.shape[1], wg.shape[2]
    bm = EXPERT_ROWS

    def blk(i, nl):
        return jnp.minimum(i, nl[0] - 1)

    return pl.pallas_call(
        _expert_kernel,
        grid_spec=pltpu.PrefetchScalarGridSpec(
            num_scalar_prefetch=2, grid=(cap // bm,),
            in_specs=[pl.BlockSpec((bm, dw), lambda i, be, nl: (blk(i, nl), 0)),
                      pl.BlockSpec((1, d, f), lambda i, be, nl: (be[blk(i, nl)], 0, 0)),
                      pl.BlockSpec((1, d, f), lambda i, be, nl: (be[blk(i, nl)], 0, 0)),
                      pl.BlockSpec((1, f, d), lambda i, be, nl: (be[blk(i, nl)], 0, 0))],
            out_specs=pl.BlockSpec((bm, d), lambda i, be, nl: (i, 0))),
        out_shape=jax.ShapeDtypeStruct((cap, d), F32),
        compiler_params=_cparams(("arbitrary",)),
        name="experts",
    )(block_expert, n_live, buf, wg, wu, wd)


def _combine_kernel(start_ref, cnt_ref, base_ref, rows_ref, x1_ref, route_ref, gate_ref, g_ref, y_hbm_ref,
                    o_ref, y_scr, sem):
    tm = POST_ROWS
    lt = LOCAL_ROWS
    tile = pl.program_id(0)
    slot = tile % 2

    def piece(slot, lrow, srow, rows):
        return pltpu.make_async_copy(y_hbm_ref.at[pl.ds(srow, rows)], y_scr.at[slot, pl.ds(lrow, rows)],
                                     sem.at[slot])

    def fetch(tile, slot):
        _for_each_run_piece(tile, start_ref, cnt_ref, base_ref, lambda *a: piece(slot, *a).start())

    @pl.when(tile == 0)
    def _():
        fetch(tile, slot)

    @pl.when(tile + 1 < pl.num_programs(0))
    def _():
        fetch(tile + 1, 1 - slot)

    lane = lax.broadcasted_iota(jnp.int32, (tm, LANES), 1)
    route = route_ref[0]
    s1 = jnp.sum(jnp.where(lane == 0, route, 0.0), axis=-1, keepdims=True).astype(jnp.int32)
    s2 = jnp.sum(jnp.where(lane == 1, route, 0.0), axis=-1, keepdims=True).astype(jnp.int32)
    col = lax.broadcasted_iota(jnp.int32, (tm, lt), 1)
    pick = jnp.where(col == s1, 1.0, jnp.where(col == s2, 1.0, 0.0)).astype(BF16)
    used = rows_ref[tile]
    _wait_rows(used, functools.partial(piece, slot))
    live = lax.broadcasted_iota(jnp.int32, (lt, 1), 0) < used
    yv = jnp.where(live, y_scr[slot], 0.0)
    hi = yv.astype(BF16)
    lo = (yv - hi.astype(F32)).astype(BF16)
    y = jnp.dot(jnp.concatenate([pick, pick], axis=1), jnp.concatenate([hi, lo], axis=0),
                preferred_element_type=F32)
    o_ref[0] = _rms(x1_ref[0] + gate_ref[0] * y, g_ref[...])


def _combine(pad_start, cnt, base, tile_rows, x1, y_sorted, route, gate, g_final):
    b, s, d = x1.shape
    tm = POST_ROWS
    nt = s // tm
    return pl.pallas_call(
        _combine_kernel,
        grid_spec=pltpu.PrefetchScalarGridSpec(
            num_scalar_prefetch=4, grid=(b * nt,),
            in_specs=[pl.BlockSpec((1, tm, d), lambda t, *_: (t // nt, t % nt, 0)),
                      pl.BlockSpec((1, tm, LANES), lambda t, *_: (t // nt, t % nt, 0)),
                      pl.BlockSpec((1, 1, d), lambda t, *_: (t // nt, 0, 0)),
                      pl.BlockSpec((1, d), lambda t, *_: (0, 0)),
                      pl.BlockSpec(memory_space=pl.ANY)],
            out_specs=pl.BlockSpec((1, tm, d), lambda t, *_: (t // nt, t % nt, 0)),
            scratch_shapes=[pltpu.VMEM((2, LOCAL_ROWS, d), F32), pltpu.SemaphoreType.DMA((2,))]),
        out_shape=jax.ShapeDtypeStruct((b, s, d), F32),
        compiler_params=_cparams(("arbitrary",)),
        name="combine",
    )(pad_start, cnt, base, tile_rows, x1, route, gate, g_final.reshape(1, d), y_sorted)


def kernel(x, c, w_ada, b_ada, g_mix, w_in, g_dil_out, g_sb_out, w_out, g_ffn,
           w_group, w_expert, w_gate, w_up, w_down, g_final):
    b, s, d = x.shape
    depth = w_ada.shape[0]
    assert s % DIL_UNIT == 0 and d == D_DIL + D_SB
    assert depth == 1, "the final rmsnorm is fused into the last layer's combine step"
    n = b * s
    ntiles = n // POST_ROWS
    bias = jnp.asarray(_dilated_bias())
    for layer in range(depth):
        mod = _ada(c, w_ada[layer], b_ada[layer])
        shift_mix, scale_mix, gate_mix, shift_ffn, scale_ffn, gate_ffn = (
            m.reshape(b, 1, d) for m in jnp.split(mod, 6, axis=-1))

        qkv_d, q_s, k_s, v_s = _premix(x, shift_mix, scale_mix, g_mix[layer], w_in[layer].astype(BF16))
        o_dil = _dilated(qkv_d, bias)
        o_sb = _stick(q_s, k_s, v_s)

        w_router = jnp.concatenate(
            [w_group[layer], w_expert[layer],
             jnp.zeros((d, LANES - N_GROUPS - N_EXPERTS), F32)], axis=1)
        x1, h2, route, cnt, base = _postmix(
            x, o_dil, o_sb, g_dil_out[layer], g_sb_out[layer], w_out[layer].astype(BF16),
            gate_mix, shift_ffn, scale_ffn, g_ffn[layer], w_router)

        bm = EXPERT_ROWS
        cnt = cnt[:, 0, ROUTE_LANE0:ROUTE_LANE0 + N_EXPERTS].astype(jnp.int32)
        base = base[:, 0, ROUTE_LANE0:ROUTE_LANE0 + N_EXPERTS].astype(jnp.int32)
        total = base[-1] + cnt[-1]
        tile_rows = jnp.sum(cnt, axis=1)
        cnt = cnt.reshape(-1)
        base = base.reshape(-1)
        padded = (total + bm - 1) // bm * bm
        pad_end = jnp.cumsum(padded)
        pad_start = (pad_end - padded).astype(jnp.int32)
        cap = -(-(2 * n + (SUBLANES - 1) * N_EXPERTS * ntiles) // bm) * bm + N_EXPERTS * bm
        n_blocks = cap // bm
        block_expert = jnp.minimum(
            jnp.sum(pad_end[None, :] <= (jnp.arange(n_blocks) * bm)[:, None], axis=1),
            N_EXPERTS - 1).astype(jnp.int32)
        pad_bounds = jnp.concatenate([pad_start, pad_end[-1:].astype(jnp.int32)])

        buf = _dispatch(pad_bounds, cnt, base, tile_rows, h2, route, cap)
        n_live = (pad_end[-1:] // bm).astype(jnp.int32)
        y_sorted = _experts(block_expert, n_live, buf, w_gate[layer], w_up[layer], w_down[layer])
        x = _combine(pad_start, cnt, base, tile_rows, x1, y_sorted, route, gate_ffn, g_final)
    return x
```

```python
import functools

import numpy as np
import jax
import jax.numpy as jnp
from jax import lax
from jax.experimental import pallas as pl
from jax.experimental.pallas import tpu as pltpu

HEAD_DIM = 64
N_HEADS_DIL = 8
N_HEADS_SB = 8
D_DIL = N_HEADS_DIL * HEAD_DIM
D_SB = N_HEADS_SB * HEAD_DIM
DILATION_PATTERNS = ((128, 1), (512, 4), (2048, 16))
N_GROUPS = 4
EXPERTS_PER_GROUP = 8
N_EXPERTS = N_GROUPS * EXPERTS_PER_GROUP
NORM_EPS = 1e-6

LANES = 128
SUBLANES = 8
DIL_STEPS = 128
DIL_UNIT = 2048
DIL_TILES_PER_TRIP = 8
SB_BLOCK = 256
SB_QUERY_ROWS = 1024
SB_BLOCKS_PER_TRIP = 2
PRE_ROWS = 512
POST_ROWS = 512
POST_ROW_GROUPS = 2
LOCAL_ROWS = -(-(2 * POST_ROWS + (SUBLANES - 1) * N_EXPERTS) // LANES) * LANES
EXPERT_ROWS = 512
ROUTE_LANE0 = N_GROUPS
VMEM_LIMIT = 56 * 1024 * 1024

F32 = jnp.float32
BF16 = jnp.bfloat16
NEG_INF = float("-inf")
LOG2E = 1.4426950408889634


def _cparams(sem):
    return pltpu.CompilerParams(dimension_semantics=sem, vmem_limit_bytes=VMEM_LIMIT)


def _rms(v, g):
    return v * lax.rsqrt(jnp.mean(v * v, axis=-1, keepdims=True) + NORM_EPS) * g


def _split3(v):
    hi = v.astype(BF16)
    r = v - hi.astype(F32)
    mid = r.astype(BF16)
    lo = (r - mid.astype(F32)).astype(BF16)
    return hi, mid, lo


def _ada_kernel(c_ref, w_ref, b_ref, o_ref):
    c = c_ref[...]
    cond = c / (1.0 + jnp.exp(-c))
    o_ref[...] = jnp.dot(cond, w_ref[...], precision=lax.Precision.HIGHEST,
                         preferred_element_type=F32) + b_ref[...]


def _ada(c, w_ada, b_ada):
    b, d = c.shape
    n = w_ada.shape[1]
    return pl.pallas_call(
        _ada_kernel,
        grid=(n // d,),
        in_specs=[pl.BlockSpec((b, d), lambda j: (0, 0)),
                  pl.BlockSpec((d, d), lambda j: (0, j)),
                  pl.BlockSpec((1, d), lambda j: (0, j))],
        out_specs=pl.BlockSpec((b, d), lambda j: (0, j)),
        out_shape=jax.ShapeDtypeStruct((b, n), F32),
        compiler_params=_cparams(("arbitrary",)),
        name="ada",
    )(c, w_ada, b_ada.reshape(1, n))


def _premix_kernel(x_ref, shift_ref, scale_ref, g_ref, w_ref, qkvd_ref, qs_ref, ks_ref, vs_ref):
    h = _rms(x_ref[0], g_ref[...]) * (1.0 + scale_ref[0]) + shift_ref[0]
    hb = h.astype(BF16)
    scale = HEAD_DIM ** -0.5 * LOG2E
    assert D_DIL == D_SB
    for j in range(6):
        cols = slice(j * D_DIL, (j + 1) * D_DIL)
        r = jnp.dot(hb, w_ref[:, cols], preferred_element_type=F32)
        if j == 0:
            qkvd_ref[0, :, cols] = r * scale
        elif j < 3:
            qkvd_ref[0, :, cols] = r
        elif j == 3:
            qs_ref[0] = (r * scale).astype(BF16)
        elif j == 4:
            ks_ref[0] = r.astype(BF16)
        else:
            vs_ref[0] = r.astype(BF16)


def _premix(x, shift, scale, g_mix, w_in_bf16):
    b, s, d = x.shape
    tm = PRE_ROWS
    mod_spec = pl.BlockSpec((1, 1, d), lambda bi, i: (bi, 0, 0))
    sb_spec = pl.BlockSpec((1, tm, D_SB), lambda bi, i: (bi, i, 0))
    return pl.pallas_call(
        _premix_kernel,
        grid=(b, s // tm),
        in_specs=[pl.BlockSpec((1, tm, d), lambda bi, i: (bi, i, 0)),
                  mod_spec, mod_spec,
                  pl.BlockSpec((1, d), lambda bi, i: (0, 0)),
                  pl.BlockSpec((d, 3 * (D_DIL + D_SB)), lambda bi, i: (0, 0))],
        out_specs=[pl.BlockSpec((1, tm, 3 * D_DIL), lambda bi, i: (bi, i, 0)),
                   sb_spec, sb_spec, sb_spec],
        out_shape=[jax.ShapeDtypeStruct((b, s, 3 * D_DIL), F32),
                   jax.ShapeDtypeStruct((b, s, D_SB), BF16),
                   jax.ShapeDtypeStruct((b, s, D_SB), BF16),
                   jax.ShapeDtypeStruct((b, s, D_SB), BF16)],
        compiler_params=_cparams(("arbitrary", "arbitrary")),
        name="premix",
    )(x, shift, scale, g_mix.reshape(1, d), w_in_bf16)


def _dilated_bias():
    n = DIL_STEPS
    slopes = np.array([2.0 ** (-8.0 * (i + 1) / N_HEADS_DIL) for i in range(N_HEADS_DIL)], dtype=np.float32)
    steps = np.arange(n)[:, None] + n - np.arange(2 * n)[None, :]
    valid = (steps >= 0) & (steps <= n)
    out = []
    for _, dilation in DILATION_PATTERNS:
        bias = -slopes[:, None, None] * (steps * dilation).astype(np.float32)[None]
        out.append(np.where(valid[None], bias.astype(np.float64) * LOG2E, -np.inf).astype(np.float32))
    return np.stack(out)


def _dil_kernel(q_ref, kc_ref, kp_ref, vc_ref, vp_ref, bias_ref, o_ref,
                kext, vext, u_scr, m_scr, l_scr):
    n = DIL_STEPS
    g = pl.program_id(1)
    kext[0:DIL_UNIT, :] = kp_ref[0]
    kext[DIL_UNIT:2 * DIL_UNIT, :] = kc_ref[0]
    vext[0:DIL_UNIT, :] = vp_ref[0]
    vext[DIL_UNIT:2 * DIL_UNIT, :] = vc_ref[0]
    lane = lax.broadcasted_iota(jnp.int32, (n, LANES), 1)
    head0 = lane < HEAD_DIM
    col = lax.broadcasted_iota(jnp.int32, (n, 2 * n), 1)

    for p, (_, dil) in enumerate(DILATION_PATTERNS):
        unit = n * dil

        def tiles(it, carry, p=p, dil=dil, unit=unit):
            rows_of, vvs, deads, ss = [], [], [], []
            for t in range(DIL_TILES_PER_TRIP):
                ti = it * DIL_TILES_PER_TRIP + t
                j = ti // dil
                r = ti % dil
                qstart = j * unit + r
                kstart = DIL_UNIT + qstart - unit
                if dil == 1:
                    rows_of.append(pl.ds(qstart, n))
                    krows = pl.ds(kstart, 2 * n)
                else:
                    rows_of.append(pl.ds(qstart, n, stride=dil))
                    krows = pl.ds(kstart, 2 * n, stride=dil)
                q = q_ref[0, rows_of[t], :]
                kk = kext[krows, :].astype(BF16)
                vvs.append(vext[krows, :].astype(BF16))
                deads.append(jnp.where(jnp.logical_and(g == 0, j == 0), n, 0))
                for h in range(2):
                    qh = jnp.where(head0 if h == 0 else jnp.logical_not(head0), q, 0.0).astype(BF16)
                    ss.append(lax.dot_general(qh, kk, (((1,), (1,)), ((), ())), preferred_element_type=F32))
            ms, ls, pes = [], [], []
            for t in range(DIL_TILES_PER_TRIP):
                for h in range(2):
                    logits = jnp.where(col < deads[t], NEG_INF, ss[2 * t + h] + bias_ref[p, h])
                    m = jnp.max(logits, axis=-1, keepdims=True)
                    pe = jnp.exp2(logits - m)
                    ls.append(jnp.sum(pe, axis=-1, keepdims=True))
                    ms.append(m)
                    pes.append(pe.astype(BF16))
            us = [jnp.dot(pes[2 * t + h], vvs[t], preferred_element_type=F32)
                  for t in range(DIL_TILES_PER_TRIP) for h in range(2)]
            for t in range(DIL_TILES_PER_TRIP):
                u_scr[p, rows_of[t], :] = jnp.where(head0, us[2 * t], us[2 * t + 1])
                m_scr[p, rows_of[t], :] = jnp.where(head0, ms[2 * t], ms[2 * t + 1])
                l_scr[p, rows_of[t], :] = jnp.where(head0, ls[2 * t], ls[2 * t + 1])
            return carry

        lax.fori_loop(0, DIL_UNIT // n // DIL_TILES_PER_TRIP, tiles, 0)

    def merge(i, carry):
        rows = pl.ds(pl.multiple_of(i * n, n), n)
        m0, m1, m2 = m_scr[0, rows, :], m_scr[1, rows, :], m_scr[2, rows, :]
        mx = jnp.maximum(jnp.maximum(m0, m1), m2)
        w0, w1, w2 = jnp.exp2(m0 - mx), jnp.exp2(m1 - mx), jnp.exp2(m2 - mx)
        num = w0 * u_scr[0, rows, :] + w1 * u_scr[1, rows, :] + w2 * u_scr[2, rows, :]
        den = w0 * l_scr[0, rows, :] + w1 * l_scr[1, rows, :] + w2 * l_scr[2, rows, :]
        o_ref[0, rows, :] = num / den
        return carry

    lax.fori_loop(0, DIL_UNIT // n, merge, 0)


def _dilated(qkv_d, bias):
    b, s, _ = qkv_d.shape
    u = DIL_UNIT
    npair = D_DIL // LANES
    cur = lambda off: pl.BlockSpec((1, u, LANES), lambda bi, g, p: (bi, g, off + p))
    prev = lambda off: pl.BlockSpec((1, u, LANES), lambda bi, g, p: (bi, jnp.maximum(g - 1, 0), off + p))
    return pl.pallas_call(
        _dil_kernel,
        grid=(b, s // u, npair),
        in_specs=[cur(0), cur(npair), prev(npair), cur(2 * npair), prev(2 * npair),
                  pl.BlockSpec((3, 2, DIL_STEPS, 2 * DIL_STEPS), lambda bi, g, p: (0, p, 0, 0))],
        out_specs=pl.BlockSpec((1, u, LANES), lambda bi, g, p: (bi, g, p)),
        out_shape=jax.ShapeDtypeStruct((b, s, D_DIL), F32),
        scratch_shapes=[pltpu.VMEM((2 * u, LANES), F32), pltpu.VMEM((2 * u, LANES), F32),
                        pltpu.VMEM((3, u, LANES), F32), pltpu.VMEM((3, u, LANES), F32),
                        pltpu.VMEM((3, u, LANES), F32)],
        compiler_params=_cparams(("arbitrary", "arbitrary", "arbitrary")),
        name="dilated",
    )(qkv_d, qkv_d, qkv_d, qkv_d, qkv_d, bias)


def _stick_kernel(q_ref, k_ref, v_ref, tri_ref, o_ref,
                  qh_scr, z_scr, w_scr, acc_scr, carry_scr):
    blk = SB_BLOCK
    nsub = SB_QUERY_ROWS // blk
    assert nsub % 2 == 0
    nchain = 2 * nsub
    qi = pl.program_id(2)
    lane = lax.broadcasted_iota(jnp.int32, (blk, LANES), 1)
    head0 = lane < HEAD_DIM
    for sub in range(nsub):
        q = q_ref[0, sub * blk:(sub + 1) * blk, :]
        zero = jnp.zeros_like(q)
        qh_scr[2 * sub] = jnp.where(head0, q, zero)
        qh_scr[2 * sub + 1] = jnp.where(head0, zero, q)
    acc_scr[...] = jnp.zeros_like(acc_scr)
    carry_scr[...] = jnp.zeros_like(carry_scr)

    def rows(kb):
        return pl.ds(pl.multiple_of(kb * blk, blk), blk)

    def scores(kb, which, slot):
        kblk = k_ref[0, rows(kb), :]
        for c in which:
            z_scr[slot * nchain + c] = lax.dot_general(
                qh_scr[c], kblk, (((1,), (1,)), ((), ())), preferred_element_type=F32)

    def weights(which, slot, diag_sub, beside=None):
        causal = (lax.broadcasted_iota(jnp.int32, (blk, blk), 1)
                  < lax.broadcasted_iota(jnp.int32, (blk, blk), 0))
        splits = {}
        for c in which:
            z = z_scr[slot * nchain + c]
            softplus = jnp.maximum(z, 0.0) + jnp.log(1.0 + jnp.exp2(-jnp.abs(z))) * LOG2E
            if c // 2 == diag_sub:
                softplus = jnp.where(causal, softplus, 0.0)
            hi = softplus.astype(BF16)
            lo = (softplus - hi.astype(F32)).astype(BF16)
            splits[c] = jnp.concatenate([hi, lo], axis=1)
        sums = {}
        for c in which:
            if beside is not None:
                beside(c)
            sums[c] = jnp.dot(splits[c], tri_ref[...], preferred_element_type=F32)
        for c in which:
            carry = carry_scr[c]
            w = jnp.exp2((z_scr[slot * nchain + c] - sums[c]) + carry[:, 0:1])
            if c // 2 == diag_sub:
                w = jnp.where(causal, w, 0.0)
            w_scr[c] = w.astype(BF16)
            carry_scr[c] = carry - sums[c][:, 0:LANES]

    def accumulate(which, kb):
        vblk = v_ref[0, rows(kb), :]
        for c in which:
            acc_scr[c] = acc_scr[c] + jnp.dot(w_scr[c], vblk, preferred_element_type=F32)

    everyone = list(range(nchain))
    top = nsub * qi + nsub - 1
    first = nsub * qi - 1
    diag = [[c for c in everyone if c // 2 >= nsub - 1 - i] for i in range(nsub)]
    for i in range(nsub):
        scores(top - i, diag[i], 2 + i)
    scores(jnp.maximum(first, 0), everyone, 0)
    for i in range(nsub):
        def previous(c, i=i):
            if i > 0 and c in diag[i - 1]:
                accumulate([c], top - (i - 1))

        weights(diag[i], 2 + i, nsub - 1 - i, beside=previous)

    def blocks(kb0, count):
        for j in range(count):
            kb = kb0 - j

            def neighbours(c, kb=kb, j=j):
                accumulate([c], kb + 1)
                scores(jnp.maximum(kb - 1, 0), [c], 1 - j % 2)

            weights(everyone, j % 2, -1, beside=neighbours)

    per = SB_BLOCKS_PER_TRIP
    trips = nsub * qi // per

    def step(i, carry):
        blocks(first - per * i, per)
        return carry

    lax.fori_loop(0, trips, step, 0)
    for rest in range(2, per, 2):
        @pl.when(nsub * qi - trips * per == rest)
        def _(rest=rest):
            blocks(first - per * trips, rest)
    accumulate(everyone, 0)
    for sub in range(nsub):
        o_ref[0, sub * blk:(sub + 1) * blk, :] = jnp.where(head0, acc_scr[2 * sub], acc_scr[2 * sub + 1])


def _stick(q_s, k_s, v_s):
    b, s, _ = q_s.shape
    blk = SB_BLOCK
    qrows = SB_QUERY_ROWS
    nchain = 2 * qrows // blk
    tri = np.tril(np.ones((blk, blk), np.float32))
    tri2 = jnp.asarray(np.concatenate([tri, tri], axis=0), BF16)
    full = pl.BlockSpec((1, s, LANES), lambda bi, p, i: (bi, 0, p))
    return pl.pallas_call(
        _stick_kernel,
        grid=(b, D_SB // LANES, s // qrows),
        in_specs=[pl.BlockSpec((1, qrows, LANES), lambda bi, p, i: (bi, i, p)), full, full,
                  pl.BlockSpec((2 * blk, blk), lambda bi, p, i: (0, 0))],
        out_specs=pl.BlockSpec((1, qrows, LANES), lambda bi, p, i: (bi, i, p)),
        out_shape=jax.ShapeDtypeStruct((b, s, D_SB), F32),
        scratch_shapes=[pltpu.VMEM((nchain, blk, LANES), BF16),
                        pltpu.VMEM(((2 + qrows // blk) * nchain, blk, blk), F32),
                        pltpu.VMEM((nchain, blk, blk), BF16),
                        pltpu.VMEM((nchain, blk, LANES), F32),
                        pltpu.VMEM((nchain, blk, LANES), F32)],
        compiler_params=_cparams(("arbitrary", "arbitrary", "arbitrary")),
        name="stick",
    )(q_s, k_s, v_s, tri2)


def _postmix_kernel(x_ref, od_ref, os_ref, gd_ref, gs_ref, wout_ref, gate_ref, shift_ref, scale_ref,
                    gffn_ref, wr_ref, tril_ref, triu_ref,
                    x1_ref, h2_ref, route_ref, cnt_ref, base_ref, carry_scr):
    tm = POST_ROWS

    @pl.when(jnp.logical_and(pl.program_id(0) == 0, pl.program_id(1) == 0))
    def _():
        carry_scr[...] = jnp.zeros_like(carry_scr)

    big = jnp.int32(LANES)
    lmax = lambda v: jnp.max(v, axis=-1, keepdims=True)
    lmin = lambda v: jnp.min(v, axis=-1, keepdims=True)
    lsum = lambda v: jnp.sum(v, axis=-1, keepdims=True)
    wr = wr_ref[...]
    w_hi = wr.astype(BF16)
    w_lo = (wr - w_hi.astype(F32)).astype(BF16)
    w3 = jnp.concatenate([w_hi, w_hi, w_lo], axis=0)

    def route_rows(rows):
        n = rows.stop - rows.start
        mixed = jnp.concatenate([_rms(od_ref[0, rows, :], gd_ref[...]), _rms(os_ref[0, rows, :], gs_ref[...])],
                                axis=-1)
        proj = jnp.dot(mixed.astype(BF16), wout_ref[...], preferred_element_type=F32)
        x1 = x_ref[0, rows, :] + gate_ref[0] * proj
        x1_ref[0, rows, :] = x1
        h2 = _rms(x1, gffn_ref[...]) * (1.0 + scale_ref[0]) + shift_ref[0]
        h_hi = h2.astype(BF16)
        h_lo = (h2 - h_hi.astype(F32)).astype(BF16)
        h2_ref[0, rows, :] = h_hi
        logits = jnp.dot(jnp.concatenate([h_hi, h_lo, h_hi], axis=1), w3, preferred_element_type=F32)

        lane = lax.broadcasted_iota(jnp.int32, (n, LANES), 1)
        gmask = lane < N_GROUPS
        gl = jnp.where(gmask, logits, NEG_INF)
        gmx = lmax(gl)
        group = lmin(jnp.where(jnp.logical_and(gmask, gl == gmx), lane, big))
        group_gate = 1.0 / lsum(jnp.exp(gl - gmx))
        lo = ROUTE_LANE0 + group * EXPERTS_PER_GROUP
        emask = jnp.logical_and(lane >= lo, lane < lo + EXPERTS_PER_GROUP)
        el = jnp.where(emask, logits, NEG_INF)
        l1 = lmax(el)
        i1 = lmin(jnp.where(el == l1, lane, big))
        el2 = jnp.where(lane == i1, NEG_INF, el)
        l2 = lmax(el2)
        i2 = lmin(jnp.where(el2 == l2, lane, big))
        r = jnp.exp(l2 - l1)
        return i1, i2, group_gate / (1.0 + r), group_gate * r / (1.0 + r)

    hm = tm // POST_ROW_GROUPS
    parts = [route_rows(slice(h * hm, (h + 1) * hm)) for h in range(POST_ROW_GROUPS)]
    i1, i2, w1, w2 = (jnp.concatenate([p[k] for p in parts], axis=0) for k in range(4))
    lane = lax.broadcasted_iota(jnp.int32, (tm, LANES), 1)

    is1 = lane == i1
    is2 = lane == i2
    oh = jnp.where(is1, 1.0, jnp.where(is2, 1.0, 0.0))
    earlier = jnp.dot(tril_ref[...], oh.astype(BF16), preferred_element_type=F32)
    runs = jnp.floor((jnp.sum(oh, axis=0, keepdims=True) + (SUBLANES - 1.0)) * (1.0 / SUBLANES))
    run_off = jnp.dot(jnp.broadcast_to(runs, (SUBLANES, LANES)).astype(BF16), triu_ref[...],
                      preferred_element_type=F32)[0:1]
    pos = earlier + run_off * SUBLANES
    slot1 = lsum(jnp.where(is1, pos, 0.0))
    slot2 = lsum(jnp.where(is2, pos, 0.0))
    cnt = runs * SUBLANES
    cnt_ref[0] = cnt
    base_ref[0] = carry_scr[...]
    carry_scr[...] = carry_scr[...] + cnt

    route_ref[0] = jnp.where(lane == 0, slot1, jnp.where(lane == 1, slot2,
                                                         jnp.where(lane == 2, w1, jnp.where(lane == 3, w2, 0.0))))


def _postmix(x, o_dil, o_sb, g_dil, g_sb, w_out_bf16, gate, shift, scale, g_ffn, w_router):
    b, s, d = x.shape
    tm = POST_ROWS
    nt = s // tm
    tril = jnp.asarray(np.tril(np.ones((tm, tm), np.float32), -1), BF16)
    triu = jnp.asarray(np.triu(np.ones((LANES, LANES), np.float32), 1), BF16)
    row = lambda w: pl.BlockSpec((1, tm, w), lambda bi, i: (bi, i, 0))
    vec = lambda w: pl.BlockSpec((1, w), lambda bi, i: (0, 0))
    mod_spec = pl.BlockSpec((1, 1, d), lambda bi, i: (bi, 0, 0))
    tile_vec = pl.BlockSpec((1, 1, LANES), lambda bi, i: (bi * nt + i, 0, 0))
    return pl.pallas_call(
        _postmix_kernel,
        grid=(b, nt),
        in_specs=[row(d), row(D_DIL), row(D_SB), vec(D_DIL), vec(D_SB),
                  pl.BlockSpec((d, d), lambda bi, i: (0, 0)),
                  mod_spec, mod_spec, mod_spec, vec(d),
                  pl.BlockSpec((d, LANES), lambda bi, i: (0, 0)),
                  pl.BlockSpec((tm, tm), lambda bi, i: (0, 0)),
                  pl.BlockSpec((LANES, LANES), lambda bi, i: (0, 0))],
        out_specs=[row(d), row(d), row(LANES), tile_vec, tile_vec],
        out_shape=[jax.ShapeDtypeStruct((b, s, d), F32),
                   jax.ShapeDtypeStruct((b, s, d), BF16),
                   jax.ShapeDtypeStruct((b, s, LANES), F32),
                   jax.ShapeDtypeStruct((b * nt, 1, LANES), F32),
                   jax.ShapeDtypeStruct((b * nt, 1, LANES), F32)],
        scratch_shapes=[pltpu.VMEM((1, LANES), F32)],
        compiler_params=_cparams(("arbitrary", "arbitrary")),
        name="postmix",
    )(x, o_dil, o_sb, g_dil.reshape(1, -1), g_sb.reshape(1, -1), w_out_bf16, gate, shift, scale,
      g_ffn.reshape(1, d), w_router, tril, triu)


def _for_each_run_piece(tile, start_ref, cnt_ref, base_ref, fn):
    def body(e, off):
        c = cnt_ref[tile * N_EXPERTS + e]
        sorted0 = start_ref[e] + base_ref[tile * N_EXPERTS + e]
        for k in range(SUBLANES.bit_length() - 1, POST_ROWS.bit_length()):
            p = 1 << k

            @pl.when((c & p) != 0)
            def _(p=p):
                done = c - (c & (2 * p - 1))
                fn(pl.multiple_of(off + done, SUBLANES), pl.multiple_of(sorted0 + done, SUBLANES), p)
        return off + c

    return lax.fori_loop(0, N_EXPERTS, body, 0)


def _wait_rows(total, piece):
    for k in range(SUBLANES.bit_length() - 1, LOCAL_ROWS.bit_length()):
        p = 1 << k

        @pl.when((total & p) != 0)
        def _(p=p):
            piece(0, 0, p).wait()


def _sort_kernel(start_ref, cnt_ref, base_ref, rows_ref, h2_ref, route_ref, buf_ref, xs_scr, sem):
    tm = POST_ROWS
    lt = LOCAL_ROWS
    d = h2_ref.shape[2]
    tile = pl.program_id(0)
    slot = tile % 2

    def piece(slot, lrow, srow, rows):
        return pltpu.make_async_copy(xs_scr.at[slot, pl.ds(lrow, rows)], buf_ref.at[pl.ds(srow, rows)],
                                     sem.at[slot])

    def drain(tile, slot):
        _wait_rows(rows_ref[tile], functools.partial(piece, slot))

    @pl.when(tile >= 2)
    def _():
        drain(tile - 2, slot)

    lane = lax.broadcasted_iota(jnp.int32, (tm, LANES), 1)
    route = route_ref[0]
    w1 = jnp.sum(jnp.where(lane == 2, route, 0.0), axis=-1, keepdims=True)
    w2 = jnp.sum(jnp.where(lane == 3, route, 0.0), axis=-1, keepdims=True)

    def pieces(w):
        hi, mid, lw = _split3(w)
        return jnp.where(lane == 0, hi.astype(F32),
                         jnp.where(lane == 1, mid.astype(F32),
                                   jnp.where(lane == 2, lw.astype(F32), 0.0))).astype(BF16)

    route_t = route.T
    s1 = route_t[0:1, :].astype(jnp.int32)
    s2 = route_t[1:2, :].astype(jnp.int32)
    row = lax.broadcasted_iota(jnp.int32, (lt, tm), 0)
    p1 = jnp.where(row == s1, 1.0, 0.0)
    p2 = jnp.where(row == s2, 1.0, 0.0)
    xs_scr[slot, :, 0:d] = jnp.dot((p1 + p2).astype(BF16), h2_ref[0], preferred_element_type=F32)
    xs_scr[slot, :, d:] = (jnp.dot(p1.astype(BF16), pieces(w1), preferred_element_type=F32)
                           + jnp.dot(p2.astype(BF16), pieces(w2), preferred_element_type=F32))
    _for_each_run_piece(tile, start_ref, cnt_ref, base_ref, lambda *a: piece(slot, *a).start())

    last = pl.num_programs(0) - 1

    @pl.when(jnp.logical_and(tile == last, tile >= 1))
    def _():
        drain(tile - 1, 1 - slot)

    @pl.when(tile == last)
    def _():
        drain(tile, slot)
        xs_scr[0] = jnp.zeros_like(xs_scr[0])

        def zeros_to(srow, rows):
            return pltpu.make_async_copy(xs_scr.at[0, pl.ds(0, rows)], buf_ref.at[pl.ds(srow, rows)], sem.at[0])

        def gaps(fn):
            def body(e, carry):
                end = start_ref[e] + base_ref[last * N_EXPERTS + e] + cnt_ref[last * N_EXPERTS + e]
                n = start_ref[e + 1] - end
                for k in range(SUBLANES.bit_length() - 1, EXPERT_ROWS.bit_length() - 1):
                    p = 1 << k

                    @pl.when((n & p) != 0)
                    def _(p=p):
                        fn(zeros_to(pl.multiple_of(end + n - (n & (2 * p - 1)), SUBLANES), p))
                return carry

            lax.fori_loop(0, N_EXPERTS, body, 0)

        def tail(fn):
            def body(i, carry):
                fn(zeros_to(pl.multiple_of(start_ref[N_EXPERTS] + i * EXPERT_ROWS, EXPERT_ROWS), EXPERT_ROWS))
                return carry

            lax.fori_loop(0, (buf_ref.shape[0] - start_ref[N_EXPERTS]) // EXPERT_ROWS, body, 0)

        for walk in (gaps, tail):
            walk(lambda cp: cp.start())
        for walk in (gaps, tail):
            walk(lambda cp: cp.wait())


def _dispatch(pad_bounds, cnt, base, tile_rows, h2, route, cap):
    b, s, d = h2.shape
    tm = POST_ROWS
    nt = s // tm
    return pl.pallas_call(
        _sort_kernel,
        grid_spec=pltpu.PrefetchScalarGridSpec(
            num_scalar_prefetch=4, grid=(b * nt,),
            in_specs=[pl.BlockSpec((1, tm, d), lambda t, *_: (t // nt, t % nt, 0)),
                      pl.BlockSpec((1, tm, LANES), lambda t, *_: (t // nt, t % nt, 0))],
            out_specs=pl.BlockSpec(memory_space=pl.ANY),
            scratch_shapes=[pltpu.VMEM((2, LOCAL_ROWS, d + LANES), F32), pltpu.SemaphoreType.DMA((2,))]),
        out_shape=jax.ShapeDtypeStruct((cap, d + LANES), F32),
        compiler_params=_cparams(("arbitrary",)),
        name="dispatch",
    )(pad_bounds, cnt, base, tile_rows, h2, route)


def _expert_kernel(be_ref, nlive_ref, x_ref, wg_ref, wu_ref, wd_ref, y_ref):
    del be_ref
    d = y_ref.shape[1]

    @pl.when(pl.program_id(0) < nlive_ref[0])
    def _():
        xb = x_ref[:, 0:d].astype(BF16)
        weight = jnp.sum(x_ref[:, d:], axis=-1, keepdims=True)
        gate = jnp.dot(xb, wg_ref[0].astype(BF16), preferred_element_type=F32)
        up = jnp.dot(xb, wu_ref[0].astype(BF16), preferred_element_type=F32)
        act = gate / (1.0 + jnp.exp(-gate)) * up
        y_ref[...] = jnp.dot(act.astype(BF16), wd_ref[0].astype(BF16), preferred_element_type=F32) * weight

    @pl.when(pl.program_id(0) >= nlive_ref[0])
    def _():
        y_ref[...] = jnp.zeros_like(y_ref)


def _experts(block_expert, n_live, buf, wg, wu, wd):
    cap, dw = buf.shape
    d, f = wg.shape[1], wg.shape[2]
    bm = EXPERT_ROWS

    def blk(i, nl):
        return jnp.minimum(i, nl[0] - 1)

    return pl.pallas_call(
        _expert_kernel,
        grid_spec=pltpu.PrefetchScalarGridSpec(
            num_scalar_prefetch=2, grid=(cap // bm,),
            in_specs=[pl.BlockSpec((bm, dw), lambda i, be, nl: (blk(i, nl), 0)),
                      pl.BlockSpec((1, d, f), lambda i, be, nl: (be[blk(i, nl)], 0, 0)),
                      pl.BlockSpec((1, d, f), lambda i, be, nl: (be[blk(i, nl)], 0, 0)),
                      pl.BlockSpec((1, f, d), lambda i, be, nl: (be[blk(i, nl)], 0, 0))],
            out_specs=pl.BlockSpec((bm, d), lambda i, be, nl: (i, 0))),
        out_shape=jax.ShapeDtypeStruct((cap, d), F32),
        compiler_params=_cparams(("arbitrary",)),
        name="experts",
    )(block_expert, n_live, buf, wg, wu, wd)


def _combine_kernel(start_ref, cnt_ref, base_ref, rows_ref, x1_ref, route_ref, gate_ref, g_ref, y_hbm_ref,
                    o_ref, y_scr, sem):
    tm = POST_ROWS
    lt = LOCAL_ROWS
    tile = pl.program_id(0)
    slot = tile % 2

    def piece(slot, lrow, srow, rows):
        return pltpu.make_async_copy(y_hbm_ref.at[pl.ds(srow, rows)], y_scr.at[slot, pl.ds(lrow, rows)],
                                     sem.at[slot])

    def fetch(tile, slot):
        _for_each_run_piece(tile, start_ref, cnt_ref, base_ref, lambda *a: piece(slot, *a).start())

    @pl.when(tile == 0)
    def _():
        fetch(tile, slot)

    @pl.when(tile + 1 < pl.num_programs(0))
    def _():
        fetch(tile + 1, 1 - slot)

    lane = lax.broadcasted_iota(jnp.int32, (tm, LANES), 1)
    route = route_ref[0]
    s1 = jnp.sum(jnp.where(lane == 0, route, 0.0), axis=-1, keepdims=True).astype(jnp.int32)
    s2 = jnp.sum(jnp.where(lane == 1, route, 0.0), axis=-1, keepdims=True).astype(jnp.int32)
    col = lax.broadcasted_iota(jnp.int32, (tm, lt), 1)
    pick = jnp.where(col == s1, 1.0, jnp.where(col == s2, 1.0, 0.0)).astype(BF16)
    used = rows_ref[tile]
    _wait_rows(used, functools.partial(piece, slot))
    live = lax.broadcasted_iota(jnp.int32, (lt, 1), 0) < used
    yv = jnp.where(live, y_scr[slot], 0.0)
    hi = yv.astype(BF16)
    lo = (yv - hi.astype(F32)).astype(BF16)
    y = jnp.dot(jnp.concatenate([pick, pick], axis=1), jnp.concatenate([hi, lo], axis=0),
                preferred_element_type=F32)
    o_ref[0] = _rms(x1_ref[0] + gate_ref[0] * y, g_ref[...])


def _combine(pad_start, cnt, base, tile_rows, x1, y_sorted, route, gate, g_final):
    b, s, d = x1.shape
    tm = POST_ROWS
    nt = s // tm
    return pl.pallas_call(
        _combine_kernel,
        grid_spec=pltpu.PrefetchScalarGridSpec(
            num_scalar_prefetch=4, grid=(b * nt,),
            in_specs=[pl.BlockSpec((1, tm, d), lambda t, *_: (t // nt, t % nt, 0)),
                      pl.BlockSpec((1, tm, LANES), lambda t, *_: (t // nt, t % nt, 0)),
                      pl.BlockSpec((1, 1, d), lambda t, *_: (t // nt, 0, 0)),
                      pl.BlockSpec((1, d), lambda t, *_: (0, 0)),
                      pl.BlockSpec(memory_space=pl.ANY)],
            out_specs=pl.BlockSpec((1, tm, d), lambda t, *_: (t // nt, t % nt, 0)),
            scratch_shapes=[pltpu.VMEM((2, LOCAL_ROWS, d), F32), pltpu.SemaphoreType.DMA((2,))]),
        out_shape=jax.ShapeDtypeStruct((b, s, d), F32),
        compiler_params=_cparams(("arbitrary",)),
        name="combine",
    )(pad_start, cnt, base, tile_rows, x1, route, gate, g_final.reshape(1, d), y_sorted)


def kernel(x, c, w_ada, b_ada, g_mix, w_in, g_dil_out, g_sb_out, w_out, g_ffn,
           w_group, w_expert, w_gate, w_up, w_down, g_final):
    b, s, d = x.shape
    depth = w_ada.shape[0]
    assert s % DIL_UNIT == 0 and d == D_DIL + D_SB
    assert depth == 1, "the final rmsnorm is fused into the last layer's combine step"
    n = b * s
    ntiles = n // POST_ROWS
    bias = jnp.asarray(_dilated_bias())
    for layer in range(depth):
        mod = _ada(c, w_ada[layer], b_ada[layer])
        shift_mix, scale_mix, gate_mix, shift_ffn, scale_ffn, gate_ffn = (
            m.reshape(b, 1, d) for m in jnp.split(mod, 6, axis=-1))

        qkv_d, q_s, k_s, v_s = _premix(x, shift_mix, scale_mix, g_mix[layer], w_in[layer].astype(BF16))
        o_dil = _dilated(qkv_d, bias)
        o_sb = _stick(q_s, k_s, v_s)

        w_router = jnp.concatenate(
            [w_group[layer], w_expert[layer],
             jnp.zeros((d, LANES - N_GROUPS - N_EXPERTS), F32)], axis=1)
        x1, h2, route, cnt, base = _postmix(
            x, o_dil, o_sb, g_dil_out[layer], g_sb_out[layer], w_out[layer].astype(BF16),
            gate_mix, shift_ffn, scale_ffn, g_ffn[layer], w_router)

        bm = EXPERT_ROWS
        cnt = cnt[:, 0, ROUTE_LANE0:ROUTE_LANE0 + N_EXPERTS].astype(jnp.int32)
        base = base[:, 0, ROUTE_LANE0:ROUTE_LANE0 + N_EXPERTS].astype(jnp.int32)
        total = base[-1] + cnt[-1]
        tile_rows = jnp.sum(cnt, axis=1)
        cnt = cnt.reshape(-1)
        base = base.reshape(-1)
        padded = (total + bm - 1) // bm * bm
        pad_end = jnp.cumsum(padded)
        pad_start = (pad_end - padded).astype(jnp.int32)
        cap = -(-(2 * n + (SUBLANES - 1) * N_EXPERTS * ntiles) // bm) * bm + N_EXPERTS * bm
        n_blocks = cap // bm
        block_expert = jnp.minimum(
            jnp.sum(pad_end[None, :] <= (jnp.arange(n_blocks) * bm)[:, None], axis=1),
            N_EXPERTS - 1).astype(jnp.int32)
        pad_bounds = jnp.concatenate([pad_start, pad_end[-1:].astype(jnp.int32)])

        buf = _dispatch(pad_bounds, cnt, base, tile_rows, h2, route, cap)
        n_live = (pad_end[-1:] // bm).astype(jnp.int32)
        y_sorted = _experts(block_expert, n_live, buf, w_gate[layer], w_up[layer], w_down[layer])
        x = _combine(pad_start, cnt, base, tile_rows, x1, y_sorted, route, gate_ffn, g_final)
    return x
```

```python
import functools

import numpy as np
import jax
import jax.numpy as jnp
from jax import lax
from jax.experimental import pallas as pl
from jax.experimental.pallas import tpu as pltpu

HEAD_DIM = 64
N_HEADS_DIL = 8
N_HEADS_SB = 8
D_DIL = N_HEADS_DIL * HEAD_DIM
D_SB = N_HEADS_SB * HEAD_DIM
DILATION_PATTERNS = ((128, 1), (512, 4), (2048, 16))
N_GROUPS = 4
EXPERTS_PER_GROUP = 8
N_EXPERTS = N_GROUPS * EXPERTS_PER_GROUP
NORM_EPS = 1e-6

LANES = 128
SUBLANES = 8
DIL_STEPS = 128
DIL_UNIT = 2048
DIL_TILES_PER_TRIP = 8
SB_BLOCK = 256
SB_QUERY_ROWS = 1024
SB_BLOCKS_PER_TRIP = 2
PRE_ROWS = 512
POST_ROWS = 512
POST_ROW_GROUPS = 2
LOCAL_ROWS = -(-(2 * POST_ROWS + (SUBLANES - 1) * N_EXPERTS) // LANES) * LANES
EXPERT_ROWS = 512
ROUTE_LANE0 = N_GROUPS
VMEM_LIMIT = 56 * 1024 * 1024

F32 = jnp.float32
BF16 = jnp.bfloat16
NEG_INF = float("-inf")
LOG2E = 1.4426950408889634


def _cparams(sem):
    return pltpu.CompilerParams(dimension_semantics=sem, vmem_limit_bytes=VMEM_LIMIT)


def _rms(v, g):
    return v * lax.rsqrt(jnp.mean(v * v, axis=-1, keepdims=True) + NORM_EPS) * g


def _split3(v):
    hi = v.astype(BF16)
    r = v - hi.astype(F32)
    mid = r.astype(BF16)
    lo = (r - mid.astype(F32)).astype(BF16)
    return hi, mid, lo


def _ada_kernel(c_ref, w_ref, b_ref, o_ref):
    c = c_ref[...]
    cond = c / (1.0 + jnp.exp(-c))
    o_ref[...] = jnp.dot(cond, w_ref[...], precision=lax.Precision.HIGHEST,
                         preferred_element_type=F32) + b_ref[...]


def _ada(c, w_ada, b_ada):
    b, d = c.shape
    n = w_ada.shape[1]
    return pl.pallas_call(
        _ada_kernel,
        grid=(n // d,),
        in_specs=[pl.BlockSpec((b, d), lambda j: (0, 0)),
                  pl.BlockSpec((d, d), lambda j: (0, j)),
                  pl.BlockSpec((1, d), lambda j: (0, j))],
        out_specs=pl.BlockSpec((b, d), lambda j: (0, j)),
        out_shape=jax.ShapeDtypeStruct((b, n), F32),
        compiler_params=_cparams(("arbitrary",)),
        name="ada",
    )(c, w_ada, b_ada.reshape(1, n))


def _premix_kernel(x_ref, shift_ref, scale_ref, g_ref, w_ref, qkvd_ref, qs_ref, ks_ref, vs_ref):
    h = _rms(x_ref[0], g_ref[...]) * (1.0 + scale_ref[0]) + shift_ref[0]
    hb = h.astype(BF16)
    scale = HEAD_DIM ** -0.5 * LOG2E
    assert D_DIL == D_SB
    for j in range(6):
        cols = slice(j * D_DIL, (j + 1) * D_DIL)
        r = jnp.dot(hb, w_ref[:, cols], preferred_element_type=F32)
        if j == 0:
            qkvd_ref[0, :, cols] = r * scale
        elif j < 3:
            qkvd_ref[0, :, cols] = r
        elif j == 3:
            qs_ref[0] = (r * scale).astype(BF16)
        elif j == 4:
            ks_ref[0] = r.astype(BF16)
        else:
            vs_ref[0] = r.astype(BF16)


def _premix(x, shift, scale, g_mix, w_in_bf16):
    b, s, d = x.shape
    tm = PRE_ROWS
    mod_spec = pl.BlockSpec((1, 1, d), lambda bi, i: (bi, 0, 0))
    sb_spec = pl.BlockSpec((1, tm, D_SB), lambda bi, i: (bi, i, 0))
    return pl.pallas_call(
        _premix_kernel,
        grid=(b, s // tm),
        in_specs=[pl.BlockSpec((1, tm, d), lambda bi, i: (bi, i, 0)),
                  mod_spec, mod_spec,
                  pl.BlockSpec((1, d), lambda bi, i: (0, 0)),
                  pl.BlockSpec((d, 3 * (D_DIL + D_SB)), lambda bi, i: (0, 0))],
        out_specs=[pl.BlockSpec((1, tm, 3 * D_DIL), lambda bi, i: (bi, i, 0)),
                   sb_spec, sb_spec, sb_spec],
        out_shape=[jax.ShapeDtypeStruct((b, s, 3 * D_DIL), F32),
                   jax.ShapeDtypeStruct((b, s, D_SB), BF16),
                   jax.ShapeDtypeStruct((b, s, D_SB), BF16),
                   jax.ShapeDtypeStruct((b, s, D_SB), BF16)],
        compiler_params=_cparams(("arbitrary", "arbitrary")),
        name="premix",
    )(x, shift, scale, g_mix.reshape(1, d), w_in_bf16)


def _dilated_bias():
    n = DIL_STEPS
    slopes = np.array([2.0 ** (-8.0 * (i + 1) / N_HEADS_DIL) for i in range(N_HEADS_DIL)], dtype=np.float32)
    steps = np.arange(n)[:, None] + n - np.arange(2 * n)[None, :]
    valid = (steps >= 0) & (steps <= n)
    out = []
    for _, dilation in DILATION_PATTERNS:
        bias = -slopes[:, None, None] * (steps * dilation).astype(np.float32)[None]
        out.append(np.where(valid[None], bias.astype(np.float64) * LOG2E, -np.inf).astype(np.float32))
    return np.stack(out)


def _dil_kernel(q_ref, kc_ref, kp_ref, vc_ref, vp_ref, bias_ref, o_ref,
                kext, vext, u_scr, m_scr, l_scr):
    n = DIL_STEPS
    g = pl.program_id(1)
    kext[0:DIL_UNIT, :] = kp_ref[0]
    kext[DIL_UNIT:2 * DIL_UNIT, :] = kc_ref[0]
    vext[0:DIL_UNIT, :] = vp_ref[0]
    vext[DIL_UNIT:2 * DIL_UNIT, :] = vc_ref[0]
    lane = lax.broadcasted_iota(jnp.int32, (n, LANES), 1)
    head0 = lane < HEAD_DIM
    col = lax.broadcasted_iota(jnp.int32, (n, 2 * n), 1)

    for p, (_, dil) in enumerate(DILATION_PATTERNS):
        unit = n * dil

        def tiles(it, carry, p=p, dil=dil, unit=unit):
            rows_of, vvs, deads, ss = [], [], [], []
            for t in range(DIL_TILES_PER_TRIP):
                ti = it * DIL_TILES_PER_TRIP + t
                j = ti // dil
                r = ti % dil
                qstart = j * unit + r
                kstart = DIL_UNIT + qstart - unit
                if dil == 1:
                    rows_of.append(pl.ds(qstart, n))
                    krows = pl.ds(kstart, 2 * n)
                else:
                    rows_of.append(pl.ds(qstart, n, stride=dil))
                    krows = pl.ds(kstart, 2 * n, stride=dil)
                q = q_ref[0, rows_of[t], :]
                kk = kext[krows, :].astype(BF16)
                vvs.append(vext[krows, :].astype(BF16))
                deads.append(jnp.where(jnp.logical_and(g == 0, j == 0), n, 0))
                for h in range(2):
                    qh = jnp.where(head0 if h == 0 else jnp.logical_not(head0), q, 0.0).astype(BF16)
                    ss.append(lax.dot_general(qh, kk, (((1,), (1,)), ((), ())), preferred_element_type=F32))
            ms, ls, pes = [], [], []
            for t in range(DIL_TILES_PER_TRIP):
                for h in range(2):
                    logits = jnp.where(col < deads[t], NEG_INF, ss[2 * t + h] + bias_ref[p, h])
                    m = jnp.max(logits, axis=-1, keepdims=True)
                    pe = jnp.exp2(logits - m)
                    ls.append(jnp.sum(pe, axis=-1, keepdims=True))
                    ms.append(m)
                    pes.append(pe.astype(BF16))
            us = [jnp.dot(pes[2 * t + h], vvs[t], preferred_element_type=F32)
                  for t in range(DIL_TILES_PER_TRIP) for h in range(2)]
            for t in range(DIL_TILES_PER_TRIP):
                u_scr[p, rows_of[t], :] = jnp.where(head0, us[2 * t], us[2 * t + 1])
                m_scr[p, rows_of[t], :] = jnp.where(head0, ms[2 * t], ms[2 * t + 1])
                l_scr[p, rows_of[t], :] = jnp.where(head0, ls[2 * t], ls[2 * t + 1])
            return carry

        lax.fori_loop(0, DIL_UNIT // n // DIL_TILES_PER_TRIP, tiles, 0)

    def merge(i, carry):
        rows = pl.ds(pl.multiple_of(i * n, n), n)
        m0, m1, m2 = m_scr[0, rows, :], m_scr[1, rows, :], m_scr[2, rows, :]
        mx = jnp.maximum(jnp.maximum(m0, m1), m2)
        w0, w1, w2 = jnp.exp2(m0 - mx), jnp.exp2(m1 - mx), jnp.exp2(m2 - mx)
        num = w0 * u_scr[0, rows, :] + w1 * u_scr[1, rows, :] + w2 * u_scr[2, rows, :]
        den = w0 * l_scr[0, rows, :] + w1 * l_scr[1, rows, :] + w2 * l_scr[2, rows, :]
        o_ref[0, rows, :] = num / den
        return carry

    lax.fori_loop(0, DIL_UNIT // n, merge, 0)


def _dilated(qkv_d, bias):
    b, s, _ = qkv_d.shape
    u = DIL_UNIT
    npair = D_DIL // LANES
    cur = lambda off: pl.BlockSpec((1, u, LANES), lambda bi, g, p: (bi, g, off + p))
    prev = lambda off: pl.BlockSpec((1, u, LANES), lambda bi, g, p: (bi, jnp.maximum(g - 1, 0), off + p))
    return pl.pallas_call(
        _dil_kernel,
        grid=(b, s // u, npair),
        in_specs=[cur(0), cur(npair), prev(npair), cur(2 * npair), prev(2 * npair),
                  pl.BlockSpec((3, 2, DIL_STEPS, 2 * DIL_STEPS), lambda bi, g, p: (0, p, 0, 0))],
        out_specs=pl.BlockSpec((1, u, LANES), lambda bi, g, p: (bi, g, p)),
        out_shape=jax.ShapeDtypeStruct((b, s, D_DIL), F32),
        scratch_shapes=[pltpu.VMEM((2 * u, LANES), F32), pltpu.VMEM((2 * u, LANES), F32),
                        pltpu.VMEM((3, u, LANES), F32), pltpu.VMEM((3, u, LANES), F32),
                        pltpu.VMEM((3, u, LANES), F32)],
        compiler_params=_cparams(("arbitrary", "arbitrary", "arbitrary")),
        name="dilated",
    )(qkv_d, qkv_d, qkv_d, qkv_d, qkv_d, bias)


def _stick_kernel(q_ref, k_ref, v_ref, tri_ref, o_ref,
                  qh_scr, z_scr, w_scr, acc_scr, carry_scr):
    blk = SB_BLOCK
    nsub = SB_QUERY_ROWS // blk
    assert nsub % 2 == 0
    nchain = 2 * nsub
    qi = pl.program_id(2)
    lane = lax.broadcasted_iota(jnp.int32, (blk, LANES), 1)
    head0 = lane < HEAD_DIM
    for sub in range(nsub):
        q = q_ref[0, sub * blk:(sub + 1) * blk, :]
        zero = jnp.zeros_like(q)
        qh_scr[2 * sub] = jnp.where(head0, q, zero)
        qh_scr[2 * sub + 1] = jnp.where(head0, zero, q)
    acc_scr[...] = jnp.zeros_like(acc_scr)
    carry_scr[...] = jnp.zeros_like(carry_scr)

    def rows(kb):
        return pl.ds(pl.multiple_of(kb * blk, blk), blk)

    def scores(kb, which, slot):
        kblk = k_ref[0, rows(kb), :]
        for c in which:
            z_scr[slot * nchain + c] = lax.dot_general(
                qh_scr[c], kblk, (((1,), (1,)), ((), ())), preferred_element_type=F32)

    def weights(which, slot, diag_sub, beside=None):
        causal = (lax.broadcasted_iota(jnp.int32, (blk, blk), 1)
                  < lax.broadcasted_iota(jnp.int32, (blk, blk), 0))
        splits = {}
        for c in which:
            z = z_scr[slot * nchain + c]
            softplus = jnp.maximum(z, 0.0) + jnp.log(1.0 + jnp.exp2(-jnp.abs(z))) * LOG2E
            if c // 2 == diag_sub:
                softplus = jnp.where(causal, softplus, 0.0)
            hi = softplus.astype(BF16)
            lo = (softplus - hi.astype(F32)).astype(BF16)
            splits[c] = jnp.concatenate([hi, lo], axis=1)
        sums = {}
        for c in which:
            if beside is not None:
                beside(c)
            sums[c] = jnp.dot(splits[c], tri_ref[...], preferred_element_type=F32)
        for c in which:
            carry = carry_scr[c]
            w = jnp.exp2((z_scr[slot * nchain + c] - sums[c]) + carry[:, 0:1])
            if c // 2 == diag_sub:
                w = jnp.where(causal, w, 0.0)
            w_scr[c] = w.astype(BF16)
            carry_scr[c] = carry - sums[c][:, 0:LANES]

    def accumulate(which, kb):
        vblk = v_ref[0, rows(kb), :]
        for c in which:
            acc_scr[c] = acc_scr[c] + jnp.dot(w_scr[c], vblk, preferred_element_type=F32)

    everyone = list(range(nchain))
    top = nsub * qi + nsub - 1
    first = nsub * qi - 1
    diag = [[c for c in everyone if c // 2 >= nsub - 1 - i] for i in range(nsub)]
    for i in range(nsub):
        scores(top - i, diag[i], 2 + i)
    scores(jnp.maximum(first, 0), everyone, 0)
    for i in range(nsub):
        def previous(c, i=i):
            if i > 0 and c in diag[i - 1]:
                accumulate([c], top - (i - 1))

        weights(diag[i], 2 + i, nsub - 1 - i, beside=previous)

    def blocks(kb0, count):
        for j in range(count):
            kb = kb0 - j

            def neighbours(c, kb=kb, j=j):
                accumulate([c], kb + 1)
                scores(jnp.maximum(kb - 1, 0), [c], 1 - j % 2)

            weights(everyone, j % 2, -1, beside=neighbours)

    per = SB_BLOCKS_PER_TRIP
    trips = nsub * qi // per

    def step(i, carry):
        blocks(first - per * i, per)
        return carry

    lax.fori_loop(0, trips, step, 0)
    for rest in range(2, per, 2):
        @pl.when(nsub * qi - trips * per == rest)
        def _(rest=rest):
            blocks(first - per * trips, rest)
    accumulate(everyone, 0)
    for sub in range(nsub):
        o_ref[0, sub * blk:(sub + 1) * blk, :] = jnp.where(head0, acc_scr[2 * sub], acc_scr[2 * sub + 1])


def _stick(q_s, k_s, v_s):
    b, s, _ = q_s.shape
    blk = SB_BLOCK
    qrows = SB_QUERY_ROWS
    nchain = 2 * qrows // blk
    tri = np.tril(np.ones((blk, blk), np.float32))
    tri2 = jnp.asarray(np.concatenate([tri, tri], axis=0), BF16)
    full = pl.BlockSpec((1, s, LANES), lambda bi, p, i: (bi, 0, p))
    return pl.pallas_call(
        _stick_kernel,
        grid=(b, D_SB // LANES, s // qrows),
        in_specs=[pl.BlockSpec((1, qrows, LANES), lambda bi, p, i: (bi, i, p)), full, full,
                  pl.BlockSpec((2 * blk, blk), lambda bi, p, i: (0, 0))],
        out_specs=pl.BlockSpec((1, qrows, LANES), lambda bi, p, i: (bi, i, p)),
        out_shape=jax.ShapeDtypeStruct((b, s, D_SB), F32),
        scratch_shapes=[pltpu.VMEM((nchain, blk, LANES), BF16),
                        pltpu.VMEM(((2 + qrows // blk) * nchain, blk, blk), F32),
                        pltpu.VMEM((nchain, blk, blk), BF16),
                        pltpu.VMEM((nchain, blk, LANES), F32),
                        pltpu.VMEM((nchain, blk, LANES), F32)],
        compiler_params=_cparams(("arbitrary", "arbitrary", "arbitrary")),
        name="stick",
    )(q_s, k_s, v_s, tri2)


def _postmix_kernel(x_ref, od_ref, os_ref, gd_ref, gs_ref, wout_ref, gate_ref, shift_ref, scale_ref,
                    gffn_ref, wr_ref, tril_ref, triu_ref,
                    x1_ref, h2_ref, route_ref, cnt_ref, base_ref, carry_scr):
    tm = POST_ROWS

    @pl.when(jnp.logical_and(pl.program_id(0) == 0, pl.program_id(1) == 0))
    def _():
        carry_scr[...] = jnp.zeros_like(carry_scr)

    big = jnp.int32(LANES)
    lmax = lambda v: jnp.max(v, axis=-1, keepdims=True)
    lmin = lambda v: jnp.min(v, axis=-1, keepdims=True)
    lsum = lambda v: jnp.sum(v, axis=-1, keepdims=True)
    wr = wr_ref[...]
    w_hi = wr.astype(BF16)
    w_lo = (wr - w_hi.astype(F32)).astype(BF16)
    w3 = jnp.concatenate([w_hi, w_hi, w_lo], axis=0)

    def route_rows(rows):
        n = rows.stop - rows.start
        mixed = jnp.concatenate([_rms(od_ref[0, rows, :], gd_ref[...]), _rms(os_ref[0, rows, :], gs_ref[...])],
                                axis=-1)
        proj = jnp.dot(mixed.astype(BF16), wout_ref[...], preferred_element_type=F32)
        x1 = x_ref[0, rows, :] + gate_ref[0] * proj
        x1_ref[0, rows, :] = x1
        h2 = _rms(x1, gffn_ref[...]) * (1.0 + scale_ref[0]) + shift_ref[0]
        h_hi = h2.astype(BF16)
        h_lo = (h2 - h_hi.astype(F32)).astype(BF16)
        h2_ref[0, rows, :] = h_hi
        logits = jnp.dot(jnp.concatenate([h_hi, h_lo, h_hi], axis=1), w3, preferred_element_type=F32)

        lane = lax.broadcasted_iota(jnp.int32, (n, LANES), 1)
        gmask = lane < N_GROUPS
        gl = jnp.where(gmask, logits, NEG_INF)
        gmx = lmax(gl)
        group = lmin(jnp.where(jnp.logical_and(gmask, gl == gmx), lane, big))
        group_gate = 1.0 / lsum(jnp.exp(gl - gmx))
        lo = ROUTE_LANE0 + group * EXPERTS_PER_GROUP
        emask = jnp.logical_and(lane >= lo, lane < lo + EXPERTS_PER_GROUP)
        el = jnp.where(emask, logits, NEG_INF)
        l1 = lmax(el)
        i1 = lmin(jnp.where(el == l1, lane, big))
        el2 = jnp.where(lane == i1, NEG_INF, el)
        l2 = lmax(el2)
        i2 = lmin(jnp.where(el2 == l2, lane, big))
        r = jnp.exp(l2 - l1)
        return i1, i2, group_gate / (1.0 + r), group_gate * r / (1.0 + r)

    hm = tm // POST_ROW_GROUPS
    parts = [route_rows(slice(h * hm, (h + 1) * hm)) for h in range(POST_ROW_GROUPS)]
    i1, i2, w1, w2 = (jnp.concatenate([p[k] for p in parts], axis=0) for k in range(4))
    lane = lax.broadcasted_iota(jnp.int32, (tm, LANES), 1)

    is1 = lane == i1
    is2 = lane == i2
    oh = jnp.where(is1, 1.0, jnp.where(is2, 1.0, 0.0))
    earlier = jnp.dot(tril_ref[...], oh.astype(BF16), preferred_element_type=F32)
    runs = jnp.floor((jnp.sum(oh, axis=0, keepdims=True) + (SUBLANES - 1.0)) * (1.0 / SUBLANES))
    run_off = jnp.dot(jnp.broadcast_to(runs, (SUBLANES, LANES)).astype(BF16), triu_ref[...],
                      preferred_element_type=F32)[0:1]
    pos = earlier + run_off * SUBLANES
    slot1 = lsum(jnp.where(is1, pos, 0.0))
    slot2 = lsum(jnp.where(is2, pos, 0.0))
    cnt = runs * SUBLANES
    cnt_ref[0] = cnt
    base_ref[0] = carry_scr[...]
    carry_scr[...] = carry_scr[...] + cnt

    route_ref[0] = jnp.where(lane == 0, slot1, jnp.where(lane == 1, slot2,
                                                         jnp.where(lane == 2, w1, jnp.where(lane == 3, w2, 0.0))))


def _postmix(x, o_dil, o_sb, g_dil, g_sb, w_out_bf16, gate, shift, scale, g_ffn, w_router):
    b, s, d = x.shape
    tm = POST_ROWS
    nt = s // tm
    tril = jnp.asarray(np.tril(np.ones((tm, tm), np.float32), -1), BF16)
    triu = jnp.asarray(np.triu(np.ones((LANES, LANES), np.float32), 1), BF16)
    row = lambda w: pl.BlockSpec((1, tm, w), lambda bi, i: (bi, i, 0))
    vec = lambda w: pl.BlockSpec((1, w), lambda bi, i: (0, 0))
    mod_spec = pl.BlockSpec((1, 1, d), lambda bi, i: (bi, 0, 0))
    tile_vec = pl.BlockSpec((1, 1, LANES), lambda bi, i: (bi * nt + i, 0, 0))
    return pl.pallas_call(
        _postmix_kernel,
        grid=(b, nt),
        in_specs=[row(d), row(D_DIL), row(D_SB), vec(D_DIL), vec(D_SB),
                  pl.BlockSpec((d, d), lambda bi, i: (0, 0)),
                  mod_spec, mod_spec, mod_spec, vec(d),
                  pl.BlockSpec((d, LANES), lambda bi, i: (0, 0)),
                  pl.BlockSpec((tm, tm), lambda bi, i: (0, 0)),
                  pl.BlockSpec((LANES, LANES), lambda bi, i: (0, 0))],
        out_specs=[row(d), row(d), row(LANES), tile_vec, tile_vec],
        out_shape=[jax.ShapeDtypeStruct((b, s, d), F32),
                   jax.ShapeDtypeStruct((b, s, d), BF16),
                   jax.ShapeDtypeStruct((b, s, LANES), F32),
                   jax.ShapeDtypeStruct((b * nt, 1, LANES), F32),
                   jax.ShapeDtypeStruct((b * nt, 1, LANES), F32)],
        scratch_shapes=[pltpu.VMEM((1, LANES), F32)],
        compiler_params=_cparams(("arbitrary", "arbitrary")),
        name="postmix",
    )(x, o_dil, o_sb, g_dil.reshape(1, -1), g_sb.reshape(1, -1), w_out_bf16, gate, shift, scale,
      g_ffn.reshape(1, d), w_router, tril, triu)


def _for_each_run_piece(tile, start_ref, cnt_ref, base_ref, fn):
    def body(e, off):
        c = cnt_ref[tile * N_EXPERTS + e]
        sorted0 = start_ref[e] + base_ref[tile * N_EXPERTS + e]
        for k in range(SUBLANES.bit_length() - 1, POST_ROWS.bit_length()):
            p = 1 << k

            @pl.when((c & p) != 0)
            def _(p=p):
                done = c - (c & (2 * p - 1))
                fn(pl.multiple_of(off + done, SUBLANES), pl.multiple_of(sorted0 + done, SUBLANES), p, k % 2)
        return off + c

    return lax.fori_loop(0, N_EXPERTS, body, 0)


def _wait_rows(total, piece):
    for k in range(SUBLANES.bit_length() - 1, LOCAL_ROWS.bit_length()):
        p = 1 << k

        @pl.when((total & p) != 0)
        def _(p=p):
            piece(0, 0, p).wait()


def _sort_kernel(start_ref, cnt_ref, base_ref, rows_ref, h2_ref, route_ref, buf_ref, xs_scr, sem):
    tm = POST_ROWS
    lt = LOCAL_ROWS
    d = h2_ref.shape[2]
    tile = pl.program_id(0)
    slot = tile % 2

    def piece(slot, lrow, srow, rows):
        return pltpu.make_async_copy(xs_scr.at[slot, pl.ds(lrow, rows)], buf_ref.at[pl.ds(srow, rows)],
                                     sem.at[slot])

    def drain(tile, slot):
        _wait_rows(rows_ref[tile], functools.partial(piece, slot))

    @pl.when(tile >= 2)
    def _():
        drain(tile - 2, slot)

    lane = lax.broadcasted_iota(jnp.int32, (tm, LANES), 1)
    route = route_ref[0]
    w1 = jnp.sum(jnp.where(lane == 2, route, 0.0), axis=-1, keepdims=True)
    w2 = jnp.sum(jnp.where(lane == 3, route, 0.0), axis=-1, keepdims=True)

    def pieces(w):
        hi, mid, lw = _split3(w)
        return jnp.where(lane == 0, hi.astype(F32),
                         jnp.where(lane == 1, mid.astype(F32),
                                   jnp.where(lane == 2, lw.astype(F32), 0.0))).astype(BF16)

    route_t = route.T
    s1 = route_t[0:1, :].astype(jnp.int32)
    s2 = route_t[1:2, :].astype(jnp.int32)
    row = lax.broadcasted_iota(jnp.int32, (lt, tm), 0)
    p1 = jnp.where(row == s1, 1.0, 0.0)
    p2 = jnp.where(row == s2, 1.0, 0.0)
    xs_scr[slot, :, 0:d] = jnp.dot((p1 + p2).astype(BF16), h2_ref[0], preferred_element_type=F32)
    xs_scr[slot, :, d:] = (jnp.dot(p1.astype(BF16), pieces(w1), preferred_element_type=F32)
                           + jnp.dot(p2.astype(BF16), pieces(w2), preferred_element_type=F32))
    _for_each_run_piece(tile, start_ref, cnt_ref, base_ref,
                        lambda l, s, r, pri: piece(slot, l, s, r).start(priority=pri))

    last = pl.num_programs(0) - 1

    @pl.when(jnp.logical_and(tile == last, tile >= 1))
    def _():
        drain(tile - 1, 1 - slot)

    @pl.when(tile == last)
    def _():
        drain(tile, slot)
        xs_scr[0] = jnp.zeros_like(xs_scr[0])

        def zeros_to(srow, rows):
            return pltpu.make_async_copy(xs_scr.at[0, pl.ds(0, rows)], buf_ref.at[pl.ds(srow, rows)], sem.at[0])

        def gaps(fn):
            def body(e, carry):
                end = start_ref[e] + base_ref[last * N_EXPERTS + e] + cnt_ref[last * N_EXPERTS + e]
                n = start_ref[e + 1] - end
                for k in range(SUBLANES.bit_length() - 1, EXPERT_ROWS.bit_length() - 1):
                    p = 1 << k

                    @pl.when((n & p) != 0)
                    def _(p=p):
                        fn(zeros_to(pl.multiple_of(end + n - (n & (2 * p - 1)), SUBLANES), p))
                return carry

            lax.fori_loop(0, N_EXPERTS, body, 0)

        def tail(fn):
            def body(i, carry):
                fn(zeros_to(pl.multiple_of(start_ref[N_EXPERTS] + i * EXPERT_ROWS, EXPERT_ROWS), EXPERT_ROWS))
                return carry

            lax.fori_loop(0, (buf_ref.shape[0] - start_ref[N_EXPERTS]) // EXPERT_ROWS, body, 0)

        for walk in (gaps, tail):
            walk(lambda cp: cp.start())
        for walk in (gaps, tail):
            walk(lambda cp: cp.wait())


def _dispatch(pad_bounds, cnt, base, tile_rows, h2, route, cap):
    b, s, d = h2.shape
    tm = POST_ROWS
    nt = s // tm
    return pl.pallas_call(
        _sort_kernel,
        grid_spec=pltpu.PrefetchScalarGridSpec(
            num_scalar_prefetch=4, grid=(b * nt,),
            in_specs=[pl.BlockSpec((1, tm, d), lambda t, *_: (t // nt, t % nt, 0)),
                      pl.BlockSpec((1, tm, LANES), lambda t, *_: (t // nt, t % nt, 0))],
            out_specs=pl.BlockSpec(memory_space=pl.ANY),
            scratch_shapes=[pltpu.VMEM((2, LOCAL_ROWS, d + LANES), F32), pltpu.SemaphoreType.DMA((2,))]),
        out_shape=jax.ShapeDtypeStruct((cap, d + LANES), F32),
        compiler_params=_cparams(("arbitrary",)),
        name="dispatch",
    )(pad_bounds, cnt, base, tile_rows, h2, route)


def _expert_kernel(be_ref, nlive_ref, x_ref, wg_ref, wu_ref, wd_ref, y_ref):
    del be_ref
    d = y_ref.shape[1]

    @pl.when(pl.program_id(0) < nlive_ref[0])
    def _():
        xb = x_ref[:, 0:d].astype(BF16)
        weight = jnp.sum(x_ref[:, d:], axis=-1, keepdims=True)
        gate = jnp.dot(xb, wg_ref[0].astype(BF16), preferred_element_type=F32)
        up = jnp.dot(xb, wu_ref[0].astype(BF16), preferred_element_type=F32)
        act = gate / (1.0 + jnp.exp(-gate)) * up
        y_ref[...] = jnp.dot(act.astype(BF16), wd_ref[0].astype(BF16), preferred_element_type=F32) * weight

    @pl.when(pl.program_id(0) >= nlive_ref[0])
    def _():
        y_ref[...] = jnp.zeros_like(y_ref)


def _experts(block_expert, n_live, buf, wg, wu, wd):
    cap, dw = buf.shape
    d, f = wg.shape[1], wg.shape[2]
    bm = EXPERT_ROWS

    def blk(i, nl):
        return jnp.minimum(i, nl[0] - 1)

    return pl.pallas_call(
        _expert_kernel,
        grid_spec=pltpu.PrefetchScalarGridSpec(
            num_scalar_prefetch=2, grid=(cap // bm,),
            in_specs=[pl.BlockSpec((bm, dw), lambda i, be, nl: (blk(i, nl), 0)),
                      pl.BlockSpec((1, d, f), lambda i, be, nl: (be[blk(i, nl)], 0, 0)),
                      pl.BlockSpec((1, d, f), lambda i, be, nl: (be[blk(i, nl)], 0, 0)),
                      pl.BlockSpec((1, f, d), lambda i, be, nl: (be[blk(i, nl)], 0, 0))],
            out_specs=pl.BlockSpec((bm, d), lambda i, be, nl: (i, 0))),
        out_shape=jax.ShapeDtypeStruct((cap, d), F32),
        compiler_params=_cparams(("arbitrary",)),
        name="experts",
    )(block_expert, n_live, buf, wg, wu, wd)


def _combine_kernel(start_ref, cnt_ref, base_ref, rows_ref, x1_ref, route_ref, gate_ref, g_ref, y_hbm_ref,
                    o_ref, y_scr, sem):
    tm = POST_ROWS
    lt = LOCAL_ROWS
    tile = pl.program_id(0)
    slot = tile % 2

    def piece(slot, lrow, srow, rows):
        return pltpu.make_async_copy(y_hbm_ref.at[pl.ds(srow, rows)], y_scr.at[slot, pl.ds(lrow, rows)],
                                     sem.at[slot])

    def fetch(tile, slot):
        _for_each_run_piece(tile, start_ref, cnt_ref, base_ref,
                            lambda l, s, r, pri: piece(slot, l, s, r).start(priority=pri))

    @pl.when(tile == 0)
    def _():
        fetch(tile, slot)

    @pl.when(tile + 1 < pl.num_programs(0))
    def _():
        fetch(tile + 1, 1 - slot)

    lane = lax.broadcasted_iota(jnp.int32, (tm, LANES), 1)
    route = route_ref[0]
    s1 = jnp.sum(jnp.where(lane == 0, route, 0.0), axis=-1, keepdims=True).astype(jnp.int32)
    s2 = jnp.sum(jnp.where(lane == 1, route, 0.0), axis=-1, keepdims=True).astype(jnp.int32)
    col = lax.broadcasted_iota(jnp.int32, (tm, lt), 1)
    pick = jnp.where(col == s1, 1.0, jnp.where(col == s2, 1.0, 0.0)).astype(BF16)
    used = rows_ref[tile]
    _wait_rows(used, functools.partial(piece, slot))
    live = lax.broadcasted_iota(jnp.int32, (lt, 1), 0) < used
    yv = jnp.where(live, y_scr[slot], 0.0)
    hi = yv.astype(BF16)
    lo = (yv - hi.astype(F32)).astype(BF16)
    y = jnp.dot(jnp.concatenate([pick, pick], axis=1), jnp.concatenate([hi, lo], axis=0),
                preferred_element_type=F32)
    o_ref[0] = _rms(x1_ref[0] + gate_ref[0] * y, g_ref[...])


def _combine(pad_start, cnt, base, tile_rows, x1, y_sorted, route, gate, g_final):
    b, s, d = x1.shape
    tm = POST_ROWS
    nt = s // tm
    return pl.pallas_call(
        _combine_kernel,
        grid_spec=pltpu.PrefetchScalarGridSpec(
            num_scalar_prefetch=4, grid=(b * nt,),
            in_specs=[pl.BlockSpec((1, tm, d), lambda t, *_: (t // nt, t % nt, 0)),
                      pl.BlockSpec((1, tm, LANES), lambda t, *_: (t // nt, t % nt, 0)),
                      pl.BlockSpec((1, 1, d), lambda t, *_: (t // nt, 0, 0)),
                      pl.BlockSpec((1, d), lambda t, *_: (0, 0)),
                      pl.BlockSpec(memory_space=pl.ANY)],
            out_specs=pl.BlockSpec((1, tm, d), lambda t, *_: (t // nt, t % nt, 0)),
            scratch_shapes=[pltpu.VMEM((2, LOCAL_ROWS, d), F32), pltpu.SemaphoreType.DMA((2,))]),
        out_shape=jax.ShapeDtypeStruct((b, s, d), F32),
        compiler_params=_cparams(("arbitrary",)),
        name="combine",
    )(pad_start, cnt, base, tile_rows, x1, route, gate, g_final.reshape(1, d), y_sorted)


def kernel(x, c, w_ada, b_ada, g_mix, w_in, g_dil_out, g_sb_out, w_out, g_ffn,
           w_group, w_expert, w_gate, w_up, w_down, g_final):
    b, s, d = x.shape
    depth = w_ada.shape[0]
    assert s % DIL_UNIT == 0 and d == D_DIL + D_SB
    assert depth == 1, "the final rmsnorm is fused into the last layer's combine step"
    n = b * s
    ntiles = n // POST_ROWS
    bias = jnp.asarray(_dilated_bias())
    for layer in range(depth):
        mod = _ada(c, w_ada[layer], b_ada[layer])
        shift_mix, scale_mix, gate_mix, shift_ffn, scale_ffn, gate_ffn = (
            m.reshape(b, 1, d) for m in jnp.split(mod, 6, axis=-1))

        qkv_d, q_s, k_s, v_s = _premix(x, shift_mix, scale_mix, g_mix[layer], w_in[layer].astype(BF16))
        o_dil = _dilated(qkv_d, bias)
        o_sb = _stick(q_s, k_s, v_s)

        w_router = jnp.concatenate(
            [w_group[layer], w_expert[layer],
             jnp.zeros((d, LANES - N_GROUPS - N_EXPERTS), F32)], axis=1)
        x1, h2, route, cnt, base = _postmix(
            x, o_dil, o_sb, g_dil_out[layer], g_sb_out[layer], w_out[layer].astype(BF16),
            gate_mix, shift_ffn, scale_ffn, g_ffn[layer], w_router)

        bm = EXPERT_ROWS
        cnt = cnt[:, 0, ROUTE_LANE0:ROUTE_LANE0 + N_EXPERTS].astype(jnp.int32)
        base = base[:, 0, ROUTE_LANE0:ROUTE_LANE0 + N_EXPERTS].astype(jnp.int32)
        total = base[-1] + cnt[-1]
        tile_rows = jnp.sum(cnt, axis=1)
        cnt = cnt.reshape(-1)
        base = base.reshape(-1)
        padded = (total + bm - 1) // bm * bm
        pad_end = jnp.cumsum(padded)
        pad_start = (pad_end - padded).astype(jnp.int32)
        cap = -(-(2 * n + (SUBLANES - 1) * N_EXPERTS * ntiles) // bm) * bm + N_EXPERTS * bm
        n_blocks = cap // bm
        block_expert = jnp.minimum(
            jnp.sum(pad_end[None, :] <= (jnp.arange(n_blocks) * bm)[:, None], axis=1),
            N_EXPERTS - 1).astype(jnp.int32)
        pad_bounds = jnp.concatenate([pad_start, pad_end[-1:].astype(jnp.int32)])

        buf = _dispatch(pad_bounds, cnt, base, tile_rows, h2, route, cap)
        n_live = (pad_end[-1:] // bm).astype(jnp.int32)
        y_sorted = _experts(block_expert, n_live, buf, w_gate[layer], w_up[layer], w_down[layer])
        x = _combine(pad_start, cnt, base, tile_rows, x1, y_sorted, route, gate_ffn, g_final)
    return x
```
